```python
import jax, jax.numpy as jnp
from jax import lax
import numpy as np

D_MODEL = 2048
BATCH = 16
SEQ = 256
DEPTH = 2
DEC_BATCH = 8
DEC_SEQ = 2048
PAST_LEN = 512

GRID_W = 64
HEAD_DIM = 128
ATTN_HEADS = (D_MODEL // 2) // HEAD_DIM
ATTN_KV_HEADS = 2
ATTN_WIDTH = ATTN_HEADS * HEAD_DIM
KV_WIDTH = ATTN_KV_HEADS * HEAD_DIM
FOURIER_WIDTH = D_MODEL // 4
FOURIER_GROUPS = 4
RWKV_WIDTH = D_MODEL // 4
RWKV_N = 64
RWKV_HEADS = RWKV_WIDTH // RWKV_N
DECAY_LORA = 64
ICLR_LORA = 64
GATE_LORA = 128
RWKV_IN = 3 * RWKV_WIDTH + DECAY_LORA + ICLR_LORA + GATE_LORA
MIX_WIDTH = FOURIER_WIDTH + ATTN_WIDTH + RWKV_WIDTH
IN_WIDTH = FOURIER_WIDTH + ATTN_WIDTH + 2 * KV_WIDTH + RWKV_IN
D_FF = 5504
Q_BLOCK = 128
ROPE_THETA = 10000.0
NORM_EPS = 1e-6
GN_EPS = 64e-5

kernel_name = "hybrid_fourier_gqa_rwkv7_diffusion_step"


def rmsnorm(x, g):
    x32 = x.astype(jnp.float32)
    y = x32 * lax.rsqrt(jnp.mean(x32 * x32, axis=-1, keepdims=True) + NORM_EPS)
    return (y * g.astype(jnp.float32)).astype(x.dtype)


def dwconv3(x, w):
    xp = jnp.pad(x, ((0, 0), (1, 1), (0, 0)))
    return w[0] * xp[:, :-2] + w[1] * xp[:, 1:-1] + w[2] * xp[:, 2:]


def grid_positions(T):
    n_rows = T // GRID_W
    rows = jnp.repeat(jnp.arange(n_rows, dtype=jnp.int32), GRID_W)
    cols = jnp.tile(jnp.arange(GRID_W, dtype=jnp.int32), n_rows)
    return rows, cols


def axial_rope(x, rows, cols):
    def rot(xh, pos):
        d = xh.shape[-1]
        inv = 1.0 / (ROPE_THETA ** (jnp.arange(0, d, 2, dtype=jnp.float32) / d))
        ang = pos.astype(jnp.float32)[:, None] * inv[None, :]
        cos = jnp.concatenate([jnp.cos(ang), jnp.cos(ang)], -1)[None, :, None, :]
        sin = jnp.concatenate([jnp.sin(ang), jnp.sin(ang)], -1)[None, :, None, :]
        x1, x2 = jnp.split(xh, 2, axis=-1)
        return xh * cos + jnp.concatenate([-x2, x1], -1) * sin
    half = x.shape[-1] // 2
    x32 = x.astype(jnp.float32)
    out = jnp.concatenate([rot(x32[..., :half], rows), rot(x32[..., half:], cols)], -1)
    return out.astype(x.dtype)


def block_attention(q, k, v):
    B, T, H, Dh = q.shape
    KV = k.shape[2]
    G = H // KV
    nb = T // Q_BLOCK
    qb = q.reshape(B, nb, Q_BLOCK, KV, G, Dh).transpose(1, 0, 2, 3, 4, 5)
    scale = Dh ** -0.5

    def one_block(qblk):
        s = jnp.einsum('bqkgd,bskd->bkgqs', qblk, k).astype(jnp.float32) * scale
        p = jax.nn.softmax(s, axis=-1).astype(v.dtype)
        return jnp.einsum('bkgqs,bskd->bqkgd', p, v)

    o = lax.map(one_block, qb)
    return o.transpose(1, 0, 2, 3, 4, 5).reshape(B, T, H * Dh)


def fourier_mix(u):
    B, T, C = u.shape
    ug = u.astype(jnp.float32).reshape(B, T, FOURIER_GROUPS, C // FOURIER_GROUPS)
    f = jnp.fft.fft2(ug, axes=(1, 3), norm="ortho")
    return jnp.real(f).reshape(B, T, C).astype(u.dtype)


def rwkv_mix(rw, lp, init_state):
    B, T, _ = rw.shape
    C, H, N = RWKV_WIDTH, RWKV_HEADS, RWKV_N
    z = rw.astype(jnp.float32)
    r, k, v, w_in, a_in, g_in = jnp.split(
        z, [C, 2 * C, 3 * C, 3 * C + DECAY_LORA, 3 * C + DECAY_LORA + ICLR_LORA], axis=-1)
    w = -jax.nn.softplus(-(lp['rw_w0'] + jnp.einsum('btl,dlc->btdc', jnp.tanh(w_in), lp['rw_w2']))) - 0.5
    decay = jnp.exp(-jnp.exp(w))
    a = jax.nn.sigmoid(lp['rw_a0'] + jnp.einsum('btl,dlc->btdc', a_in, lp['rw_a2']))
    g = jax.nn.sigmoid(g_in) @ lp['rw_g2']
    kk = (k * lp['rw_kk']).reshape(B, T, H, N)
    kk = kk * lax.rsqrt(jnp.sum(kk * kk, axis=-1, keepdims=True) + 1e-12)
    kmod = k[:, :, None, :] * (1.0 + (a - 1.0) * lp['rw_ka'])

    def heads(t):
        return t.reshape(t.shape[:-1] + (H, N))

    def both(t):
        return jnp.stack([t, t[:, ::-1]], axis=2)

    def orient(t):
        return jnp.stack([t[:, :, 0], t[:, ::-1, 1]], axis=2)

    xs = (both(heads(r)), orient(heads(decay)), orient(heads(kmod)), both(heads(v)), both(kk), orient(heads(a)))
    xs = tuple(jnp.moveaxis(t, 1, 0) for t in xs)

    def step(S, inp):
        r_t, w_t, k_t, v_t, kk_t, a_t = inp
        sa = jnp.einsum('bdhvk,bdhk->bdhv', S, kk_t)
        S = (S * w_t[..., None, :] - sa[..., :, None] * (kk_t * a_t)[..., None, :]
             + v_t[..., :, None] * k_t[..., None, :])
        return S, jnp.einsum('bdhvk,bdhk->bdhv', S, r_t)

    S_fin, y = lax.scan(step, init_state.astype(jnp.float32), xs)
    y = jnp.moveaxis(y, 0, 1)
    y = y[:, :, 0] + y[:, ::-1, 1]
    mu = jnp.mean(y, axis=-1, keepdims=True)
    var = jnp.mean(jnp.square(y - mu), axis=-1, keepdims=True)
    y = ((y - mu) * lax.rsqrt(var + GN_EPS)).reshape(B, T, C) * lp['rw_lnx_g'] + lp['rw_lnx_b']
    bonus = jnp.einsum('bthn,btdhn,hn->bth', heads(r), heads(kmod), lp['rw_rk'])[..., None] * heads(v)
    out = (y + bonus.reshape(B, T, C)) * g
    return out.astype(rw.dtype), S_fin.astype(rw.dtype)


def trunk_layer(x, mod, pos, ctx_k, ctx_v, init_state, lp):
    B, T, _ = x.shape
    shift1, scale1, gate1, shift2, scale2, gate2 = jnp.split(mod, 6, axis=-1)
    h = rmsnorm(x, lp['norm1_g']) * (1.0 + scale1) + shift1
    proj = h @ lp['w_in']
    u_f, q, k, v, rw = jnp.split(
        proj, [FOURIER_WIDTH, FOURIER_WIDTH + ATTN_WIDTH, FOURIER_WIDTH + ATTN_WIDTH + KV_WIDTH,
               FOURIER_WIDTH + ATTN_WIDTH + 2 * KV_WIDTH], axis=-1)
    f_out = fourier_mix(u_f)
    q = rmsnorm(q.reshape(B, T, ATTN_HEADS, HEAD_DIM), lp['q_norm_g'])
    k = rmsnorm(k.reshape(B, T, ATTN_KV_HEADS, HEAD_DIM), lp['k_norm_g'])
    v = v.reshape(B, T, ATTN_KV_HEADS, HEAD_DIM)
    keys, vals = k, v
    if pos is not None:
        q = axial_rope(q, pos[0], pos[1])
        keys = axial_rope(k, pos[0], pos[1])
    if ctx_k is not None:
        keys = jnp.concatenate([keys, ctx_k.astype(keys.dtype)], axis=1)
        vals = jnp.concatenate([vals, ctx_v.astype(vals.dtype)], axis=1)
    a_out = block_attention(q, keys, vals)
    r_out, final_state = rwkv_mix(dwconv3(rw, lp['rw_conv']), lp, init_state)
    mix = jnp.concatenate([f_out, a_out, r_out], axis=-1) @ lp['w_out']
    x = x + gate1 * mix
    h2 = rmsnorm(x, lp['norm2_g']) * (1.0 + scale2) + shift2
    u = dwconv3(h2 @ lp['ffn_up'], lp['ffn_conv_w']) + lp['ffn_conv_b']
    ua, ug = jnp.split(u, 2, axis=-1)
    x = x + gate2 * ((jax.nn.silu(ug) * ua) @ lp['ffn_down'])
    return x, k, v, final_state


def setup_inputs(seed: int = 0) -> dict:
    key = jax.random.key(seed)
    ks = iter(jax.random.split(key, 40))

    def nrm(shape, scale):
        return jax.random.normal(next(ks), shape, jnp.float32) * scale

    conv_base = jnp.array([0.25, 0.5, 0.25], jnp.float32)[None, :, None]
    return {
        "x_prompt": nrm((BATCH, SEQ, D_MODEL), 1.0),
        "x_sample": nrm((DEC_BATCH, DEC_SEQ, D_MODEL), 1.0),
        "cache_attn_k": nrm((DEC_BATCH, DEPTH, PAST_LEN, ATTN_KV_HEADS, HEAD_DIM), 1.0),
        "cache_attn_v": nrm((DEC_BATCH, DEPTH, PAST_LEN, ATTN_KV_HEADS, HEAD_DIM), 1.0),
        "state_rwkv": nrm((DEC_BATCH, DEPTH, 2, RWKV_HEADS, RWKV_N, RWKV_N), 0.3),
        "c": nrm((DEC_BATCH, D_MODEL), 1.0),
        "c_ctx": nrm((D_MODEL,), 1.0),
        "w_ada": nrm((DEPTH, D_MODEL, 6 * D_MODEL), 0.5 * D_MODEL ** -0.5),
        "b_ada": nrm((DEPTH, 6 * D_MODEL), 0.01),
        "norm1_g": 1.0 + nrm((DEPTH, D_MODEL), 0.02),
        "norm2_g": 1.0 + nrm((DEPTH, D_MODEL), 0.02),
        "w_in": nrm((DEPTH, D_MODEL, IN_WIDTH), D_MODEL ** -0.5),
        "w_out": nrm((DEPTH, MIX_WIDTH, D_MODEL), MIX_WIDTH ** -0.5),
        "q_norm_g": 1.0 + nrm((DEPTH, HEAD_DIM), 0.02),
        "k_norm_g": 1.0 + nrm((DEPTH, HEAD_DIM), 0.02),
        "rw_conv": conv_base + nrm((DEPTH, 3, RWKV_IN), 0.05),
        "rw_w0": nrm((DEPTH, 2, RWKV_WIDTH), 0.5),
        "rw_w2": nrm((DEPTH, 2, DECAY_LORA, RWKV_WIDTH), 0.5 * DECAY_LORA ** -0.5),
        "rw_a0": nrm((DEPTH, 2, RWKV_WIDTH), 0.5),
        "rw_a2": nrm((DEPTH, 2, ICLR_LORA, RWKV_WIDTH), 0.5 * ICLR_LORA ** -0.5),
        "rw_g2": nrm((DEPTH, GATE_LORA, RWKV_WIDTH), GATE_LORA ** -0.5),
        "rw_kk": 0.85 + nrm((DEPTH, RWKV_WIDTH), 0.05),
        "rw_ka": 1.0 + nrm((DEPTH, RWKV_WIDTH), 0.05),
        "rw_rk": nrm((DEPTH, RWKV_HEADS, RWKV_N), 0.1),
        "rw_lnx_g": 1.0 + nrm((DEPTH, RWKV_WIDTH), 0.02),
        "rw_lnx_b": nrm((DEPTH, RWKV_WIDTH), 0.01),
        "ffn_up": nrm((DEPTH, D_MODEL, 2 * D_FF), D_MODEL ** -0.5),
        "ffn_conv_w": conv_base + nrm((DEPTH, 3, 2 * D_FF), 0.05),
        "ffn_conv_b": nrm((DEPTH, 2 * D_FF), 0.01),
        "ffn_down": nrm((DEPTH, D_FF, D_MODEL), D_FF ** -0.5),
        "final_norm_g": 1.0 + nrm((D_MODEL,), 0.02),
    }


def reference(x_prompt, x_sample, cache_attn_k, cache_attn_v, state_rwkv, c, c_ctx,
              w_ada, b_ada, norm1_g, norm2_g, w_in, w_out, q_norm_g, k_norm_g,
              rw_conv, rw_w0, rw_w2, rw_a0, rw_a2, rw_g2, rw_kk, rw_ka, rw_rk, rw_lnx_g, rw_lnx_b,
              ffn_up, ffn_conv_w, ffn_conv_b, ffn_down, final_norm_g):
    B_p = x_prompt.shape[0]
    pos_lat = grid_positions(x_sample.shape[1])
    xp, xs = x_prompt, x_sample
    new_k, new_v, new_s = [], [], []
    for l in range(DEPTH):
        lp = dict(norm1_g=norm1_g[l], norm2_g=norm2_g[l], w_in=w_in[l], w_out=w_out[l],
                  q_norm_g=q_norm_g[l], k_norm_g=k_norm_g[l], rw_conv=rw_conv[l],
                  rw_w0=rw_w0[l], rw_w2=rw_w2[l], rw_a0=rw_a0[l], rw_a2=rw_a2[l], rw_g2=rw_g2[l],
                  rw_kk=rw_kk[l], rw_ka=rw_ka[l], rw_rk=rw_rk[l], rw_lnx_g=rw_lnx_g[l],
                  rw_lnx_b=rw_lnx_b[l], ffn_up=ffn_up[l], ffn_conv_w=ffn_conv_w[l],
                  ffn_conv_b=ffn_conv_b[l], ffn_down=ffn_down[l])
        mod_ctx = (jax.nn.silu(c_ctx) @ w_ada[l] + b_ada[l])[None, None, :]
        zero_state = jnp.zeros((B_p, 2, RWKV_HEADS, RWKV_N, RWKV_N), jnp.float32)
        xp, k_ctx, v_ctx, s_ctx = trunk_layer(xp, mod_ctx, None, None, None, zero_state, lp)
        new_k.append(k_ctx)
        new_v.append(v_ctx)
        new_s.append(s_ctx)
        mod_lat = (jax.nn.silu(c) @ w_ada[l] + b_ada[l])[:, None, :]
        xs, _, _, _ = trunk_layer(xs, mod_lat, pos_lat, cache_attn_k[:, l], cache_attn_v[:, l],
                                  state_rwkv[:, l], lp)
    y_prompt = rmsnorm(xp, final_norm_g)
    y_sample = rmsnorm(xs, final_norm_g)
    new_attn_k = jnp.stack(new_k, axis=1)
    new_attn_v = jnp.stack(new_v, axis=1)
    new_state_rwkv = jnp.stack(new_s, axis=1)
    return (y_prompt, y_sample, new_attn_k, new_attn_v, new_state_rwkv)
```

```python
import functools
import math

import jax
import jax.numpy as jnp
from jax import lax
from jax.experimental import pallas as pl
from jax.experimental.pallas import tpu as pltpu

D_MODEL = 2048
GRID_W = 64
HEAD_DIM = 128
ATTN_HEADS = 8
ATTN_KV_HEADS = 2
KV_GROUP = ATTN_HEADS // ATTN_KV_HEADS
ATTN_WIDTH = ATTN_HEADS * HEAD_DIM
KV_WIDTH = ATTN_KV_HEADS * HEAD_DIM
FOURIER_WIDTH = 512
FOURIER_GROUPS = 4
FOURIER_GROUP_WIDTH = FOURIER_WIDTH // FOURIER_GROUPS
RWKV_WIDTH = 512
RWKV_N = 64
RWKV_HEADS = RWKV_WIDTH // RWKV_N
DECAY_LORA = 64
ICLR_LORA = 64
GATE_LORA = 128
LORA_IN = DECAY_LORA + ICLR_LORA + GATE_LORA
RWKV_IN = 3 * RWKV_WIDTH + LORA_IN
IN_WIDTH = FOURIER_WIDTH + ATTN_WIDTH + 2 * KV_WIDTH + RWKV_IN
D_FF = 5504
ROPE_THETA = 10000.0
NORM_EPS = 1e-6
GN_EPS = 64e-5

MOD_ROWS = 16
FFN_TILE = 512
D_FF_PAD = -(-D_FF // FFN_TILE) * FFN_TILE
HALO = 16
CHUNK = 64
MIB = 2 ** 20

F32 = jnp.float32
BF16 = jnp.bfloat16
HIGHEST = lax.Precision.HIGHEST


def _params(semantics, vmem_mib):
    return pltpu.CompilerParams(dimension_semantics=semantics, vmem_limit_bytes=vmem_mib * MIB)


def _resident(shape):
    return pl.BlockSpec(shape, lambda *_: (0,) * len(shape), pipeline_mode=pl.Buffered(1))


def _dot(a, b):
    return jnp.dot(a, b, preferred_element_type=F32)


def _dot_nt(a, b):
    return lax.dot_general(a, b, (((1,), (1,)), ((), ())), preferred_element_type=F32)


def _dot_tn(a, b):
    return lax.dot_general(a, b, (((0,), (0,)), ((), ())), preferred_element_type=F32)


def _split_bf16(a):
    hi = a.astype(BF16)
    lo = (a - hi.astype(F32)).astype(BF16)
    return hi, lo


def _dot3(a, b):
    ah, al = _split_bf16(a)
    bh, bl = _split_bf16(b)
    return _dot(ah, bh) + (_dot(ah, bl) + _dot(al, bh))


def _modulated_rmsnorm(x, g, scale, shift):
    ms = jnp.mean(x * x, axis=-1, keepdims=True)
    return (x * lax.rsqrt(ms + NORM_EPS) * g) * (1.0 + scale) + shift


def _mod_body(c_ref, w_ref, b_ref, o_ref):
    c = c_ref[...]
    s = (c * jax.nn.sigmoid(c)).astype(BF16)
    o_ref[...] = _dot(s, w_ref[...].astype(BF16)) + b_ref[...]


def _adaln_mod(cvec, w_ada, b_ada):
    depth, d, n = w_ada.shape
    tn = 1024
    return pl.pallas_call(
        _mod_body,
        grid=(depth, n // tn),
        in_specs=[pl.BlockSpec((MOD_ROWS, d), lambda l, j: (0, 0)),
                  pl.BlockSpec((None, d, tn), lambda l, j: (l, 0, j)),
                  pl.BlockSpec((None, 1, tn), lambda l, j: (l, 0, j))],
        out_specs=pl.BlockSpec((None, MOD_ROWS, tn), lambda l, j: (l, 0, j)),
        out_shape=jax.ShapeDtypeStruct((depth, MOD_ROWS, n), F32),
        compiler_params=_params(("parallel", "parallel"), 40),
        name="adaln_mod",
    )(cvec, w_ada, b_ada.reshape(depth, 1, n))


_IN_SPLITS = (("f", 0, FOURIER_WIDTH, BF16),
              ("q", FOURIER_WIDTH, ATTN_WIDTH, F32),
              ("k", FOURIER_WIDTH + ATTN_WIDTH, KV_WIDTH, F32),
              ("v", FOURIER_WIDTH + ATTN_WIDTH + KV_WIDTH, KV_WIDTH, F32),
              ("rw", FOURIER_WIDTH + ATTN_WIDTH + 2 * KV_WIDTH, RWKV_IN, F32))


def _inproj_body(x_ref, g_ref, sc_ref, sh_ref, w_ref, *o_refs):
    h = _modulated_rmsnorm(x_ref[...], g_ref[...], sc_ref[...], sh_ref[...]).astype(BF16)
    for o_ref, (_, start, width, _) in zip(o_refs, _IN_SPLITS):
        o_ref[...] = _dot(h, w_ref[:, start:start + width]).astype(o_ref.dtype)


def _in_proj(x, g, scale, shift, w, seg_rows):
    rows, d = x.shape
    tm = 512 if rows % 512 == 0 else 256
    seg = lambda i: (i * tm // seg_rows, 0, 0)
    return pl.pallas_call(
        _inproj_body,
        grid=(rows // tm,),
        in_specs=[pl.BlockSpec((tm, d), lambda i: (i, 0)),
                  _resident((1, d)),
                  pl.BlockSpec((None, 1, d), seg),
                  pl.BlockSpec((None, 1, d), seg),
                  _resident(w.shape)],
        out_specs=[pl.BlockSpec((tm, width), lambda i: (i, 0)) for _, _, width, _ in _IN_SPLITS],
        out_shape=[jax.ShapeDtypeStruct((rows, width), dt) for _, _, width, dt in _IN_SPLITS],
        compiler_params=_params(("parallel",), 56),
        name="in_proj",
    )(x, g, scale, shift, w)


def _dft_tables(t):
    def angles(n):
        i = jnp.arange(n, dtype=jnp.int32)
        return (2.0 * math.pi / n) * ((i[:, None] * i[None, :]) % n).astype(F32)
    at = angles(t)
    time_tab = jnp.concatenate([jnp.cos(at), -jnp.sin(at)], axis=1).astype(BF16)
    ac = angles(FOURIER_GROUP_WIDTH)
    eye = jnp.eye(FOURIER_GROUPS, dtype=F32)
    chan_tab = jnp.concatenate([jnp.kron(eye, jnp.cos(ac)), jnp.kron(eye, jnp.sin(ac))], axis=1).astype(BF16)
    return time_tab, chan_tab


def _fourier_body(u_ref, ct_ref, cc_ref, o_ref, ab_scr, *, t, norm):
    @pl.when(pl.program_id(1) == 0)
    def _():
        ab = _dot(u_ref[...], cc_ref[...])
        ab_scr[0:t, :] = ab[:, :FOURIER_WIDTH].astype(BF16)
        ab_scr[t:2 * t, :] = ab[:, FOURIER_WIDTH:].astype(BF16)
    o_ref[...] = (_dot(ct_ref[...], ab_scr[...]) * norm).astype(o_ref.dtype)


def _fourier_mix(u, time_tab, chan_tab, row0, nseq, t):
    tm = min(t, 512)
    nt = t // tm
    seq0 = row0 // t
    return pl.pallas_call(
        functools.partial(_fourier_body, t=t, norm=1.0 / math.sqrt(t * FOURIER_GROUP_WIDTH)),
        grid=(nseq, nt),
        in_specs=[pl.BlockSpec((t, FOURIER_WIDTH), lambda b, i: (seq0 + b, 0)),
                  pl.BlockSpec((tm, 2 * t), lambda b, i: (i, 0)),
                  _resident(chan_tab.shape)],
        out_specs=pl.BlockSpec((tm, FOURIER_WIDTH), lambda b, i: (b * nt + i, 0)),
        out_shape=jax.ShapeDtypeStruct((nseq * t, FOURIER_WIDTH), BF16),
        scratch_shapes=[pltpu.VMEM((2 * t, FOURIER_WIDTH), BF16)],
        compiler_params=_params(("parallel", "arbitrary"), 40),
        name=f"fourier_mix_t{t}",
    )(u, time_tab, chan_tab)


def _rope_tables(t):
    pos = jnp.arange(t, dtype=jnp.int32)
    rows = (pos // GRID_W).astype(F32)
    cols = (pos % GRID_W).astype(F32)
    half = HEAD_DIM // 2
    inv = 1.0 / (ROPE_THETA ** (jnp.arange(0, half, 2, dtype=F32) / half))
    def tab(p):
        ang = p[:, None] * inv[None, :]
        return (jnp.concatenate([jnp.cos(ang), jnp.cos(ang)], -1),
                jnp.concatenate([-jnp.sin(ang), jnp.sin(ang)], -1))
    cr, sr = tab(rows)
    cc, sc = tab(cols)
    return jnp.concatenate([cr, cc], -1), jnp.concatenate([sr, sc], -1)


def _qkv_body(*refs, rope):
    if rope:
        q_ref, k_ref, v_ref, gq_ref, gk_ref, cos_ref, sin_ref, qo_ref, ko_ref, vo_ref = refs
        cos, sin = cos_ref[...], sin_ref[...]
        lane = lax.broadcasted_iota(jnp.int32, (1, HEAD_DIM), 1)
        low = (lane % (HEAD_DIM // 2)) < (HEAD_DIM // 4)
    else:
        q_ref, k_ref, v_ref, gq_ref, gk_ref, qo_ref, ko_ref, vo_ref, kn_ref = refs

    def head_norm(xh, g):
        ms = jnp.mean(xh * xh, axis=-1, keepdims=True)
        return xh * lax.rsqrt(ms + NORM_EPS) * g

    def rotate(xh):
        partner = jnp.where(low, pltpu.roll(xh, HEAD_DIM - HEAD_DIM // 4, 1), pltpu.roll(xh, HEAD_DIM // 4, 1))
        return xh * cos + partner * sin

    scale = HEAD_DIM ** -0.5
    for h in range(ATTN_HEADS):
        sl = slice(h * HEAD_DIM, (h + 1) * HEAD_DIM)
        qh = head_norm(q_ref[:, sl], gq_ref[...])
        if rope:
            qh = rotate(qh)
        qo_ref[:, sl] = (qh * scale).astype(BF16)
    for j in range(ATTN_KV_HEADS):
        sl = slice(j * HEAD_DIM, (j + 1) * HEAD_DIM)
        kh = head_norm(k_ref[:, sl], gk_ref[...])
        if rope:
            ko_ref[:, sl] = rotate(kh).astype(BF16)
        else:
            kn_ref[:, sl] = kh
            ko_ref[:, sl] = kh.astype(BF16)
    vo_ref[...] = v_ref[...].astype(BF16)


def _qkv_prep(q, k, v, gq, gk, row0, nrows, rope_tabs=None):
    tt = 256
    b0 = row0 // tt
    rope = rope_tabs is not None
    row = lambda w: pl.BlockSpec((tt, w), lambda i: (b0 + i, 0))
    out = lambda w: pl.BlockSpec((tt, w), lambda i: (i, 0))
    in_specs = [row(ATTN_WIDTH), row(KV_WIDTH), row(KV_WIDTH), _resident((1, HEAD_DIM)), _resident((1, HEAD_DIM))]
    args = [q, k, v, gq, gk]
    out_specs = [out(ATTN_WIDTH), out(KV_WIDTH), out(KV_WIDTH)]
    out_shape = [jax.ShapeDtypeStruct((nrows, ATTN_WIDTH), BF16),
                 jax.ShapeDtypeStruct((nrows, KV_WIDTH), BF16),
                 jax.ShapeDtypeStruct((nrows, KV_WIDTH), BF16)]
    if rope:
        t = rope_tabs[0].shape[0]
        nt = t // tt
        tab = pl.BlockSpec((tt, HEAD_DIM), lambda i: (i % nt, 0))
        in_specs += [tab, tab]
        args += list(rope_tabs)
    else:
        out_specs.append(out(KV_WIDTH))
        out_shape.append(jax.ShapeDtypeStruct((nrows, KV_WIDTH), F32))
    return pl.pallas_call(
        functools.partial(_qkv_body, rope=rope),
        grid=(nrows // tt,),
        in_specs=in_specs, out_specs=out_specs, out_shape=out_shape,
        compiler_params=_params(("parallel",), 32),
        name="qkv_prep_rope" if rope else "qkv_prep",
    )(*args)


def _attn_body(*refs, cached):
    if cached:
        q_ref, k_ref, v_ref, kc_ref, vc_ref, o_ref = refs
    else:
        q_ref, k_ref, v_ref, o_ref = refs
    for j in range(ATTN_KV_HEADS):
        kv = slice(j * HEAD_DIM, (j + 1) * HEAD_DIM)
        kj, vj = k_ref[:, kv], v_ref[:, kv]
        if cached:
            kc, vc = kc_ref[:, kv].astype(BF16), vc_ref[:, kv].astype(BF16)
        for g in range(KV_GROUP):
            sl = slice((j * KV_GROUP + g) * HEAD_DIM, (j * KV_GROUP + g + 1) * HEAD_DIM)
            qh = q_ref[:, sl]
            s = _dot_nt(qh, kj)
            m = jnp.max(s, axis=-1, keepdims=True)
            if cached:
                sc = _dot_nt(qh, kc)
                m = jnp.maximum(m, jnp.max(sc, axis=-1, keepdims=True))
            p = jnp.exp(s - m)
            l = jnp.sum(p, axis=-1, keepdims=True)
            acc = _dot(p.astype(BF16), vj)
            if cached:
                pc = jnp.exp(sc - m)
                l = l + jnp.sum(pc, axis=-1, keepdims=True)
                acc = acc + _dot(pc.astype(BF16), vc)
            o_ref[:, sl] = (acc / l).astype(o_ref.dtype)


def _attention(q, k, v, nseq, t, cache_k=None, cache_v=None):
    tq = 256
    nq = t // tq
    cached = cache_k is not None
    in_specs = [pl.BlockSpec((tq, ATTN_WIDTH), lambda b, i: (b * nq + i, 0)),
                pl.BlockSpec((t, KV_WIDTH), lambda b, i: (b, 0)),
                pl.BlockSpec((t, KV_WIDTH), lambda b, i: (b, 0))]
    args = [q, k, v]
    if cached:
        past = cache_k.shape[1]
        in_specs += [pl.BlockSpec((None, past, KV_WIDTH), lambda b, i: (b, 0, 0))] * 2
        args += [cache_k, cache_v]
    return pl.pallas_call(
        functools.partial(_attn_body, cached=cached),
        grid=(nseq, nq),
        in_specs=in_specs,
        out_specs=pl.BlockSpec((tq, ATTN_WIDTH), lambda b, i: (b * nq + i, 0)),
        out_shape=jax.ShapeDtypeStruct((nseq * t, ATTN_WIDTH), BF16),
        compiler_params=_params(("parallel", "parallel"), 48),
        name="attention_cached" if cached else "attention",
    )(*args)


def _seq_position(row_start, n, rows_lat, t_lat, t_ctx):
    rows = row_start + lax.broadcasted_iota(jnp.int32, (n, 1), 0)
    length = jnp.where(rows < rows_lat, t_lat, t_ctx)
    return lax.rem(rows, length), length


def _conv3(ext_ref, w_ref, tm, pos, length):
    not_first = (pos != 0).astype(F32)
    not_last = (pos != length - 1).astype(F32)
    return (w_ref[0:1, :] * (ext_ref[HALO - 1:HALO - 1 + tm, :] * not_first)
            + w_ref[1:2, :] * ext_ref[HALO:HALO + tm, :]
            + w_ref[2:3, :] * (ext_ref[HALO + 1:HALO + 1 + tm, :] * not_last))


def _head_sum_matrix():
    i = lax.broadcasted_iota(jnp.int32, (RWKV_WIDTH, RWKV_WIDTH), 0) // RWKV_N
    j = lax.broadcasted_iota(jnp.int32, (RWKV_WIDTH, RWKV_WIDTH), 1) // RWKV_N
    return (i == j).astype(F32)


def _head_sums(x, ones_bd):
    return jnp.dot(x, ones_bd, preferred_element_type=F32, precision=HIGHEST)


def _rwkv_prep_body(rw_ref, prev_ref, next_ref, cw_ref, wl_ref, w0_ref, a0_ref, kks_ref, ka_ref, rk_ref,
                    r_ref, v_ref, kk_ref, g_ref, bonus_ref, lw_ref, km_ref, b_ref, ext_scr,
                    *, tm, rows_lat, t_lat, t_ctx):
    ext_scr[0:HALO, :] = prev_ref[...]
    ext_scr[HALO:HALO + tm, :] = rw_ref[...]
    ext_scr[HALO + tm:, :] = next_ref[...]
    pos, length = _seq_position(pl.program_id(0) * tm, tm, rows_lat, t_lat, t_ctx)
    z = _conv3(ext_scr, cw_ref, tm, pos, length)
    c = RWKV_WIDTH
    r, k, v = z[:, :c], z[:, c:2 * c], z[:, 2 * c:3 * c]
    zl = z[:, 3 * c:]
    lane = lax.broadcasted_iota(jnp.int32, (1, LORA_IN), 1)
    lora_in = jnp.where(lane < DECAY_LORA, jnp.tanh(zl),
                        jnp.where(lane < DECAY_LORA + ICLR_LORA, zl, jax.nn.sigmoid(zl)))
    lora = jnp.dot(lora_in, wl_ref[...], preferred_element_type=F32, precision=HIGHEST)
    ones_bd = _head_sum_matrix()
    kk = k * kks_ref[...]
    kk = kk * lax.rsqrt(_head_sums(kk * kk, ones_bd) + 1e-12)
    r_ref[...] = r
    v_ref[...] = v
    kk_ref[...] = kk
    g_ref[...] = lora[:, 4 * c:5 * c]
    kmod_sum = jnp.zeros_like(k)
    for d in range(2):
        wpre = w0_ref[d:d + 1, :] + lora[:, d * c:(d + 1) * c]
        w = -jax.nn.softplus(-wpre) - 0.5
        lw_ref[d] = -jnp.exp(w)
        a = jax.nn.sigmoid(a0_ref[d:d + 1, :] + lora[:, (2 + d) * c:(3 + d) * c])
        kmod = k * (1.0 + (a - 1.0) * ka_ref[...])
        km_ref[d] = kmod
        b_ref[d] = kk * a
        kmod_sum = kmod_sum + kmod
    bonus_ref[...] = _head_sums(r * kmod_sum * rk_ref[...], ones_bd) * v


def _rwkv_prep(rw, conv_w, lora_w, w0, a0, kk_scale, ka, rk, rows_lat, t_lat, t_ctx):
    rows = rw.shape[0]
    tm = 256
    nh = tm // HALO
    last = rows // HALO - 1
    c = RWKV_WIDTH
    one = lambda: pl.BlockSpec((tm, c), lambda i: (i, 0))
    two = lambda: pl.BlockSpec((2, tm, c), lambda i: (0, i, 0))
    return pl.pallas_call(
        functools.partial(_rwkv_prep_body, tm=tm, rows_lat=rows_lat, t_lat=t_lat, t_ctx=t_ctx),
        grid=(rows // tm,),
        in_specs=[pl.BlockSpec((tm, RWKV_IN), lambda i: (i, 0)),
                  pl.BlockSpec((HALO, RWKV_IN), lambda i: (jnp.maximum(i * nh - 1, 0), 0)),
                  pl.BlockSpec((HALO, RWKV_IN), lambda i: (jnp.minimum((i + 1) * nh, last), 0)),
                  _resident(conv_w.shape), _resident(lora_w.shape), _resident(w0.shape), _resident(a0.shape),
                  _resident(kk_scale.shape), _resident(ka.shape), _resident(rk.shape)],
        out_specs=[one(), one(), one(), one(), one(), two(), two(), two()],
        out_shape=[jax.ShapeDtypeStruct((rows, c), F32)] * 5 + [jax.ShapeDtypeStruct((2, rows, c), F32)] * 3,
        scratch_shapes=[pltpu.VMEM((tm + 2 * HALO, RWKV_IN), F32)],
        compiler_params=_params(("parallel",), 48),
        name="rwkv_prep",
    )(rw, rw, rw, conv_w, lora_w, w0, a0, kk_scale, ka, rk)


def _unit_lower_inverse(n_mat, eye):
    x = eye - n_mat
    p = _dot3(n_mat, n_mat)
    steps = int(math.log2(CHUNK)) - 1
    for s in range(steps):
        x = x + _dot3(x, p)
        if s + 1 < steps:
            p = _dot3(p, p)
    return x


def _rwkv_scan_body(r_ref, v_ref, kk_ref, lw_ref, km_ref, b_ref, s0_ref, y_ref, sfin_ref, s_scr):
    d = pl.program_id(1)
    c = pl.program_id(2)

    @pl.when(c == 0)
    def _():
        s_scr[...] = s0_ref[...]

    n = CHUNK
    row = lax.broadcasted_iota(jnp.int32, (n, n), 0)
    col = lax.broadcasted_iota(jnp.int32, (n, n), 1)
    sign = 1 - 2 * d
    before = (col - row) * sign < 0
    upto = (col - row) * sign <= 0
    eye = (col == row).astype(F32)

    lw = lw_ref[...]
    cs = jnp.dot(upto.astype(F32), lw, preferred_element_type=F32, precision=HIGHEST)
    tot = jnp.sum(lw, axis=0, keepdims=True)
    grow = jnp.exp(-cs)
    kkt = (kk_ref[...] * jnp.exp(cs - lw)).astype(BF16)
    rt = (r_ref[...] * jnp.exp(cs)).astype(BF16)
    b = b_ref[...]
    km = km_ref[...]
    bt = (b * grow).astype(BF16)
    kt = (km * grow).astype(BF16)
    to_end = jnp.exp(tot - cs)
    bh = (b * to_end).astype(BF16)
    kh = (km * to_end).astype(BF16)
    w_all = jnp.exp(tot)
    v = v_ref[...]
    vb = v.astype(BF16)

    for h in range(RWKV_HEADS):
        sl = slice(h * RWKV_N, (h + 1) * RWKV_N)
        a_ab = jnp.where(before, _dot_nt(kkt[:, sl], bt[:, sl]), 0.0)
        a_ak = jnp.where(before, _dot_nt(kkt[:, sl], kt[:, sl]), 0.0)
        a_rb = jnp.where(upto, _dot_nt(rt[:, sl], bt[:, sl]), 0.0)
        a_rk = jnp.where(upto, _dot_nt(rt[:, sl], kt[:, sl]), 0.0)
        tinv = _unit_lower_inverse(a_ab, eye)
        s = s_scr[h]
        sb = s.astype(BF16)
        rhs = -(_dot_nt(kkt[:, sl], sb) + _dot(a_ak.astype(BF16), vb[:, sl]))
        u = _dot3(tinv, rhs)
        ub = u.astype(BF16)
        y = _dot_nt(rt[:, sl], sb) + _dot(a_rb.astype(BF16), ub) + _dot(a_rk.astype(BF16), vb[:, sl])
        y_ref[:, sl] = y
        s_scr[h] = s * w_all[:, sl] + _dot_tn(ub, bh[:, sl]) + _dot_tn(vb[:, sl], kh[:, sl])

    @pl.when(c == pl.num_programs(2) - 1)
    def _():
        sfin_ref[...] = s_scr[...]


def _rwkv_scan(r, v, kk, lw, km, b, s0, row0, nseq, t):
    n = CHUNK
    nc = t // n
    c0 = row0 // n
    c = RWKV_WIDTH
    chunk = lambda bb, d, cc: cc + d * (nc - 1 - 2 * cc)
    one = pl.BlockSpec((n, c), lambda bb, d, cc: (c0 + bb * nc + chunk(bb, d, cc), 0))
    two = pl.BlockSpec((None, n, c), lambda bb, d, cc: (d, c0 + bb * nc + chunk(bb, d, cc), 0))
    state = pl.BlockSpec((None, None, RWKV_HEADS, RWKV_N, RWKV_N), lambda bb, d, cc: (bb, d, 0, 0, 0))
    return pl.pallas_call(
        _rwkv_scan_body,
        grid=(nseq, 2, nc),
        in_specs=[one, one, one, two, two, two, state],
        out_specs=[pl.BlockSpec((None, n, c), lambda bb, d, cc: (d, bb * nc + chunk(bb, d, cc), 0)), state],
        out_shape=[jax.ShapeDtypeStruct((2, nseq * t, c), F32),
                   jax.ShapeDtypeStruct((nseq, 2, RWKV_HEADS, RWKV_N, RWKV_N), F32)],
        scratch_shapes=[pltpu.VMEM((RWKV_HEADS, RWKV_N, RWKV_N), F32)],
        compiler_params=_params(("parallel", "parallel", "arbitrary"), 32),
        name=f"rwkv_scan_t{t}",
    )(r, v, kk, lw, km, b, s0)


def _rwkv_out_body(y_ref, bonus_ref, g_ref, lg_ref, lb_ref, o_ref):
    ones_bd = _head_sum_matrix()
    y = y_ref[0] + y_ref[1]
    mu = _head_sums(y, ones_bd) * (1.0 / RWKV_N)
    yc = y - mu
    var = _head_sums(yc * yc, ones_bd) * (1.0 / RWKV_N)
    yn = yc * lax.rsqrt(var + GN_EPS) * lg_ref[...] + lb_ref[...]
    o_ref[...] = ((yn + bonus_ref[...]) * g_ref[...]).astype(o_ref.dtype)


def _rwkv_out(y, bonus, g, row0, ln_g, ln_b):
    nrows = y.shape[1]
    tm = 256
    b0 = row0 // tm
    c = RWKV_WIDTH
    src = pl.BlockSpec((tm, c), lambda i: (b0 + i, 0))
    return pl.pallas_call(
        _rwkv_out_body,
        grid=(nrows // tm,),
        in_specs=[pl.BlockSpec((2, tm, c), lambda i: (0, i, 0)), src, src, _resident((1, c)), _resident((1, c))],
        out_specs=pl.BlockSpec((tm, c), lambda i: (i, 0)),
        out_shape=jax.ShapeDtypeStruct((nrows, c), BF16),
        compiler_params=_params(("parallel",), 32),
        name="rwkv_out",
    )(y, bonus, g, ln_g, ln_b)


def _outproj_body(f_ref, a_ref, r_ref, x_ref, gate_ref, w_ref, o_ref):
    f0, a0, r0 = 0, FOURIER_WIDTH, FOURIER_WIDTH + ATTN_WIDTH
    mix = (_dot(f_ref[...], w_ref[f0:a0, :]) + _dot(a_ref[...], w_ref[a0:r0, :])
           + _dot(r_ref[...], w_ref[r0:, :]))
    o_ref[...] = x_ref[...] + gate_ref[...] * mix


def _out_proj(f, a, r, x, gate, w, seg_rows):
    rows, d = x.shape
    tm = 512 if rows % 512 == 0 else 256
    row = lambda arr: pl.BlockSpec((tm, arr.shape[1]), lambda i: (i, 0))
    return pl.pallas_call(
        _outproj_body,
        grid=(rows // tm,),
        in_specs=[row(f), row(a), row(r), row(x),
                  pl.BlockSpec((None, 1, d), lambda i: (i * tm // seg_rows, 0, 0)),
                  _resident(w.shape)],
        out_specs=row(x),
        out_shape=jax.ShapeDtypeStruct(x.shape, F32),
        compiler_params=_params(("parallel",), 48),
        name="out_proj",
    )(f, a, r, x, gate, w)


def _ffn_body(x_ref, prev_ref, next_ref, g_ref, sc_ref, sh_ref, gate_ref, wa_ref, wg_ref,
              cwa_ref, cwg_ref, ba_ref, bg_ref, wd_ref, o_ref, h_scr, ua_scr, ug_scr, acc_scr,
              *, tm, rows_lat, t_lat, t_ctx):
    j = pl.program_id(1)

    @pl.when(j == 0)
    def _():
        norm = lambda x: _modulated_rmsnorm(x, g_ref[...], sc_ref[...], sh_ref[...]).astype(BF16)
        h_scr[0:HALO, :] = norm(prev_ref[...])
        h_scr[HALO:HALO + tm, :] = norm(x_ref[...])
        h_scr[HALO + tm:, :] = norm(next_ref[...])

    h = h_scr[...]
    ua_scr[...] = _dot(h, wa_ref[...])
    ug_scr[...] = _dot(h, wg_ref[...])
    pos, length = _seq_position(pl.program_id(0) * tm, tm, rows_lat, t_lat, t_ctx)
    ua = _conv3(ua_scr, cwa_ref, tm, pos, length) + ba_ref[...]
    ug = _conv3(ug_scr, cwg_ref, tm, pos, length) + bg_ref[...]
    act = (ug * jax.nn.sigmoid(ug) * ua).astype(BF16)
    part = _dot(act, wd_ref[...])

    @pl.when(j == 0)
    def _():
        acc_scr[...] = part

    @pl.when(j > 0)
    def _():
        acc_scr[...] += part

    @pl.when(j == pl.num_programs(1) - 1)
    def _():
        o_ref[...] = x_ref[...] + gate_ref[...] * acc_scr[...]


def _conv_ffn(x, g, scale, shift, gate, wa, wg, cwa, cwg, ba, bg, wd, seg_rows, rows_lat, t_lat, t_ctx):
    rows, d = x.shape
    tm = 512 if rows % 512 == 0 else 256
    tf = FFN_TILE
    nh = tm // HALO
    last = rows // HALO - 1
    seg = lambda i, j: (i * tm // seg_rows, 0, 0)
    col = lambda r: pl.BlockSpec((r, tf), lambda i, j: (0, j))
    return pl.pallas_call(
        functools.partial(_ffn_body, tm=tm, rows_lat=rows_lat, t_lat=t_lat, t_ctx=t_ctx),
        grid=(rows // tm, D_FF_PAD // tf),
        in_specs=[pl.BlockSpec((tm, d), lambda i, j: (i, 0)),
                  pl.BlockSpec((HALO, d), lambda i, j: (jnp.maximum(i * nh - 1, 0), 0)),
                  pl.BlockSpec((HALO, d), lambda i, j: (jnp.minimum((i + 1) * nh, last), 0)),
                  pl.BlockSpec((1, d), lambda i, j: (0, 0)),
                  pl.BlockSpec((None, 1, d), seg), pl.BlockSpec((None, 1, d), seg), pl.BlockSpec((None, 1, d), seg),
                  col(d), col(d), col(3), col(3), col(1), col(1),
                  pl.BlockSpec((tf, d), lambda i, j: (j, 0))],
        out_specs=pl.BlockSpec((tm, d), lambda i, j: (i, 0)),
        out_shape=jax.ShapeDtypeStruct(x.shape, F32),
        scratch_shapes=[pltpu.VMEM((tm + 2 * HALO, d), BF16),
                        pltpu.VMEM((tm + 2 * HALO, tf), F32),
                        pltpu.VMEM((tm + 2 * HALO, tf), F32),
                        pltpu.VMEM((tm, d), F32)],
        compiler_params=_params(("parallel", "arbitrary"), 56),
        name="conv_ffn",
    )(x, x, x, g, scale, shift, gate, wa, wg, cwa, cwg, ba, bg, wd)


def _final_norm_body(x_ref, g_ref, o_ref):
    x = x_ref[...]
    ms = jnp.mean(x * x, axis=-1, keepdims=True)
    o_ref[...] = x * lax.rsqrt(ms + NORM_EPS) * g_ref[...]


def _final_norm(x, g, row0, nrows):
    tm = 256
    b0 = row0 // tm
    d = x.shape[1]
    return pl.pallas_call(
        _final_norm_body,
        grid=(nrows // tm,),
        in_specs=[pl.BlockSpec((tm, d), lambda i: (b0 + i, 0)), _resident((1, d))],
        out_specs=pl.BlockSpec((tm, d), lambda i: (i, 0)),
        out_shape=jax.ShapeDtypeStruct((nrows, d), F32),
        compiler_params=_params(("parallel",), 32),
        name="final_norm",
    )(x, g)


def _lora_weight(w2, a2, g2):
    c = RWKV_WIDTH
    wl = jnp.zeros((LORA_IN, 5 * c), F32)
    wl = wl.at[:DECAY_LORA, 0:c].set(w2[0]).at[:DECAY_LORA, c:2 * c].set(w2[1])
    wl = wl.at[DECAY_LORA:DECAY_LORA + ICLR_LORA, 2 * c:3 * c].set(a2[0])
    wl = wl.at[DECAY_LORA:DECAY_LORA + ICLR_LORA, 3 * c:4 * c].set(a2[1])
    return wl.at[DECAY_LORA + ICLR_LORA:, 4 * c:].set(g2)


def _pad_cols(w):
    return jnp.pad(w, ((0, 0), (0, D_FF_PAD - D_FF)))


def _forward(x_lat, x_ctx, cache_k, cache_v, state, c, c_ctx, w_ada, b_ada, norm1_g, norm2_g, w_in, w_out,
             q_norm_g, k_norm_g, rw_conv, rw_w0, rw_w2, rw_a0, rw_a2, rw_g2, rw_kk, rw_ka, rw_rk,
             rw_lnx_g, rw_lnx_b, ffn_up, ffn_conv_w, ffn_conv_b, ffn_down, final_norm_g):
    n_lat, t_lat, d = x_lat.shape
    n_ctx, t_ctx, _ = x_ctx.shape
    depth = w_ada.shape[0]
    past = cache_k.shape[2]
    rows_lat, rows_ctx = n_lat * t_lat, n_ctx * t_ctx
    assert rows_ctx % t_lat == 0 and n_lat + rows_ctx // t_lat <= MOD_ROWS
    assert t_lat % 256 == 0 and t_ctx % 128 == 0 and t_lat % GRID_W == 0

    x = jnp.concatenate([x_lat.reshape(rows_lat, d), x_ctx.reshape(rows_ctx, d)], axis=0)
    cvec = jnp.concatenate([c, jnp.broadcast_to(c_ctx[None, :], (MOD_ROWS - n_lat, d))], axis=0)
    mod = _adaln_mod(cvec, w_ada, b_ada).reshape(depth, MOD_ROWS, 6, 1, d)

    rope_tabs = _rope_tables(t_lat)
    time_lat, chan_tab = _dft_tables(t_lat)
    time_ctx, _ = _dft_tables(t_ctx)
    zero_state = jnp.zeros((n_ctx, 2, RWKV_HEADS, RWKV_N, RWKV_N), F32)
    row2 = lambda a: a.reshape(1, -1)

    new_k, new_v, new_s = [], [], []
    for l in range(depth):
        shift1, scale1, gate1, shift2, scale2, gate2 = (mod[l, :, i] for i in range(6))
        f, q, k, v, rw = _in_proj(x, row2(norm1_g[l]), scale1, shift1, w_in[l].astype(BF16), t_lat)

        f_lat = _fourier_mix(f, time_lat, chan_tab, 0, n_lat, t_lat)
        f_ctx = _fourier_mix(f, time_ctx, chan_tab, rows_lat, n_ctx, t_ctx)

        gq, gk = row2(q_norm_g[l]), row2(k_norm_g[l])
        q_lat, k_lat, v_lat = _qkv_prep(q, k, v, gq, gk, 0, rows_lat, rope_tabs)
        q_c, k_c, v_c, k_norm = _qkv_prep(q, k, v, gq, gk, rows_lat, rows_ctx)
        a_lat = _attention(q_lat, k_lat, v_lat, n_lat, t_lat,
                           cache_k[:, l].reshape(n_lat, past, KV_WIDTH), cache_v[:, l].reshape(n_lat, past, KV_WIDTH))
        a_ctx = _attention(q_c, k_c, v_c, n_ctx, t_ctx)
        new_k.append(k_norm.reshape(n_ctx, t_ctx, ATTN_KV_HEADS, HEAD_DIM))
        new_v.append(v[rows_lat:].reshape(n_ctx, t_ctx, ATTN_KV_HEADS, HEAD_DIM))

        r_, v_, kk, g_, bonus, lw, km, b_ = _rwkv_prep(
            rw, rw_conv[l], _lora_weight(rw_w2[l], rw_a2[l], rw_g2[l]), rw_w0[l], rw_a0[l],
            row2(rw_kk[l]), row2(rw_ka[l]), row2(rw_rk[l]), rows_lat, t_lat, t_ctx)
        y_lat, _ = _rwkv_scan(r_, v_, kk, lw, km, b_, state[:, l], 0, n_lat, t_lat)
        y_ctx, s_ctx = _rwkv_scan(r_, v_, kk, lw, km, b_, zero_state, rows_lat, n_ctx, t_ctx)
        ln_g, ln_b = row2(rw_lnx_g[l]), row2(rw_lnx_b[l])
        r_lat = _rwkv_out(y_lat, bonus, g_, 0, ln_g, ln_b)
        r_ctx = _rwkv_out(y_ctx, bonus, g_, rows_lat, ln_g, ln_b)
        new_s.append(s_ctx)

        cat = lambda a, b: jnp.concatenate([a, b], axis=0)
        x = _out_proj(cat(f_lat, f_ctx), cat(a_lat, a_ctx), cat(r_lat, r_ctx), x, gate1,
                      w_out[l].astype(BF16), t_lat)

        up = ffn_up[l]
        x = _conv_ffn(x, row2(norm2_g[l]), scale2, shift2, gate2,
                      _pad_cols(up[:, :D_FF]).astype(BF16), _pad_cols(up[:, D_FF:]).astype(BF16),
                      _pad_cols(ffn_conv_w[l][:, :D_FF]), _pad_cols(ffn_conv_w[l][:, D_FF:]),
                      _pad_cols(row2(ffn_conv_b[l][:D_FF])), _pad_cols(row2(ffn_conv_b[l][D_FF:])),
                      jnp.pad(ffn_down[l], ((0, D_FF_PAD - D_FF), (0, 0))).astype(BF16),
                      t_lat, rows_lat, t_lat, t_ctx)

    fg = row2(final_norm_g)
    y_lat = _final_norm(x, fg, 0, rows_lat).reshape(n_lat, t_lat, d)
    y_ctx = _final_norm(x, fg, rows_lat, rows_ctx).reshape(n_ctx, t_ctx, d)
    return (y_ctx, y_lat, jnp.stack(new_k, axis=1), jnp.stack(new_v, axis=1), jnp.stack(new_s, axis=1))


def kernel(x_prompt, x_sample, cache_attn_k, cache_attn_v, state_rwkv, c, c_ctx, w_ada, b_ada, norm1_g, norm2_g, w_in, w_out, q_norm_g, k_norm_g, rw_conv, rw_w0, rw_w2, rw_a0, rw_a2, rw_g2, rw_kk, rw_ka, rw_rk, rw_lnx_g, rw_lnx_b, ffn_up, ffn_conv_w, ffn_conv_b, ffn_down, final_norm_g):
    return _forward(x_sample, x_prompt, cache_attn_k, cache_attn_v, state_rwkv, c, c_ctx, w_ada, b_ada,
                    norm1_g, norm2_g, w_in, w_out, q_norm_g, k_norm_g, rw_conv, rw_w0, rw_w2, rw_a0, rw_a2,
                    rw_g2, rw_kk, rw_ka, rw_rk, rw_lnx_g, rw_lnx_b, ffn_up, ffn_conv_w, ffn_conv_b, ffn_down,
                    final_norm_g)
```

```python
import functools
import math

import jax
import jax.numpy as jnp
from jax import lax
from jax.experimental import pallas as pl
from jax.experimental.pallas import tpu as pltpu

D_MODEL = 2048
GRID_W = 64
HEAD_DIM = 128
ATTN_HEADS = 8
ATTN_KV_HEADS = 2
KV_GROUP = ATTN_HEADS // ATTN_KV_HEADS
ATTN_WIDTH = ATTN_HEADS * HEAD_DIM
KV_WIDTH = ATTN_KV_HEADS * HEAD_DIM
FOURIER_WIDTH = 512
FOURIER_GROUPS = 4
FOURIER_GROUP_WIDTH = FOURIER_WIDTH // FOURIER_GROUPS
RWKV_WIDTH = 512
RWKV_N = 64
RWKV_HEADS = RWKV_WIDTH // RWKV_N
DECAY_LORA = 64
ICLR_LORA = 64
GATE_LORA = 128
LORA_IN = DECAY_LORA + ICLR_LORA + GATE_LORA
RWKV_IN = 3 * RWKV_WIDTH + LORA_IN
IN_WIDTH = FOURIER_WIDTH + ATTN_WIDTH + 2 * KV_WIDTH + RWKV_IN
D_FF = 5504
ROPE_THETA = 10000.0
NORM_EPS = 1e-6
GN_EPS = 64e-5

MOD_ROWS = 16
FFN_TILE = 512
D_FF_PAD = -(-D_FF // FFN_TILE) * FFN_TILE
HALO = 16
CHUNK = 64
MIB = 2 ** 20

F32 = jnp.float32
BF16 = jnp.bfloat16
HIGHEST = lax.Precision.HIGHEST


def _params(semantics, vmem_mib):
    return pltpu.CompilerParams(dimension_semantics=semantics, vmem_limit_bytes=vmem_mib * MIB)


def _resident(shape):
    return pl.BlockSpec(shape, lambda *_: (0,) * len(shape), pipeline_mode=pl.Buffered(1))


def _dot(a, b):
    return jnp.dot(a, b, preferred_element_type=F32)


def _dot_nt(a, b):
    return lax.dot_general(a, b, (((1,), (1,)), ((), ())), preferred_element_type=F32)


def _dot_tn(a, b):
    return lax.dot_general(a, b, (((0,), (0,)), ((), ())), preferred_element_type=F32)


def _modulated_rmsnorm(x, g, scale, shift):
    ms = jnp.mean(x * x, axis=-1, keepdims=True)
    return (x * lax.rsqrt(ms + NORM_EPS) * g) * (1.0 + scale) + shift


def _mod_body(c_ref, w_ref, b_ref, o_ref):
    c = c_ref[...]
    s = (c * jax.nn.sigmoid(c)).astype(BF16)
    o_ref[...] = _dot(s, w_ref[...].astype(BF16)) + b_ref[...]


def _adaln_mod(cvec, w_ada, b_ada):
    depth, d, n = w_ada.shape
    tn = 1024
    return pl.pallas_call(
        _mod_body,
        grid=(depth, n // tn),
        in_specs=[pl.BlockSpec((MOD_ROWS, d), lambda l, j: (0, 0)),
                  pl.BlockSpec((None, d, tn), lambda l, j: (l, 0, j)),
                  pl.BlockSpec((None, 1, tn), lambda l, j: (l, 0, j))],
        out_specs=pl.BlockSpec((None, MOD_ROWS, tn), lambda l, j: (l, 0, j)),
        out_shape=jax.ShapeDtypeStruct((depth, MOD_ROWS, n), F32),
        compiler_params=_params(("parallel", "parallel"), 40),
        name="adaln_mod",
    )(cvec, w_ada, b_ada.reshape(depth, 1, n))


_IN_SPLITS = (("f", 0, FOURIER_WIDTH, BF16),
              ("q", FOURIER_WIDTH, ATTN_WIDTH, F32),
              ("k", FOURIER_WIDTH + ATTN_WIDTH, KV_WIDTH, F32),
              ("v", FOURIER_WIDTH + ATTN_WIDTH + KV_WIDTH, KV_WIDTH, F32),
              ("rw", FOURIER_WIDTH + ATTN_WIDTH + 2 * KV_WIDTH, RWKV_IN, F32))


def _inproj_body(x_ref, g_ref, sc_ref, sh_ref, w_ref, *o_refs):
    h = _modulated_rmsnorm(x_ref[...], g_ref[...], sc_ref[...], sh_ref[...]).astype(BF16)
    for o_ref, (_, start, width, _) in zip(o_refs, _IN_SPLITS):
        o_ref[...] = _dot(h, w_ref[:, start:start + width]).astype(o_ref.dtype)


def _in_proj(x, g, scale, shift, w, seg_rows):
    rows, d = x.shape
    tm = 512 if rows % 512 == 0 else 256
    seg = lambda i: (i * tm // seg_rows, 0, 0)
    return pl.pallas_call(
        _inproj_body,
        grid=(rows // tm,),
        in_specs=[pl.BlockSpec((tm, d), lambda i: (i, 0)),
                  _resident((1, d)),
                  pl.BlockSpec((None, 1, d), seg),
                  pl.BlockSpec((None, 1, d), seg),
                  _resident(w.shape)],
        out_specs=[pl.BlockSpec((tm, width), lambda i: (i, 0)) for _, _, width, _ in _IN_SPLITS],
        out_shape=[jax.ShapeDtypeStruct((rows, width), dt) for _, _, width, dt in _IN_SPLITS],
        compiler_params=_params(("parallel",), 56),
        name="in_proj",
    )(x, g, scale, shift, w)


def _dft_tables(t):
    def angles(n):
        i = jnp.arange(n, dtype=jnp.int32)
        return (2.0 * math.pi / n) * ((i[:, None] * i[None, :]) % n).astype(F32)
    at = angles(t)
    time_tab = jnp.concatenate([jnp.cos(at), -jnp.sin(at)], axis=1).astype(BF16)
    ac = angles(FOURIER_GROUP_WIDTH)
    eye = jnp.eye(FOURIER_GROUPS, dtype=F32)
    chan_tab = jnp.concatenate([jnp.kron(eye, jnp.cos(ac)), jnp.kron(eye, jnp.sin(ac))], axis=1).astype(BF16)
    return time_tab, chan_tab


def _fourier_body(u_ref, ct_ref, cc_ref, o_ref, ab_scr, *, t, norm):
    @pl.when(pl.program_id(1) == 0)
    def _():
        ab = _dot(u_ref[...], cc_ref[...])
        ab_scr[0:t, :] = ab[:, :FOURIER_WIDTH].astype(BF16)
        ab_scr[t:2 * t, :] = ab[:, FOURIER_WIDTH:].astype(BF16)
    o_ref[...] = (_dot(ct_ref[...], ab_scr[...]) * norm).astype(o_ref.dtype)


def _fourier_mix(u, time_tab, chan_tab, row0, nseq, t):
    tm = min(t, 512)
    nt = t // tm
    seq0 = row0 // t
    return pl.pallas_call(
        functools.partial(_fourier_body, t=t, norm=1.0 / math.sqrt(t * FOURIER_GROUP_WIDTH)),
        grid=(nseq, nt),
        in_specs=[pl.BlockSpec((t, FOURIER_WIDTH), lambda b, i: (seq0 + b, 0)),
                  pl.BlockSpec((tm, 2 * t), lambda b, i: (i, 0)),
                  _resident(chan_tab.shape)],
        out_specs=pl.BlockSpec((tm, FOURIER_WIDTH), lambda b, i: (b * nt + i, 0)),
        out_shape=jax.ShapeDtypeStruct((nseq * t, FOURIER_WIDTH), BF16),
        scratch_shapes=[pltpu.VMEM((2 * t, FOURIER_WIDTH), BF16)],
        compiler_params=_params(("parallel", "arbitrary"), 40),
        name=f"fourier_mix_t{t}",
    )(u, time_tab, chan_tab)


def _rope_tables(t):
    pos = jnp.arange(t, dtype=jnp.int32)
    rows = (pos // GRID_W).astype(F32)
    cols = (pos % GRID_W).astype(F32)
    half = HEAD_DIM // 2
    inv = 1.0 / (ROPE_THETA ** (jnp.arange(0, half, 2, dtype=F32) / half))
    def tab(p):
        ang = p[:, None] * inv[None, :]
        return (jnp.concatenate([jnp.cos(ang), jnp.cos(ang)], -1),
                jnp.concatenate([-jnp.sin(ang), jnp.sin(ang)], -1))
    cr, sr = tab(rows)
    cc, sc = tab(cols)
    return jnp.concatenate([cr, cc], -1), jnp.concatenate([sr, sc], -1)


def _qkv_body(*refs, rope):
    if rope:
        q_ref, k_ref, v_ref, gq_ref, gk_ref, cos_ref, sin_ref, qo_ref, ko_ref, vo_ref = refs
        cos, sin = cos_ref[...], sin_ref[...]
        lane = lax.broadcasted_iota(jnp.int32, (1, HEAD_DIM), 1)
        low = (lane % (HEAD_DIM // 2)) < (HEAD_DIM // 4)
    else:
        q_ref, k_ref, v_ref, gq_ref, gk_ref, qo_ref, ko_ref, vo_ref, kn_ref = refs

    def head_norm(xh, g):
        ms = jnp.mean(xh * xh, axis=-1, keepdims=True)
        return xh * lax.rsqrt(ms + NORM_EPS) * g

    def rotate(xh):
        partner = jnp.where(low, pltpu.roll(xh, HEAD_DIM - HEAD_DIM // 4, 1), pltpu.roll(xh, HEAD_DIM // 4, 1))
        return xh * cos + partner * sin

    scale = HEAD_DIM ** -0.5
    for h in range(ATTN_HEADS):
        sl = slice(h * HEAD_DIM, (h + 1) * HEAD_DIM)
        qh = head_norm(q_ref[:, sl], gq_ref[...])
        if rope:
            qh = rotate(qh)
        qo_ref[:, sl] = (qh * scale).astype(BF16)
    for j in range(ATTN_KV_HEADS):
        sl = slice(j * HEAD_DIM, (j + 1) * HEAD_DIM)
        kh = head_norm(k_ref[:, sl], gk_ref[...])
        if rope:
            ko_ref[:, sl] = rotate(kh).astype(BF16)
        else:
            kn_ref[:, sl] = kh
            ko_ref[:, sl] = kh.astype(BF16)
    vo_ref[...] = v_ref[...].astype(BF16)


def _qkv_prep(q, k, v, gq, gk, row0, nrows, rope_tabs=None):
    tt = 256
    b0 = row0 // tt
    rope = rope_tabs is not None
    row = lambda w: pl.BlockSpec((tt, w), lambda i: (b0 + i, 0))
    out = lambda w: pl.BlockSpec((tt, w), lambda i: (i, 0))
    in_specs = [row(ATTN_WIDTH), row(KV_WIDTH), row(KV_WIDTH), _resident((1, HEAD_DIM)), _resident((1, HEAD_DIM))]
    args = [q, k, v, gq, gk]
    out_specs = [out(ATTN_WIDTH), out(KV_WIDTH), out(KV_WIDTH)]
    out_shape = [jax.ShapeDtypeStruct((nrows, ATTN_WIDTH), BF16),
                 jax.ShapeDtypeStruct((nrows, KV_WIDTH), BF16),
                 jax.ShapeDtypeStruct((nrows, KV_WIDTH), BF16)]
    if rope:
        t = rope_tabs[0].shape[0]
        nt = t // tt
        tab = pl.BlockSpec((tt, HEAD_DIM), lambda i: (i % nt, 0))
        in_specs += [tab, tab]
        args += list(rope_tabs)
    else:
        out_specs.append(out(KV_WIDTH))
        out_shape.append(jax.ShapeDtypeStruct((nrows, KV_WIDTH), F32))
    return pl.pallas_call(
        functools.partial(_qkv_body, rope=rope),
        grid=(nrows // tt,),
        in_specs=in_specs, out_specs=out_specs, out_shape=out_shape,
        compiler_params=_params(("parallel",), 32),
        name="qkv_prep_rope" if rope else "qkv_prep",
    )(*args)


def _attn_body(*refs, cached):
    if cached:
        q_ref, k_ref, v_ref, kc_ref, vc_ref, o_ref = refs
    else:
        q_ref, k_ref, v_ref, o_ref = refs
    for j in range(ATTN_KV_HEADS):
        kv = slice(j * HEAD_DIM, (j + 1) * HEAD_DIM)
        kj, vj = k_ref[:, kv], v_ref[:, kv]
        if cached:
            kc, vc = kc_ref[:, kv].astype(BF16), vc_ref[:, kv].astype(BF16)
        for g in range(KV_GROUP):
            sl = slice((j * KV_GROUP + g) * HEAD_DIM, (j * KV_GROUP + g + 1) * HEAD_DIM)
            qh = q_ref[:, sl]
            s = _dot_nt(qh, kj)
            m = jnp.max(s, axis=-1, keepdims=True)
            if cached:
                sc = _dot_nt(qh, kc)
                m = jnp.maximum(m, jnp.max(sc, axis=-1, keepdims=True))
            p = jnp.exp(s - m)
            l = jnp.sum(p, axis=-1, keepdims=True)
            acc = _dot(p.astype(BF16), vj)
            if cached:
                pc = jnp.exp(sc - m)
                l = l + jnp.sum(pc, axis=-1, keepdims=True)
                acc = acc + _dot(pc.astype(BF16), vc)
            o_ref[:, sl] = (acc / l).astype(o_ref.dtype)


def _attention(q, k, v, nseq, t, cache_k=None, cache_v=None):
    tq = 256
    nq = t // tq
    cached = cache_k is not None
    in_specs = [pl.BlockSpec((tq, ATTN_WIDTH), lambda b, i: (b * nq + i, 0)),
                pl.BlockSpec((t, KV_WIDTH), lambda b, i: (b, 0)),
                pl.BlockSpec((t, KV_WIDTH), lambda b, i: (b, 0))]
    args = [q, k, v]
    if cached:
        past = cache_k.shape[1]
        in_specs += [pl.BlockSpec((None, past, KV_WIDTH), lambda b, i: (b, 0, 0))] * 2
        args += [cache_k, cache_v]
    return pl.pallas_call(
        functools.partial(_attn_body, cached=cached),
        grid=(nseq, nq),
        in_specs=in_specs,
        out_specs=pl.BlockSpec((tq, ATTN_WIDTH), lambda b, i: (b * nq + i, 0)),
        out_shape=jax.ShapeDtypeStruct((nseq * t, ATTN_WIDTH), BF16),
        compiler_params=_params(("parallel", "parallel"), 48),
        name="attention_cached" if cached else "attention",
    )(*args)


def _seq_position(row_start, n, rows_lat, t_lat, t_ctx):
    rows = row_start + lax.broadcasted_iota(jnp.int32, (n, 1), 0)
    length = jnp.where(rows < rows_lat, t_lat, t_ctx)
    return lax.rem(rows, length), length


def _conv3(ext_ref, w_ref, tm, pos, length):
    not_first = (pos != 0).astype(F32)
    not_last = (pos != length - 1).astype(F32)
    return (w_ref[0:1, :] * (ext_ref[HALO - 1:HALO - 1 + tm, :] * not_first)
            + w_ref[1:2, :] * ext_ref[HALO:HALO + tm, :]
            + w_ref[2:3, :] * (ext_ref[HALO + 1:HALO + 1 + tm, :] * not_last))


def _head_sum_matrix():
    i = lax.broadcasted_iota(jnp.int32, (RWKV_WIDTH, RWKV_WIDTH), 0) // RWKV_N
    j = lax.broadcasted_iota(jnp.int32, (RWKV_WIDTH, RWKV_WIDTH), 1) // RWKV_N
    return (i == j).astype(F32)


def _head_sums(x, ones_bd):
    return jnp.dot(x, ones_bd, preferred_element_type=F32, precision=HIGHEST)


def _rwkv_prep_body(rw_ref, prev_ref, next_ref, cw_ref, wl_ref, w0_ref, a0_ref, kks_ref, ka_ref, rk_ref,
                    r_ref, v_ref, kk_ref, g_ref, bonus_ref, lw_ref, km_ref, b_ref, ext_scr,
                    *, tm, rows_lat, t_lat, t_ctx):
    ext_scr[0:HALO, :] = prev_ref[...]
    ext_scr[HALO:HALO + tm, :] = rw_ref[...]
    ext_scr[HALO + tm:, :] = next_ref[...]
    pos, length = _seq_position(pl.program_id(0) * tm, tm, rows_lat, t_lat, t_ctx)
    z = _conv3(ext_scr, cw_ref, tm, pos, length)
    c = RWKV_WIDTH
    r, k, v = z[:, :c], z[:, c:2 * c], z[:, 2 * c:3 * c]
    zl = z[:, 3 * c:]
    lane = lax.broadcasted_iota(jnp.int32, (1, LORA_IN), 1)
    lora_in = jnp.where(lane < DECAY_LORA, jnp.tanh(zl),
                        jnp.where(lane < DECAY_LORA + ICLR_LORA, zl, jax.nn.sigmoid(zl)))
    lora = jnp.dot(lora_in, wl_ref[...], preferred_element_type=F32, precision=HIGHEST)
    ones_bd = _head_sum_matrix()
    kk = k * kks_ref[...]
    kk = kk * lax.rsqrt(_head_sums(kk * kk, ones_bd) + 1e-12)
    r_ref[...] = r
    v_ref[...] = v
    kk_ref[...] = kk
    g_ref[...] = lora[:, 4 * c:5 * c]
    kmod_sum = jnp.zeros_like(k)
    for d in range(2):
        wpre = w0_ref[d:d + 1, :] + lora[:, d * c:(d + 1) * c]
        w = -jax.nn.softplus(-wpre) - 0.5
        lw_ref[d] = -jnp.exp(w)
        a = jax.nn.sigmoid(a0_ref[d:d + 1, :] + lora[:, (2 + d) * c:(3 + d) * c])
        kmod = k * (1.0 + (a - 1.0) * ka_ref[...])
        km_ref[d] = kmod
        b_ref[d] = kk * a
        kmod_sum = kmod_sum + kmod
    bonus_ref[...] = _head_sums(r * kmod_sum * rk_ref[...], ones_bd) * v


def _rwkv_prep(rw, conv_w, lora_w, w0, a0, kk_scale, ka, rk, rows_lat, t_lat, t_ctx):
    rows = rw.shape[0]
    tm = 256
    nh = tm // HALO
    last = rows // HALO - 1
    c = RWKV_WIDTH
    one = lambda: pl.BlockSpec((tm, c), lambda i: (i, 0))
    two = lambda: pl.BlockSpec((2, tm, c), lambda i: (0, i, 0))
    return pl.pallas_call(
        functools.partial(_rwkv_prep_body, tm=tm, rows_lat=rows_lat, t_lat=t_lat, t_ctx=t_ctx),
        grid=(rows // tm,),
        in_specs=[pl.BlockSpec((tm, RWKV_IN), lambda i: (i, 0)),
                  pl.BlockSpec((HALO, RWKV_IN), lambda i: (jnp.maximum(i * nh - 1, 0), 0)),
                  pl.BlockSpec((HALO, RWKV_IN), lambda i: (jnp.minimum((i + 1) * nh, last), 0)),
                  _resident(conv_w.shape), _resident(lora_w.shape), _resident(w0.shape), _resident(a0.shape),
                  _resident(kk_scale.shape), _resident(ka.shape), _resident(rk.shape)],
        out_specs=[one(), one(), one(), one(), one(), two(), two(), two()],
        out_shape=[jax.ShapeDtypeStruct((rows, c), F32)] * 5 + [jax.ShapeDtypeStruct((2, rows, c), F32)] * 3,
        scratch_shapes=[pltpu.VMEM((tm + 2 * HALO, RWKV_IN), F32)],
        compiler_params=_params(("parallel",), 48),
        name="rwkv_prep",
    )(rw, rw, rw, conv_w, lora_w, w0, a0, kk_scale, ka, rk)


def _scan_direction_operands(r_ref, v_ref, kk_ref, lw_ref, km_ref, b_ref, backward):
    n = CHUNK
    row = lax.broadcasted_iota(jnp.int32, (n, n), 0)
    col = lax.broadcasted_iota(jnp.int32, (n, n), 1)
    upto = (col >= row) if backward else (col <= row)
    lw = lw_ref[...]
    cs = jnp.dot(upto.astype(F32), lw, preferred_element_type=F32, precision=HIGHEST)
    tot = jnp.sum(lw, axis=0, keepdims=True)
    grow = jnp.exp(-cs)
    to_end = jnp.exp(tot - cs)
    b = b_ref[...]
    km = km_ref[...]
    return dict(
        kkt=(kk_ref[...] * jnp.exp(cs - lw)).astype(BF16),
        rt=(r_ref[...] * jnp.exp(cs)).astype(BF16),
        bt=(b * grow).astype(BF16), kt=(km * grow).astype(BF16),
        bh=(b * to_end).astype(BF16), kh=(km * to_end).astype(BF16),
        vb=v_ref[...].astype(BF16), w_all=jnp.exp(tot))


def _rwkv_scan_body(rf_ref, vf_ref, kkf_ref, lwf_ref, kmf_ref, bf_ref,
                    rb_ref, vb_ref, kkb_ref, lwb_ref, kmb_ref, bb_ref, s0_ref,
                    yf_ref, yb_ref, sfin_ref, s_scr):
    c = pl.program_id(1)

    @pl.when(c == 0)
    def _():
        s_scr[...] = s0_ref[...]

    n = CHUNK
    row = lax.broadcasted_iota(jnp.int32, (2 * n, 2 * n), 0)
    col = lax.broadcasted_iota(jnp.int32, (2 * n, 2 * n), 1)
    t_idx, s_idx, read_rows = row % n, col % n, row >= n
    eye = (lax.broadcasted_iota(jnp.int32, (n, n), 0) == lax.broadcasted_iota(jnp.int32, (n, n), 1)).astype(F32)
    ops = (_scan_direction_operands(rf_ref, vf_ref, kkf_ref, lwf_ref, kmf_ref, bf_ref, False),
           _scan_direction_operands(rb_ref, vb_ref, kkb_ref, lwb_ref, kmb_ref, bb_ref, True))
    same_step = jnp.logical_and(read_rows, s_idx == t_idx)
    masks = (jnp.logical_or(s_idx < t_idx, same_step), jnp.logical_or(s_idx > t_idx, same_step))
    y_refs = (yf_ref, yb_ref)
    chains = [(d, h) for d in range(2) for h in range(RWKV_HEADS)]
    head = lambda d, name, h: ops[d][name][:, h * RWKV_N:(h + 1) * RWKV_N]
    cat = lambda a, b: jnp.concatenate([a, b], axis=0)

    lhs = [cat(head(d, "kkt", h), head(d, "rt", h)) for d, h in chains]
    coef = [jnp.where(masks[d], _dot_nt(l, cat(head(d, "bt", h), head(d, "kt", h))), 0.0)
            for l, (d, h) in zip(lhs, chains)]
    n_mat = [a[:n, :n] for a in coef]
    state = [s_scr[d, h] for d, h in chains]
    state_b = [s.astype(BF16) for s in state]
    read = [_dot_nt(l, sb) for l, sb in zip(lhs, state_b)]
    akv = [_dot(a[:n, n:].astype(BF16), head(d, "vb", h)) for a, (d, h) in zip(coef, chains)]

    x = [eye - m for m in n_mat]
    p = [_dot(m.astype(BF16), m.astype(BF16)) for m in n_mat]
    steps = int(math.log2(n)) - 1
    for s in range(steps):
        pb = [q.astype(BF16) for q in p]
        x = [xi + _dot(xi.astype(BF16), q) for xi, q in zip(x, pb)]
        if s + 1 < steps:
            p = [_dot(q, q) for q in pb]

    u = [_dot(xi.astype(BF16), (-(rd[:n] + ak)).astype(BF16)) for xi, rd, ak in zip(x, read, akv)]
    uv = [cat(ui.astype(BF16), head(d, "vb", h)) for ui, (d, h) in zip(u, chains)]
    for (d, h), a, rd, uvi, s in zip(chains, coef, read, uv, state):
        sl = slice(h * RWKV_N, (h + 1) * RWKV_N)
        y_refs[d][:, sl] = rd[n:] + _dot(a[n:].astype(BF16), uvi)
        s_scr[d, h] = s * ops[d]["w_all"][:, sl] + _dot_tn(uvi, cat(head(d, "bh", h), head(d, "kh", h)))

    @pl.when(c == pl.num_programs(1) - 1)
    def _():
        sfin_ref[...] = s_scr[...]


def _rwkv_scan(r, v, kk, lw, km, b, s0, row0, nseq, t):
    n = CHUNK
    nc = t // n
    c0 = row0 // n
    c = RWKV_WIDTH
    fwd = lambda bb, cc: c0 + bb * nc + cc
    bwd = lambda bb, cc: c0 + bb * nc + nc - 1 - cc
    one = lambda chunk: pl.BlockSpec((n, c), lambda bb, cc: (chunk(bb, cc), 0))
    two = lambda chunk, d: pl.BlockSpec((None, n, c), lambda bb, cc: (d, chunk(bb, cc), 0))
    state = pl.BlockSpec((None, 2, RWKV_HEADS, RWKV_N, RWKV_N), lambda bb, cc: (bb, 0, 0, 0, 0))
    y_shape = jax.ShapeDtypeStruct((nseq * t, c), F32)
    return pl.pallas_call(
        _rwkv_scan_body,
        grid=(nseq, nc),
        in_specs=[one(fwd), one(fwd), one(fwd), two(fwd, 0), two(fwd, 0), two(fwd, 0),
                  one(bwd), one(bwd), one(bwd), two(bwd, 1), two(bwd, 1), two(bwd, 1), state],
        out_specs=[pl.BlockSpec((n, c), lambda bb, cc: (bb * nc + cc, 0)),
                   pl.BlockSpec((n, c), lambda bb, cc: (bb * nc + nc - 1 - cc, 0)), state],
        out_shape=[y_shape, y_shape, jax.ShapeDtypeStruct((nseq, 2, RWKV_HEADS, RWKV_N, RWKV_N), F32)],
        scratch_shapes=[pltpu.VMEM((2, RWKV_HEADS, RWKV_N, RWKV_N), F32)],
        compiler_params=_params(("parallel", "arbitrary"), 32),
        name=f"rwkv_scan_t{t}",
    )(r, v, kk, lw, km, b, r, v, kk, lw, km, b, s0)


def _rwkv_out_body(yf_ref, yb_ref, bonus_ref, g_ref, lg_ref, lb_ref, o_ref):
    ones_bd = _head_sum_matrix()
    y = yf_ref[...] + yb_ref[...]
    mu = _head_sums(y, ones_bd) * (1.0 / RWKV_N)
    yc = y - mu
    var = _head_sums(yc * yc, ones_bd) * (1.0 / RWKV_N)
    yn = yc * lax.rsqrt(var + GN_EPS) * lg_ref[...] + lb_ref[...]
    o_ref[...] = ((yn + bonus_ref[...]) * g_ref[...]).astype(o_ref.dtype)


def _rwkv_out(yf, yb, bonus, g, row0, ln_g, ln_b):
    nrows = yf.shape[0]
    tm = 256
    b0 = row0 // tm
    c = RWKV_WIDTH
    src = pl.BlockSpec((tm, c), lambda i: (b0 + i, 0))
    return pl.pallas_call(
        _rwkv_out_body,
        grid=(nrows // tm,),
        in_specs=[pl.BlockSpec((tm, c), lambda i: (i, 0))] * 2 + [src, src, _resident((1, c)), _resident((1, c))],
        out_specs=pl.BlockSpec((tm, c), lambda i: (i, 0)),
        out_shape=jax.ShapeDtypeStruct((nrows, c), BF16),
        compiler_params=_params(("parallel",), 32),
        name="rwkv_out",
    )(yf, yb, bonus, g, ln_g, ln_b)


def _outproj_body(f_ref, a_ref, r_ref, x_ref, gate_ref, w_ref, o_ref):
    f0, a0, r0 = 0, FOURIER_WIDTH, FOURIER_WIDTH + ATTN_WIDTH
    mix = (_dot(f_ref[...], w_ref[f0:a0, :]) + _dot(a_ref[...], w_ref[a0:r0, :])
           + _dot(r_ref[...], w_ref[r0:, :]))
    o_ref[...] = x_ref[...] + gate_ref[...] * mix


def _out_proj(f, a, r, x, gate, w, seg_rows):
    rows, d = x.shape
    tm = 512 if rows % 512 == 0 else 256
    row = lambda arr: pl.BlockSpec((tm, arr.shape[1]), lambda i: (i, 0))
    return pl.pallas_call(
        _outproj_body,
        grid=(rows // tm,),
        in_specs=[row(f), row(a), row(r), row(x),
                  pl.BlockSpec((None, 1, d), lambda i: (i * tm // seg_rows, 0, 0)),
                  _resident(w.shape)],
        out_specs=row(x),
        out_shape=jax.ShapeDtypeStruct(x.shape, F32),
        compiler_params=_params(("parallel",), 48),
        name="out_proj",
    )(f, a, r, x, gate, w)


def _ffn_body(x_ref, prev_ref, next_ref, g_ref, sc_ref, sh_ref, gate_ref, wa_ref, wg_ref,
              cwa_ref, cwg_ref, ba_ref, bg_ref, wd_ref, o_ref, h_scr, ua_scr, ug_scr, acc_scr,
              *, tm, rows_lat, t_lat, t_ctx):
    j = pl.program_id(1)

    @pl.when(j == 0)
    def _():
        norm = lambda x: _modulated_rmsnorm(x, g_ref[...], sc_ref[...], sh_ref[...]).astype(BF16)
        h_scr[0:HALO, :] = norm(prev_ref[...])
        h_scr[HALO:HALO + tm, :] = norm(x_ref[...])
        h_scr[HALO + tm:, :] = norm(next_ref[...])

    h = h_scr[...]
    ua_scr[...] = _dot(h, wa_ref[...])
    ug_scr[...] = _dot(h, wg_ref[...])
    pos, length = _seq_position(pl.program_id(0) * tm, tm, rows_lat, t_lat, t_ctx)
    ua = _conv3(ua_scr, cwa_ref, tm, pos, length) + ba_ref[...]
    ug = _conv3(ug_scr, cwg_ref, tm, pos, length) + bg_ref[...]
    act = (ug * jax.nn.sigmoid(ug) * ua).astype(BF16)
    part = _dot(act, wd_ref[...])

    @pl.when(j == 0)
    def _():
        acc_scr[...] = part

    @pl.when(j > 0)
    def _():
        acc_scr[...] += part

    @pl.when(j == pl.num_programs(1) - 1)
    def _():
        o_ref[...] = x_ref[...] + gate_ref[...] * acc_scr[...]


def _conv_ffn(x, g, scale, shift, gate, wa, wg, cwa, cwg, ba, bg, wd, seg_rows, rows_lat, t_lat, t_ctx):
    rows, d = x.shape
    tm = 512 if rows % 512 == 0 else 256
    tf = FFN_TILE
    nh = tm // HALO
    last = rows // HALO - 1
    seg = lambda i, j: (i * tm // seg_rows, 0, 0)
    col = lambda r: pl.BlockSpec((r, tf), lambda i, j: (0, j))
    return pl.pallas_call(
        functools.partial(_ffn_body, tm=tm, rows_lat=rows_lat, t_lat=t_lat, t_ctx=t_ctx),
        grid=(rows // tm, D_FF_PAD // tf),
        in_specs=[pl.BlockSpec((tm, d), lambda i, j: (i, 0)),
                  pl.BlockSpec((HALO, d), lambda i, j: (jnp.maximum(i * nh - 1, 0), 0)),
                  pl.BlockSpec((HALO, d), lambda i, j: (jnp.minimum((i + 1) * nh, last), 0)),
                  pl.BlockSpec((1, d), lambda i, j: (0, 0)),
                  pl.BlockSpec((None, 1, d), seg), pl.BlockSpec((None, 1, d), seg), pl.BlockSpec((None, 1, d), seg),
                  col(d), col(d), col(3), col(3), col(1), col(1),
                  pl.BlockSpec((tf, d), lambda i, j: (j, 0))],
        out_specs=pl.BlockSpec((tm, d), lambda i, j: (i, 0)),
        out_shape=jax.ShapeDtypeStruct(x.shape, F32),
        scratch_shapes=[pltpu.VMEM((tm + 2 * HALO, d), BF16),
                        pltpu.VMEM((tm + 2 * HALO, tf), F32),
                        pltpu.VMEM((tm + 2 * HALO, tf), F32),
                        pltpu.VMEM((tm, d), F32)],
        compiler_params=_params(("parallel", "arbitrary"), 56),
        name="conv_ffn",
    )(x, x, x, g, scale, shift, gate, wa, wg, cwa, cwg, ba, bg, wd)


def _final_norm_body(x_ref, g_ref, o_ref):
    x = x_ref[...]
    ms = jnp.mean(x * x, axis=-1, keepdims=True)
    o_ref[...] = x * lax.rsqrt(ms + NORM_EPS) * g_ref[...]


def _final_norm(x, g, row0, nrows):
    tm = 256
    b0 = row0 // tm
    d = x.shape[1]
    return pl.pallas_call(
        _final_norm_body,
        grid=(nrows // tm,),
        in_specs=[pl.BlockSpec((tm, d), lambda i: (b0 + i, 0)), _resident((1, d))],
        out_specs=pl.BlockSpec((tm, d), lambda i: (i, 0)),
        out_shape=jax.ShapeDtypeStruct((nrows, d), F32),
        compiler_params=_params(("parallel",), 32),
        name="final_norm",
    )(x, g)


def _lora_weight(w2, a2, g2):
    c = RWKV_WIDTH
    wl = jnp.zeros((LORA_IN, 5 * c), F32)
    wl = wl.at[:DECAY_LORA, 0:c].set(w2[0]).at[:DECAY_LORA, c:2 * c].set(w2[1])
    wl = wl.at[DECAY_LORA:DECAY_LORA + ICLR_LORA, 2 * c:3 * c].set(a2[0])
    wl = wl.at[DECAY_LORA:DECAY_LORA + ICLR_LORA, 3 * c:4 * c].set(a2[1])
    return wl.at[DECAY_LORA + ICLR_LORA:, 4 * c:].set(g2)


def _pad_cols(w):
    return jnp.pad(w, ((0, 0), (0, D_FF_PAD - D_FF)))


def _forward(x_lat, x_ctx, cache_k, cache_v, state, c, c_ctx, w_ada, b_ada, norm1_g, norm2_g, w_in, w_out,
             q_norm_g, k_norm_g, rw_conv, rw_w0, rw_w2, rw_a0, rw_a2, rw_g2, rw_kk, rw_ka, rw_rk,
             rw_lnx_g, rw_lnx_b, ffn_up, ffn_conv_w, ffn_conv_b, ffn_down, final_norm_g):
    n_lat, t_lat, d = x_lat.shape
    n_ctx, t_ctx, _ = x_ctx.shape
    depth = w_ada.shape[0]
    past = cache_k.shape[2]
    rows_lat, rows_ctx = n_lat * t_lat, n_ctx * t_ctx
    assert rows_ctx % t_lat == 0 and n_lat + rows_ctx // t_lat <= MOD_ROWS
    assert t_lat % 256 == 0 and t_ctx % 128 == 0 and t_lat % GRID_W == 0

    x = jnp.concatenate([x_lat.reshape(rows_lat, d), x_ctx.reshape(rows_ctx, d)], axis=0)
    cvec = jnp.concatenate([c, jnp.broadcast_to(c_ctx[None, :], (MOD_ROWS - n_lat, d))], axis=0)
    mod = _adaln_mod(cvec, w_ada, b_ada).reshape(depth, MOD_ROWS, 6, 1, d)

    rope_tabs = _rope_tables(t_lat)
    time_lat, chan_tab = _dft_tables(t_lat)
    time_ctx, _ = _dft_tables(t_ctx)
    zero_state = jnp.zeros((n_ctx, 2, RWKV_HEADS, RWKV_N, RWKV_N), F32)
    row2 = lambda a: a.reshape(1, -1)

    new_k, new_v, new_s = [], [], []
    for l in range(depth):
        shift1, scale1, gate1, shift2, scale2, gate2 = (mod[l, :, i] for i in range(6))
        f, q, k, v, rw = _in_proj(x, row2(norm1_g[l]), scale1, shift1, w_in[l].astype(BF16), t_lat)

        f_lat = _fourier_mix(f, time_lat, chan_tab, 0, n_lat, t_lat)
        f_ctx = _fourier_mix(f, time_ctx, chan_tab, rows_lat, n_ctx, t_ctx)

        gq, gk = row2(q_norm_g[l]), row2(k_norm_g[l])
        q_lat, k_lat, v_lat = _qkv_prep(q, k, v, gq, gk, 0, rows_lat, rope_tabs)
        q_c, k_c, v_c, k_norm = _qkv_prep(q, k, v, gq, gk, rows_lat, rows_ctx)
        a_lat = _attention(q_lat, k_lat, v_lat, n_lat, t_lat,
                           cache_k[:, l].reshape(n_lat, past, KV_WIDTH), cache_v[:, l].reshape(n_lat, past, KV_WIDTH))
        a_ctx = _attention(q_c, k_c, v_c, n_ctx, t_ctx)
        new_k.append(k_norm.reshape(n_ctx, t_ctx, ATTN_KV_HEADS, HEAD_DIM))
        new_v.append(v[rows_lat:].reshape(n_ctx, t_ctx, ATTN_KV_HEADS, HEAD_DIM))

        r_, v_, kk, g_, bonus, lw, km, b_ = _rwkv_prep(
            rw, rw_conv[l], _lora_weight(rw_w2[l], rw_a2[l], rw_g2[l]), rw_w0[l], rw_a0[l],
            row2(rw_kk[l]), row2(rw_ka[l]), row2(rw_rk[l]), rows_lat, t_lat, t_ctx)
        yf_lat, yb_lat, _ = _rwkv_scan(r_, v_, kk, lw, km, b_, state[:, l], 0, n_lat, t_lat)
        yf_ctx, yb_ctx, s_ctx = _rwkv_scan(r_, v_, kk, lw, km, b_, zero_state, rows_lat, n_ctx, t_ctx)
        ln_g, ln_b = row2(rw_lnx_g[l]), row2(rw_lnx_b[l])
        r_lat = _rwkv_out(yf_lat, yb_lat, bonus, g_, 0, ln_g, ln_b)
        r_ctx = _rwkv_out(yf_ctx, yb_ctx, bonus, g_, rows_lat, ln_g, ln_b)
        new_s.append(s_ctx)

        cat = lambda a, b: jnp.concatenate([a, b], axis=0)
        x = _out_proj(cat(f_lat, f_ctx), cat(a_lat, a_ctx), cat(r_lat, r_ctx), x, gate1,
                      w_out[l].astype(BF16), t_lat)

        up = ffn_up[l]
        x = _conv_ffn(x, row2(norm2_g[l]), scale2, shift2, gate2,
                      _pad_cols(up[:, :D_FF]).astype(BF16), _pad_cols(up[:, D_FF:]).astype(BF16),
                      _pad_cols(ffn_conv_w[l][:, :D_FF]), _pad_cols(ffn_conv_w[l][:, D_FF:]),
                      _pad_cols(row2(ffn_conv_b[l][:D_FF])), _pad_cols(row2(ffn_conv_b[l][D_FF:])),
                      jnp.pad(ffn_down[l], ((0, D_FF_PAD - D_FF), (0, 0))).astype(BF16),
                      t_lat, rows_lat, t_lat, t_ctx)

    fg = row2(final_norm_g)
    y_lat = _final_norm(x, fg, 0, rows_lat).reshape(n_lat, t_lat, d)
    y_ctx = _final_norm(x, fg, rows_lat, rows_ctx).reshape(n_ctx, t_ctx, d)
    return (y_ctx, y_lat, jnp.stack(new_k, axis=1), jnp.stack(new_v, axis=1), jnp.stack(new_s, axis=1))


def kernel(x_prompt, x_sample, cache_attn_k, cache_attn_v, state_rwkv, c, c_ctx, w_ada, b_ada, norm1_g, norm2_g, w_in, w_out, q_norm_g, k_norm_g, rw_conv, rw_w0, rw_w2, rw_a0, rw_a2, rw_g2, rw_kk, rw_ka, rw_rk, rw_lnx_g, rw_lnx_b, ffn_up, ffn_conv_w, ffn_conv_b, ffn_down, final_norm_g):
    return _forward(x_sample, x_prompt, cache_attn_k, cache_attn_v, state_rwkv, c, c_ctx, w_ada, b_ada,
                    norm1_g, norm2_g, w_in, w_out, q_norm_g, k_norm_g, rw_conv, rw_w0, rw_w2, rw_a0, rw_a2,
                    rw_g2, rw_kk, rw_ka, rw_rk, rw_lnx_g, rw_lnx_b, ffn_up, ffn_conv_w, ffn_conv_b, ffn_down,
                    final_norm_g)
```

```python
import functools
import math

import jax
import jax.numpy as jnp
from jax import lax
from jax.experimental import pallas as pl
from jax.experimental.pallas import tpu as pltpu

D_MODEL = 2048
GRID_W = 64
HEAD_DIM = 128
ATTN_HEADS = 8
ATTN_KV_HEADS = 2
KV_GROUP = ATTN_HEADS // ATTN_KV_HEADS
ATTN_WIDTH = ATTN_HEADS * HEAD_DIM
KV_WIDTH = ATTN_KV_HEADS * HEAD_DIM
FOURIER_WIDTH = 512
FOURIER_GROUPS = 4
FOURIER_GROUP_WIDTH = FOURIER_WIDTH // FOURIER_GROUPS
RWKV_WIDTH = 512
RWKV_N = 64
RWKV_HEADS = RWKV_WIDTH // RWKV_N
DECAY_LORA = 64
ICLR_LORA = 64
GATE_LORA = 128
LORA_IN = DECAY_LORA + ICLR_LORA + GATE_LORA
RWKV_IN = 3 * RWKV_WIDTH + LORA_IN
IN_WIDTH = FOURIER_WIDTH + ATTN_WIDTH + 2 * KV_WIDTH + RWKV_IN
D_FF = 5504
ROPE_THETA = 10000.0
NORM_EPS = 1e-6
GN_EPS = 64e-5

MOD_ROWS = 16
FFN_TILE = 512
FFN_SUB = 256
D_FF_PAD = -(-D_FF // FFN_TILE) * FFN_TILE
HALO = 16
CHUNK = 64
MIB = 2 ** 20

F32 = jnp.float32
BF16 = jnp.bfloat16
HIGHEST = lax.Precision.HIGHEST


def _params(semantics, vmem_mib):
    return pltpu.CompilerParams(dimension_semantics=semantics, vmem_limit_bytes=vmem_mib * MIB)


def _resident(shape):
    return pl.BlockSpec(shape, lambda *_: (0,) * len(shape), pipeline_mode=pl.Buffered(1))


def _dot(a, b):
    return jnp.dot(a, b, preferred_element_type=F32)


def _dot_nt(a, b):
    return lax.dot_general(a, b, (((1,), (1,)), ((), ())), preferred_element_type=F32)


def _dot_tn(a, b):
    return lax.dot_general(a, b, (((0,), (0,)), ((), ())), preferred_element_type=F32)


def _modulated_rmsnorm(x, g, scale, shift):
    ms = jnp.mean(x * x, axis=-1, keepdims=True)
    return (x * lax.rsqrt(ms + NORM_EPS) * g) * (1.0 + scale) + shift


def _mod_body(c_ref, w_ref, b_ref, o_ref):
    c = c_ref[...]
    s = (c * jax.nn.sigmoid(c)).astype(BF16)
    o_ref[...] = _dot(s, w_ref[...].astype(BF16)) + b_ref[...]


def _adaln_mod(cvec, w_ada, b_ada):
    depth, d, n = w_ada.shape
    tn = 1024
    return pl.pallas_call(
        _mod_body,
        grid=(depth, n // tn),
        in_specs=[pl.BlockSpec((MOD_ROWS, d), lambda l, j: (0, 0)),
                  pl.BlockSpec((None, d, tn), lambda l, j: (l, 0, j)),
                  pl.BlockSpec((None, 1, tn), lambda l, j: (l, 0, j))],
        out_specs=pl.BlockSpec((None, MOD_ROWS, tn), lambda l, j: (l, 0, j)),
        out_shape=jax.ShapeDtypeStruct((depth, MOD_ROWS, n), F32),
        compiler_params=_params(("parallel", "parallel"), 40),
        name="adaln_mod",
    )(cvec, w_ada, b_ada.reshape(depth, 1, n))


_IN_SPLITS = (("f", 0, FOURIER_WIDTH, BF16),
              ("q", FOURIER_WIDTH, ATTN_WIDTH, F32),
              ("k", FOURIER_WIDTH + ATTN_WIDTH, KV_WIDTH, F32),
              ("v", FOURIER_WIDTH + ATTN_WIDTH + KV_WIDTH, KV_WIDTH, F32),
              ("rw", FOURIER_WIDTH + ATTN_WIDTH + 2 * KV_WIDTH, RWKV_IN, F32))


def _inproj_body(x_ref, g_ref, sc_ref, sh_ref, w_ref, *o_refs):
    h = _modulated_rmsnorm(x_ref[...], g_ref[...], sc_ref[...], sh_ref[...]).astype(BF16)
    for o_ref, (_, start, width, _) in zip(o_refs, _IN_SPLITS):
        o_ref[...] = _dot(h, w_ref[:, start:start + width]).astype(o_ref.dtype)


def _in_proj(x, g, scale, shift, w, seg_rows):
    rows, d = x.shape
    tm = 512 if rows % 512 == 0 else 256
    seg = lambda i: (i * tm // seg_rows, 0, 0)
    return pl.pallas_call(
        _inproj_body,
        grid=(rows // tm,),
        in_specs=[pl.BlockSpec((tm, d), lambda i: (i, 0)),
                  _resident((1, d)),
                  pl.BlockSpec((None, 1, d), seg),
                  pl.BlockSpec((None, 1, d), seg),
                  _resident(w.shape)],
        out_specs=[pl.BlockSpec((tm, width), lambda i: (i, 0)) for _, _, width, _ in _IN_SPLITS],
        out_shape=[jax.ShapeDtypeStruct((rows, width), dt) for _, _, width, dt in _IN_SPLITS],
        compiler_params=_params(("parallel",), 56),
        name="in_proj",
    )(x, g, scale, shift, w)


def _dft_tables(t):
    def angles(n):
        i = jnp.arange(n, dtype=jnp.int32)
        return (2.0 * math.pi / n) * ((i[:, None] * i[None, :]) % n).astype(F32)
    at = angles(t)
    time_tab = jnp.concatenate([jnp.cos(at), -jnp.sin(at)], axis=1).astype(BF16)
    ac = angles(FOURIER_GROUP_WIDTH)
    eye = jnp.eye(FOURIER_GROUPS, dtype=F32)
    chan_tab = jnp.concatenate([jnp.kron(eye, jnp.cos(ac)), jnp.kron(eye, jnp.sin(ac))], axis=1).astype(BF16)
    return time_tab, chan_tab


def _fourier_body(u_ref, ct_ref, cc_ref, o_ref, ab_scr, *, t, norm):
    @pl.when(pl.program_id(1) == 0)
    def _():
        ab = _dot(u_ref[...], cc_ref[...])
        ab_scr[0:t, :] = ab[:, :FOURIER_WIDTH].astype(BF16)
        ab_scr[t:2 * t, :] = ab[:, FOURIER_WIDTH:].astype(BF16)
    o_ref[...] = (_dot(ct_ref[...], ab_scr[...]) * norm).astype(o_ref.dtype)


def _fourier_mix(u, time_tab, chan_tab, row0, nseq, t):
    tm = min(t, 512)
    nt = t // tm
    seq0 = row0 // t
    return pl.pallas_call(
        functools.partial(_fourier_body, t=t, norm=1.0 / math.sqrt(t * FOURIER_GROUP_WIDTH)),
        grid=(nseq, nt),
        in_specs=[pl.BlockSpec((t, FOURIER_WIDTH), lambda b, i: (seq0 + b, 0)),
                  pl.BlockSpec((tm, 2 * t), lambda b, i: (i, 0)),
                  _resident(chan_tab.shape)],
        out_specs=pl.BlockSpec((tm, FOURIER_WIDTH), lambda b, i: (b * nt + i, 0)),
        out_shape=jax.ShapeDtypeStruct((nseq * t, FOURIER_WIDTH), BF16),
        scratch_shapes=[pltpu.VMEM((2 * t, FOURIER_WIDTH), BF16)],
        compiler_params=_params(("parallel", "arbitrary"), 40),
        name=f"fourier_mix_t{t}",
    )(u, time_tab, chan_tab)


def _rope_tables(t):
    pos = jnp.arange(t, dtype=jnp.int32)
    rows = (pos // GRID_W).astype(F32)
    cols = (pos % GRID_W).astype(F32)
    half = HEAD_DIM // 2
    inv = 1.0 / (ROPE_THETA ** (jnp.arange(0, half, 2, dtype=F32) / half))
    def tab(p):
        ang = p[:, None] * inv[None, :]
        return (jnp.concatenate([jnp.cos(ang), jnp.cos(ang)], -1),
                jnp.concatenate([-jnp.sin(ang), jnp.sin(ang)], -1))
    cr, sr = tab(rows)
    cc, sc = tab(cols)
    return jnp.concatenate([cr, cc], -1), jnp.concatenate([sr, sc], -1)


def _qkv_body(*refs, rope):
    if rope:
        q_ref, k_ref, v_ref, gq_ref, gk_ref, cos_ref, sin_ref, qo_ref, ko_ref, vo_ref = refs
        cos, sin = cos_ref[...], sin_ref[...]
        lane = lax.broadcasted_iota(jnp.int32, (1, HEAD_DIM), 1)
        low = (lane % (HEAD_DIM // 2)) < (HEAD_DIM // 4)
    else:
        q_ref, k_ref, v_ref, gq_ref, gk_ref, qo_ref, ko_ref, vo_ref, kn_ref = refs

    def head_norm(xh, g):
        ms = jnp.mean(xh * xh, axis=-1, keepdims=True)
        return xh * lax.rsqrt(ms + NORM_EPS) * g

    def rotate(xh):
        partner = jnp.where(low, pltpu.roll(xh, HEAD_DIM - HEAD_DIM // 4, 1), pltpu.roll(xh, HEAD_DIM // 4, 1))
        return xh * cos + partner * sin

    scale = HEAD_DIM ** -0.5
    for h in range(ATTN_HEADS):
        sl = slice(h * HEAD_DIM, (h + 1) * HEAD_DIM)
        qh = head_norm(q_ref[:, sl], gq_ref[...])
        if rope:
            qh = rotate(qh)
        qo_ref[:, sl] = (qh * scale).astype(BF16)
    for j in range(ATTN_KV_HEADS):
        sl = slice(j * HEAD_DIM, (j + 1) * HEAD_DIM)
        kh = head_norm(k_ref[:, sl], gk_ref[...])
        if rope:
            ko_ref[:, sl] = rotate(kh).astype(BF16)
        else:
            kn_ref[:, sl] = kh
            ko_ref[:, sl] = kh.astype(BF16)
    vo_ref[...] = v_ref[...].astype(BF16)


def _qkv_prep(q, k, v, gq, gk, row0, nrows, rope_tabs=None):
    tt = 256
    b0 = row0 // tt
    rope = rope_tabs is not None
    row = lambda w: pl.BlockSpec((tt, w), lambda i: (b0 + i, 0))
    out = lambda w: pl.BlockSpec((tt, w), lambda i: (i, 0))
    in_specs = [row(ATTN_WIDTH), row(KV_WIDTH), row(KV_WIDTH), _resident((1, HEAD_DIM)), _resident((1, HEAD_DIM))]
    args = [q, k, v, gq, gk]
    out_specs = [out(ATTN_WIDTH), out(KV_WIDTH), out(KV_WIDTH)]
    out_shape = [jax.ShapeDtypeStruct((nrows, ATTN_WIDTH), BF16),
                 jax.ShapeDtypeStruct((nrows, KV_WIDTH), BF16),
                 jax.ShapeDtypeStruct((nrows, KV_WIDTH), BF16)]
    if rope:
        t = rope_tabs[0].shape[0]
        nt = t // tt
        tab = pl.BlockSpec((tt, HEAD_DIM), lambda i: (i % nt, 0))
        in_specs += [tab, tab]
        args += list(rope_tabs)
    else:
        out_specs.append(out(KV_WIDTH))
        out_shape.append(jax.ShapeDtypeStruct((nrows, KV_WIDTH), F32))
    return pl.pallas_call(
        functools.partial(_qkv_body, rope=rope),
        grid=(nrows // tt,),
        in_specs=in_specs, out_specs=out_specs, out_shape=out_shape,
        compiler_params=_params(("parallel",), 32),
        name="qkv_prep_rope" if rope else "qkv_prep",
    )(*args)


def _attn_body(*refs, cached):
    if cached:
        q_ref, k_ref, v_ref, kc_ref, vc_ref, o_ref = refs
    else:
        q_ref, k_ref, v_ref, o_ref = refs
    for j in range(ATTN_KV_HEADS):
        kv = slice(j * HEAD_DIM, (j + 1) * HEAD_DIM)
        kj, vj = k_ref[:, kv], v_ref[:, kv]
        if cached:
            kc, vc = kc_ref[:, kv].astype(BF16), vc_ref[:, kv].astype(BF16)
        for g in range(KV_GROUP):
            sl = slice((j * KV_GROUP + g) * HEAD_DIM, (j * KV_GROUP + g + 1) * HEAD_DIM)
            qh = q_ref[:, sl]
            s = _dot_nt(qh, kj)
            m = jnp.max(s, axis=-1, keepdims=True)
            if cached:
                sc = _dot_nt(qh, kc)
                m = jnp.maximum(m, jnp.max(sc, axis=-1, keepdims=True))
            p = jnp.exp(s - m)
            l = jnp.sum(p, axis=-1, keepdims=True)
            acc = _dot(p.astype(BF16), vj)
            if cached:
                pc = jnp.exp(sc - m)
                l = l + jnp.sum(pc, axis=-1, keepdims=True)
                acc = acc + _dot(pc.astype(BF16), vc)
            o_ref[:, sl] = (acc / l).astype(o_ref.dtype)


def _attention(q, k, v, nseq, t, cache_k=None, cache_v=None):
    tq = 256
    nq = t // tq
    cached = cache_k is not None
    in_specs = [pl.BlockSpec((tq, ATTN_WIDTH), lambda b, i: (b * nq + i, 0)),
                pl.BlockSpec((t, KV_WIDTH), lambda b, i: (b, 0)),
                pl.BlockSpec((t, KV_WIDTH), lambda b, i: (b, 0))]
    args = [q, k, v]
    if cached:
        past = cache_k.shape[1]
        in_specs += [pl.BlockSpec((None, past, KV_WIDTH), lambda b, i: (b, 0, 0))] * 2
        args += [cache_k, cache_v]
    return pl.pallas_call(
        functools.partial(_attn_body, cached=cached),
        grid=(nseq, nq),
        in_specs=in_specs,
        out_specs=pl.BlockSpec((tq, ATTN_WIDTH), lambda b, i: (b * nq + i, 0)),
        out_shape=jax.ShapeDtypeStruct((nseq * t, ATTN_WIDTH), BF16),
        compiler_params=_params(("parallel", "parallel"), 48),
        name="attention_cached" if cached else "attention",
    )(*args)


def _boundary_masks(row_start, tm, rows_lat, t_lat, t_ctx):
    in_lat = row_start < rows_lat
    length = jnp.where(in_lat, t_lat, t_ctx)
    base = jnp.where(in_lat, lax.rem(row_start, t_lat), lax.rem(row_start - rows_lat, t_ctx))
    pos = base + lax.broadcasted_iota(jnp.int32, (tm, 1), 0)
    first = pos == 0
    last = pos == length - 1
    for k in range(1, tm // min(t_lat, t_ctx) + 1):
        first = jnp.logical_or(first, pos == k * length)
        last = jnp.logical_or(last, pos == (k + 1) * length - 1)
    return 1.0 - first.astype(F32), 1.0 - last.astype(F32)


def _conv3(ext_ref, w_ref, tm, not_first, not_last, cols=slice(None)):
    return (w_ref[0:1, cols] * (ext_ref[HALO - 1:HALO - 1 + tm, cols] * not_first)
            + w_ref[1:2, cols] * ext_ref[HALO:HALO + tm, cols]
            + w_ref[2:3, cols] * (ext_ref[HALO + 1:HALO + 1 + tm, cols] * not_last))


def _head_sum_matrix():
    i = lax.broadcasted_iota(jnp.int32, (RWKV_WIDTH, RWKV_WIDTH), 0) // RWKV_N
    j = lax.broadcasted_iota(jnp.int32, (RWKV_WIDTH, RWKV_WIDTH), 1) // RWKV_N
    return (i == j).astype(F32)


def _head_sums(x, ones_bd):
    return jnp.dot(x, ones_bd, preferred_element_type=F32, precision=HIGHEST)


def _rwkv_prep_body(rw_ref, prev_ref, next_ref, cw_ref, wl_ref, w0_ref, a0_ref, kks_ref, ka_ref, rk_ref,
                    r_ref, v_ref, kk_ref, g_ref, bonus_ref, lw_ref, km_ref, b_ref, ext_scr,
                    *, tm, rows_lat, t_lat, t_ctx):
    ext_scr[0:HALO, :] = prev_ref[...]
    ext_scr[HALO:HALO + tm, :] = rw_ref[...]
    ext_scr[HALO + tm:, :] = next_ref[...]
    not_first, not_last = _boundary_masks(pl.program_id(0) * tm, tm, rows_lat, t_lat, t_ctx)
    z = _conv3(ext_scr, cw_ref, tm, not_first, not_last)
    c = RWKV_WIDTH
    r, k, v = z[:, :c], z[:, c:2 * c], z[:, 2 * c:3 * c]
    zl = z[:, 3 * c:]
    lane = lax.broadcasted_iota(jnp.int32, (1, LORA_IN), 1)
    lora_in = jnp.where(lane < DECAY_LORA, jnp.tanh(zl),
                        jnp.where(lane < DECAY_LORA + ICLR_LORA, zl, jax.nn.sigmoid(zl)))
    lora = jnp.dot(lora_in, wl_ref[...], preferred_element_type=F32, precision=HIGHEST)
    ones_bd = _head_sum_matrix()
    kk = k * kks_ref[...]
    kk = kk * lax.rsqrt(_head_sums(kk * kk, ones_bd) + 1e-12)
    r_ref[...] = r
    v_ref[...] = v
    kk_ref[...] = kk
    g_ref[...] = lora[:, 4 * c:5 * c]
    kmod_sum = jnp.zeros_like(k)
    for d in range(2):
        wpre = w0_ref[d:d + 1, :] + lora[:, d * c:(d + 1) * c]
        w = -jax.nn.softplus(-wpre) - 0.5
        lw_ref[d] = -jnp.exp(w)
        a = jax.nn.sigmoid(a0_ref[d:d + 1, :] + lora[:, (2 + d) * c:(3 + d) * c])
        kmod = k * (1.0 + (a - 1.0) * ka_ref[...])
        km_ref[d] = kmod
        b_ref[d] = kk * a
        kmod_sum = kmod_sum + kmod
    bonus_ref[...] = _head_sums(r * kmod_sum * rk_ref[...], ones_bd) * v


def _rwkv_prep(rw, conv_w, lora_w, w0, a0, kk_scale, ka, rk, rows_lat, t_lat, t_ctx):
    rows = rw.shape[0]
    tm = 256
    nh = tm // HALO
    last = rows // HALO - 1
    c = RWKV_WIDTH
    one = lambda: pl.BlockSpec((tm, c), lambda i: (i, 0))
    two = lambda: pl.BlockSpec((2, tm, c), lambda i: (0, i, 0))
    return pl.pallas_call(
        functools.partial(_rwkv_prep_body, tm=tm, rows_lat=rows_lat, t_lat=t_lat, t_ctx=t_ctx),
        grid=(rows // tm,),
        in_specs=[pl.BlockSpec((tm, RWKV_IN), lambda i: (i, 0)),
                  pl.BlockSpec((HALO, RWKV_IN), lambda i: (jnp.maximum(i * nh - 1, 0), 0)),
                  pl.BlockSpec((HALO, RWKV_IN), lambda i: (jnp.minimum((i + 1) * nh, last), 0)),
                  _resident(conv_w.shape), _resident(lora_w.shape), _resident(w0.shape), _resident(a0.shape),
                  _resident(kk_scale.shape), _resident(ka.shape), _resident(rk.shape)],
        out_specs=[one(), one(), one(), one(), one(), two(), two(), two()],
        out_shape=[jax.ShapeDtypeStruct((rows, c), F32)] * 5 + [jax.ShapeDtypeStruct((2, rows, c), F32)] * 3,
        scratch_shapes=[pltpu.VMEM((tm + 2 * HALO, RWKV_IN), F32)],
        compiler_params=_params(("parallel",), 48),
        name="rwkv_prep",
    )(rw, rw, rw, conv_w, lora_w, w0, a0, kk_scale, ka, rk)


def _scan_direction_operands(r_ref, v_ref, kk_ref, lw_ref, km_ref, b_ref, backward):
    n = CHUNK
    row = lax.broadcasted_iota(jnp.int32, (n, n), 0)
    col = lax.broadcasted_iota(jnp.int32, (n, n), 1)
    upto = (col >= row) if backward else (col <= row)
    lw = lw_ref[...]
    cs = jnp.dot(upto.astype(F32), lw, preferred_element_type=F32, precision=HIGHEST)
    tot = jnp.sum(lw, axis=0, keepdims=True)
    grow = jnp.exp(-cs)
    to_end = jnp.exp(tot - cs)
    b = b_ref[...]
    km = km_ref[...]
    return dict(
        kkt=(kk_ref[...] * jnp.exp(cs - lw)).astype(BF16),
        rt=(r_ref[...] * jnp.exp(cs)).astype(BF16),
        bt=(b * grow).astype(BF16), kt=(km * grow).astype(BF16),
        bh=(b * to_end).astype(BF16), kh=(km * to_end).astype(BF16),
        vb=v_ref[...].astype(BF16), w_all=jnp.exp(tot))


def _rwkv_scan_body(rf_ref, vf_ref, kkf_ref, lwf_ref, kmf_ref, bf_ref,
                    rb_ref, vb_ref, kkb_ref, lwb_ref, kmb_ref, bb_ref, s0_ref,
                    yf_ref, yb_ref, sfin_ref, s_scr):
    c = pl.program_id(1)

    @pl.when(c == 0)
    def _():
        s_scr[...] = s0_ref[...]

    n = CHUNK
    row = lax.broadcasted_iota(jnp.int32, (2 * n, 2 * n), 0)
    col = lax.broadcasted_iota(jnp.int32, (2 * n, 2 * n), 1)
    t_idx, s_idx, read_rows = row % n, col % n, row >= n
    eye = (lax.broadcasted_iota(jnp.int32, (n, n), 0) == lax.broadcasted_iota(jnp.int32, (n, n), 1)).astype(F32)
    ops = (_scan_direction_operands(rf_ref, vf_ref, kkf_ref, lwf_ref, kmf_ref, bf_ref, False),
           _scan_direction_operands(rb_ref, vb_ref, kkb_ref, lwb_ref, kmb_ref, bb_ref, True))
    same_step = jnp.logical_and(read_rows, s_idx == t_idx)
    masks = (jnp.logical_or(s_idx < t_idx, same_step), jnp.logical_or(s_idx > t_idx, same_step))
    y_refs = (yf_ref, yb_ref)
    chains = [(d, h) for d in range(2) for h in range(RWKV_HEADS)]
    head = lambda d, name, h: ops[d][name][:, h * RWKV_N:(h + 1) * RWKV_N]
    cat = lambda a, b: jnp.concatenate([a, b], axis=0)

    lhs = [cat(head(d, "kkt", h), head(d, "rt", h)) for d, h in chains]
    coef = [jnp.where(masks[d], _dot_nt(l, cat(head(d, "bt", h), head(d, "kt", h))), 0.0)
            for l, (d, h) in zip(lhs, chains)]
    n_mat = [a[:n, :n] for a in coef]
    state = [s_scr[d, h] for d, h in chains]
    state_b = [s.astype(BF16) for s in state]
    read = [_dot_nt(l, sb) for l, sb in zip(lhs, state_b)]
    akv = [_dot(a[:n, n:].astype(BF16), head(d, "vb", h)) for a, (d, h) in zip(coef, chains)]

    x = [eye - m for m in n_mat]
    p = [_dot(m.astype(BF16), m.astype(BF16)) for m in n_mat]
    steps = int(math.log2(n)) - 1
    for s in range(steps):
        pb = [q.astype(BF16) for q in p]
        x = [xi + _dot(xi.astype(BF16), q) for xi, q in zip(x, pb)]
        if s + 1 < steps:
            p = [_dot(q, q) for q in pb]

    u = [_dot(xi.astype(BF16), (-(rd[:n] + ak)).astype(BF16)) for xi, rd, ak in zip(x, read, akv)]
    uv = [cat(ui.astype(BF16), head(d, "vb", h)) for ui, (d, h) in zip(u, chains)]
    for (d, h), a, rd, uvi, s in zip(chains, coef, read, uv, state):
        sl = slice(h * RWKV_N, (h + 1) * RWKV_N)
        y_refs[d][:, sl] = rd[n:] + _dot(a[n:].astype(BF16), uvi)
        s_scr[d, h] = s * ops[d]["w_all"][:, sl] + _dot_tn(uvi, cat(head(d, "bh", h), head(d, "kh", h)))

    @pl.when(c == pl.num_programs(1) - 1)
    def _():
        sfin_ref[...] = s_scr[...]


def _rwkv_scan(r, v, kk, lw, km, b, s0, row0, nseq, t):
    n = CHUNK
    nc = t // n
    c0 = row0 // n
    c = RWKV_WIDTH
    fwd = lambda bb, cc: c0 + bb * nc + cc
    bwd = lambda bb, cc: c0 + bb * nc + nc - 1 - cc
    one = lambda chunk: pl.BlockSpec((n, c), lambda bb, cc: (chunk(bb, cc), 0))
    two = lambda chunk, d: pl.BlockSpec((None, n, c), lambda bb, cc: (d, chunk(bb, cc), 0))
    state = pl.BlockSpec((None, 2, RWKV_HEADS, RWKV_N, RWKV_N), lambda bb, cc: (bb, 0, 0, 0, 0))
    y_shape = jax.ShapeDtypeStruct((nseq * t, c), F32)
    return pl.pallas_call(
        _rwkv_scan_body,
        grid=(nseq, nc),
        in_specs=[one(fwd), one(fwd), one(fwd), two(fwd, 0), two(fwd, 0), two(fwd, 0),
                  one(bwd), one(bwd), one(bwd), two(bwd, 1), two(bwd, 1), two(bwd, 1), state],
        out_specs=[pl.BlockSpec((n, c), lambda bb, cc: (bb * nc + cc, 0)),
                   pl.BlockSpec((n, c), lambda bb, cc: (bb * nc + nc - 1 - cc, 0)), state],
        out_shape=[y_shape, y_shape, jax.ShapeDtypeStruct((nseq, 2, RWKV_HEADS, RWKV_N, RWKV_N), F32)],
        scratch_shapes=[pltpu.VMEM((2, RWKV_HEADS, RWKV_N, RWKV_N), F32)],
        compiler_params=_params(("parallel", "arbitrary"), 32),
        name=f"rwkv_scan_t{t}",
    )(r, v, kk, lw, km, b, r, v, kk, lw, km, b, s0)


def _rwkv_out_body(yf_ref, yb_ref, bonus_ref, g_ref, lg_ref, lb_ref, o_ref):
    ones_bd = _head_sum_matrix()
    y = yf_ref[...] + yb_ref[...]
    mu = _head_sums(y, ones_bd) * (1.0 / RWKV_N)
    yc = y - mu
    var = _head_sums(yc * yc, ones_bd) * (1.0 / RWKV_N)
    yn = yc * lax.rsqrt(var + GN_EPS) * lg_ref[...] + lb_ref[...]
    o_ref[...] = ((yn + bonus_ref[...]) * g_ref[...]).astype(o_ref.dtype)


def _rwkv_out(yf, yb, bonus, g, row0, ln_g, ln_b):
    nrows = yf.shape[0]
    tm = 256
    b0 = row0 // tm
    c = RWKV_WIDTH
    src = pl.BlockSpec((tm, c), lambda i: (b0 + i, 0))
    return pl.pallas_call(
        _rwkv_out_body,
        grid=(nrows // tm,),
        in_specs=[pl.BlockSpec((tm, c), lambda i: (i, 0))] * 2 + [src, src, _resident((1, c)), _resident((1, c))],
        out_specs=pl.BlockSpec((tm, c), lambda i: (i, 0)),
        out_shape=jax.ShapeDtypeStruct((nrows, c), BF16),
        compiler_params=_params(("parallel",), 32),
        name="rwkv_out",
    )(yf, yb, bonus, g, ln_g, ln_b)


def _outproj_body(f_ref, a_ref, r_ref, x_ref, gate_ref, w_ref, o_ref):
    f0, a0, r0 = 0, FOURIER_WIDTH, FOURIER_WIDTH + ATTN_WIDTH
    mix = (_dot(f_ref[...], w_ref[f0:a0, :]) + _dot(a_ref[...], w_ref[a0:r0, :])
           + _dot(r_ref[...], w_ref[r0:, :]))
    o_ref[...] = x_ref[...] + gate_ref[...] * mix


def _out_proj(f, a, r, x, gate, w, seg_rows):
    rows, d = x.shape
    tm = 512 if rows % 512 == 0 else 256
    row = lambda arr: pl.BlockSpec((tm, arr.shape[1]), lambda i: (i, 0))
    return pl.pallas_call(
        _outproj_body,
        grid=(rows // tm,),
        in_specs=[row(f), row(a), row(r), row(x),
                  pl.BlockSpec((None, 1, d), lambda i: (i * tm // seg_rows, 0, 0)),
                  _resident(w.shape)],
        out_specs=row(x),
        out_shape=jax.ShapeDtypeStruct(x.shape, F32),
        compiler_params=_params(("parallel",), 48),
        name="out_proj",
    )(f, a, r, x, gate, w)


def _ffn_body(x_ref, prev_ref, next_ref, g_ref, sc_ref, sh_ref, gate_ref, wa_ref, wg_ref,
              cwa_ref, cwg_ref, ba_ref, bg_ref, wd_ref, o_ref,
              h_scr, ua_scr, ug_scr, acc_scr, act_cur, act_new,
              *, tm, rows_lat, t_lat, t_ctx):
    j = pl.program_id(1)
    nj = pl.num_programs(1) - 1

    @pl.when(j == 0)
    def _():
        norm = lambda x: _modulated_rmsnorm(x, g_ref[...], sc_ref[...], sh_ref[...]).astype(BF16)
        h_scr[0:HALO, :] = norm(prev_ref[...])
        h_scr[HALO:HALO + tm, :] = norm(x_ref[...])
        h_scr[HALO + tm:, :] = norm(next_ref[...])
        act_new[...] = jnp.zeros_like(act_new)
        acc_scr[...] = jnp.zeros_like(acc_scr)

    @pl.when(j < nj)
    def _():
        act_cur[...] = act_new[...]
        h = h_scr[...]
        subs = [slice(s, s + FFN_SUB) for s in range(0, wa_ref.shape[1], FFN_SUB)]
        for cs in subs:
            ua_scr[:, cs] = _dot(h, wa_ref[:, cs])
            ug_scr[:, cs] = _dot(h, wg_ref[:, cs])
        not_first, not_last = _boundary_masks(pl.program_id(0) * tm, tm, rows_lat, t_lat, t_ctx)
        for cs in subs:
            ua = _conv3(ua_scr, cwa_ref, tm, not_first, not_last, cs) + ba_ref[:, cs]
            ug = _conv3(ug_scr, cwg_ref, tm, not_first, not_last, cs) + bg_ref[:, cs]
            act_new[:, cs] = (ug * jax.nn.sigmoid(ug) * ua).astype(BF16)
        acc_scr[...] += _dot(act_cur[...], wd_ref[...])

    @pl.when(j == nj)
    def _():
        o_ref[...] = x_ref[...] + gate_ref[...] * (acc_scr[...] + _dot(act_new[...], wd_ref[...]))


def _conv_ffn(x, g, scale, shift, gate, wa, wg, cwa, cwg, ba, bg, wd, seg_rows, rows_lat, t_lat, t_ctx):
    rows, d = x.shape
    tm = 512 if rows % 512 == 0 else 256
    tf = FFN_TILE
    nh = tm // HALO
    last = rows // HALO - 1
    nj = D_FF_PAD // tf
    seg = lambda i, j: (i * tm // seg_rows, 0, 0)
    col = lambda r: pl.BlockSpec((r, tf), lambda i, j: (0, jnp.minimum(j, nj - 1)))
    return pl.pallas_call(
        functools.partial(_ffn_body, tm=tm, rows_lat=rows_lat, t_lat=t_lat, t_ctx=t_ctx),
        grid=(rows // tm, nj + 1),
        in_specs=[pl.BlockSpec((tm, d), lambda i, j: (i, 0)),
                  pl.BlockSpec((HALO, d), lambda i, j: (jnp.maximum(i * nh - 1, 0), 0)),
                  pl.BlockSpec((HALO, d), lambda i, j: (jnp.minimum((i + 1) * nh, last), 0)),
                  pl.BlockSpec((1, d), lambda i, j: (0, 0)),
                  pl.BlockSpec((None, 1, d), seg), pl.BlockSpec((None, 1, d), seg), pl.BlockSpec((None, 1, d), seg),
                  col(d), col(d), col(3), col(3), col(1), col(1),
                  pl.BlockSpec((tf, d), lambda i, j: (jnp.maximum(j - 1, 0), 0))],
        out_specs=pl.BlockSpec((tm, d), lambda i, j: (i, 0)),
        out_shape=jax.ShapeDtypeStruct(x.shape, F32),
        scratch_shapes=[pltpu.VMEM((tm + 2 * HALO, d), BF16),
                        pltpu.VMEM((tm + 2 * HALO, tf), F32),
                        pltpu.VMEM((tm + 2 * HALO, tf), F32),
                        pltpu.VMEM((tm, d), F32),
                        pltpu.VMEM((tm, tf), BF16),
                        pltpu.VMEM((tm, tf), BF16)],
        compiler_params=_params(("parallel", "arbitrary"), 56),
        name="conv_ffn",
    )(x, x, x, g, scale, shift, gate, wa, wg, cwa, cwg, ba, bg, wd)


def _final_norm_body(x_ref, g_ref, o_ref):
    x = x_ref[...]
    ms = jnp.mean(x * x, axis=-1, keepdims=True)
    o_ref[...] = x * lax.rsqrt(ms + NORM_EPS) * g_ref[...]


def _final_norm(x, g, row0, nrows):
    tm = 256
    b0 = row0 // tm
    d = x.shape[1]
    return pl.pallas_call(
        _final_norm_body,
        grid=(nrows // tm,),
        in_specs=[pl.BlockSpec((tm, d), lambda i: (b0 + i, 0)), _resident((1, d))],
        out_specs=pl.BlockSpec((tm, d), lambda i: (i, 0)),
        out_shape=jax.ShapeDtypeStruct((nrows, d), F32),
        compiler_params=_params(("parallel",), 32),
        name="final_norm",
    )(x, g)


def _lora_weight(w2, a2, g2):
    c = RWKV_WIDTH
    wl = jnp.zeros((LORA_IN, 5 * c), F32)
    wl = wl.at[:DECAY_LORA, 0:c].set(w2[0]).at[:DECAY_LORA, c:2 * c].set(w2[1])
    wl = wl.at[DECAY_LORA:DECAY_LORA + ICLR_LORA, 2 * c:3 * c].set(a2[0])
    wl = wl.at[DECAY_LORA:DECAY_LORA + ICLR_LORA, 3 * c:4 * c].set(a2[1])
    return wl.at[DECAY_LORA + ICLR_LORA:, 4 * c:].set(g2)


def _pad_cols(w):
    return jnp.pad(w, ((0, 0), (0, D_FF_PAD - D_FF)))


def _forward(x_lat, x_ctx, cache_k, cache_v, state, c, c_ctx, w_ada, b_ada, norm1_g, norm2_g, w_in, w_out,
             q_norm_g, k_norm_g, rw_conv, rw_w0, rw_w2, rw_a0, rw_a2, rw_g2, rw_kk, rw_ka, rw_rk,
             rw_lnx_g, rw_lnx_b, ffn_up, ffn_conv_w, ffn_conv_b, ffn_down, final_norm_g):
    n_lat, t_lat, d = x_lat.shape
    n_ctx, t_ctx, _ = x_ctx.shape
    depth = w_ada.shape[0]
    past = cache_k.shape[2]
    rows_lat, rows_ctx = n_lat * t_lat, n_ctx * t_ctx
    assert rows_ctx % t_lat == 0 and n_lat + rows_ctx // t_lat <= MOD_ROWS
    assert t_lat % 256 == 0 and t_ctx % 128 == 0 and t_lat % GRID_W == 0

    x = jnp.concatenate([x_lat.reshape(rows_lat, d), x_ctx.reshape(rows_ctx, d)], axis=0)
    cvec = jnp.concatenate([c, jnp.broadcast_to(c_ctx[None, :], (MOD_ROWS - n_lat, d))], axis=0)
    mod = _adaln_mod(cvec, w_ada, b_ada).reshape(depth, MOD_ROWS, 6, 1, d)

    rope_tabs = _rope_tables(t_lat)
    time_lat, chan_tab = _dft_tables(t_lat)
    time_ctx, _ = _dft_tables(t_ctx)
    zero_state = jnp.zeros((n_ctx, 2, RWKV_HEADS, RWKV_N, RWKV_N), F32)
    row2 = lambda a: a.reshape(1, -1)

    new_k, new_v, new_s = [], [], []
    for l in range(depth):
        shift1, scale1, gate1, shift2, scale2, gate2 = (mod[l, :, i] for i in range(6))
        f, q, k, v, rw = _in_proj(x, row2(norm1_g[l]), scale1, shift1, w_in[l].astype(BF16), t_lat)

        f_lat = _fourier_mix(f, time_lat, chan_tab, 0, n_lat, t_lat)
        f_ctx = _fourier_mix(f, time_ctx, chan_tab, rows_lat, n_ctx, t_ctx)

        gq, gk = row2(q_norm_g[l]), row2(k_norm_g[l])
        q_lat, k_lat, v_lat = _qkv_prep(q, k, v, gq, gk, 0, rows_lat, rope_tabs)
        q_c, k_c, v_c, k_norm = _qkv_prep(q, k, v, gq, gk, rows_lat, rows_ctx)
        a_lat = _attention(q_lat, k_lat, v_lat, n_lat, t_lat,
                           cache_k[:, l].reshape(n_lat, past, KV_WIDTH), cache_v[:, l].reshape(n_lat, past, KV_WIDTH))
        a_ctx = _attention(q_c, k_c, v_c, n_ctx, t_ctx)
        new_k.append(k_norm.reshape(n_ctx, t_ctx, ATTN_KV_HEADS, HEAD_DIM))
        new_v.append(v[rows_lat:].reshape(n_ctx, t_ctx, ATTN_KV_HEADS, HEAD_DIM))

        r_, v_, kk, g_, bonus, lw, km, b_ = _rwkv_prep(
            rw, rw_conv[l], _lora_weight(rw_w2[l], rw_a2[l], rw_g2[l]), rw_w0[l], rw_a0[l],
            row2(rw_kk[l]), row2(rw_ka[l]), row2(rw_rk[l]), rows_lat, t_lat, t_ctx)
        yf_lat, yb_lat, _ = _rwkv_scan(r_, v_, kk, lw, km, b_, state[:, l], 0, n_lat, t_lat)
        yf_ctx, yb_ctx, s_ctx = _rwkv_scan(r_, v_, kk, lw, km, b_, zero_state, rows_lat, n_ctx, t_ctx)
        ln_g, ln_b = row2(rw_lnx_g[l]), row2(rw_lnx_b[l])
        r_lat = _rwkv_out(yf_lat, yb_lat, bonus, g_, 0, ln_g, ln_b)
        r_ctx = _rwkv_out(yf_ctx, yb_ctx, bonus, g_, rows_lat, ln_g, ln_b)
        new_s.append(s_ctx)

        cat = lambda a, b: jnp.concatenate([a, b], axis=0)
        x = _out_proj(cat(f_lat, f_ctx), cat(a_lat, a_ctx), cat(r_lat, r_ctx), x, gate1,
                      w_out[l].astype(BF16), t_lat)

        up = ffn_up[l]
        x = _conv_ffn(x, row2(norm2_g[l]), scale2, shift2, gate2,
                      _pad_cols(up[:, :D_FF]).astype(BF16), _pad_cols(up[:, D_FF:]).astype(BF16),
                      _pad_cols(ffn_conv_w[l][:, :D_FF]), _pad_cols(ffn_conv_w[l][:, D_FF:]),
                      _pad_cols(row2(ffn_conv_b[l][:D_FF])), _pad_cols(row2(ffn_conv_b[l][D_FF:])),
                      jnp.pad(ffn_down[l], ((0, D_FF_PAD - D_FF), (0, 0))).astype(BF16),
                      t_lat, rows_lat, t_lat, t_ctx)

    fg = row2(final_norm_g)
    y_lat = _final_norm(x, fg, 0, rows_lat).reshape(n_lat, t_lat, d)
    y_ctx = _final_norm(x, fg, rows_lat, rows_ctx).reshape(n_ctx, t_ctx, d)
    return (y_ctx, y_lat, jnp.stack(new_k, axis=1), jnp.stack(new_v, axis=1), jnp.stack(new_s, axis=1))


def kernel(x_prompt, x_sample, cache_attn_k, cache_attn_v, state_rwkv, c, c_ctx, w_ada, b_ada, norm1_g, norm2_g, w_in, w_out, q_norm_g, k_norm_g, rw_conv, rw_w0, rw_w2, rw_a0, rw_a2, rw_g2, rw_kk, rw_ka, rw_rk, rw_lnx_g, rw_lnx_b, ffn_up, ffn_conv_w, ffn_conv_b, ffn_down, final_norm_g):
    return _forward(x_sample, x_prompt, cache_attn_k, cache_attn_v, state_rwkv, c, c_ctx, w_ada, b_ada,
                    norm1_g, norm2_g, w_in, w_out, q_norm_g, k_norm_g, rw_conv, rw_w0, rw_w2, rw_a0, rw_a2,
                    rw_g2, rw_kk, rw_ka, rw_rk, rw_lnx_g, rw_lnx_b, ffn_up, ffn_conv_w, ffn_conv_b, ffn_down,
                    final_norm_g)
```

```python
import functools
import math

import jax
import jax.numpy as jnp
from jax import lax
from jax.experimental import pallas as pl
from jax.experimental.pallas import tpu as pltpu

D_MODEL = 2048
GRID_W = 64
HEAD_DIM = 128
ATTN_HEADS = 8
ATTN_KV_HEADS = 2
KV_GROUP = ATTN_HEADS // ATTN_KV_HEADS
ATTN_WIDTH = ATTN_HEADS * HEAD_DIM
KV_WIDTH = ATTN_KV_HEADS * HEAD_DIM
FOURIER_WIDTH = 512
FOURIER_GROUPS = 4
FOURIER_GROUP_WIDTH = FOURIER_WIDTH // FOURIER_GROUPS
RWKV_WIDTH = 512
RWKV_N = 64
RWKV_HEADS = RWKV_WIDTH // RWKV_N
DECAY_LORA = 64
ICLR_LORA = 64
GATE_LORA = 128
LORA_IN = DECAY_LORA + ICLR_LORA + GATE_LORA
RWKV_IN = 3 * RWKV_WIDTH + LORA_IN
IN_WIDTH = FOURIER_WIDTH + ATTN_WIDTH + 2 * KV_WIDTH + RWKV_IN
D_FF = 5504
ROPE_THETA = 10000.0
NORM_EPS = 1e-6
GN_EPS = 64e-5

MOD_ROWS = 16
FFN_TILE = 512
FFN_SUB = 256
D_FF_PAD = -(-D_FF // FFN_TILE) * FFN_TILE
HALO = 16
CHUNK = 64
MIB = 2 ** 20

F32 = jnp.float32
BF16 = jnp.bfloat16
HIGHEST = lax.Precision.HIGHEST


def _params(semantics, vmem_mib):
    return pltpu.CompilerParams(dimension_semantics=semantics, vmem_limit_bytes=vmem_mib * MIB)


def _resident(shape):
    return pl.BlockSpec(shape, lambda *_: (0,) * len(shape), pipeline_mode=pl.Buffered(1))


def _dot(a, b):
    return jnp.dot(a, b, preferred_element_type=F32)


def _dot_nt(a, b):
    return lax.dot_general(a, b, (((1,), (1,)), ((), ())), preferred_element_type=F32)


def _dot_tn(a, b):
    return lax.dot_general(a, b, (((0,), (0,)), ((), ())), preferred_element_type=F32)


def _modulated_rmsnorm(x, g, scale, shift):
    ms = jnp.mean(x * x, axis=-1, keepdims=True)
    return (x * lax.rsqrt(ms + NORM_EPS) * g) * (1.0 + scale) + shift


def _mod_body(c_ref, w_ref, b_ref, o_ref):
    c = c_ref[...]
    s = (c * jax.nn.sigmoid(c)).astype(BF16)
    o_ref[...] = _dot(s, w_ref[...].astype(BF16)) + b_ref[...]


def _adaln_mod(cvec, w_ada, b_ada):
    depth, d, n = w_ada.shape
    tn = 1024
    return pl.pallas_call(
        _mod_body,
        grid=(depth, n // tn),
        in_specs=[pl.BlockSpec((MOD_ROWS, d), lambda l, j: (0, 0)),
                  pl.BlockSpec((None, d, tn), lambda l, j: (l, 0, j)),
                  pl.BlockSpec((None, 1, tn), lambda l, j: (l, 0, j))],
        out_specs=pl.BlockSpec((None, MOD_ROWS, tn), lambda l, j: (l, 0, j)),
        out_shape=jax.ShapeDtypeStruct((depth, MOD_ROWS, n), F32),
        compiler_params=_params(("parallel", "parallel"), 40),
        name="adaln_mod",
    )(cvec, w_ada, b_ada.reshape(depth, 1, n))


_IN_SPLITS = (("f", 0, FOURIER_WIDTH, BF16),
              ("q", FOURIER_WIDTH, ATTN_WIDTH, F32),
              ("k", FOURIER_WIDTH + ATTN_WIDTH, KV_WIDTH, F32),
              ("v", FOURIER_WIDTH + ATTN_WIDTH + KV_WIDTH, KV_WIDTH, F32),
              ("rw", FOURIER_WIDTH + ATTN_WIDTH + 2 * KV_WIDTH, RWKV_IN, F32))


def _inproj_body(x_ref, g_ref, sc_ref, sh_ref, w_ref, *o_refs):
    h = _modulated_rmsnorm(x_ref[...], g_ref[...], sc_ref[...], sh_ref[...]).astype(BF16)
    for o_ref, (_, start, width, _) in zip(o_refs, _IN_SPLITS):
        o_ref[...] = _dot(h, w_ref[:, start:start + width]).astype(o_ref.dtype)


def _in_proj(x, g, scale, shift, w, seg_rows):
    rows, d = x.shape
    tm = 512 if rows % 512 == 0 else 256
    seg = lambda i: (i * tm // seg_rows, 0, 0)
    return pl.pallas_call(
        _inproj_body,
        grid=(rows // tm,),
        in_specs=[pl.BlockSpec((tm, d), lambda i: (i, 0)),
                  _resident((1, d)),
                  pl.BlockSpec((None, 1, d), seg),
                  pl.BlockSpec((None, 1, d), seg),
                  _resident(w.shape)],
        out_specs=[pl.BlockSpec((tm, width), lambda i: (i, 0)) for _, _, width, _ in _IN_SPLITS],
        out_shape=[jax.ShapeDtypeStruct((rows, width), dt) for _, _, width, dt in _IN_SPLITS],
        compiler_params=_params(("parallel",), 56),
        name="in_proj",
    )(x, g, scale, shift, w)


def _dft_tables(t):
    def angles(n):
        i = jnp.arange(n, dtype=jnp.int32)
        return (2.0 * math.pi / n) * ((i[:, None] * i[None, :]) % n).astype(F32)
    at = angles(t)
    time_tab = jnp.concatenate([jnp.cos(at), -jnp.sin(at)], axis=1).astype(BF16)
    ac = angles(FOURIER_GROUP_WIDTH)
    eye = jnp.eye(FOURIER_GROUPS, dtype=F32)
    chan_tab = jnp.concatenate([jnp.kron(eye, jnp.cos(ac)), jnp.kron(eye, jnp.sin(ac))], axis=1).astype(BF16)
    return time_tab, chan_tab


def _fourier_body(u_ref, ct_ref, cc_ref, o_ref, ab_scr, *, t, norm):
    @pl.when(pl.program_id(1) == 0)
    def _():
        ab = _dot(u_ref[...], cc_ref[...])
        ab_scr[0:t, :] = ab[:, :FOURIER_WIDTH].astype(BF16)
        ab_scr[t:2 * t, :] = ab[:, FOURIER_WIDTH:].astype(BF16)
    o_ref[...] = (_dot(ct_ref[...], ab_scr[...]) * norm).astype(o_ref.dtype)


def _fourier_mix(u, time_tab, chan_tab, row0, nseq, t):
    tm = min(t, 512)
    nt = t // tm
    seq0 = row0 // t
    return pl.pallas_call(
        functools.partial(_fourier_body, t=t, norm=1.0 / math.sqrt(t * FOURIER_GROUP_WIDTH)),
        grid=(nseq, nt),
        in_specs=[pl.BlockSpec((t, FOURIER_WIDTH), lambda b, i: (seq0 + b, 0)),
                  pl.BlockSpec((tm, 2 * t), lambda b, i: (i, 0)),
                  _resident(chan_tab.shape)],
        out_specs=pl.BlockSpec((tm, FOURIER_WIDTH), lambda b, i: (b * nt + i, 0)),
        out_shape=jax.ShapeDtypeStruct((nseq * t, FOURIER_WIDTH), BF16),
        scratch_shapes=[pltpu.VMEM((2 * t, FOURIER_WIDTH), BF16)],
        compiler_params=_params(("parallel", "arbitrary"), 40),
        name=f"fourier_mix_t{t}",
    )(u, time_tab, chan_tab)


def _rope_tables(t):
    pos = jnp.arange(t, dtype=jnp.int32)
    rows = (pos // GRID_W).astype(F32)
    cols = (pos % GRID_W).astype(F32)
    half = HEAD_DIM // 2
    inv = 1.0 / (ROPE_THETA ** (jnp.arange(0, half, 2, dtype=F32) / half))
    def tab(p):
        ang = p[:, None] * inv[None, :]
        return (jnp.concatenate([jnp.cos(ang), jnp.cos(ang)], -1),
                jnp.concatenate([-jnp.sin(ang), jnp.sin(ang)], -1))
    cr, sr = tab(rows)
    cc, sc = tab(cols)
    return jnp.concatenate([cr, cc], -1), jnp.concatenate([sr, sc], -1)


def _qkv_body(*refs, rope):
    if rope:
        q_ref, k_ref, v_ref, gq_ref, gk_ref, cos_ref, sin_ref, qo_ref, ko_ref, vo_ref = refs
        cos, sin = cos_ref[...], sin_ref[...]
        lane = lax.broadcasted_iota(jnp.int32, (1, HEAD_DIM), 1)
        low = (lane % (HEAD_DIM // 2)) < (HEAD_DIM // 4)
    else:
        q_ref, k_ref, v_ref, gq_ref, gk_ref, qo_ref, ko_ref, vo_ref, kn_ref = refs

    def head_norm(xh, g):
        ms = jnp.mean(xh * xh, axis=-1, keepdims=True)
        return xh * lax.rsqrt(ms + NORM_EPS) * g

    def rotate(xh):
        partner = jnp.where(low, pltpu.roll(xh, HEAD_DIM - HEAD_DIM // 4, 1), pltpu.roll(xh, HEAD_DIM // 4, 1))
        return xh * cos + partner * sin

    scale = HEAD_DIM ** -0.5
    for h in range(ATTN_HEADS):
        sl = slice(h * HEAD_DIM, (h + 1) * HEAD_DIM)
        qh = head_norm(q_ref[:, sl], gq_ref[...])
        if rope:
            qh = rotate(qh)
        qo_ref[:, sl] = (qh * scale).astype(BF16)
    for j in range(ATTN_KV_HEADS):
        sl = slice(j * HEAD_DIM, (j + 1) * HEAD_DIM)
        kh = head_norm(k_ref[:, sl], gk_ref[...])
        if rope:
            ko_ref[:, sl] = rotate(kh).astype(BF16)
        else:
            kn_ref[:, sl] = kh
            ko_ref[:, sl] = kh.astype(BF16)
    vo_ref[...] = v_ref[...].astype(BF16)


def _qkv_prep(q, k, v, gq, gk, row0, nrows, rope_tabs=None):
    tt = 256
    b0 = row0 // tt
    rope = rope_tabs is not None
    row = lambda w: pl.BlockSpec((tt, w), lambda i: (b0 + i, 0))
    out = lambda w: pl.BlockSpec((tt, w), lambda i: (i, 0))
    in_specs = [row(ATTN_WIDTH), row(KV_WIDTH), row(KV_WIDTH), _resident((1, HEAD_DIM)), _resident((1, HEAD_DIM))]
    args = [q, k, v, gq, gk]
    out_specs = [out(ATTN_WIDTH), out(KV_WIDTH), out(KV_WIDTH)]
    out_shape = [jax.ShapeDtypeStruct((nrows, ATTN_WIDTH), BF16),
                 jax.ShapeDtypeStruct((nrows, KV_WIDTH), BF16),
                 jax.ShapeDtypeStruct((nrows, KV_WIDTH), BF16)]
    if rope:
        t = rope_tabs[0].shape[0]
        nt = t // tt
        tab = pl.BlockSpec((tt, HEAD_DIM), lambda i: (i % nt, 0))
        in_specs += [tab, tab]
        args += list(rope_tabs)
    else:
        out_specs.append(out(KV_WIDTH))
        out_shape.append(jax.ShapeDtypeStruct((nrows, KV_WIDTH), F32))
    return pl.pallas_call(
        functools.partial(_qkv_body, rope=rope),
        grid=(nrows // tt,),
        in_specs=in_specs, out_specs=out_specs, out_shape=out_shape,
        compiler_params=_params(("parallel",), 32),
        name="qkv_prep_rope" if rope else "qkv_prep",
    )(*args)


def _attn_body(*refs, cached):
    if cached:
        q_ref, k_ref, v_ref, kc_ref, vc_ref, o_ref = refs
    else:
        q_ref, k_ref, v_ref, o_ref = refs
    for j in range(ATTN_KV_HEADS):
        kv = slice(j * HEAD_DIM, (j + 1) * HEAD_DIM)
        kj, vj = k_ref[:, kv], v_ref[:, kv]
        if cached:
            kc, vc = kc_ref[:, kv].astype(BF16), vc_ref[:, kv].astype(BF16)
        for g in range(KV_GROUP):
            sl = slice((j * KV_GROUP + g) * HEAD_DIM, (j * KV_GROUP + g + 1) * HEAD_DIM)
            qh = q_ref[:, sl]
            s = _dot_nt(qh, kj)
            m = jnp.max(s, axis=-1, keepdims=True)
            if cached:
                sc = _dot_nt(qh, kc)
                m = jnp.maximum(m, jnp.max(sc, axis=-1, keepdims=True))
            p = jnp.exp(s - m)
            l = jnp.sum(p, axis=-1, keepdims=True)
            acc = _dot(p.astype(BF16), vj)
            if cached:
                pc = jnp.exp(sc - m)
                l = l + jnp.sum(pc, axis=-1, keepdims=True)
                acc = acc + _dot(pc.astype(BF16), vc)
            o_ref[:, sl] = (acc / l).astype(o_ref.dtype)


def _attention(q, k, v, nseq, t, cache_k=None, cache_v=None):
    tq = 256
    nq = t // tq
    cached = cache_k is not None
    in_specs = [pl.BlockSpec((tq, ATTN_WIDTH), lambda b, i: (b * nq + i, 0)),
                pl.BlockSpec((t, KV_WIDTH), lambda b, i: (b, 0)),
                pl.BlockSpec((t, KV_WIDTH), lambda b, i: (b, 0))]
    args = [q, k, v]
    if cached:
        past = cache_k.shape[1]
        in_specs += [pl.BlockSpec((None, past, KV_WIDTH), lambda b, i: (b, 0, 0))] * 2
        args += [cache_k, cache_v]
    return pl.pallas_call(
        functools.partial(_attn_body, cached=cached),
        grid=(nseq, nq),
        in_specs=in_specs,
        out_specs=pl.BlockSpec((tq, ATTN_WIDTH), lambda b, i: (b * nq + i, 0)),
        out_shape=jax.ShapeDtypeStruct((nseq * t, ATTN_WIDTH), BF16),
        compiler_params=_params(("parallel", "parallel"), 48),
        name="attention_cached" if cached else "attention",
    )(*args)


def _boundary_masks(row_start, tm, rows_lat, t_lat, t_ctx):
    in_lat = row_start < rows_lat
    length = jnp.where(in_lat, t_lat, t_ctx)
    base = jnp.where(in_lat, lax.rem(row_start, t_lat), lax.rem(row_start - rows_lat, t_ctx))
    pos = base + lax.broadcasted_iota(jnp.int32, (tm, 1), 0)
    first = pos == 0
    last = pos == length - 1
    for k in range(1, tm // min(t_lat, t_ctx) + 1):
        first = jnp.logical_or(first, pos == k * length)
        last = jnp.logical_or(last, pos == (k + 1) * length - 1)
    return 1.0 - first.astype(F32), 1.0 - last.astype(F32)


def _conv3(ext_ref, w_ref, tm, not_first, not_last, cols=slice(None)):
    return (w_ref[0:1, cols] * (ext_ref[HALO - 1:HALO - 1 + tm, cols] * not_first)
            + w_ref[1:2, cols] * ext_ref[HALO:HALO + tm, cols]
            + w_ref[2:3, cols] * (ext_ref[HALO + 1:HALO + 1 + tm, cols] * not_last))


def _head_sum_matrix():
    i = lax.broadcasted_iota(jnp.int32, (RWKV_WIDTH, RWKV_WIDTH), 0) // RWKV_N
    j = lax.broadcasted_iota(jnp.int32, (RWKV_WIDTH, RWKV_WIDTH), 1) // RWKV_N
    return (i == j).astype(F32)


def _head_sums(x, ones_bd):
    return jnp.dot(x, ones_bd, preferred_element_type=F32, precision=HIGHEST)


def _rwkv_prep_body(rw_ref, prev_ref, next_ref, cw_ref, wl_ref, w0_ref, a0_ref, kks_ref, ka_ref, rk_ref,
                    r_ref, v_ref, kk_ref, g_ref, bonus_ref, lw_ref, km_ref, b_ref, ext_scr,
                    *, tm, rows_lat, t_lat, t_ctx):
    ext_scr[0:HALO, :] = prev_ref[...]
    ext_scr[HALO:HALO + tm, :] = rw_ref[...]
    ext_scr[HALO + tm:, :] = next_ref[...]
    not_first, not_last = _boundary_masks(pl.program_id(0) * tm, tm, rows_lat, t_lat, t_ctx)
    z = _conv3(ext_scr, cw_ref, tm, not_first, not_last)
    c = RWKV_WIDTH
    r, k, v = z[:, :c], z[:, c:2 * c], z[:, 2 * c:3 * c]
    zl = z[:, 3 * c:]
    lane = lax.broadcasted_iota(jnp.int32, (1, LORA_IN), 1)
    lora_in = jnp.where(lane < DECAY_LORA, jnp.tanh(zl),
                        jnp.where(lane < DECAY_LORA + ICLR_LORA, zl, jax.nn.sigmoid(zl)))
    lora = jnp.dot(lora_in, wl_ref[...], preferred_element_type=F32, precision=HIGHEST)
    ones_bd = _head_sum_matrix()
    kk = k * kks_ref[...]
    kk = kk * lax.rsqrt(_head_sums(kk * kk, ones_bd) + 1e-12)
    r_ref[...] = r
    v_ref[...] = v
    kk_ref[...] = kk
    g_ref[...] = lora[:, 4 * c:5 * c]
    kmod_sum = jnp.zeros_like(k)
    for d in range(2):
        wpre = w0_ref[d:d + 1, :] + lora[:, d * c:(d + 1) * c]
        w = -jax.nn.softplus(-wpre) - 0.5
        lw_ref[d] = -jnp.exp(w)
        a = jax.nn.sigmoid(a0_ref[d:d + 1, :] + lora[:, (2 + d) * c:(3 + d) * c])
        kmod = k * (1.0 + (a - 1.0) * ka_ref[...])
        km_ref[d] = kmod
        b_ref[d] = kk * a
        kmod_sum = kmod_sum + kmod
    bonus_ref[...] = _head_sums(r * kmod_sum * rk_ref[...], ones_bd) * v


def _rwkv_prep(rw, conv_w, lora_w, w0, a0, kk_scale, ka, rk, rows_lat, t_lat, t_ctx):
    rows = rw.shape[0]
    tm = 256
    nh = tm // HALO
    last = rows // HALO - 1
    c = RWKV_WIDTH
    one = lambda: pl.BlockSpec((tm, c), lambda i: (i, 0))
    two = lambda: pl.BlockSpec((2, tm, c), lambda i: (0, i, 0))
    return pl.pallas_call(
        functools.partial(_rwkv_prep_body, tm=tm, rows_lat=rows_lat, t_lat=t_lat, t_ctx=t_ctx),
        grid=(rows // tm,),
        in_specs=[pl.BlockSpec((tm, RWKV_IN), lambda i: (i, 0)),
                  pl.BlockSpec((HALO, RWKV_IN), lambda i: (jnp.maximum(i * nh - 1, 0), 0)),
                  pl.BlockSpec((HALO, RWKV_IN), lambda i: (jnp.minimum((i + 1) * nh, last), 0)),
                  _resident(conv_w.shape), _resident(lora_w.shape), _resident(w0.shape), _resident(a0.shape),
                  _resident(kk_scale.shape), _resident(ka.shape), _resident(rk.shape)],
        out_specs=[one(), one(), one(), one(), one(), two(), two(), two()],
        out_shape=[jax.ShapeDtypeStruct((rows, c), F32)] * 5 + [jax.ShapeDtypeStruct((2, rows, c), F32)] * 3,
        scratch_shapes=[pltpu.VMEM((tm + 2 * HALO, RWKV_IN), F32)],
        compiler_params=_params(("parallel",), 48),
        name="rwkv_prep",
    )(rw, rw, rw, conv_w, lora_w, w0, a0, kk_scale, ka, rk)


def _scan_direction_operands(r_ref, v_ref, kk_ref, lw_ref, km_ref, b_ref, backward):
    n = CHUNK
    row = lax.broadcasted_iota(jnp.int32, (n, n), 0)
    col = lax.broadcasted_iota(jnp.int32, (n, n), 1)
    upto = (col >= row) if backward else (col <= row)
    lw = lw_ref[...]
    cs = jnp.dot(upto.astype(F32), lw, preferred_element_type=F32, precision=HIGHEST)
    tot = jnp.sum(lw, axis=0, keepdims=True)
    grow = jnp.exp(-cs)
    to_end = jnp.exp(tot - cs)
    b = b_ref[...]
    km = km_ref[...]
    return dict(
        kkt=(kk_ref[...] * jnp.exp(cs - lw)).astype(BF16),
        rt=(r_ref[...] * jnp.exp(cs)).astype(BF16),
        bt=(b * grow).astype(BF16), kt=(km * grow).astype(BF16),
        bh=(b * to_end).astype(BF16), kh=(km * to_end).astype(BF16),
        vb=v_ref[...].astype(BF16), w_all=jnp.exp(tot))


def _rwkv_scan_body(rf_ref, vf_ref, kkf_ref, lwf_ref, kmf_ref, bf_ref,
                    rb_ref, vb_ref, kkb_ref, lwb_ref, kmb_ref, bb_ref, s0_ref,
                    yf_ref, yb_ref, sfin_ref, s_scr):
    c = pl.program_id(1)

    @pl.when(c == 0)
    def _():
        s_scr[...] = s0_ref[...]

    n = CHUNK
    row = lax.broadcasted_iota(jnp.int32, (2 * n, 2 * n), 0)
    col = lax.broadcasted_iota(jnp.int32, (2 * n, 2 * n), 1)
    t_idx, s_idx, read_rows = row % n, col % n, row >= n
    eye = (lax.broadcasted_iota(jnp.int32, (n, n), 0) == lax.broadcasted_iota(jnp.int32, (n, n), 1)).astype(F32)
    ops = (_scan_direction_operands(rf_ref, vf_ref, kkf_ref, lwf_ref, kmf_ref, bf_ref, False),
           _scan_direction_operands(rb_ref, vb_ref, kkb_ref, lwb_ref, kmb_ref, bb_ref, True))
    same_step = jnp.logical_and(read_rows, s_idx == t_idx)
    masks = (jnp.logical_or(s_idx < t_idx, same_step), jnp.logical_or(s_idx > t_idx, same_step))
    y_refs = (yf_ref, yb_ref)
    chains = [(d, h) for d in range(2) for h in range(RWKV_HEADS)]
    head = lambda d, name, h: ops[d][name][:, h * RWKV_N:(h + 1) * RWKV_N]
    cat = lambda a, b: jnp.concatenate([a, b], axis=0)

    lhs = [cat(head(d, "kkt", h), head(d, "rt", h)) for d, h in chains]
    coef = [jnp.where(masks[d], _dot_nt(l, cat(head(d, "bt", h), head(d, "kt", h))), 0.0)
            for l, (d, h) in zip(lhs, chains)]
    n_mat = [a[:n, :n] for a in coef]
    state = [s_scr[d, h] for d, h in chains]
    state_b = [s.astype(BF16) for s in state]
    read = [_dot_nt(l, sb) for l, sb in zip(lhs, state_b)]
    akv = [_dot(a[:n, n:].astype(BF16), head(d, "vb", h)) for a, (d, h) in zip(coef, chains)]

    x = [eye - m for m in n_mat]
    p = [_dot(m.astype(BF16), m.astype(BF16)) for m in n_mat]
    steps = int(math.log2(n)) - 1
    for s in range(steps):
        pb = [q.astype(BF16) for q in p]
        x = [xi + _dot(xi.astype(BF16), q) for xi, q in zip(x, pb)]
        if s + 1 < steps:
            p = [_dot(q, q) for q in pb]

    u = [_dot(xi.astype(BF16), (-(rd[:n] + ak)).astype(BF16)) for xi, rd, ak in zip(x, read, akv)]
    uv = [cat(ui.astype(BF16), head(d, "vb", h)) for ui, (d, h) in zip(u, chains)]
    for (d, h), a, rd, uvi, s in zip(chains, coef, read, uv, state):
        sl = slice(h * RWKV_N, (h + 1) * RWKV_N)
        y_refs[d][:, sl] = rd[n:] + _dot(a[n:].astype(BF16), uvi)
        s_scr[d, h] = s * ops[d]["w_all"][:, sl] + _dot_tn(uvi, cat(head(d, "bh", h), head(d, "kh", h)))

    @pl.when(c == pl.num_programs(1) - 1)
    def _():
        sfin_ref[...] = s_scr[...]


def _rwkv_scan(r, v, kk, lw, km, b, s0, row0, nseq, t):
    n = CHUNK
    nc = t // n
    c0 = row0 // n
    c = RWKV_WIDTH
    fwd = lambda bb, cc: c0 + bb * nc + cc
    bwd = lambda bb, cc: c0 + bb * nc + nc - 1 - cc
    one = lambda chunk: pl.BlockSpec((n, c), lambda bb, cc: (chunk(bb, cc), 0))
    two = lambda chunk, d: pl.BlockSpec((None, n, c), lambda bb, cc: (d, chunk(bb, cc), 0))
    state = pl.BlockSpec((None, 2, RWKV_HEADS, RWKV_N, RWKV_N), lambda bb, cc: (bb, 0, 0, 0, 0))
    y_shape = jax.ShapeDtypeStruct((nseq * t, c), F32)
    return pl.pallas_call(
        _rwkv_scan_body,
        grid=(nseq, nc),
        in_specs=[one(fwd), one(fwd), one(fwd), two(fwd, 0), two(fwd, 0), two(fwd, 0),
                  one(bwd), one(bwd), one(bwd), two(bwd, 1), two(bwd, 1), two(bwd, 1), state],
        out_specs=[pl.BlockSpec((n, c), lambda bb, cc: (bb * nc + cc, 0)),
                   pl.BlockSpec((n, c), lambda bb, cc: (bb * nc + nc - 1 - cc, 0)), state],
        out_shape=[y_shape, y_shape, jax.ShapeDtypeStruct((nseq, 2, RWKV_HEADS, RWKV_N, RWKV_N), F32)],
        scratch_shapes=[pltpu.VMEM((2, RWKV_HEADS, RWKV_N, RWKV_N), F32)],
        compiler_params=_params(("parallel", "arbitrary"), 32),
        name=f"rwkv_scan_t{t}",
    )(r, v, kk, lw, km, b, r, v, kk, lw, km, b, s0)


def _rwkv_out_body(yf_ref, yb_ref, bonus_ref, g_ref, lg_ref, lb_ref, o_ref):
    ones_bd = _head_sum_matrix()
    y = yf_ref[...] + yb_ref[...]
    mu = _head_sums(y, ones_bd) * (1.0 / RWKV_N)
    yc = y - mu
    var = _head_sums(yc * yc, ones_bd) * (1.0 / RWKV_N)
    yn = yc * lax.rsqrt(var + GN_EPS) * lg_ref[...] + lb_ref[...]
    o_ref[...] = ((yn + bonus_ref[...]) * g_ref[...]).astype(o_ref.dtype)


def _rwkv_out(yf, yb, bonus, g, row0, ln_g, ln_b):
    nrows = yf.shape[0]
    tm = 256
    b0 = row0 // tm
    c = RWKV_WIDTH
    src = pl.BlockSpec((tm, c), lambda i: (b0 + i, 0))
    return pl.pallas_call(
        _rwkv_out_body,
        grid=(nrows // tm,),
        in_specs=[pl.BlockSpec((tm, c), lambda i: (i, 0))] * 2 + [src, src, _resident((1, c)), _resident((1, c))],
        out_specs=pl.BlockSpec((tm, c), lambda i: (i, 0)),
        out_shape=jax.ShapeDtypeStruct((nrows, c), BF16),
        compiler_params=_params(("parallel",), 32),
        name="rwkv_out",
    )(yf, yb, bonus, g, ln_g, ln_b)


def _outproj_body(f_ref, a_ref, r_ref, x_ref, gate_ref, w_ref, o_ref):
    f0, a0, r0 = 0, FOURIER_WIDTH, FOURIER_WIDTH + ATTN_WIDTH
    mix = (_dot(f_ref[...], w_ref[f0:a0, :]) + _dot(a_ref[...], w_ref[a0:r0, :])
           + _dot(r_ref[...], w_ref[r0:, :]))
    o_ref[...] = x_ref[...] + gate_ref[...] * mix


def _out_proj(f, a, r, x, gate, w, seg_rows):
    rows, d = x.shape
    tm = 512 if rows % 512 == 0 else 256
    row = lambda arr: pl.BlockSpec((tm, arr.shape[1]), lambda i: (i, 0))
    return pl.pallas_call(
        _outproj_body,
        grid=(rows // tm,),
        in_specs=[row(f), row(a), row(r), row(x),
                  pl.BlockSpec((None, 1, d), lambda i: (i * tm // seg_rows, 0, 0)),
                  _resident(w.shape)],
        out_specs=row(x),
        out_shape=jax.ShapeDtypeStruct(x.shape, F32),
        compiler_params=_params(("parallel",), 48),
        name="out_proj",
    )(f, a, r, x, gate, w)


def _ffn_body(x_ref, prev_ref, next_ref, g_ref, sc_ref, sh_ref, gate_ref, wa_ref, wg_ref,
              cwa_ref, cwg_ref, ba_ref, bg_ref, wd_ref, o_ref,
              h_scr, ua_scr, ug_scr, act_cur, act_new,
              *, tm, rows_lat, t_lat, t_ctx):
    j = pl.program_id(1)
    nj = pl.num_programs(1) - 1

    @pl.when(j == 0)
    def _():
        norm = lambda x: _modulated_rmsnorm(x, g_ref[...], sc_ref[...], sh_ref[...]).astype(BF16)
        x = x_ref[...]
        h_scr[0:HALO, :] = norm(prev_ref[...])
        h_scr[HALO:HALO + tm, :] = norm(x)
        h_scr[HALO + tm:, :] = norm(next_ref[...])
        act_new[...] = jnp.zeros_like(act_new)
        o_ref[...] = x

    @pl.when(j < nj)
    def _():
        act_cur[...] = act_new[...]
        h = h_scr[...]
        subs = [slice(s, s + FFN_SUB) for s in range(0, wa_ref.shape[1], FFN_SUB)]
        for cs in subs:
            ua_scr[:, cs] = _dot(h, wa_ref[:, cs])
            ug_scr[:, cs] = _dot(h, wg_ref[:, cs])
        not_first, not_last = _boundary_masks(pl.program_id(0) * tm, tm, rows_lat, t_lat, t_ctx)
        for cs in subs:
            ua = _conv3(ua_scr, cwa_ref, tm, not_first, not_last, cs) + ba_ref[:, cs]
            ug = _conv3(ug_scr, cwg_ref, tm, not_first, not_last, cs) + bg_ref[:, cs]
            act_new[:, cs] = (ug * jax.nn.sigmoid(ug) * ua).astype(BF16)
        o_ref[...] += gate_ref[...] * _dot(act_cur[...], wd_ref[...])

    @pl.when(j == nj)
    def _():
        o_ref[...] += gate_ref[...] * _dot(act_new[...], wd_ref[...])


def _conv_ffn(x, g, scale, shift, gate, wa, wg, cwa, cwg, ba, bg, wd, seg_rows, rows_lat, t_lat, t_ctx):
    rows, d = x.shape
    tm = next(t for t in (1024, 512, 256) if rows % t == 0 and rows_lat % t == 0 and seg_rows % t == 0)
    tf = FFN_TILE
    nh = tm // HALO
    last = rows // HALO - 1
    nj = D_FF_PAD // tf
    seg = lambda i, j: (i * tm // seg_rows, 0, 0)
    col = lambda r: pl.BlockSpec((r, tf), lambda i, j: (0, jnp.minimum(j, nj - 1)))
    return pl.pallas_call(
        functools.partial(_ffn_body, tm=tm, rows_lat=rows_lat, t_lat=t_lat, t_ctx=t_ctx),
        grid=(rows // tm, nj + 1),
        in_specs=[pl.BlockSpec((tm, d), lambda i, j: (i, 0), pipeline_mode=pl.Buffered(1)),
                  pl.BlockSpec((HALO, d), lambda i, j: (jnp.maximum(i * nh - 1, 0), 0)),
                  pl.BlockSpec((HALO, d), lambda i, j: (jnp.minimum((i + 1) * nh, last), 0)),
                  pl.BlockSpec((1, d), lambda i, j: (0, 0)),
                  pl.BlockSpec((None, 1, d), seg), pl.BlockSpec((None, 1, d), seg), pl.BlockSpec((None, 1, d), seg),
                  col(d), col(d), col(3), col(3), col(1), col(1),
                  pl.BlockSpec((tf, d), lambda i, j: (jnp.maximum(j - 1, 0), 0))],
        out_specs=pl.BlockSpec((tm, d), lambda i, j: (i, 0)),
        out_shape=jax.ShapeDtypeStruct(x.shape, F32),
        scratch_shapes=[pltpu.VMEM((tm + 2 * HALO, d), BF16),
                        pltpu.VMEM((tm + 2 * HALO, tf), F32),
                        pltpu.VMEM((tm + 2 * HALO, tf), F32),
                        pltpu.VMEM((tm, tf), BF16),
                        pltpu.VMEM((tm, tf), BF16)],
        compiler_params=_params(("parallel", "arbitrary"), 56),
        name="conv_ffn",
    )(x, x, x, g, scale, shift, gate, wa, wg, cwa, cwg, ba, bg, wd)


def _final_norm_body(x_ref, g_ref, o_ref):
    x = x_ref[...]
    ms = jnp.mean(x * x, axis=-1, keepdims=True)
    o_ref[...] = x * lax.rsqrt(ms + NORM_EPS) * g_ref[...]


def _final_norm(x, g, row0, nrows):
    tm = 256
    b0 = row0 // tm
    d = x.shape[1]
    return pl.pallas_call(
        _final_norm_body,
        grid=(nrows // tm,),
        in_specs=[pl.BlockSpec((tm, d), lambda i: (b0 + i, 0)), _resident((1, d))],
        out_specs=pl.BlockSpec((tm, d), lambda i: (i, 0)),
        out_shape=jax.ShapeDtypeStruct((nrows, d), F32),
        compiler_params=_params(("parallel",), 32),
        name="final_norm",
    )(x, g)


def _lora_weight(w2, a2, g2):
    c = RWKV_WIDTH
    wl = jnp.zeros((LORA_IN, 5 * c), F32)
    wl = wl.at[:DECAY_LORA, 0:c].set(w2[0]).at[:DECAY_LORA, c:2 * c].set(w2[1])
    wl = wl.at[DECAY_LORA:DECAY_LORA + ICLR_LORA, 2 * c:3 * c].set(a2[0])
    wl = wl.at[DECAY_LORA:DECAY_LORA + ICLR_LORA, 3 * c:4 * c].set(a2[1])
    return wl.at[DECAY_LORA + ICLR_LORA:, 4 * c:].set(g2)


def _pad_cols(w):
    return jnp.pad(w, ((0, 0), (0, D_FF_PAD - D_FF)))


def _forward(x_lat, x_ctx, cache_k, cache_v, state, c, c_ctx, w_ada, b_ada, norm1_g, norm2_g, w_in, w_out,
             q_norm_g, k_norm_g, rw_conv, rw_w0, rw_w2, rw_a0, rw_a2, rw_g2, rw_kk, rw_ka, rw_rk,
             rw_lnx_g, rw_lnx_b, ffn_up, ffn_conv_w, ffn_conv_b, ffn_down, final_norm_g):
    n_lat, t_lat, d = x_lat.shape
    n_ctx, t_ctx, _ = x_ctx.shape
    depth = w_ada.shape[0]
    past = cache_k.shape[2]
    rows_lat, rows_ctx = n_lat * t_lat, n_ctx * t_ctx
    assert rows_ctx % t_lat == 0 and n_lat + rows_ctx // t_lat <= MOD_ROWS
    assert t_lat % 256 == 0 and t_ctx % 128 == 0 and t_lat % GRID_W == 0

    x = jnp.concatenate([x_lat.reshape(rows_lat, d), x_ctx.reshape(rows_ctx, d)], axis=0)
    cvec = jnp.concatenate([c, jnp.broadcast_to(c_ctx[None, :], (MOD_ROWS - n_lat, d))], axis=0)
    mod = _adaln_mod(cvec, w_ada, b_ada).reshape(depth, MOD_ROWS, 6, 1, d)

    rope_tabs = _rope_tables(t_lat)
    time_lat, chan_tab = _dft_tables(t_lat)
    time_ctx, _ = _dft_tables(t_ctx)
    zero_state = jnp.zeros((n_ctx, 2, RWKV_HEADS, RWKV_N, RWKV_N), F32)
    row2 = lambda a: a.reshape(1, -1)

    new_k, new_v, new_s = [], [], []
    for l in range(depth):
        shift1, scale1, gate1, shift2, scale2, gate2 = (mod[l, :, i] for i in range(6))
        f, q, k, v, rw = _in_proj(x, row2(norm1_g[l]), scale1, shift1, w_in[l].astype(BF16), t_lat)

        f_lat = _fourier_mix(f, time_lat, chan_tab, 0, n_lat, t_lat)
        f_ctx = _fourier_mix(f, time_ctx, chan_tab, rows_lat, n_ctx, t_ctx)

        gq, gk = row2(q_norm_g[l]), row2(k_norm_g[l])
        q_lat, k_lat, v_lat = _qkv_prep(q, k, v, gq, gk, 0, rows_lat, rope_tabs)
        q_c, k_c, v_c, k_norm = _qkv_prep(q, k, v, gq, gk, rows_lat, rows_ctx)
        a_lat = _attention(q_lat, k_lat, v_lat, n_lat, t_lat,
                           cache_k[:, l].reshape(n_lat, past, KV_WIDTH), cache_v[:, l].reshape(n_lat, past, KV_WIDTH))
        a_ctx = _attention(q_c, k_c, v_c, n_ctx, t_ctx)
        new_k.append(k_norm.reshape(n_ctx, t_ctx, ATTN_KV_HEADS, HEAD_DIM))
        new_v.append(v[rows_lat:].reshape(n_ctx, t_ctx, ATTN_KV_HEADS, HEAD_DIM))

        r_, v_, kk, g_, bonus, lw, km, b_ = _rwkv_prep(
            rw, rw_conv[l], _lora_weight(rw_w2[l], rw_a2[l], rw_g2[l]), rw_w0[l], rw_a0[l],
            row2(rw_kk[l]), row2(rw_ka[l]), row2(rw_rk[l]), rows_lat, t_lat, t_ctx)
        yf_lat, yb_lat, _ = _rwkv_scan(r_, v_, kk, lw, km, b_, state[:, l], 0, n_lat, t_lat)
        yf_ctx, yb_ctx, s_ctx = _rwkv_scan(r_, v_, kk, lw, km, b_, zero_state, rows_lat, n_ctx, t_ctx)
        ln_g, ln_b = row2(rw_lnx_g[l]), row2(rw_lnx_b[l])
        r_lat = _rwkv_out(yf_lat, yb_lat, bonus, g_, 0, ln_g, ln_b)
        r_ctx = _rwkv_out(yf_ctx, yb_ctx, bonus, g_, rows_lat, ln_g, ln_b)
        new_s.append(s_ctx)

        cat = lambda a, b: jnp.concatenate([a, b], axis=0)
        x = _out_proj(cat(f_lat, f_ctx), cat(a_lat, a_ctx), cat(r_lat, r_ctx), x, gate1,
                      w_out[l].astype(BF16), t_lat)

        up = ffn_up[l]
        x = _conv_ffn(x, row2(norm2_g[l]), scale2, shift2, gate2,
                      _pad_cols(up[:, :D_FF]).astype(BF16), _pad_cols(up[:, D_FF:]).astype(BF16),
                      _pad_cols(ffn_conv_w[l][:, :D_FF]), _pad_cols(ffn_conv_w[l][:, D_FF:]),
                      _pad_cols(row2(ffn_conv_b[l][:D_FF])), _pad_cols(row2(ffn_conv_b[l][D_FF:])),
                      jnp.pad(ffn_down[l], ((0, D_FF_PAD - D_FF), (0, 0))).astype(BF16),
                      t_lat, rows_lat, t_lat, t_ctx)

    fg = row2(final_norm_g)
    y_lat = _final_norm(x, fg, 0, rows_lat).reshape(n_lat, t_lat, d)
    y_ctx = _final_norm(x, fg, rows_lat, rows_ctx).reshape(n_ctx, t_ctx, d)
    return (y_ctx, y_lat, jnp.stack(new_k, axis=1), jnp.stack(new_v, axis=1), jnp.stack(new_s, axis=1))


def kernel(x_prompt, x_sample, cache_attn_k, cache_attn_v, state_rwkv, c, c_ctx, w_ada, b_ada, norm1_g, norm2_g, w_in, w_out, q_norm_g, k_norm_g, rw_conv, rw_w0, rw_w2, rw_a0, rw_a2, rw_g2, rw_kk, rw_ka, rw_rk, rw_lnx_g, rw_lnx_b, ffn_up, ffn_conv_w, ffn_conv_b, ffn_down, final_norm_g):
    return _forward(x_sample, x_prompt, cache_attn_k, cache_attn_v, state_rwkv, c, c_ctx, w_ada, b_ada,
                    norm1_g, norm2_g, w_in, w_out, q_norm_g, k_norm_g, rw_conv, rw_w0, rw_w2, rw_a0, rw_a2,
                    rw_g2, rw_kk, rw_ka, rw_rk, rw_lnx_g, rw_lnx_b, ffn_up, ffn_conv_w, ffn_conv_b, ffn_down,
                    final_norm_g)
```

```python
import functools
import math

import jax
import jax.numpy as jnp
import numpy as np
from jax import lax
from jax.experimental import pallas as pl
from jax.experimental.pallas import tpu as pltpu

D_MODEL = 2048
GRID_W = 64
HEAD_DIM = 128
ATTN_HEADS = 8
ATTN_KV_HEADS = 2
KV_GROUP = ATTN_HEADS // ATTN_KV_HEADS
ATTN_WIDTH = ATTN_HEADS * HEAD_DIM
KV_WIDTH = ATTN_KV_HEADS * HEAD_DIM
FOURIER_WIDTH = 512
FOURIER_GROUPS = 4
FOURIER_GROUP_WIDTH = FOURIER_WIDTH // FOURIER_GROUPS
RWKV_WIDTH = 512
RWKV_N = 64
RWKV_HEADS = RWKV_WIDTH // RWKV_N
DECAY_LORA = 64
ICLR_LORA = 64
GATE_LORA = 128
LORA_IN = DECAY_LORA + ICLR_LORA + GATE_LORA
RWKV_IN = 3 * RWKV_WIDTH + LORA_IN
IN_WIDTH = FOURIER_WIDTH + ATTN_WIDTH + 2 * KV_WIDTH + RWKV_IN
D_FF = 5504
ROPE_THETA = 10000.0
NORM_EPS = 1e-6
GN_EPS = 64e-5

MOD_ROWS = 16
FFN_TILE = 512
FFN_SUB = 256
D_FF_PAD = -(-D_FF // FFN_TILE) * FFN_TILE
HALO = 16
CHUNK = 64
MIB = 2 ** 20

F32 = jnp.float32
BF16 = jnp.bfloat16


def _params(semantics, vmem_mib):
    return pltpu.CompilerParams(dimension_semantics=semantics, vmem_limit_bytes=vmem_mib * MIB)


def _resident(shape):
    return pl.BlockSpec(shape, lambda *_: (0,) * len(shape), pipeline_mode=pl.Buffered(1))


def _dot(a, b):
    return jnp.dot(a, b, preferred_element_type=F32)


def _dot_nt(a, b):
    return lax.dot_general(a, b, (((1,), (1,)), ((), ())), preferred_element_type=F32)


def _dot_tn(a, b):
    return lax.dot_general(a, b, (((0,), (0,)), ((), ())), preferred_element_type=F32)


def _modulated_rmsnorm(x, g, scale, shift):
    ms = jnp.mean(x * x, axis=-1, keepdims=True)
    return (x * lax.rsqrt(ms + NORM_EPS) * g) * (1.0 + scale) + shift


def _mod_body(c_ref, w_ref, b_ref, o_ref):
    c = c_ref[...]
    s = (c * jax.nn.sigmoid(c)).astype(BF16)
    o_ref[...] = _dot(s, w_ref[...].astype(BF16)) + b_ref[...]


def _adaln_mod(cvec, w_ada, b_ada):
    depth, d, n = w_ada.shape
    tn = 1024
    return pl.pallas_call(
        _mod_body,
        grid=(depth, n // tn),
        in_specs=[pl.BlockSpec((MOD_ROWS, d), lambda l, j: (0, 0)),
                  pl.BlockSpec((None, d, tn), lambda l, j: (l, 0, j)),
                  pl.BlockSpec((None, 1, tn), lambda l, j: (l, 0, j))],
        out_specs=pl.BlockSpec((None, MOD_ROWS, tn), lambda l, j: (l, 0, j)),
        out_shape=jax.ShapeDtypeStruct((depth, MOD_ROWS, n), F32),
        compiler_params=_params(("parallel", "parallel"), 40),
        name="adaln_mod",
    )(cvec, w_ada, b_ada.reshape(depth, 1, n))


_IN_SPLITS = (("f", 0, FOURIER_WIDTH, BF16),
              ("q", FOURIER_WIDTH, ATTN_WIDTH, F32),
              ("k", FOURIER_WIDTH + ATTN_WIDTH, KV_WIDTH, F32),
              ("v", FOURIER_WIDTH + ATTN_WIDTH + KV_WIDTH, KV_WIDTH, F32),
              ("rw", FOURIER_WIDTH + ATTN_WIDTH + 2 * KV_WIDTH, RWKV_IN, F32))


def _inproj_body(x_ref, g_ref, sc_ref, sh_ref, w_ref, *o_refs):
    h = _modulated_rmsnorm(x_ref[...], g_ref[...], sc_ref[...], sh_ref[...]).astype(BF16)
    for o_ref, (_, start, width, _) in zip(o_refs, _IN_SPLITS):
        o_ref[...] = _dot(h, w_ref[:, start:start + width]).astype(o_ref.dtype)


def _in_proj(x, g, scale, shift, w, seg_rows):
    rows, d = x.shape
    tm = 512 if rows % 512 == 0 else 256
    seg = lambda i: (i * tm // seg_rows, 0, 0)
    return pl.pallas_call(
        _inproj_body,
        grid=(rows // tm,),
        in_specs=[pl.BlockSpec((tm, d), lambda i: (i, 0)),
                  _resident((1, d)),
                  pl.BlockSpec((None, 1, d), seg),
                  pl.BlockSpec((None, 1, d), seg),
                  _resident(w.shape)],
        out_specs=[pl.BlockSpec((tm, width), lambda i: (i, 0)) for _, _, width, _ in _IN_SPLITS],
        out_shape=[jax.ShapeDtypeStruct((rows, width), dt) for _, _, width, dt in _IN_SPLITS],
        compiler_params=_params(("parallel",), 56),
        name="in_proj",
    )(x, g, scale, shift, w)


def _dft_tables(t):
    def angles(n):
        i = np.arange(n, dtype=np.int64)
        return (2.0 * math.pi / n) * ((i[:, None] * i[None, :]) % n)
    at = angles(t)
    time_tab = np.concatenate([np.cos(at), -np.sin(at)], axis=1).astype(np.float32)
    ac = angles(FOURIER_GROUP_WIDTH)
    eye = np.eye(FOURIER_GROUPS)
    chan_tab = np.concatenate([np.kron(eye, np.cos(ac)), np.kron(eye, np.sin(ac))], axis=1).astype(np.float32)
    return jnp.asarray(time_tab).astype(BF16), jnp.asarray(chan_tab).astype(BF16)


def _fourier_body(u_ref, ct_ref, cc_ref, o_ref, ab_scr, *, t, norm):
    @pl.when(pl.program_id(1) == 0)
    def _():
        ab = _dot(u_ref[...], cc_ref[...])
        ab_scr[0:t, :] = ab[:, :FOURIER_WIDTH].astype(BF16)
        ab_scr[t:2 * t, :] = ab[:, FOURIER_WIDTH:].astype(BF16)
    o_ref[...] = (_dot(ct_ref[...], ab_scr[...]) * norm).astype(o_ref.dtype)


def _fourier_mix(u, time_tab, chan_tab, row0, nseq, t):
    tm = min(t, 512)
    nt = t // tm
    seq0 = row0 // t
    return pl.pallas_call(
        functools.partial(_fourier_body, t=t, norm=1.0 / math.sqrt(t * FOURIER_GROUP_WIDTH)),
        grid=(nseq, nt),
        in_specs=[pl.BlockSpec((t, FOURIER_WIDTH), lambda b, i: (seq0 + b, 0)),
                  pl.BlockSpec((tm, 2 * t), lambda b, i: (i, 0)),
                  _resident(chan_tab.shape)],
        out_specs=pl.BlockSpec((tm, FOURIER_WIDTH), lambda b, i: (b * nt + i, 0)),
        out_shape=jax.ShapeDtypeStruct((nseq * t, FOURIER_WIDTH), BF16),
        scratch_shapes=[pltpu.VMEM((2 * t, FOURIER_WIDTH), BF16)],
        compiler_params=_params(("parallel", "arbitrary"), 40),
        name=f"fourier_mix_t{t}",
    )(u, time_tab, chan_tab)


def _rope_tables(t):
    pos = jnp.arange(t, dtype=jnp.int32)
    rows = (pos // GRID_W).astype(F32)
    cols = (pos % GRID_W).astype(F32)
    half = HEAD_DIM // 2
    inv = 1.0 / (ROPE_THETA ** (jnp.arange(0, half, 2, dtype=F32) / half))
    def tab(p):
        ang = p[:, None] * inv[None, :]
        return (jnp.concatenate([jnp.cos(ang), jnp.cos(ang)], -1),
                jnp.concatenate([-jnp.sin(ang), jnp.sin(ang)], -1))
    cr, sr = tab(rows)
    cc, sc = tab(cols)
    return jnp.concatenate([cr, cc], -1), jnp.concatenate([sr, sc], -1)


def _qkv_body(*refs, rope):
    if rope:
        q_ref, k_ref, v_ref, gq_ref, gk_ref, cos_ref, sin_ref, qo_ref, ko_ref, vo_ref = refs
        cos, sin = cos_ref[...], sin_ref[...]
        lane = lax.broadcasted_iota(jnp.int32, (1, HEAD_DIM), 1)
        low = (lane % (HEAD_DIM // 2)) < (HEAD_DIM // 4)
    else:
        q_ref, k_ref, v_ref, gq_ref, gk_ref, qo_ref, ko_ref, vo_ref, kn_ref = refs

    def head_norm(xh, g):
        ms = jnp.mean(xh * xh, axis=-1, keepdims=True)
        return xh * lax.rsqrt(ms + NORM_EPS) * g

    def rotate(xh):
        partner = jnp.where(low, pltpu.roll(xh, HEAD_DIM - HEAD_DIM // 4, 1), pltpu.roll(xh, HEAD_DIM // 4, 1))
        return xh * cos + partner * sin

    scale = HEAD_DIM ** -0.5
    for h in range(ATTN_HEADS):
        sl = slice(h * HEAD_DIM, (h + 1) * HEAD_DIM)
        qh = head_norm(q_ref[:, sl], gq_ref[...])
        if rope:
            qh = rotate(qh)
        qo_ref[:, sl] = (qh * scale).astype(BF16)
    for j in range(ATTN_KV_HEADS):
        sl = slice(j * HEAD_DIM, (j + 1) * HEAD_DIM)
        kh = head_norm(k_ref[:, sl], gk_ref[...])
        if rope:
            ko_ref[:, sl] = rotate(kh).astype(BF16)
        else:
            kn_ref[:, sl] = kh
            ko_ref[:, sl] = kh.astype(BF16)
    vo_ref[...] = v_ref[...].astype(BF16)


def _qkv_prep(q, k, v, gq, gk, row0, nrows, rope_tabs=None):
    tt = 256
    b0 = row0 // tt
    rope = rope_tabs is not None
    row = lambda w: pl.BlockSpec((tt, w), lambda i: (b0 + i, 0))
    out = lambda w: pl.BlockSpec((tt, w), lambda i: (i, 0))
    in_specs = [row(ATTN_WIDTH), row(KV_WIDTH), row(KV_WIDTH), _resident((1, HEAD_DIM)), _resident((1, HEAD_DIM))]
    args = [q, k, v, gq, gk]
    out_specs = [out(ATTN_WIDTH), out(KV_WIDTH), out(KV_WIDTH)]
    out_shape = [jax.ShapeDtypeStruct((nrows, ATTN_WIDTH), BF16),
                 jax.ShapeDtypeStruct((nrows, KV_WIDTH), BF16),
                 jax.ShapeDtypeStruct((nrows, KV_WIDTH), BF16)]
    if rope:
        t = rope_tabs[0].shape[0]
        nt = t // tt
        tab = pl.BlockSpec((tt, HEAD_DIM), lambda i: (i % nt, 0))
        in_specs += [tab, tab]
        args += list(rope_tabs)
    else:
        out_specs.append(out(KV_WIDTH))
        out_shape.append(jax.ShapeDtypeStruct((nrows, KV_WIDTH), F32))
    return pl.pallas_call(
        functools.partial(_qkv_body, rope=rope),
        grid=(nrows // tt,),
        in_specs=in_specs, out_specs=out_specs, out_shape=out_shape,
        compiler_params=_params(("parallel",), 32),
        name="qkv_prep_rope" if rope else "qkv_prep",
    )(*args)


def _attn_body(*refs, cached):
    if cached:
        q_ref, k_ref, v_ref, kc_ref, vc_ref, o_ref = refs
    else:
        q_ref, k_ref, v_ref, o_ref = refs
    for j in range(ATTN_KV_HEADS):
        kv = slice(j * HEAD_DIM, (j + 1) * HEAD_DIM)
        kj, vj = k_ref[:, kv], v_ref[:, kv]
        if cached:
            kc, vc = kc_ref[:, kv].astype(BF16), vc_ref[:, kv].astype(BF16)
        for g in range(KV_GROUP):
            sl = slice((j * KV_GROUP + g) * HEAD_DIM, (j * KV_GROUP + g + 1) * HEAD_DIM)
            qh = q_ref[:, sl]
            s = _dot_nt(qh, kj)
            m = jnp.max(s, axis=-1, keepdims=True)
            if cached:
                sc = _dot_nt(qh, kc)
                m = jnp.maximum(m, jnp.max(sc, axis=-1, keepdims=True))
            p = jnp.exp(s - m)
            l = jnp.sum(p, axis=-1, keepdims=True)
            acc = _dot(p.astype(BF16), vj)
            if cached:
                pc = jnp.exp(sc - m)
                l = l + jnp.sum(pc, axis=-1, keepdims=True)
                acc = acc + _dot(pc.astype(BF16), vc)
            o_ref[:, sl] = (acc / l).astype(o_ref.dtype)


def _attention(q, k, v, nseq, t, cache_k=None, cache_v=None, layer=0):
    tq = 256
    nq = t // tq
    cached = cache_k is not None
    in_specs = [pl.BlockSpec((tq, ATTN_WIDTH), lambda b, i: (b * nq + i, 0)),
                pl.BlockSpec((t, KV_WIDTH), lambda b, i: (b, 0)),
                pl.BlockSpec((t, KV_WIDTH), lambda b, i: (b, 0))]
    args = [q, k, v]
    if cached:
        past = cache_k.shape[2]
        in_specs += [pl.BlockSpec((None, None, past, KV_WIDTH), lambda b, i: (b, layer, 0, 0))] * 2
        args += [cache_k, cache_v]
    return pl.pallas_call(
        functools.partial(_attn_body, cached=cached),
        grid=(nseq, nq),
        in_specs=in_specs,
        out_specs=pl.BlockSpec((tq, ATTN_WIDTH), lambda b, i: (b * nq + i, 0)),
        out_shape=jax.ShapeDtypeStruct((nseq * t, ATTN_WIDTH), BF16),
        compiler_params=_params(("parallel", "parallel"), 48),
        name="attention_cached" if cached else "attention",
    )(*args)


def _boundary_masks(row_start, tm, rows_lat, t_lat, t_ctx):
    in_lat = row_start < rows_lat
    length = jnp.where(in_lat, t_lat, t_ctx)
    base = jnp.where(in_lat, lax.rem(row_start, t_lat), lax.rem(row_start - rows_lat, t_ctx))
    pos = base + lax.broadcasted_iota(jnp.int32, (tm, 1), 0)
    first = pos == 0
    last = pos == length - 1
    for k in range(1, tm // min(t_lat, t_ctx) + 1):
        first = jnp.logical_or(first, pos == k * length)
        last = jnp.logical_or(last, pos == (k + 1) * length - 1)
    return 1.0 - first.astype(F32), 1.0 - last.astype(F32)


def _conv3(ext_ref, w_ref, tm, not_first, not_last, cols=slice(None)):
    return (w_ref[0:1, cols] * (ext_ref[HALO - 1:HALO - 1 + tm, cols] * not_first)
            + w_ref[1:2, cols] * ext_ref[HALO:HALO + tm, cols]
            + w_ref[2:3, cols] * (ext_ref[HALO + 1:HALO + 1 + tm, cols] * not_last))


def _head_sum_matrix():
    i = lax.broadcasted_iota(jnp.int32, (RWKV_WIDTH, RWKV_WIDTH), 0) // RWKV_N
    j = lax.broadcasted_iota(jnp.int32, (RWKV_WIDTH, RWKV_WIDTH), 1) // RWKV_N
    return (i == j).astype(BF16)


def _bf16_pieces(x, n):
    pieces = []
    for _ in range(n - 1):
        p = x.astype(BF16)
        pieces.append(p)
        x = x - p.astype(F32)
    return pieces + [x.astype(BF16)]


def _head_sums(x, ones_bd):
    hi, lo = _bf16_pieces(x, 2)
    return _dot(hi, ones_bd) + _dot(lo, ones_bd)


def _rwkv_prep_body(rw_ref, prev_ref, next_ref, cw_ref, wl_ref, w0_ref, a0_ref, kks_ref, ka_ref, rk_ref,
                    r_ref, v_ref, kk_ref, g_ref, bonus_ref, lw_ref, km_ref, b_ref, ext_scr,
                    *, tm, rows_lat, t_lat, t_ctx):
    ext_scr[0:HALO, :] = prev_ref[...]
    ext_scr[HALO:HALO + tm, :] = rw_ref[...]
    ext_scr[HALO + tm:, :] = next_ref[...]
    not_first, not_last = _boundary_masks(pl.program_id(0) * tm, tm, rows_lat, t_lat, t_ctx)
    z = _conv3(ext_scr, cw_ref, tm, not_first, not_last)
    c = RWKV_WIDTH
    r, k, v = z[:, :c], z[:, c:2 * c], z[:, 2 * c:3 * c]
    zl = z[:, 3 * c:]
    lane = lax.broadcasted_iota(jnp.int32, (1, LORA_IN), 1)
    lora_in = jnp.where(lane < DECAY_LORA, jnp.tanh(zl),
                        jnp.where(lane < DECAY_LORA + ICLR_LORA, zl, jax.nn.sigmoid(zl)))
    lora = _dot(lora_in.astype(BF16), wl_ref[...])
    ones_bd = _head_sum_matrix()
    kk = k * kks_ref[...]
    kk = kk * lax.rsqrt(_head_sums(kk * kk, ones_bd) + 1e-12)
    r_ref[...] = r
    v_ref[...] = v
    kk_ref[...] = kk
    g_ref[...] = lora[:, 4 * c:5 * c]
    kmod_sum = jnp.zeros_like(k)
    for d in range(2):
        wpre = w0_ref[d:d + 1, :] + lora[:, d * c:(d + 1) * c]
        w = -jax.nn.softplus(-wpre) - 0.5
        lw_ref[d] = -jnp.exp(w)
        a = jax.nn.sigmoid(a0_ref[d:d + 1, :] + lora[:, (2 + d) * c:(3 + d) * c])
        kmod = k * (1.0 + (a - 1.0) * ka_ref[...])
        km_ref[d] = kmod
        b_ref[d] = kk * a
        kmod_sum = kmod_sum + kmod
    bonus_ref[...] = _head_sums(r * kmod_sum * rk_ref[...], ones_bd) * v


def _rwkv_prep(rw, conv_w, lora_w, w0, a0, kk_scale, ka, rk, rows_lat, t_lat, t_ctx):
    rows = rw.shape[0]
    tm = 256
    nh = tm // HALO
    last = rows // HALO - 1
    c = RWKV_WIDTH
    one = lambda: pl.BlockSpec((tm, c), lambda i: (i, 0))
    two = lambda: pl.BlockSpec((2, tm, c), lambda i: (0, i, 0))
    return pl.pallas_call(
        functools.partial(_rwkv_prep_body, tm=tm, rows_lat=rows_lat, t_lat=t_lat, t_ctx=t_ctx),
        grid=(rows // tm,),
        in_specs=[pl.BlockSpec((tm, RWKV_IN), lambda i: (i, 0)),
                  pl.BlockSpec((HALO, RWKV_IN), lambda i: (jnp.maximum(i * nh - 1, 0), 0)),
                  pl.BlockSpec((HALO, RWKV_IN), lambda i: (jnp.minimum((i + 1) * nh, last), 0)),
                  _resident(conv_w.shape), _resident(lora_w.shape), _resident(w0.shape), _resident(a0.shape),
                  _resident(kk_scale.shape), _resident(ka.shape), _resident(rk.shape)],
        out_specs=[one(), one(), one(), one(), one(), two(), two(), two()],
        out_shape=[jax.ShapeDtypeStruct((rows, c), F32)] * 5 + [jax.ShapeDtypeStruct((2, rows, c), F32)] * 3,
        scratch_shapes=[pltpu.VMEM((tm + 2 * HALO, RWKV_IN), F32)],
        compiler_params=_params(("parallel",), 48),
        name="rwkv_prep",
    )(rw, rw, rw, conv_w, lora_w, w0, a0, kk_scale, ka, rk)


def _scan_direction_operands(r_ref, v_ref, kk_ref, lw_ref, km_ref, b_ref, backward):
    n = CHUNK
    row = lax.broadcasted_iota(jnp.int32, (n, n), 0)
    col = lax.broadcasted_iota(jnp.int32, (n, n), 1)
    upto = (col >= row) if backward else (col <= row)
    lw = lw_ref[...]
    tri = upto.astype(BF16)
    cs = sum(_dot(tri, piece) for piece in _bf16_pieces(lw, 3))
    tot = jnp.sum(lw, axis=0, keepdims=True)
    grow = jnp.exp(-cs)
    to_end = jnp.exp(tot - cs)
    b = b_ref[...]
    km = km_ref[...]
    return dict(
        kkt=(kk_ref[...] * jnp.exp(cs - lw)).astype(BF16),
        rt=(r_ref[...] * jnp.exp(cs)).astype(BF16),
        bt=(b * grow).astype(BF16), kt=(km * grow).astype(BF16),
        bh=(b * to_end).astype(BF16), kh=(km * to_end).astype(BF16),
        vb=v_ref[...].astype(BF16), w_all=jnp.exp(tot))


def _rwkv_scan_body(rf_ref, vf_ref, kkf_ref, lwf_ref, kmf_ref, bf_ref,
                    rb_ref, vb_ref, kkb_ref, lwb_ref, kmb_ref, bb_ref, s0_ref,
                    yf_ref, yb_ref, sfin_ref, s_scr):
    c = pl.program_id(1)

    @pl.when(c == 0)
    def _():
        s_scr[...] = s0_ref[...]

    n = CHUNK
    row = lax.broadcasted_iota(jnp.int32, (2 * n, 2 * n), 0)
    col = lax.broadcasted_iota(jnp.int32, (2 * n, 2 * n), 1)
    t_idx, s_idx, read_rows = row % n, col % n, row >= n
    eye = (lax.broadcasted_iota(jnp.int32, (n, n), 0) == lax.broadcasted_iota(jnp.int32, (n, n), 1)).astype(F32)
    ops = (_scan_direction_operands(rf_ref, vf_ref, kkf_ref, lwf_ref, kmf_ref, bf_ref, False),
           _scan_direction_operands(rb_ref, vb_ref, kkb_ref, lwb_ref, kmb_ref, bb_ref, True))
    same_step = jnp.logical_and(read_rows, s_idx == t_idx)
    masks = (jnp.logical_or(s_idx < t_idx, same_step), jnp.logical_or(s_idx > t_idx, same_step))
    y_refs = (yf_ref, yb_ref)
    chains = [(d, h) for d in range(2) for h in range(RWKV_HEADS)]
    head = lambda d, name, h: ops[d][name][:, h * RWKV_N:(h + 1) * RWKV_N]
    cat = lambda a, b: jnp.concatenate([a, b], axis=0)

    lhs = [cat(head(d, "kkt", h), head(d, "rt", h)) for d, h in chains]
    coef = [jnp.where(masks[d], _dot_nt(l, cat(head(d, "bt", h), head(d, "kt", h))), 0.0)
            for l, (d, h) in zip(lhs, chains)]
    n_mat = [a[:n, :n] for a in coef]
    state = [s_scr[d, h] for d, h in chains]
    state_b = [s.astype(BF16) for s in state]
    read = [_dot_nt(l, sb) for l, sb in zip(lhs, state_b)]
    akv = [_dot(a[:n, n:].astype(BF16), head(d, "vb", h)) for a, (d, h) in zip(coef, chains)]

    x = [eye - m for m in n_mat]
    p = [_dot(m.astype(BF16), m.astype(BF16)) for m in n_mat]
    steps = int(math.log2(n)) - 1
    for s in range(steps):
        pb = [q.astype(BF16) for q in p]
        x = [xi + _dot(xi.astype(BF16), q) for xi, q in zip(x, pb)]
        if s + 1 < steps:
            p = [_dot(q, q) for q in pb]

    u = [_dot(xi.astype(BF16), (-(rd[:n] + ak)).astype(BF16)) for xi, rd, ak in zip(x, read, akv)]
    uv = [cat(ui.astype(BF16), head(d, "vb", h)) for ui, (d, h) in zip(u, chains)]
    for (d, h), a, rd, uvi, s in zip(chains, coef, read, uv, state):
        sl = slice(h * RWKV_N, (h + 1) * RWKV_N)
        y_refs[d][:, sl] = rd[n:] + _dot(a[n:].astype(BF16), uvi)
        s_scr[d, h] = s * ops[d]["w_all"][:, sl] + _dot_tn(uvi, cat(head(d, "bh", h), head(d, "kh", h)))

    @pl.when(c == pl.num_programs(1) - 1)
    def _():
        sfin_ref[...] = s_scr[...]


def _rwkv_scan(r, v, kk, lw, km, b, states, layer, row0, nseq, t):
    n = CHUNK
    nc = t // n
    c0 = row0 // n
    c = RWKV_WIDTH
    fwd = lambda bb, cc: c0 + bb * nc + cc
    bwd = lambda bb, cc: c0 + bb * nc + nc - 1 - cc
    one = lambda chunk: pl.BlockSpec((n, c), lambda bb, cc: (chunk(bb, cc), 0))
    two = lambda chunk, d: pl.BlockSpec((None, n, c), lambda bb, cc: (d, chunk(bb, cc), 0))
    state = pl.BlockSpec((None, 2, RWKV_HEADS, RWKV_N, RWKV_N), lambda bb, cc: (bb, 0, 0, 0, 0))
    state_in = pl.BlockSpec((None, None, 2, RWKV_HEADS, RWKV_N, RWKV_N), lambda bb, cc: (bb, layer, 0, 0, 0, 0))
    y_shape = jax.ShapeDtypeStruct((nseq * t, c), F32)
    return pl.pallas_call(
        _rwkv_scan_body,
        grid=(nseq, nc),
        in_specs=[one(fwd), one(fwd), one(fwd), two(fwd, 0), two(fwd, 0), two(fwd, 0),
                  one(bwd), one(bwd), one(bwd), two(bwd, 1), two(bwd, 1), two(bwd, 1), state_in],
        out_specs=[pl.BlockSpec((n, c), lambda bb, cc: (bb * nc + cc, 0)),
                   pl.BlockSpec((n, c), lambda bb, cc: (bb * nc + nc - 1 - cc, 0)), state],
        out_shape=[y_shape, y_shape, jax.ShapeDtypeStruct((nseq, 2, RWKV_HEADS, RWKV_N, RWKV_N), F32)],
        scratch_shapes=[pltpu.VMEM((2, RWKV_HEADS, RWKV_N, RWKV_N), F32)],
        compiler_params=_params(("parallel", "arbitrary"), 32),
        name=f"rwkv_scan_t{t}",
    )(r, v, kk, lw, km, b, r, v, kk, lw, km, b, states)


def _rwkv_out_body(yf_ref, yb_ref, bonus_ref, g_ref, lg_ref, lb_ref, o_ref):
    ones_bd = _head_sum_matrix()
    y = yf_ref[...] + yb_ref[...]
    mu = _head_sums(y, ones_bd) * (1.0 / RWKV_N)
    yc = y - mu
    var = _head_sums(yc * yc, ones_bd) * (1.0 / RWKV_N)
    yn = yc * lax.rsqrt(var + GN_EPS) * lg_ref[...] + lb_ref[...]
    o_ref[...] = ((yn + bonus_ref[...]) * g_ref[...]).astype(o_ref.dtype)


def _rwkv_out(yf, yb, bonus, g, row0, ln_g, ln_b):
    nrows = yf.shape[0]
    tm = 256
    b0 = row0 // tm
    c = RWKV_WIDTH
    src = pl.BlockSpec((tm, c), lambda i: (b0 + i, 0))
    return pl.pallas_call(
        _rwkv_out_body,
        grid=(nrows // tm,),
        in_specs=[pl.BlockSpec((tm, c), lambda i: (i, 0))] * 2 + [src, src, _resident((1, c)), _resident((1, c))],
        out_specs=pl.BlockSpec((tm, c), lambda i: (i, 0)),
        out_shape=jax.ShapeDtypeStruct((nrows, c), BF16),
        compiler_params=_params(("parallel",), 32),
        name="rwkv_out",
    )(yf, yb, bonus, g, ln_g, ln_b)


def _outproj_body(f_ref, a_ref, r_ref, x_ref, gate_ref, w_ref, o_ref):
    f0, a0, r0 = 0, FOURIER_WIDTH, FOURIER_WIDTH + ATTN_WIDTH
    mix = (_dot(f_ref[...], w_ref[f0:a0, :]) + _dot(a_ref[...], w_ref[a0:r0, :])
           + _dot(r_ref[...], w_ref[r0:, :]))
    o_ref[...] = x_ref[...] + gate_ref[...] * mix


def _out_proj(f, a, r, x, gate, w, seg_rows):
    rows, d = x.shape
    tm = 512 if rows % 512 == 0 else 256
    row = lambda arr: pl.BlockSpec((tm, arr.shape[1]), lambda i: (i, 0))
    return pl.pallas_call(
        _outproj_body,
        grid=(rows // tm,),
        in_specs=[row(f), row(a), row(r), row(x),
                  pl.BlockSpec((None, 1, d), lambda i: (i * tm // seg_rows, 0, 0)),
                  _resident(w.shape)],
        out_specs=row(x),
        out_shape=jax.ShapeDtypeStruct(x.shape, F32),
        compiler_params=_params(("parallel",), 48),
        name="out_proj",
    )(f, a, r, x, gate, w)


def _ffn_body(x_ref, prev_ref, next_ref, g_ref, sc_ref, sh_ref, gate_ref, wa_ref, wg_ref,
              cwa_ref, cwg_ref, ba_ref, bg_ref, wd_ref, o_ref,
              h_scr, ua_scr, ug_scr, act_cur, act_new,
              *, tm, rows_lat, t_lat, t_ctx):
    j = pl.program_id(1)
    nj = pl.num_programs(1) - 1

    @pl.when(j == 0)
    def _():
        norm = lambda x: _modulated_rmsnorm(x, g_ref[...], sc_ref[...], sh_ref[...]).astype(BF16)
        x = x_ref[...]
        h_scr[0:HALO, :] = norm(prev_ref[...])
        h_scr[HALO:HALO + tm, :] = norm(x)
        h_scr[HALO + tm:, :] = norm(next_ref[...])
        act_new[...] = jnp.zeros_like(act_new)
        o_ref[...] = x

    @pl.when(j < nj)
    def _():
        act_cur[...] = act_new[...]
        h = h_scr[...]
        subs = [slice(s, s + FFN_SUB) for s in range(0, wa_ref.shape[1], FFN_SUB)]
        for cs in subs:
            ua_scr[:, cs] = _dot(h, wa_ref[:, cs])
            ug_scr[:, cs] = _dot(h, wg_ref[:, cs])
        not_first, not_last = _boundary_masks(pl.program_id(0) * tm, tm, rows_lat, t_lat, t_ctx)
        for cs in subs:
            ua = _conv3(ua_scr, cwa_ref, tm, not_first, not_last, cs) + ba_ref[:, cs]
            ug = _conv3(ug_scr, cwg_ref, tm, not_first, not_last, cs) + bg_ref[:, cs]
            act_new[:, cs] = (ug * jax.nn.sigmoid(ug) * ua).astype(BF16)
        o_ref[...] += gate_ref[...] * _dot(act_cur[...], wd_ref[...])

    @pl.when(j == nj)
    def _():
        o_ref[...] += gate_ref[...] * _dot(act_new[...], wd_ref[...])


def _conv_ffn(x, g, scale, shift, gate, wa, wg, cwa, cwg, ba, bg, wd, seg_rows, rows_lat, t_lat, t_ctx):
    rows, d = x.shape
    tm = next(t for t in (1024, 512, 256) if rows % t == 0 and rows_lat % t == 0 and seg_rows % t == 0)
    tf = FFN_TILE
    nh = tm // HALO
    last = rows // HALO - 1
    nj = D_FF_PAD // tf
    seg = lambda i, j: (i * tm // seg_rows, 0, 0)
    col = lambda r: pl.BlockSpec((r, tf), lambda i, j: (0, jnp.minimum(j, nj - 1)))
    return pl.pallas_call(
        functools.partial(_ffn_body, tm=tm, rows_lat=rows_lat, t_lat=t_lat, t_ctx=t_ctx),
        grid=(rows // tm, nj + 1),
        in_specs=[pl.BlockSpec((tm, d), lambda i, j: (i, 0), pipeline_mode=pl.Buffered(1)),
                  pl.BlockSpec((HALO, d), lambda i, j: (jnp.maximum(i * nh - 1, 0), 0)),
                  pl.BlockSpec((HALO, d), lambda i, j: (jnp.minimum((i + 1) * nh, last), 0)),
                  pl.BlockSpec((1, d), lambda i, j: (0, 0)),
                  pl.BlockSpec((None, 1, d), seg), pl.BlockSpec((None, 1, d), seg), pl.BlockSpec((None, 1, d), seg),
                  col(d), col(d), col(3), col(3), col(1), col(1),
                  pl.BlockSpec((tf, d), lambda i, j: (jnp.maximum(j - 1, 0), 0))],
        out_specs=pl.BlockSpec((tm, d), lambda i, j: (i, 0)),
        out_shape=jax.ShapeDtypeStruct(x.shape, F32),
        scratch_shapes=[pltpu.VMEM((tm + 2 * HALO, d), BF16),
                        pltpu.VMEM((tm + 2 * HALO, tf), F32),
                        pltpu.VMEM((tm + 2 * HALO, tf), F32),
                        pltpu.VMEM((tm, tf), BF16),
                        pltpu.VMEM((tm, tf), BF16)],
        compiler_params=_params(("parallel", "arbitrary"), 56),
        name="conv_ffn",
    )(x, x, x, g, scale, shift, gate, wa, wg, cwa, cwg, ba, bg, wd)


def _final_norm_body(x_ref, g_ref, o_ref):
    x = x_ref[...]
    ms = jnp.mean(x * x, axis=-1, keepdims=True)
    o_ref[...] = x * lax.rsqrt(ms + NORM_EPS) * g_ref[...]


def _final_norm(x, g, row0, nrows):
    tm = 256
    b0 = row0 // tm
    d = x.shape[1]
    return pl.pallas_call(
        _final_norm_body,
        grid=(nrows // tm,),
        in_specs=[pl.BlockSpec((tm, d), lambda i: (b0 + i, 0)), _resident((1, d))],
        out_specs=pl.BlockSpec((tm, d), lambda i: (i, 0)),
        out_shape=jax.ShapeDtypeStruct((nrows, d), F32),
        compiler_params=_params(("parallel",), 32),
        name="final_norm",
    )(x, g)


def _lora_weight(w2, a2, g2):
    c = RWKV_WIDTH
    wl = jnp.zeros((LORA_IN, 5 * c), F32)
    wl = wl.at[:DECAY_LORA, 0:c].set(w2[0]).at[:DECAY_LORA, c:2 * c].set(w2[1])
    wl = wl.at[DECAY_LORA:DECAY_LORA + ICLR_LORA, 2 * c:3 * c].set(a2[0])
    wl = wl.at[DECAY_LORA:DECAY_LORA + ICLR_LORA, 3 * c:4 * c].set(a2[1])
    return wl.at[DECAY_LORA + ICLR_LORA:, 4 * c:].set(g2)


def _pad_cols(w):
    return jnp.pad(w, ((0, 0), (0, D_FF_PAD - D_FF)))


def _forward(x_lat, x_ctx, cache_k, cache_v, state, c, c_ctx, w_ada, b_ada, norm1_g, norm2_g, w_in, w_out,
             q_norm_g, k_norm_g, rw_conv, rw_w0, rw_w2, rw_a0, rw_a2, rw_g2, rw_kk, rw_ka, rw_rk,
             rw_lnx_g, rw_lnx_b, ffn_up, ffn_conv_w, ffn_conv_b, ffn_down, final_norm_g):
    n_lat, t_lat, d = x_lat.shape
    n_ctx, t_ctx, _ = x_ctx.shape
    depth = w_ada.shape[0]
    past = cache_k.shape[2]
    rows_lat, rows_ctx = n_lat * t_lat, n_ctx * t_ctx
    assert rows_ctx % t_lat == 0 and n_lat + rows_ctx // t_lat <= MOD_ROWS
    assert t_lat % 256 == 0 and t_ctx % 128 == 0 and t_lat % GRID_W == 0

    x = jnp.concatenate([x_lat.reshape(rows_lat, d), x_ctx.reshape(rows_ctx, d)], axis=0)
    cvec = jnp.concatenate([c, jnp.broadcast_to(c_ctx[None, :], (MOD_ROWS - n_lat, d))], axis=0)
    mod = _adaln_mod(cvec, w_ada, b_ada).reshape(depth, MOD_ROWS, 6, 1, d)

    rope_tabs = _rope_tables(t_lat)
    time_lat, chan_tab = _dft_tables(t_lat)
    time_ctx, _ = _dft_tables(t_ctx)
    zero_state = jnp.zeros((n_ctx, 1, 2, RWKV_HEADS, RWKV_N, RWKV_N), F32)
    row2 = lambda a: a.reshape(1, -1)

    new_k, new_v, new_s = [], [], []
    for l in range(depth):
        shift1, scale1, gate1, shift2, scale2, gate2 = (mod[l, :, i] for i in range(6))
        f, q, k, v, rw = _in_proj(x, row2(norm1_g[l]), scale1, shift1, w_in[l].astype(BF16), t_lat)

        f_lat = _fourier_mix(f, time_lat, chan_tab, 0, n_lat, t_lat)
        f_ctx = _fourier_mix(f, time_ctx, chan_tab, rows_lat, n_ctx, t_ctx)

        gq, gk = row2(q_norm_g[l]), row2(k_norm_g[l])
        q_lat, k_lat, v_lat = _qkv_prep(q, k, v, gq, gk, 0, rows_lat, rope_tabs)
        q_c, k_c, v_c, k_norm = _qkv_prep(q, k, v, gq, gk, rows_lat, rows_ctx)
        a_lat = _attention(q_lat, k_lat, v_lat, n_lat, t_lat,
                           cache_k.reshape(n_lat, depth, past, KV_WIDTH), cache_v.reshape(n_lat, depth, past, KV_WIDTH), l)
        a_ctx = _attention(q_c, k_c, v_c, n_ctx, t_ctx)
        new_k.append(k_norm.reshape(n_ctx, t_ctx, ATTN_KV_HEADS, HEAD_DIM))
        new_v.append(v[rows_lat:].reshape(n_ctx, t_ctx, ATTN_KV_HEADS, HEAD_DIM))

        r_, v_, kk, g_, bonus, lw, km, b_ = _rwkv_prep(
            rw, rw_conv[l], _lora_weight(rw_w2[l], rw_a2[l], rw_g2[l]).astype(BF16), rw_w0[l], rw_a0[l],
            row2(rw_kk[l]), row2(rw_ka[l]), row2(rw_rk[l]), rows_lat, t_lat, t_ctx)
        yf_lat, yb_lat, _ = _rwkv_scan(r_, v_, kk, lw, km, b_, state, l, 0, n_lat, t_lat)
        yf_ctx, yb_ctx, s_ctx = _rwkv_scan(r_, v_, kk, lw, km, b_, zero_state, 0, rows_lat, n_ctx, t_ctx)
        ln_g, ln_b = row2(rw_lnx_g[l]), row2(rw_lnx_b[l])
        r_lat = _rwkv_out(yf_lat, yb_lat, bonus, g_, 0, ln_g, ln_b)
        r_ctx = _rwkv_out(yf_ctx, yb_ctx, bonus, g_, rows_lat, ln_g, ln_b)
        new_s.append(s_ctx)

        cat = lambda a, b: jnp.concatenate([a, b], axis=0)
        x = _out_proj(cat(f_lat, f_ctx), cat(a_lat, a_ctx), cat(r_lat, r_ctx), x, gate1,
                      w_out[l].astype(BF16), t_lat)

        up = ffn_up[l]
        x = _conv_ffn(x, row2(norm2_g[l]), scale2, shift2, gate2,
                      _pad_cols(up[:, :D_FF]).astype(BF16), _pad_cols(up[:, D_FF:]).astype(BF16),
                      _pad_cols(ffn_conv_w[l][:, :D_FF]), _pad_cols(ffn_conv_w[l][:, D_FF:]),
                      _pad_cols(row2(ffn_conv_b[l][:D_FF])), _pad_cols(row2(ffn_conv_b[l][D_FF:])),
                      jnp.pad(ffn_down[l], ((0, D_FF_PAD - D_FF), (0, 0))).astype(BF16),
                      t_lat, rows_lat, t_lat, t_ctx)

    fg = row2(final_norm_g)
    y_lat = _final_norm(x, fg, 0, rows_lat).reshape(n_lat, t_lat, d)
    y_ctx = _final_norm(x, fg, rows_lat, rows_ctx).reshape(n_ctx, t_ctx, d)
    return (y_ctx, y_lat, jnp.stack(new_k, axis=1), jnp.stack(new_v, axis=1), jnp.stack(new_s, axis=1))


def kernel(x_prompt, x_sample, cache_attn_k, cache_attn_v, state_rwkv, c, c_ctx, w_ada, b_ada, norm1_g, norm2_g, w_in, w_out, q_norm_g, k_norm_g, rw_conv, rw_w0, rw_w2, rw_a0, rw_a2, rw_g2, rw_kk, rw_ka, rw_rk, rw_lnx_g, rw_lnx_b, ffn_up, ffn_conv_w, ffn_conv_b, ffn_down, final_norm_g):
    return _forward(x_sample, x_prompt, cache_attn_k, cache_attn_v, state_rwkv, c, c_ctx, w_ada, b_ada,
                    norm1_g, norm2_g, w_in, w_out, q_norm_g, k_norm_g, rw_conv, rw_w0, rw_w2, rw_a0, rw_a2,
                    rw_g2, rw_kk, rw_ka, rw_rk, rw_lnx_g, rw_lnx_b, ffn_up, ffn_conv_w, ffn_conv_b, ffn_down,
                    final_norm_g)
```

```python
import functools
import math

import jax
import jax.numpy as jnp
import numpy as np
from jax import lax
from jax.experimental import pallas as pl
from jax.experimental.pallas import tpu as pltpu

D_MODEL = 2048
GRID_W = 64
HEAD_DIM = 128
ATTN_HEADS = 8
ATTN_KV_HEADS = 2
KV_GROUP = ATTN_HEADS // ATTN_KV_HEADS
ATTN_WIDTH = ATTN_HEADS * HEAD_DIM
KV_WIDTH = ATTN_KV_HEADS * HEAD_DIM
FOURIER_WIDTH = 512
FOURIER_GROUPS = 4
FOURIER_GROUP_WIDTH = FOURIER_WIDTH // FOURIER_GROUPS
RWKV_WIDTH = 512
RWKV_N = 64
RWKV_HEADS = RWKV_WIDTH // RWKV_N
DECAY_LORA = 64
ICLR_LORA = 64
GATE_LORA = 128
LORA_IN = DECAY_LORA + ICLR_LORA + GATE_LORA
RWKV_IN = 3 * RWKV_WIDTH + LORA_IN
IN_WIDTH = FOURIER_WIDTH + ATTN_WIDTH + 2 * KV_WIDTH + RWKV_IN
D_FF = 5504
ROPE_THETA = 10000.0
NORM_EPS = 1e-6
GN_EPS = 64e-5

MOD_ROWS = 16
FFN_TILE = 512
FFN_SUB = 256
D_FF_PAD = -(-D_FF // FFN_TILE) * FFN_TILE
HALO = 16
CHUNK = 64
SEQ_ALIGN = 256
MIB = 2 ** 20

F32 = jnp.float32
BF16 = jnp.bfloat16


def _params(semantics, vmem_mib):
    return pltpu.CompilerParams(dimension_semantics=semantics, vmem_limit_bytes=vmem_mib * MIB)


def _resident(shape):
    return pl.BlockSpec(shape, lambda *_: (0,) * len(shape), pipeline_mode=pl.Buffered(1))


def _dot(a, b):
    return jnp.dot(a, b, preferred_element_type=F32)


def _dot_nt(a, b):
    return lax.dot_general(a, b, (((1,), (1,)), ((), ())), preferred_element_type=F32)


def _dot_tn(a, b):
    return lax.dot_general(a, b, (((0,), (0,)), ((), ())), preferred_element_type=F32)


def _modulated_rmsnorm(x, g, scale, shift):
    ms = jnp.mean(x * x, axis=-1, keepdims=True)
    return (x * lax.rsqrt(ms + NORM_EPS) * g) * (1.0 + scale) + shift


def _mod_body(c_ref, w_ref, b_ref, o_ref):
    c = c_ref[...]
    s = (c * jax.nn.sigmoid(c)).astype(BF16)
    o_ref[...] = _dot(s, w_ref[...].astype(BF16)) + b_ref[...]


def _adaln_mod(cvec, w_ada, b_ada):
    depth, d, n = w_ada.shape
    tn = 1024
    return pl.pallas_call(
        _mod_body,
        grid=(depth, n // tn),
        in_specs=[pl.BlockSpec((MOD_ROWS, d), lambda l, j: (0, 0)),
                  pl.BlockSpec((None, d, tn), lambda l, j: (l, 0, j)),
                  pl.BlockSpec((None, 1, tn), lambda l, j: (l, 0, j))],
        out_specs=pl.BlockSpec((None, MOD_ROWS, tn), lambda l, j: (l, 0, j)),
        out_shape=jax.ShapeDtypeStruct((depth, MOD_ROWS, n), F32),
        compiler_params=_params(("parallel", "parallel"), 40),
        name="adaln_mod",
    )(cvec, w_ada, b_ada.reshape(depth, 1, n))


_IN_SPLITS = (("f", 0, FOURIER_WIDTH, BF16),
              ("q", FOURIER_WIDTH, ATTN_WIDTH, F32),
              ("k", FOURIER_WIDTH + ATTN_WIDTH, KV_WIDTH, F32),
              ("v", FOURIER_WIDTH + ATTN_WIDTH + KV_WIDTH, KV_WIDTH, F32),
              ("rw", FOURIER_WIDTH + ATTN_WIDTH + 2 * KV_WIDTH, RWKV_IN, F32))


def _inproj_body(x_ref, g_ref, sc_ref, sh_ref, w_ref, *o_refs):
    h = _modulated_rmsnorm(x_ref[...], g_ref[...], sc_ref[...], sh_ref[...]).astype(BF16)
    for o_ref, (_, start, width, _) in zip(o_refs, _IN_SPLITS):
        o_ref[...] = _dot(h, w_ref[:, start:start + width]).astype(o_ref.dtype)


def _in_proj(x, g, scale, shift, w, seg_rows):
    rows, d = x.shape
    tm = 512 if rows % 512 == 0 else 256
    seg = lambda i: (i * tm // seg_rows, 0, 0)
    return pl.pallas_call(
        _inproj_body,
        grid=(rows // tm,),
        in_specs=[pl.BlockSpec((tm, d), lambda i: (i, 0)),
                  _resident((1, d)),
                  pl.BlockSpec((None, 1, d), seg),
                  pl.BlockSpec((None, 1, d), seg),
                  _resident(w.shape)],
        out_specs=[pl.BlockSpec((tm, width), lambda i: (i, 0)) for _, _, width, _ in _IN_SPLITS],
        out_shape=[jax.ShapeDtypeStruct((rows, width), dt) for _, _, width, dt in _IN_SPLITS],
        compiler_params=_params(("parallel",), 56),
        name="in_proj",
    )(x, g, scale, shift, w)


def _dft_tables(t):
    def angles(n):
        i = np.arange(n, dtype=np.int64)
        return (2.0 * math.pi / n) * ((i[:, None] * i[None, :]) % n)
    at = angles(t)
    time_tab = np.concatenate([np.cos(at), -np.sin(at)], axis=1).astype(np.float32)
    ac = angles(FOURIER_GROUP_WIDTH)
    eye = np.eye(FOURIER_GROUPS)
    chan_tab = np.concatenate([np.kron(eye, np.cos(ac)), np.kron(eye, np.sin(ac))], axis=1).astype(np.float32)
    return jnp.asarray(time_tab).astype(BF16), jnp.asarray(chan_tab).astype(BF16)


def _fourier_body(u_ref, ct_ref, cc_ref, o_ref, ab_scr, *, t, norm):
    @pl.when(pl.program_id(1) == 0)
    def _():
        ab = _dot(u_ref[...], cc_ref[...])
        ab_scr[0:t, :] = ab[:, :FOURIER_WIDTH].astype(BF16)
        ab_scr[t:2 * t, :] = ab[:, FOURIER_WIDTH:].astype(BF16)
    o_ref[...] = (_dot(ct_ref[...], ab_scr[...]) * norm).astype(o_ref.dtype)


def _fourier_mix(u, time_tab, chan_tab, row0, nseq, t):
    tm = min(t, 512)
    nt = t // tm
    seq0 = row0 // t
    return pl.pallas_call(
        functools.partial(_fourier_body, t=t, norm=1.0 / math.sqrt(t * FOURIER_GROUP_WIDTH)),
        grid=(nseq, nt),
        in_specs=[pl.BlockSpec((t, FOURIER_WIDTH), lambda b, i: (seq0 + b, 0)),
                  pl.BlockSpec((tm, 2 * t), lambda b, i: (i, 0)),
                  _resident(chan_tab.shape)],
        out_specs=pl.BlockSpec((tm, FOURIER_WIDTH), lambda b, i: (b * nt + i, 0)),
        out_shape=jax.ShapeDtypeStruct((nseq * t, FOURIER_WIDTH), BF16),
        scratch_shapes=[pltpu.VMEM((2 * t, FOURIER_WIDTH), BF16)],
        compiler_params=_params(("parallel", "arbitrary"), 40),
        name=f"fourier_mix_t{t}",
    )(u, time_tab, chan_tab)


def _rope_tables(t):
    pos = jnp.arange(t, dtype=jnp.int32)
    rows = (pos // GRID_W).astype(F32)
    cols = (pos % GRID_W).astype(F32)
    half = HEAD_DIM // 2
    inv = 1.0 / (ROPE_THETA ** (jnp.arange(0, half, 2, dtype=F32) / half))
    def tab(p):
        ang = p[:, None] * inv[None, :]
        return (jnp.concatenate([jnp.cos(ang), jnp.cos(ang)], -1),
                jnp.concatenate([-jnp.sin(ang), jnp.sin(ang)], -1))
    cr, sr = tab(rows)
    cc, sc = tab(cols)
    return jnp.concatenate([cr, cc], -1), jnp.concatenate([sr, sc], -1)


def _qkv_body(*refs, rope):
    if rope:
        q_ref, k_ref, v_ref, gq_ref, gk_ref, cos_ref, sin_ref, qo_ref, ko_ref, vo_ref = refs
        cos, sin = cos_ref[...], sin_ref[...]
        lane = lax.broadcasted_iota(jnp.int32, (1, HEAD_DIM), 1)
        low = (lane % (HEAD_DIM // 2)) < (HEAD_DIM // 4)
    else:
        q_ref, k_ref, v_ref, gq_ref, gk_ref, qo_ref, ko_ref, vo_ref, kn_ref = refs

    def head_norm(xh, g):
        ms = jnp.mean(xh * xh, axis=-1, keepdims=True)
        return xh * lax.rsqrt(ms + NORM_EPS) * g

    def rotate(xh):
        partner = jnp.where(low, pltpu.roll(xh, HEAD_DIM - HEAD_DIM // 4, 1), pltpu.roll(xh, HEAD_DIM // 4, 1))
        return xh * cos + partner * sin

    scale = HEAD_DIM ** -0.5
    for h in range(ATTN_HEADS):
        sl = slice(h * HEAD_DIM, (h + 1) * HEAD_DIM)
        qh = head_norm(q_ref[:, sl], gq_ref[...])
        if rope:
            qh = rotate(qh)
        qo_ref[:, sl] = (qh * scale).astype(BF16)
    for j in range(ATTN_KV_HEADS):
        sl = slice(j * HEAD_DIM, (j + 1) * HEAD_DIM)
        kh = head_norm(k_ref[:, sl], gk_ref[...])
        if rope:
            ko_ref[:, sl] = rotate(kh).astype(BF16)
        else:
            kn_ref[:, sl] = kh
            ko_ref[:, sl] = kh.astype(BF16)
    vo_ref[...] = v_ref[...].astype(BF16)


def _qkv_prep(q, k, v, gq, gk, row0, nrows, rope_tabs=None):
    tt = 256
    b0 = row0 // tt
    rope = rope_tabs is not None
    row = lambda w: pl.BlockSpec((tt, w), lambda i: (b0 + i, 0))
    out = lambda w: pl.BlockSpec((tt, w), lambda i: (i, 0))
    in_specs = [row(ATTN_WIDTH), row(KV_WIDTH), row(KV_WIDTH), _resident((1, HEAD_DIM)), _resident((1, HEAD_DIM))]
    args = [q, k, v, gq, gk]
    out_specs = [out(ATTN_WIDTH), out(KV_WIDTH), out(KV_WIDTH)]
    out_shape = [jax.ShapeDtypeStruct((nrows, ATTN_WIDTH), BF16),
                 jax.ShapeDtypeStruct((nrows, KV_WIDTH), BF16),
                 jax.ShapeDtypeStruct((nrows, KV_WIDTH), BF16)]
    if rope:
        t = rope_tabs[0].shape[0]
        nt = t // tt
        tab = pl.BlockSpec((tt, HEAD_DIM), lambda i: (i % nt, 0))
        in_specs += [tab, tab]
        args += list(rope_tabs)
    else:
        out_specs.append(out(KV_WIDTH))
        out_shape.append(jax.ShapeDtypeStruct((nrows, KV_WIDTH), F32))
    return pl.pallas_call(
        functools.partial(_qkv_body, rope=rope),
        grid=(nrows // tt,),
        in_specs=in_specs, out_specs=out_specs, out_shape=out_shape,
        compiler_params=_params(("parallel",), 32),
        name="qkv_prep_rope" if rope else "qkv_prep",
    )(*args)


def _attn_body(*refs, cached):
    if cached:
        q_ref, k_ref, v_ref, kc_ref, vc_ref, o_ref = refs
    else:
        q_ref, k_ref, v_ref, o_ref = refs
    for j in range(ATTN_KV_HEADS):
        kv = slice(j * HEAD_DIM, (j + 1) * HEAD_DIM)
        kj, vj = k_ref[:, kv], v_ref[:, kv]
        if cached:
            kc, vc = kc_ref[:, kv].astype(BF16), vc_ref[:, kv].astype(BF16)
        for g in range(KV_GROUP):
            sl = slice((j * KV_GROUP + g) * HEAD_DIM, (j * KV_GROUP + g + 1) * HEAD_DIM)
            qh = q_ref[:, sl]
            s = _dot_nt(qh, kj)
            m = jnp.max(s, axis=-1, keepdims=True)
            if cached:
                sc = _dot_nt(qh, kc)
                m = jnp.maximum(m, jnp.max(sc, axis=-1, keepdims=True))
            p = jnp.exp(s - m)
            l = jnp.sum(p, axis=-1, keepdims=True)
            acc = _dot(p.astype(BF16), vj)
            if cached:
                pc = jnp.exp(sc - m)
                l = l + jnp.sum(pc, axis=-1, keepdims=True)
                acc = acc + _dot(pc.astype(BF16), vc)
            o_ref[:, sl] = (acc / l).astype(o_ref.dtype)


def _attention(q, k, v, nseq, t, cache_k=None, cache_v=None, layer=0):
    tq = 256
    nq = t // tq
    cached = cache_k is not None
    in_specs = [pl.BlockSpec((tq, ATTN_WIDTH), lambda b, i: (b * nq + i, 0)),
                pl.BlockSpec((t, KV_WIDTH), lambda b, i: (b, 0)),
                pl.BlockSpec((t, KV_WIDTH), lambda b, i: (b, 0))]
    args = [q, k, v]
    if cached:
        past = cache_k.shape[2]
        in_specs += [pl.BlockSpec((None, None, past, KV_WIDTH), lambda b, i: (b, layer, 0, 0))] * 2
        args += [cache_k, cache_v]
    return pl.pallas_call(
        functools.partial(_attn_body, cached=cached),
        grid=(nseq, nq),
        in_specs=in_specs,
        out_specs=pl.BlockSpec((tq, ATTN_WIDTH), lambda b, i: (b * nq + i, 0)),
        out_shape=jax.ShapeDtypeStruct((nseq * t, ATTN_WIDTH), BF16),
        compiler_params=_params(("parallel", "parallel"), 48),
        name="attention_cached" if cached else "attention",
    )(*args)


def _boundary_masks(row_start, tm, rows_lat, t_lat, t_ctx):
    in_lat = row_start < rows_lat
    length = jnp.where(in_lat, t_lat, t_ctx)
    base = jnp.where(in_lat, lax.rem(row_start, t_lat), lax.rem(row_start - rows_lat, t_ctx))
    pos = base + lax.broadcasted_iota(jnp.int32, (tm, 1), 0)
    first = pos == 0
    last = pos == length - 1
    for k in range(1, tm // min(t_lat, t_ctx) + 1):
        first = jnp.logical_or(first, pos == k * length)
        last = jnp.logical_or(last, pos == (k + 1) * length - 1)
    return 1.0 - first.astype(F32), 1.0 - last.astype(F32)


def _conv3(ext_ref, w_ref, tm, not_first, not_last, cols=slice(None)):
    return (w_ref[0:1, cols] * (ext_ref[HALO - 1:HALO - 1 + tm, cols] * not_first)
            + w_ref[1:2, cols] * ext_ref[HALO:HALO + tm, cols]
            + w_ref[2:3, cols] * (ext_ref[HALO + 1:HALO + 1 + tm, cols] * not_last))


def _head_sum_matrix():
    i = lax.broadcasted_iota(jnp.int32, (RWKV_WIDTH, RWKV_WIDTH), 0) // RWKV_N
    j = lax.broadcasted_iota(jnp.int32, (RWKV_WIDTH, RWKV_WIDTH), 1) // RWKV_N
    return (i == j).astype(BF16)


def _bf16_pieces(x, n):
    pieces = []
    for _ in range(n - 1):
        p = x.astype(BF16)
        pieces.append(p)
        x = x - p.astype(F32)
    return pieces + [x.astype(BF16)]


def _head_sums(x, ones_bd):
    hi, lo = _bf16_pieces(x, 2)
    return _dot(hi, ones_bd) + _dot(lo, ones_bd)


def _rwkv_prep_body(rw_ref, prev_ref, next_ref, cw_ref, wl_ref, w0_ref, a0_ref, kks_ref, ka_ref, rk_ref,
                    r_ref, v_ref, kk_ref, g_ref, bonus_ref, lw_ref, km_ref, b_ref, ext_scr,
                    *, tm, rows_lat, t_lat, t_ctx):
    ext_scr[0:HALO, :] = prev_ref[...]
    ext_scr[HALO:HALO + tm, :] = rw_ref[...]
    ext_scr[HALO + tm:, :] = next_ref[...]
    not_first, not_last = _boundary_masks(pl.program_id(0) * tm, tm, rows_lat, t_lat, t_ctx)
    z = _conv3(ext_scr, cw_ref, tm, not_first, not_last)
    c = RWKV_WIDTH
    r, k, v = z[:, :c], z[:, c:2 * c], z[:, 2 * c:3 * c]
    zl = z[:, 3 * c:]
    lane = lax.broadcasted_iota(jnp.int32, (1, LORA_IN), 1)
    lora_in = jnp.where(lane < DECAY_LORA, jnp.tanh(zl),
                        jnp.where(lane < DECAY_LORA + ICLR_LORA, zl, jax.nn.sigmoid(zl)))
    lora = _dot(lora_in.astype(BF16), wl_ref[...])
    ones_bd = _head_sum_matrix()
    kk = k * kks_ref[...]
    kk = kk * lax.rsqrt(_head_sums(kk * kk, ones_bd) + 1e-12)
    r_ref[...] = r
    v_ref[...] = v
    kk_ref[...] = kk
    g_ref[...] = lora[:, 4 * c:5 * c]
    kmod_sum = jnp.zeros_like(k)
    for d in range(2):
        wpre = w0_ref[d:d + 1, :] + lora[:, d * c:(d + 1) * c]
        w = -jax.nn.softplus(-wpre) - 0.5
        lw_ref[d] = -jnp.exp(w)
        a = jax.nn.sigmoid(a0_ref[d:d + 1, :] + lora[:, (2 + d) * c:(3 + d) * c])
        kmod = k * (1.0 + (a - 1.0) * ka_ref[...])
        km_ref[d] = kmod
        b_ref[d] = kk * a
        kmod_sum = kmod_sum + kmod
    bonus_ref[...] = _head_sums(r * kmod_sum * rk_ref[...], ones_bd) * v


def _rwkv_prep(rw, conv_w, lora_w, w0, a0, kk_scale, ka, rk, rows_lat, t_lat, t_ctx):
    rows = rw.shape[0]
    tm = 256
    nh = tm // HALO
    last = rows // HALO - 1
    c = RWKV_WIDTH
    one = lambda: pl.BlockSpec((tm, c), lambda i: (i, 0))
    two = lambda: pl.BlockSpec((2, tm, c), lambda i: (0, i, 0))
    return pl.pallas_call(
        functools.partial(_rwkv_prep_body, tm=tm, rows_lat=rows_lat, t_lat=t_lat, t_ctx=t_ctx),
        grid=(rows // tm,),
        in_specs=[pl.BlockSpec((tm, RWKV_IN), lambda i: (i, 0)),
                  pl.BlockSpec((HALO, RWKV_IN), lambda i: (jnp.maximum(i * nh - 1, 0), 0)),
                  pl.BlockSpec((HALO, RWKV_IN), lambda i: (jnp.minimum((i + 1) * nh, last), 0)),
                  _resident(conv_w.shape), _resident(lora_w.shape), _resident(w0.shape), _resident(a0.shape),
                  _resident(kk_scale.shape), _resident(ka.shape), _resident(rk.shape)],
        out_specs=[one(), one(), one(), one(), one(), two(), two(), two()],
        out_shape=[jax.ShapeDtypeStruct((rows, c), F32)] * 5 + [jax.ShapeDtypeStruct((2, rows, c), F32)] * 3,
        scratch_shapes=[pltpu.VMEM((tm + 2 * HALO, RWKV_IN), F32)],
        compiler_params=_params(("parallel",), 48),
        name="rwkv_prep",
    )(rw, rw, rw, conv_w, lora_w, w0, a0, kk_scale, ka, rk)


def _scan_direction_operands(r_ref, v_ref, kk_ref, lw_ref, km_ref, b_ref, backward):
    n = CHUNK
    row = lax.broadcasted_iota(jnp.int32, (n, n), 0)
    col = lax.broadcasted_iota(jnp.int32, (n, n), 1)
    upto = (col >= row) if backward else (col <= row)
    lw = lw_ref[...]
    tri = upto.astype(BF16)
    cs = sum(_dot(tri, piece) for piece in _bf16_pieces(lw, 3))
    tot = jnp.sum(lw, axis=0, keepdims=True)
    grow = jnp.exp(-cs)
    to_end = jnp.exp(tot - cs)
    b = b_ref[...]
    km = km_ref[...]
    return dict(
        kkt=(kk_ref[...] * jnp.exp(cs - lw)).astype(BF16),
        rt=(r_ref[...] * jnp.exp(cs)).astype(BF16),
        bt=(b * grow).astype(BF16), kt=(km * grow).astype(BF16),
        bh=(b * to_end).astype(BF16), kh=(km * to_end).astype(BF16),
        vb=v_ref[...].astype(BF16), w_all=jnp.exp(tot))


def _rwkv_scan_body(rf_ref, vf_ref, kkf_ref, lwf_ref, kmf_ref, bf_ref,
                    rb_ref, vb_ref, kkb_ref, lwb_ref, kmb_ref, bb_ref, s0_ref,
                    yf_ref, yb_ref, sfin_ref, s_scr):
    c = pl.program_id(1)

    @pl.when(c == 0)
    def _():
        s_scr[...] = s0_ref[...]

    n = CHUNK
    row = lax.broadcasted_iota(jnp.int32, (2 * n, 2 * n), 0)
    col = lax.broadcasted_iota(jnp.int32, (2 * n, 2 * n), 1)
    t_idx, s_idx, read_rows = row % n, col % n, row >= n
    eye = (lax.broadcasted_iota(jnp.int32, (n, n), 0) == lax.broadcasted_iota(jnp.int32, (n, n), 1)).astype(F32)
    ops = (_scan_direction_operands(rf_ref, vf_ref, kkf_ref, lwf_ref, kmf_ref, bf_ref, False),
           _scan_direction_operands(rb_ref, vb_ref, kkb_ref, lwb_ref, kmb_ref, bb_ref, True))
    same_step = jnp.logical_and(read_rows, s_idx == t_idx)
    masks = (jnp.logical_or(s_idx < t_idx, same_step), jnp.logical_or(s_idx > t_idx, same_step))
    y_refs = (yf_ref, yb_ref)
    chains = [(d, h) for d in range(2) for h in range(RWKV_HEADS)]
    head = lambda d, name, h: ops[d][name][:, h * RWKV_N:(h + 1) * RWKV_N]
    cat = lambda a, b: jnp.concatenate([a, b], axis=0)

    lhs = [cat(head(d, "kkt", h), head(d, "rt", h)) for d, h in chains]
    coef = [jnp.where(masks[d], _dot_nt(l, cat(head(d, "bt", h), head(d, "kt", h))), 0.0)
            for l, (d, h) in zip(lhs, chains)]
    n_mat = [a[:n, :n] for a in coef]
    state = [s_scr[d, h] for d, h in chains]
    state_b = [s.astype(BF16) for s in state]
    read = [_dot_nt(l, sb) for l, sb in zip(lhs, state_b)]
    akv = [_dot(a[:n, n:].astype(BF16), head(d, "vb", h)) for a, (d, h) in zip(coef, chains)]

    x = [eye - m for m in n_mat]
    p = [_dot(m.astype(BF16), m.astype(BF16)) for m in n_mat]
    steps = int(math.log2(n)) - 1
    for s in range(steps):
        pb = [q.astype(BF16) for q in p]
        x = [xi + _dot(xi.astype(BF16), q) for xi, q in zip(x, pb)]
        if s + 1 < steps:
            p = [_dot(q, q) for q in pb]

    u = [_dot(xi.astype(BF16), (-(rd[:n] + ak)).astype(BF16)) for xi, rd, ak in zip(x, read, akv)]
    uv = [cat(ui.astype(BF16), head(d, "vb", h)) for ui, (d, h) in zip(u, chains)]
    for (d, h), a, rd, uvi, s in zip(chains, coef, read, uv, state):
        sl = slice(h * RWKV_N, (h + 1) * RWKV_N)
        y_refs[d][:, sl] = rd[n:] + _dot(a[n:].astype(BF16), uvi)
        s_scr[d, h] = s * ops[d]["w_all"][:, sl] + _dot_tn(uvi, cat(head(d, "bh", h), head(d, "kh", h)))

    @pl.when(c == pl.num_programs(1) - 1)
    def _():
        sfin_ref[...] = s_scr[...]


def _rwkv_scan(r, v, kk, lw, km, b, states, layer, row0, nseq, t):
    n = CHUNK
    nc = t // n
    c0 = row0 // n
    c = RWKV_WIDTH
    fwd = lambda bb, cc: c0 + bb * nc + cc
    bwd = lambda bb, cc: c0 + bb * nc + nc - 1 - cc
    one = lambda chunk: pl.BlockSpec((n, c), lambda bb, cc: (chunk(bb, cc), 0))
    two = lambda chunk, d: pl.BlockSpec((None, n, c), lambda bb, cc: (d, chunk(bb, cc), 0))
    state = pl.BlockSpec((None, 2, RWKV_HEADS, RWKV_N, RWKV_N), lambda bb, cc: (bb, 0, 0, 0, 0))
    state_in = pl.BlockSpec((None, None, 2, RWKV_HEADS, RWKV_N, RWKV_N), lambda bb, cc: (bb, layer, 0, 0, 0, 0))
    y_shape = jax.ShapeDtypeStruct((nseq * t, c), F32)
    return pl.pallas_call(
        _rwkv_scan_body,
        grid=(nseq, nc),
        in_specs=[one(fwd), one(fwd), one(fwd), two(fwd, 0), two(fwd, 0), two(fwd, 0),
                  one(bwd), one(bwd), one(bwd), two(bwd, 1), two(bwd, 1), two(bwd, 1), state_in],
        out_specs=[pl.BlockSpec((n, c), lambda bb, cc: (bb * nc + cc, 0)),
                   pl.BlockSpec((n, c), lambda bb, cc: (bb * nc + nc - 1 - cc, 0)), state],
        out_shape=[y_shape, y_shape, jax.ShapeDtypeStruct((nseq, 2, RWKV_HEADS, RWKV_N, RWKV_N), F32)],
        scratch_shapes=[pltpu.VMEM((2, RWKV_HEADS, RWKV_N, RWKV_N), F32)],
        compiler_params=_params(("parallel", "arbitrary"), 32),
        name=f"rwkv_scan_t{t}",
    )(r, v, kk, lw, km, b, r, v, kk, lw, km, b, states)


def _rwkv_out_body(yf_ref, yb_ref, bonus_ref, g_ref, lg_ref, lb_ref, o_ref):
    ones_bd = _head_sum_matrix()
    y = yf_ref[...] + yb_ref[...]
    mu = _head_sums(y, ones_bd) * (1.0 / RWKV_N)
    yc = y - mu
    var = _head_sums(yc * yc, ones_bd) * (1.0 / RWKV_N)
    yn = yc * lax.rsqrt(var + GN_EPS) * lg_ref[...] + lb_ref[...]
    o_ref[...] = ((yn + bonus_ref[...]) * g_ref[...]).astype(o_ref.dtype)


def _rwkv_out(yf, yb, bonus, g, row0, ln_g, ln_b):
    nrows = yf.shape[0]
    tm = 256
    b0 = row0 // tm
    c = RWKV_WIDTH
    src = pl.BlockSpec((tm, c), lambda i: (b0 + i, 0))
    return pl.pallas_call(
        _rwkv_out_body,
        grid=(nrows // tm,),
        in_specs=[pl.BlockSpec((tm, c), lambda i: (i, 0))] * 2 + [src, src, _resident((1, c)), _resident((1, c))],
        out_specs=pl.BlockSpec((tm, c), lambda i: (i, 0)),
        out_shape=jax.ShapeDtypeStruct((nrows, c), BF16),
        compiler_params=_params(("parallel",), 32),
        name="rwkv_out",
    )(yf, yb, bonus, g, ln_g, ln_b)


def _outproj_body(f_ref, a_ref, r_ref, x_ref, gate_ref, w_ref, o_ref):
    f0, a0, r0 = 0, FOURIER_WIDTH, FOURIER_WIDTH + ATTN_WIDTH
    mix = (_dot(f_ref[...], w_ref[f0:a0, :]) + _dot(a_ref[...], w_ref[a0:r0, :])
           + _dot(r_ref[...], w_ref[r0:, :]))
    o_ref[...] = x_ref[...] + gate_ref[...] * mix


def _out_proj(f, a, r, x, gate, w, seg_rows):
    rows, d = x.shape
    tm = 512 if rows % 512 == 0 else 256
    row = lambda arr: pl.BlockSpec((tm, arr.shape[1]), lambda i: (i, 0))
    return pl.pallas_call(
        _outproj_body,
        grid=(rows // tm,),
        in_specs=[row(f), row(a), row(r), row(x),
                  pl.BlockSpec((None, 1, d), lambda i: (i * tm // seg_rows, 0, 0)),
                  _resident(w.shape)],
        out_specs=row(x),
        out_shape=jax.ShapeDtypeStruct(x.shape, F32),
        compiler_params=_params(("parallel",), 48),
        name="out_proj",
    )(f, a, r, x, gate, w)


def _edge_masks(block_row, rows_lat, t_lat, t_ctx):
    in_lat = block_row < rows_lat
    length = jnp.where(in_lat, t_lat, t_ctx)
    off = jnp.where(in_lat, block_row, block_row - rows_lat)
    starts = (lax.rem(off, length) == 0).astype(F32)
    ends = (lax.rem(off + SEQ_ALIGN, length) == 0).astype(F32)
    sub = lax.broadcasted_iota(jnp.int32, (8, 1), 0)
    return 1.0 - starts * (sub == 0).astype(F32), 1.0 - ends * (sub == 7).astype(F32)


def _conv3_block(ext_ref, w_ref, b_ref, cols, r0, not_first8, not_last8):
    w0, w1, w2 = w_ref[0:1, cols], w_ref[1:2, cols], w_ref[2:3, cols]

    def rows(lo, hi, prev_mask=None, next_mask=None):
        at = lambda shift: ext_ref[HALO + r0 + shift + lo:HALO + r0 + shift + hi, cols]
        prev, nxt = at(-1), at(1)
        if prev_mask is not None:
            prev = prev * prev_mask
        if next_mask is not None:
            nxt = nxt * next_mask
        return w0 * prev + w1 * at(0) + w2 * nxt

    n = SEQ_ALIGN
    out = jnp.concatenate([rows(0, 8, prev_mask=not_first8), rows(8, n - 8), rows(n - 8, n, next_mask=not_last8)],
                          axis=0)
    return out + b_ref[:, cols]


def _ffn_body(x_ref, prev_ref, next_ref, g_ref, sc_ref, sh_ref, gate_ref, wa_ref, wg_ref,
              cwa_ref, cwg_ref, ba_ref, bg_ref, wd_ref, o_ref,
              h_scr, ua_scr, ug_scr, act_cur, act_new,
              *, tm, rows_lat, t_lat, t_ctx):
    j = pl.program_id(1)
    nj = pl.num_programs(1) - 1

    @pl.when(j == 0)
    def _():
        norm = lambda x: _modulated_rmsnorm(x, g_ref[...], sc_ref[...], sh_ref[...]).astype(BF16)
        h_scr[0:HALO, :] = norm(prev_ref[...])
        h_scr[HALO:HALO + tm, :] = norm(x_ref[...])
        h_scr[HALO + tm:, :] = norm(next_ref[...])
        act_new[...] = jnp.zeros_like(act_new)
        o_ref[...] = jnp.zeros_like(o_ref)

    @pl.when(j < nj)
    def _():
        act_cur[...] = act_new[...]
        subs = [slice(s, s + FFN_SUB) for s in range(0, wa_ref.shape[1], FFN_SUB)]
        for cs in subs:
            ua_scr[:, cs] = _dot(h_scr[...], wa_ref[:, cs])
            ug_scr[:, cs] = _dot(h_scr[...], wg_ref[:, cs])
        blocks = range(0, tm, SEQ_ALIGN)
        edges = [_edge_masks(pl.program_id(0) * tm + r0, rows_lat, t_lat, t_ctx) for r0 in blocks]
        for cs in subs:
            for r0, edge in zip(blocks, edges):
                ua = _conv3_block(ua_scr, cwa_ref, ba_ref, cs, r0, *edge)
                ug = _conv3_block(ug_scr, cwg_ref, bg_ref, cs, r0, *edge)
                act_new[r0:r0 + SEQ_ALIGN, cs] = (ug * jax.nn.sigmoid(ug) * ua).astype(BF16)
        o_ref[...] += _dot(act_cur[...], wd_ref[...])

    @pl.when(j == nj)
    def _():
        o_ref[...] = x_ref[...] + gate_ref[...] * (o_ref[...] + _dot(act_new[...], wd_ref[...]))


def _conv_ffn(x, g, scale, shift, gate, wa, wg, cwa, cwg, ba, bg, wd, layer, seg_rows, rows_lat, t_lat, t_ctx):
    rows, d = x.shape
    tm = next(t for t in (1024, 512, 256) if rows % t == 0 and rows_lat % t == 0 and seg_rows % t == 0)
    tf = FFN_TILE
    nh = tm // HALO
    last = rows // HALO - 1
    nj = D_FF_PAD // tf
    seg = lambda i, j: (i * tm // seg_rows, 0, 0)
    col = lambda r: pl.BlockSpec((r, tf), lambda i, j: (0, jnp.minimum(j, nj - 1)))
    up = pl.BlockSpec((None, d, tf), lambda i, j: (layer, 0, jnp.minimum(j, nj - 1)))
    return pl.pallas_call(
        functools.partial(_ffn_body, tm=tm, rows_lat=rows_lat, t_lat=t_lat, t_ctx=t_ctx),
        grid=(rows // tm, nj + 1),
        in_specs=[pl.BlockSpec((tm, d), lambda i, j: (i, 0), pipeline_mode=pl.Buffered(1)),
                  pl.BlockSpec((HALO, d), lambda i, j: (jnp.maximum(i * nh - 1, 0), 0)),
                  pl.BlockSpec((HALO, d), lambda i, j: (jnp.minimum((i + 1) * nh, last), 0)),
                  pl.BlockSpec((1, d), lambda i, j: (0, 0)),
                  pl.BlockSpec((None, 1, d), seg), pl.BlockSpec((None, 1, d), seg), pl.BlockSpec((None, 1, d), seg),
                  up, up, col(3), col(3), col(1), col(1),
                  pl.BlockSpec((None, tf, d), lambda i, j: (layer, jnp.maximum(j - 1, 0), 0))],
        out_specs=pl.BlockSpec((tm, d), lambda i, j: (i, 0)),
        out_shape=jax.ShapeDtypeStruct(x.shape, F32),
        scratch_shapes=[pltpu.VMEM((tm + 2 * HALO, d), BF16),
                        pltpu.VMEM((tm + 2 * HALO, tf), F32),
                        pltpu.VMEM((tm + 2 * HALO, tf), F32),
                        pltpu.VMEM((tm, tf), BF16),
                        pltpu.VMEM((tm, tf), BF16)],
        compiler_params=_params(("parallel", "arbitrary"), 56),
        name="conv_ffn",
    )(x, x, x, g, scale, shift, gate, wa, wg, cwa, cwg, ba, bg, wd)


def _final_norm_body(x_ref, g_ref, o_ref):
    x = x_ref[...]
    ms = jnp.mean(x * x, axis=-1, keepdims=True)
    o_ref[...] = x * lax.rsqrt(ms + NORM_EPS) * g_ref[...]


def _final_norm(x, g, row0, nrows):
    tm = 256
    b0 = row0 // tm
    d = x.shape[1]
    return pl.pallas_call(
        _final_norm_body,
        grid=(nrows // tm,),
        in_specs=[pl.BlockSpec((tm, d), lambda i: (b0 + i, 0)), _resident((1, d))],
        out_specs=pl.BlockSpec((tm, d), lambda i: (i, 0)),
        out_shape=jax.ShapeDtypeStruct((nrows, d), F32),
        compiler_params=_params(("parallel",), 32),
        name="final_norm",
    )(x, g)


def _lora_weight(w2, a2, g2):
    c = RWKV_WIDTH
    wl = jnp.zeros((LORA_IN, 5 * c), F32)
    wl = wl.at[:DECAY_LORA, 0:c].set(w2[0]).at[:DECAY_LORA, c:2 * c].set(w2[1])
    wl = wl.at[DECAY_LORA:DECAY_LORA + ICLR_LORA, 2 * c:3 * c].set(a2[0])
    wl = wl.at[DECAY_LORA:DECAY_LORA + ICLR_LORA, 3 * c:4 * c].set(a2[1])
    return wl.at[DECAY_LORA + ICLR_LORA:, 4 * c:].set(g2)


LANES = 128


def _split_up_body(a_ref, g_ref, oa_ref, og_ref, *, nvalid):
    keep = pl.program_id(1) < nvalid
    oa_ref[...] = jnp.where(keep, a_ref[...], 0.0).astype(BF16)
    og_ref[...] = jnp.where(keep, g_ref[...], 0.0).astype(BF16)


def _split_ffn_up(ffn_up):
    depth, d, _ = ffn_up.shape
    nvalid = D_FF // LANES
    src = lambda half: pl.BlockSpec((None, d, LANES),
                                    lambda l, j: (l, 0, half * nvalid + jnp.minimum(j, nvalid - 1)))
    dst = pl.BlockSpec((None, d, LANES), lambda l, j: (l, 0, j))
    shape = jax.ShapeDtypeStruct((depth, d, D_FF_PAD), BF16)
    return pl.pallas_call(
        functools.partial(_split_up_body, nvalid=nvalid),
        grid=(depth, D_FF_PAD // LANES),
        in_specs=[src(0), src(1)], out_specs=[dst, dst], out_shape=[shape, shape],
        compiler_params=_params(("parallel", "parallel"), 32),
        name="split_ffn_up",
    )(ffn_up, ffn_up)


def _pad_down_body(w_ref, o_ref, *, tk):
    row = pl.program_id(1) * tk + lax.broadcasted_iota(jnp.int32, (tk, 1), 0)
    o_ref[...] = jnp.where(row < D_FF, w_ref[...], 0.0).astype(BF16)


def _pad_ffn_down(ffn_down):
    depth, _, d = ffn_down.shape
    tk = FFN_TILE
    return pl.pallas_call(
        functools.partial(_pad_down_body, tk=tk),
        grid=(depth, D_FF_PAD // tk),
        in_specs=[pl.BlockSpec((None, tk, d), lambda l, j: (l, j, 0))],
        out_specs=pl.BlockSpec((None, tk, d), lambda l, j: (l, j, 0)),
        out_shape=jax.ShapeDtypeStruct((depth, D_FF_PAD, d), BF16),
        compiler_params=_params(("parallel", "parallel"), 32),
        name="pad_ffn_down",
    )(ffn_down)


def _pad_cols(w):
    return jnp.pad(w, ((0, 0), (0, D_FF_PAD - D_FF)))


def _forward(x_lat, x_ctx, cache_k, cache_v, state, c, c_ctx, w_ada, b_ada, norm1_g, norm2_g, w_in, w_out,
             q_norm_g, k_norm_g, rw_conv, rw_w0, rw_w2, rw_a0, rw_a2, rw_g2, rw_kk, rw_ka, rw_rk,
             rw_lnx_g, rw_lnx_b, ffn_up, ffn_conv_w, ffn_conv_b, ffn_down, final_norm_g):
    n_lat, t_lat, d = x_lat.shape
    n_ctx, t_ctx, _ = x_ctx.shape
    depth = w_ada.shape[0]
    past = cache_k.shape[2]
    rows_lat, rows_ctx = n_lat * t_lat, n_ctx * t_ctx
    assert rows_ctx % t_lat == 0 and n_lat + rows_ctx // t_lat <= MOD_ROWS
    assert t_lat % SEQ_ALIGN == 0 and t_ctx % SEQ_ALIGN == 0 and t_lat % GRID_W == 0

    x = jnp.concatenate([x_lat.reshape(rows_lat, d), x_ctx.reshape(rows_ctx, d)], axis=0)
    cvec = jnp.concatenate([c, jnp.broadcast_to(c_ctx[None, :], (MOD_ROWS - n_lat, d))], axis=0)
    mod = _adaln_mod(cvec, w_ada, b_ada).reshape(depth, MOD_ROWS, 6, 1, d)

    ffn_wa, ffn_wg = _split_ffn_up(ffn_up)
    ffn_wd = _pad_ffn_down(ffn_down)
    rope_tabs = _rope_tables(t_lat)
    time_lat, chan_tab = _dft_tables(t_lat)
    time_ctx, _ = _dft_tables(t_ctx)
    zero_state = jnp.zeros((n_ctx, 1, 2, RWKV_HEADS, RWKV_N, RWKV_N), F32)
    row2 = lambda a: a.reshape(1, -1)

    new_k, new_v, new_s = [], [], []
    for l in range(depth):
        shift1, scale1, gate1, shift2, scale2, gate2 = (mod[l, :, i] for i in range(6))
        f, q, k, v, rw = _in_proj(x, row2(norm1_g[l]), scale1, shift1, w_in[l].astype(BF16), t_lat)

        f_lat = _fourier_mix(f, time_lat, chan_tab, 0, n_lat, t_lat)
        f_ctx = _fourier_mix(f, time_ctx, chan_tab, rows_lat, n_ctx, t_ctx)

        gq, gk = row2(q_norm_g[l]), row2(k_norm_g[l])
        q_lat, k_lat, v_lat = _qkv_prep(q, k, v, gq, gk, 0, rows_lat, rope_tabs)
        q_c, k_c, v_c, k_norm = _qkv_prep(q, k, v, gq, gk, rows_lat, rows_ctx)
        a_lat = _attention(q_lat, k_lat, v_lat, n_lat, t_lat,
                           cache_k.reshape(n_lat, depth, past, KV_WIDTH), cache_v.reshape(n_lat, depth, past, KV_WIDTH), l)
        a_ctx = _attention(q_c, k_c, v_c, n_ctx, t_ctx)
        new_k.append(k_norm.reshape(n_ctx, t_ctx, ATTN_KV_HEADS, HEAD_DIM))
        new_v.append(v[rows_lat:].reshape(n_ctx, t_ctx, ATTN_KV_HEADS, HEAD_DIM))

        r_, v_, kk, g_, bonus, lw, km, b_ = _rwkv_prep(
            rw, rw_conv[l], _lora_weight(rw_w2[l], rw_a2[l], rw_g2[l]).astype(BF16), rw_w0[l], rw_a0[l],
            row2(rw_kk[l]), row2(rw_ka[l]), row2(rw_rk[l]), rows_lat, t_lat, t_ctx)
        yf_lat, yb_lat, _ = _rwkv_scan(r_, v_, kk, lw, km, b_, state, l, 0, n_lat, t_lat)
        yf_ctx, yb_ctx, s_ctx = _rwkv_scan(r_, v_, kk, lw, km, b_, zero_state, 0, rows_lat, n_ctx, t_ctx)
        ln_g, ln_b = row2(rw_lnx_g[l]), row2(rw_lnx_b[l])
        r_lat = _rwkv_out(yf_lat, yb_lat, bonus, g_, 0, ln_g, ln_b)
        r_ctx = _rwkv_out(yf_ctx, yb_ctx, bonus, g_, rows_lat, ln_g, ln_b)
        new_s.append(s_ctx)

        cat = lambda a, b: jnp.concatenate([a, b], axis=0)
        x = _out_proj(cat(f_lat, f_ctx), cat(a_lat, a_ctx), cat(r_lat, r_ctx), x, gate1,
                      w_out[l].astype(BF16), t_lat)

        x = _conv_ffn(x, row2(norm2_g[l]), scale2, shift2, gate2, ffn_wa, ffn_wg,
                      _pad_cols(ffn_conv_w[l][:, :D_FF]), _pad_cols(ffn_conv_w[l][:, D_FF:]),
                      _pad_cols(row2(ffn_conv_b[l][:D_FF])), _pad_cols(row2(ffn_conv_b[l][D_FF:])),
                      ffn_wd, l, t_lat, rows_lat, t_lat, t_ctx)

    fg = row2(final_norm_g)
    y_lat = _final_norm(x, fg, 0, rows_lat).reshape(n_lat, t_lat, d)
    y_ctx = _final_norm(x, fg, rows_lat, rows_ctx).reshape(n_ctx, t_ctx, d)
    return (y_ctx, y_lat, jnp.stack(new_k, axis=1), jnp.stack(new_v, axis=1), jnp.stack(new_s, axis=1))


def kernel(x_prompt, x_sample, cache_attn_k, cache_attn_v, state_rwkv, c, c_ctx, w_ada, b_ada, norm1_g, norm2_g, w_in, w_out, q_norm_g, k_norm_g, rw_conv, rw_w0, rw_w2, rw_a0, rw_a2, rw_g2, rw_kk, rw_ka, rw_rk, rw_lnx_g, rw_lnx_b, ffn_up, ffn_conv_w, ffn_conv_b, ffn_down, final_norm_g):
    return _forward(x_sample, x_prompt, cache_attn_k, cache_attn_v, state_rwkv, c, c_ctx, w_ada, b_ada,
                    norm1_g, norm2_g, w_in, w_out, q_norm_g, k_norm_g, rw_conv, rw_w0, rw_w2, rw_a0, rw_a2,
                    rw_g2, rw_kk, rw_ka, rw_rk, rw_lnx_g, rw_lnx_b, ffn_up, ffn_conv_w, ffn_conv_b, ffn_down,
                    final_norm_g)
```

```python
import functools
import math

import jax
import jax.numpy as jnp
import numpy as np
from jax import lax
from jax.experimental import pallas as pl
from jax.experimental.pallas import tpu as pltpu

D_MODEL = 2048
GRID_W = 64
HEAD_DIM = 128
ATTN_HEADS = 8
ATTN_KV_HEADS = 2
KV_GROUP = ATTN_HEADS // ATTN_KV_HEADS
ATTN_WIDTH = ATTN_HEADS * HEAD_DIM
KV_WIDTH = ATTN_KV_HEADS * HEAD_DIM
FOURIER_WIDTH = 512
FOURIER_GROUPS = 4
FOURIER_GROUP_WIDTH = FOURIER_WIDTH // FOURIER_GROUPS
RWKV_WIDTH = 512
RWKV_N = 64
RWKV_HEADS = RWKV_WIDTH // RWKV_N
DECAY_LORA = 64
ICLR_LORA = 64
GATE_LORA = 128
LORA_IN = DECAY_LORA + ICLR_LORA + GATE_LORA
RWKV_IN = 3 * RWKV_WIDTH + LORA_IN
IN_WIDTH = FOURIER_WIDTH + ATTN_WIDTH + 2 * KV_WIDTH + RWKV_IN
D_FF = 5504
ROPE_THETA = 10000.0
NORM_EPS = 1e-6
GN_EPS = 64e-5

MOD_ROWS = 16
FFN_TILE = 512
FFN_SUB = 256
D_FF_PAD = -(-D_FF // FFN_TILE) * FFN_TILE
HALO = 16
CHUNK = 64
SEQ_ALIGN = 256
MIB = 2 ** 20

F32 = jnp.float32
BF16 = jnp.bfloat16


def _params(semantics, vmem_mib):
    return pltpu.CompilerParams(dimension_semantics=semantics, vmem_limit_bytes=vmem_mib * MIB)


def _resident(shape):
    return pl.BlockSpec(shape, lambda *_: (0,) * len(shape), pipeline_mode=pl.Buffered(1))


def _resident_layer(stacked_shape, layer):
    shape = stacked_shape[1:]
    return pl.BlockSpec((None,) + tuple(shape), lambda *_: (layer,) + (0,) * len(shape),
                        pipeline_mode=pl.Buffered(1))


def _dot(a, b):
    return jnp.dot(a, b, preferred_element_type=F32)


def _dot_nt(a, b):
    return lax.dot_general(a, b, (((1,), (1,)), ((), ())), preferred_element_type=F32)


def _dot_tn(a, b):
    return lax.dot_general(a, b, (((0,), (0,)), ((), ())), preferred_element_type=F32)


def _modulated_rmsnorm(x, g, scale, shift):
    ms = jnp.mean(x * x, axis=-1, keepdims=True)
    return (x * lax.rsqrt(ms + NORM_EPS) * g) * (1.0 + scale) + shift


def _mod_body(c_ref, w_ref, b_ref, o_ref):
    c = c_ref[...]
    s = (c * jax.nn.sigmoid(c)).astype(BF16)
    o_ref[...] = _dot(s, w_ref[...].astype(BF16)) + b_ref[...]


def _adaln_mod(cvec, w_ada, b_ada):
    depth, d, n = w_ada.shape
    tn = 1024
    return pl.pallas_call(
        _mod_body,
        grid=(depth, n // tn),
        in_specs=[pl.BlockSpec((MOD_ROWS, d), lambda l, j: (0, 0)),
                  pl.BlockSpec((None, d, tn), lambda l, j: (l, 0, j)),
                  pl.BlockSpec((None, 1, tn), lambda l, j: (l, 0, j))],
        out_specs=pl.BlockSpec((None, MOD_ROWS, tn), lambda l, j: (l, 0, j)),
        out_shape=jax.ShapeDtypeStruct((depth, MOD_ROWS, n), F32),
        compiler_params=_params(("parallel", "parallel"), 40),
        name="adaln_mod",
    )(cvec, w_ada, b_ada.reshape(depth, 1, n))


_IN_SPLITS = (("f", 0, FOURIER_WIDTH, BF16),
              ("q", FOURIER_WIDTH, ATTN_WIDTH, F32),
              ("k", FOURIER_WIDTH + ATTN_WIDTH, KV_WIDTH, F32),
              ("v", FOURIER_WIDTH + ATTN_WIDTH + KV_WIDTH, KV_WIDTH, F32),
              ("rw", FOURIER_WIDTH + ATTN_WIDTH + 2 * KV_WIDTH, RWKV_IN, F32))


def _inproj_body(x_ref, g_ref, sc_ref, sh_ref, w_ref, *o_refs):
    h = _modulated_rmsnorm(x_ref[...], g_ref[...], sc_ref[...], sh_ref[...]).astype(BF16)
    for o_ref, (_, start, width, _) in zip(o_refs, _IN_SPLITS):
        o_ref[...] = _dot(h, w_ref[:, start:start + width]).astype(o_ref.dtype)


def _segment_index(seg, tm):
    seg0, seg_rows = seg
    assert seg_rows % tm == 0
    return lambda i, *_: (seg0 + i * tm // seg_rows, 0, 0)


def _in_proj(x, g, scale, shift, w, layer, seg):
    rows, d = x.shape
    tm = 512 if rows % 512 == 0 else 256
    seg = _segment_index(seg, tm)
    return pl.pallas_call(
        _inproj_body,
        grid=(rows // tm,),
        in_specs=[pl.BlockSpec((tm, d), lambda i: (i, 0)),
                  _resident((1, d)),
                  pl.BlockSpec((None, 1, d), seg),
                  pl.BlockSpec((None, 1, d), seg),
                  _resident_layer(w.shape, layer)],
        out_specs=[pl.BlockSpec((tm, width), lambda i: (i, 0)) for _, _, width, _ in _IN_SPLITS],
        out_shape=[jax.ShapeDtypeStruct((rows, width), dt) for _, _, width, dt in _IN_SPLITS],
        compiler_params=_params(("parallel",), 56),
        name="in_proj",
    )(x, g, scale, shift, w)


def _dft_tables(t):
    def angles(n):
        i = np.arange(n, dtype=np.int64)
        return (2.0 * math.pi / n) * ((i[:, None] * i[None, :]) % n)
    at = angles(t)
    time_tab = np.concatenate([np.cos(at), -np.sin(at)], axis=1).astype(np.float32)
    ac = angles(FOURIER_GROUP_WIDTH)
    eye = np.eye(FOURIER_GROUPS)
    chan_tab = np.concatenate([np.kron(eye, np.cos(ac)), np.kron(eye, np.sin(ac))], axis=1).astype(np.float32)
    return jnp.asarray(time_tab).astype(BF16), jnp.asarray(chan_tab).astype(BF16)


def _fourier_body(u_ref, ct_ref, cc_ref, o_ref, ab_scr, *, t, norm):
    @pl.when(pl.program_id(1) == 0)
    def _():
        ab = _dot(u_ref[...], cc_ref[...])
        ab_scr[0:t, :] = ab[:, :FOURIER_WIDTH].astype(BF16)
        ab_scr[t:2 * t, :] = ab[:, FOURIER_WIDTH:].astype(BF16)
    o_ref[...] = (_dot(ct_ref[...], ab_scr[...]) * norm).astype(o_ref.dtype)


def _fourier_mix(u, time_tab, chan_tab, row0, nseq, t):
    tm = min(t, 512)
    nt = t // tm
    seq0 = row0 // t
    return pl.pallas_call(
        functools.partial(_fourier_body, t=t, norm=1.0 / math.sqrt(t * FOURIER_GROUP_WIDTH)),
        grid=(nseq, nt),
        in_specs=[pl.BlockSpec((t, FOURIER_WIDTH), lambda b, i: (seq0 + b, 0)),
                  pl.BlockSpec((tm, 2 * t), lambda b, i: (i, 0)),
                  _resident(chan_tab.shape)],
        out_specs=pl.BlockSpec((tm, FOURIER_WIDTH), lambda b, i: (b * nt + i, 0)),
        out_shape=jax.ShapeDtypeStruct((nseq * t, FOURIER_WIDTH), BF16),
        scratch_shapes=[pltpu.VMEM((2 * t, FOURIER_WIDTH), BF16)],
        compiler_params=_params(("parallel", "arbitrary"), 40),
        name=f"fourier_mix_t{t}",
    )(u, time_tab, chan_tab)


def _rope_tables(t):
    pos = jnp.arange(t, dtype=jnp.int32)
    rows = (pos // GRID_W).astype(F32)
    cols = (pos % GRID_W).astype(F32)
    half = HEAD_DIM // 2
    inv = 1.0 / (ROPE_THETA ** (jnp.arange(0, half, 2, dtype=F32) / half))
    def tab(p):
        ang = p[:, None] * inv[None, :]
        return (jnp.concatenate([jnp.cos(ang), jnp.cos(ang)], -1),
                jnp.concatenate([-jnp.sin(ang), jnp.sin(ang)], -1))
    cr, sr = tab(rows)
    cc, sc = tab(cols)
    return jnp.concatenate([cr, cc], -1), jnp.concatenate([sr, sc], -1)


def _qkv_body(*refs, rope):
    if rope:
        q_ref, k_ref, v_ref, gq_ref, gk_ref, cos_ref, sin_ref, qo_ref, ko_ref, vo_ref = refs
        cos, sin = cos_ref[...], sin_ref[...]
        lane = lax.broadcasted_iota(jnp.int32, (1, HEAD_DIM), 1)
        low = (lane % (HEAD_DIM // 2)) < (HEAD_DIM // 4)
    else:
        q_ref, k_ref, v_ref, gq_ref, gk_ref, qo_ref, ko_ref, vo_ref, kn_ref = refs

    def head_norm(xh, g):
        ms = jnp.mean(xh * xh, axis=-1, keepdims=True)
        return xh * lax.rsqrt(ms + NORM_EPS) * g

    def rotate(xh):
        partner = jnp.where(low, pltpu.roll(xh, HEAD_DIM - HEAD_DIM // 4, 1), pltpu.roll(xh, HEAD_DIM // 4, 1))
        return xh * cos + partner * sin

    scale = HEAD_DIM ** -0.5
    for h in range(ATTN_HEADS):
        sl = slice(h * HEAD_DIM, (h + 1) * HEAD_DIM)
        qh = head_norm(q_ref[:, sl], gq_ref[...])
        if rope:
            qh = rotate(qh)
        qo_ref[:, sl] = (qh * scale).astype(BF16)
    for j in range(ATTN_KV_HEADS):
        sl = slice(j * HEAD_DIM, (j + 1) * HEAD_DIM)
        kh = head_norm(k_ref[:, sl], gk_ref[...])
        if rope:
            ko_ref[:, sl] = rotate(kh).astype(BF16)
        else:
            kn_ref[:, sl] = kh
            ko_ref[:, sl] = kh.astype(BF16)
    vo_ref[...] = v_ref[...].astype(BF16)


def _qkv_prep(q, k, v, gq, gk, row0, nrows, rope_tabs=None):
    tt = 256
    b0 = row0 // tt
    rope = rope_tabs is not None
    row = lambda w: pl.BlockSpec((tt, w), lambda i: (b0 + i, 0))
    out = lambda w: pl.BlockSpec((tt, w), lambda i: (i, 0))
    in_specs = [row(ATTN_WIDTH), row(KV_WIDTH), row(KV_WIDTH), _resident((1, HEAD_DIM)), _resident((1, HEAD_DIM))]
    args = [q, k, v, gq, gk]
    out_specs = [out(ATTN_WIDTH), out(KV_WIDTH), out(KV_WIDTH)]
    out_shape = [jax.ShapeDtypeStruct((nrows, ATTN_WIDTH), BF16),
                 jax.ShapeDtypeStruct((nrows, KV_WIDTH), BF16),
                 jax.ShapeDtypeStruct((nrows, KV_WIDTH), BF16)]
    if rope:
        t = rope_tabs[0].shape[0]
        nt = t // tt
        tab = pl.BlockSpec((tt, HEAD_DIM), lambda i: (i % nt, 0))
        in_specs += [tab, tab]
        args += list(rope_tabs)
    else:
        out_specs.append(out(KV_WIDTH))
        out_shape.append(jax.ShapeDtypeStruct((nrows, KV_WIDTH), F32))
    return pl.pallas_call(
        functools.partial(_qkv_body, rope=rope),
        grid=(nrows // tt,),
        in_specs=in_specs, out_specs=out_specs, out_shape=out_shape,
        compiler_params=_params(("parallel",), 32),
        name="qkv_prep_rope" if rope else "qkv_prep",
    )(*args)


def _attn_body(*refs, cached):
    if cached:
        q_ref, k_ref, v_ref, kc_ref, vc_ref, o_ref = refs
    else:
        q_ref, k_ref, v_ref, o_ref = refs
    for j in range(ATTN_KV_HEADS):
        kv = slice(j * HEAD_DIM, (j + 1) * HEAD_DIM)
        kj, vj = k_ref[:, kv], v_ref[:, kv]
        if cached:
            kc, vc = kc_ref[:, kv].astype(BF16), vc_ref[:, kv].astype(BF16)
        for g in range(KV_GROUP):
            sl = slice((j * KV_GROUP + g) * HEAD_DIM, (j * KV_GROUP + g + 1) * HEAD_DIM)
            qh = q_ref[:, sl]
            s = _dot_nt(qh, kj)
            m = jnp.max(s, axis=-1, keepdims=True)
            if cached:
                sc = _dot_nt(qh, kc)
                m = jnp.maximum(m, jnp.max(sc, axis=-1, keepdims=True))
            p = jnp.exp(s - m)
            l = jnp.sum(p, axis=-1, keepdims=True)
            acc = _dot(p.astype(BF16), vj)
            if cached:
                pc = jnp.exp(sc - m)
                l = l + jnp.sum(pc, axis=-1, keepdims=True)
                acc = acc + _dot(pc.astype(BF16), vc)
            o_ref[:, sl] = (acc / l).astype(o_ref.dtype)


def _attention(q, k, v, nseq, t, cache_k=None, cache_v=None, layer=0):
    tq = 256
    nq = t // tq
    cached = cache_k is not None
    in_specs = [pl.BlockSpec((tq, ATTN_WIDTH), lambda b, i: (b * nq + i, 0)),
                pl.BlockSpec((t, KV_WIDTH), lambda b, i: (b, 0)),
                pl.BlockSpec((t, KV_WIDTH), lambda b, i: (b, 0))]
    args = [q, k, v]
    if cached:
        past = cache_k.shape[2]
        in_specs += [pl.BlockSpec((None, None, past, KV_WIDTH), lambda b, i: (b, layer, 0, 0))] * 2
        args += [cache_k, cache_v]
    return pl.pallas_call(
        functools.partial(_attn_body, cached=cached),
        grid=(nseq, nq),
        in_specs=in_specs,
        out_specs=pl.BlockSpec((tq, ATTN_WIDTH), lambda b, i: (b * nq + i, 0)),
        out_shape=jax.ShapeDtypeStruct((nseq * t, ATTN_WIDTH), BF16),
        compiler_params=_params(("parallel", "parallel"), 48),
        name="attention_cached" if cached else "attention",
    )(*args)


def _boundary_masks(row_start, tm, rows_lat, t_lat, t_ctx):
    in_lat = row_start < rows_lat
    length = jnp.where(in_lat, t_lat, t_ctx)
    base = jnp.where(in_lat, lax.rem(row_start, t_lat), lax.rem(row_start - rows_lat, t_ctx))
    pos = base + lax.broadcasted_iota(jnp.int32, (tm, 1), 0)
    first = pos == 0
    last = pos == length - 1
    for k in range(1, tm // min(t_lat, t_ctx) + 1):
        first = jnp.logical_or(first, pos == k * length)
        last = jnp.logical_or(last, pos == (k + 1) * length - 1)
    return 1.0 - first.astype(F32), 1.0 - last.astype(F32)


def _conv3(ext_ref, w_ref, tm, not_first, not_last, cols=slice(None)):
    return (w_ref[0:1, cols] * (ext_ref[HALO - 1:HALO - 1 + tm, cols] * not_first)
            + w_ref[1:2, cols] * ext_ref[HALO:HALO + tm, cols]
            + w_ref[2:3, cols] * (ext_ref[HALO + 1:HALO + 1 + tm, cols] * not_last))


def _head_sum_matrix():
    i = lax.broadcasted_iota(jnp.int32, (RWKV_WIDTH, RWKV_WIDTH), 0) // RWKV_N
    j = lax.broadcasted_iota(jnp.int32, (RWKV_WIDTH, RWKV_WIDTH), 1) // RWKV_N
    return (i == j).astype(BF16)


def _bf16_pieces(x, n):
    pieces = []
    for _ in range(n - 1):
        p = x.astype(BF16)
        pieces.append(p)
        x = x - p.astype(F32)
    return pieces + [x.astype(BF16)]


def _head_sums(x, ones_bd):
    hi, lo = _bf16_pieces(x, 2)
    return _dot(hi, ones_bd) + _dot(lo, ones_bd)


def _rwkv_prep_body(rw_ref, prev_ref, next_ref, cw_ref, wl_ref, w0_ref, a0_ref, kks_ref, ka_ref, rk_ref,
                    r_ref, v_ref, kk_ref, g_ref, bonus_ref, lw_ref, km_ref, b_ref, ext_scr,
                    *, tm, rows_lat, t_lat, t_ctx):
    ext_scr[0:HALO, :] = prev_ref[...]
    ext_scr[HALO:HALO + tm, :] = rw_ref[...]
    ext_scr[HALO + tm:, :] = next_ref[...]
    not_first, not_last = _boundary_masks(pl.program_id(0) * tm, tm, rows_lat, t_lat, t_ctx)
    z = _conv3(ext_scr, cw_ref, tm, not_first, not_last)
    c = RWKV_WIDTH
    r, k, v = z[:, :c], z[:, c:2 * c], z[:, 2 * c:3 * c]
    zl = z[:, 3 * c:]
    lane = lax.broadcasted_iota(jnp.int32, (1, LORA_IN), 1)
    lora_in = jnp.where(lane < DECAY_LORA, jnp.tanh(zl),
                        jnp.where(lane < DECAY_LORA + ICLR_LORA, zl, jax.nn.sigmoid(zl)))
    lora = _dot(lora_in.astype(BF16), wl_ref[...])
    ones_bd = _head_sum_matrix()
    kk = k * kks_ref[...]
    kk = kk * lax.rsqrt(_head_sums(kk * kk, ones_bd) + 1e-12)
    r_ref[...] = r
    v_ref[...] = v
    kk_ref[...] = kk
    g_ref[...] = lora[:, 4 * c:5 * c]
    kmod_sum = jnp.zeros_like(k)
    for d in range(2):
        wpre = w0_ref[d:d + 1, :] + lora[:, d * c:(d + 1) * c]
        w = -jax.nn.softplus(-wpre) - 0.5
        lw_ref[d] = -jnp.exp(w)
        a = jax.nn.sigmoid(a0_ref[d:d + 1, :] + lora[:, (2 + d) * c:(3 + d) * c])
        kmod = k * (1.0 + (a - 1.0) * ka_ref[...])
        km_ref[d] = kmod
        b_ref[d] = kk * a
        kmod_sum = kmod_sum + kmod
    bonus_ref[...] = _head_sums(r * kmod_sum * rk_ref[...], ones_bd) * v


def _rwkv_prep(rw, conv_w, lora_w, w0, a0, kk_scale, ka, rk, rows_lat, t_lat, t_ctx):
    rows = rw.shape[0]
    tm = 256
    nh = tm // HALO
    last = rows // HALO - 1
    c = RWKV_WIDTH
    one = lambda: pl.BlockSpec((tm, c), lambda i: (i, 0))
    two = lambda: pl.BlockSpec((2, tm, c), lambda i: (0, i, 0))
    return pl.pallas_call(
        functools.partial(_rwkv_prep_body, tm=tm, rows_lat=rows_lat, t_lat=t_lat, t_ctx=t_ctx),
        grid=(rows // tm,),
        in_specs=[pl.BlockSpec((tm, RWKV_IN), lambda i: (i, 0)),
                  pl.BlockSpec((HALO, RWKV_IN), lambda i: (jnp.maximum(i * nh - 1, 0), 0)),
                  pl.BlockSpec((HALO, RWKV_IN), lambda i: (jnp.minimum((i + 1) * nh, last), 0)),
                  _resident(conv_w.shape), _resident(lora_w.shape), _resident(w0.shape), _resident(a0.shape),
                  _resident(kk_scale.shape), _resident(ka.shape), _resident(rk.shape)],
        out_specs=[one(), one(), one(), one(), one(), two(), two(), two()],
        out_shape=[jax.ShapeDtypeStruct((rows, c), F32)] * 5 + [jax.ShapeDtypeStruct((2, rows, c), F32)] * 3,
        scratch_shapes=[pltpu.VMEM((tm + 2 * HALO, RWKV_IN), F32)],
        compiler_params=_params(("parallel",), 48),
        name="rwkv_prep",
    )(rw, rw, rw, conv_w, lora_w, w0, a0, kk_scale, ka, rk)


def _scan_direction_operands(r_ref, v_ref, kk_ref, lw_ref, km_ref, b_ref, backward):
    n = CHUNK
    row = lax.broadcasted_iota(jnp.int32, (n, n), 0)
    col = lax.broadcasted_iota(jnp.int32, (n, n), 1)
    upto = (col >= row) if backward else (col <= row)
    lw = lw_ref[...]
    tri = upto.astype(BF16)
    cs = sum(_dot(tri, piece) for piece in _bf16_pieces(lw, 3))
    tot = jnp.sum(lw, axis=0, keepdims=True)
    grow = jnp.exp(-cs)
    to_end = jnp.exp(tot - cs)
    b = b_ref[...]
    km = km_ref[...]
    return dict(
        kkt=(kk_ref[...] * jnp.exp(cs - lw)).astype(BF16),
        rt=(r_ref[...] * jnp.exp(cs)).astype(BF16),
        bt=(b * grow).astype(BF16), kt=(km * grow).astype(BF16),
        bh=(b * to_end).astype(BF16), kh=(km * to_end).astype(BF16),
        vb=v_ref[...].astype(BF16), w_all=jnp.exp(tot))


def _rwkv_scan_body(rf_ref, vf_ref, kkf_ref, lwf_ref, kmf_ref, bf_ref,
                    rb_ref, vb_ref, kkb_ref, lwb_ref, kmb_ref, bb_ref, s0_ref,
                    yf_ref, yb_ref, sfin_ref, s_scr):
    c = pl.program_id(1)

    @pl.when(c == 0)
    def _():
        s_scr[...] = s0_ref[...]

    n = CHUNK
    row = lax.broadcasted_iota(jnp.int32, (2 * n, 2 * n), 0)
    col = lax.broadcasted_iota(jnp.int32, (2 * n, 2 * n), 1)
    t_idx, s_idx, read_rows = row % n, col % n, row >= n
    eye = (lax.broadcasted_iota(jnp.int32, (n, n), 0) == lax.broadcasted_iota(jnp.int32, (n, n), 1)).astype(F32)
    ops = (_scan_direction_operands(rf_ref, vf_ref, kkf_ref, lwf_ref, kmf_ref, bf_ref, False),
           _scan_direction_operands(rb_ref, vb_ref, kkb_ref, lwb_ref, kmb_ref, bb_ref, True))
    same_step = jnp.logical_and(read_rows, s_idx == t_idx)
    masks = (jnp.logical_or(s_idx < t_idx, same_step), jnp.logical_or(s_idx > t_idx, same_step))
    y_refs = (yf_ref, yb_ref)
    chains = [(d, h) for d in range(2) for h in range(RWKV_HEADS)]
    head = lambda d, name, h: ops[d][name][:, h * RWKV_N:(h + 1) * RWKV_N]
    cat = lambda a, b: jnp.concatenate([a, b], axis=0)

    lhs = [cat(head(d, "kkt", h), head(d, "rt", h)) for d, h in chains]
    coef = [jnp.where(masks[d], _dot_nt(l, cat(head(d, "bt", h), head(d, "kt", h))), 0.0)
            for l, (d, h) in zip(lhs, chains)]
    n_mat = [a[:n, :n] for a in coef]
    state = [s_scr[d, h] for d, h in chains]
    state_b = [s.astype(BF16) for s in state]
    read = [_dot_nt(l, sb) for l, sb in zip(lhs, state_b)]
    akv = [_dot(a[:n, n:].astype(BF16), head(d, "vb", h)) for a, (d, h) in zip(coef, chains)]

    x = [eye - m for m in n_mat]
    p = [_dot(m.astype(BF16), m.astype(BF16)) for m in n_mat]
    steps = int(math.log2(n)) - 1
    for s in range(steps):
        pb = [q.astype(BF16) for q in p]
        x = [xi + _dot(xi.astype(BF16), q) for xi, q in zip(x, pb)]
        if s + 1 < steps:
            p = [_dot(q, q) for q in pb]

    u = [_dot(xi.astype(BF16), (-(rd[:n] + ak)).astype(BF16)) for xi, rd, ak in zip(x, read, akv)]
    uv = [cat(ui.astype(BF16), head(d, "vb", h)) for ui, (d, h) in zip(u, chains)]
    for (d, h), a, rd, uvi, s in zip(chains, coef, read, uv, state):
        sl = slice(h * RWKV_N, (h + 1) * RWKV_N)
        y_refs[d][:, sl] = rd[n:] + _dot(a[n:].astype(BF16), uvi)
        s_scr[d, h] = s * ops[d]["w_all"][:, sl] + _dot_tn(uvi, cat(head(d, "bh", h), head(d, "kh", h)))

    @pl.when(c == pl.num_programs(1) - 1)
    def _():
        sfin_ref[...] = s_scr[...]


def _rwkv_scan(r, v, kk, lw, km, b, states, layer, row0, nseq, t):
    n = CHUNK
    nc = t // n
    c0 = row0 // n
    c = RWKV_WIDTH
    fwd = lambda bb, cc: c0 + bb * nc + cc
    bwd = lambda bb, cc: c0 + bb * nc + nc - 1 - cc
    one = lambda chunk: pl.BlockSpec((n, c), lambda bb, cc: (chunk(bb, cc), 0))
    two = lambda chunk, d: pl.BlockSpec((None, n, c), lambda bb, cc: (d, chunk(bb, cc), 0))
    state = pl.BlockSpec((None, 2, RWKV_HEADS, RWKV_N, RWKV_N), lambda bb, cc: (bb, 0, 0, 0, 0))
    state_in = pl.BlockSpec((None, None, 2, RWKV_HEADS, RWKV_N, RWKV_N), lambda bb, cc: (bb, layer, 0, 0, 0, 0))
    y_shape = jax.ShapeDtypeStruct((nseq * t, c), F32)
    return pl.pallas_call(
        _rwkv_scan_body,
        grid=(nseq, nc),
        in_specs=[one(fwd), one(fwd), one(fwd), two(fwd, 0), two(fwd, 0), two(fwd, 0),
                  one(bwd), one(bwd), one(bwd), two(bwd, 1), two(bwd, 1), two(bwd, 1), state_in],
        out_specs=[pl.BlockSpec((n, c), lambda bb, cc: (bb * nc + cc, 0)),
                   pl.BlockSpec((n, c), lambda bb, cc: (bb * nc + nc - 1 - cc, 0)), state],
        out_shape=[y_shape, y_shape, jax.ShapeDtypeStruct((nseq, 2, RWKV_HEADS, RWKV_N, RWKV_N), F32)],
        scratch_shapes=[pltpu.VMEM((2, RWKV_HEADS, RWKV_N, RWKV_N), F32)],
        compiler_params=_params(("parallel", "arbitrary"), 32),
        name=f"rwkv_scan_t{t}",
    )(r, v, kk, lw, km, b, r, v, kk, lw, km, b, states)


def _rwkv_out_body(yf_ref, yb_ref, bonus_ref, g_ref, lg_ref, lb_ref, o_ref):
    ones_bd = _head_sum_matrix()
    y = yf_ref[...] + yb_ref[...]
    mu = _head_sums(y, ones_bd) * (1.0 / RWKV_N)
    yc = y - mu
    var = _head_sums(yc * yc, ones_bd) * (1.0 / RWKV_N)
    yn = yc * lax.rsqrt(var + GN_EPS) * lg_ref[...] + lb_ref[...]
    o_ref[...] = ((yn + bonus_ref[...]) * g_ref[...]).astype(o_ref.dtype)


def _rwkv_out(yf, yb, bonus, g, row0, ln_g, ln_b):
    nrows = yf.shape[0]
    tm = 256
    b0 = row0 // tm
    c = RWKV_WIDTH
    src = pl.BlockSpec((tm, c), lambda i: (b0 + i, 0))
    return pl.pallas_call(
        _rwkv_out_body,
        grid=(nrows // tm,),
        in_specs=[pl.BlockSpec((tm, c), lambda i: (i, 0))] * 2 + [src, src, _resident((1, c)), _resident((1, c))],
        out_specs=pl.BlockSpec((tm, c), lambda i: (i, 0)),
        out_shape=jax.ShapeDtypeStruct((nrows, c), BF16),
        compiler_params=_params(("parallel",), 32),
        name="rwkv_out",
    )(yf, yb, bonus, g, ln_g, ln_b)


def _outproj_body(f_ref, a_ref, r_ref, x_ref, gate_ref, w_ref, o_ref):
    f0, a0, r0 = 0, FOURIER_WIDTH, FOURIER_WIDTH + ATTN_WIDTH
    mix = (_dot(f_ref[...], w_ref[f0:a0, :]) + _dot(a_ref[...], w_ref[a0:r0, :])
           + _dot(r_ref[...], w_ref[r0:, :]))
    o_ref[...] = x_ref[...] + gate_ref[...] * mix


def _out_proj(f, a, r, x, gate, w, layer, seg):
    rows, d = x.shape
    tm = 512 if rows % 512 == 0 else 256
    row = lambda arr: pl.BlockSpec((tm, arr.shape[1]), lambda i: (i, 0))
    return pl.pallas_call(
        _outproj_body,
        grid=(rows // tm,),
        in_specs=[row(f), row(a), row(r), row(x),
                  pl.BlockSpec((None, 1, d), _segment_index(seg, tm)),
                  _resident_layer(w.shape, layer)],
        out_specs=row(x),
        out_shape=jax.ShapeDtypeStruct(x.shape, F32),
        compiler_params=_params(("parallel",), 48),
        name="out_proj",
    )(f, a, r, x, gate, w)


def _edge_masks(block_row, rows_lat, t_lat, t_ctx):
    in_lat = block_row < rows_lat
    length = jnp.where(in_lat, t_lat, t_ctx)
    off = jnp.where(in_lat, block_row, block_row - rows_lat)
    starts = (lax.rem(off, length) == 0).astype(F32)
    ends = (lax.rem(off + SEQ_ALIGN, length) == 0).astype(F32)
    sub = lax.broadcasted_iota(jnp.int32, (8, 1), 0)
    return 1.0 - starts * (sub == 0).astype(F32), 1.0 - ends * (sub == 7).astype(F32)


def _conv3_block(ext_ref, w_ref, b_ref, cols, r0, not_first8, not_last8):
    w0, w1, w2 = w_ref[0:1, cols], w_ref[1:2, cols], w_ref[2:3, cols]

    def rows(lo, hi, prev_mask=None, next_mask=None):
        at = lambda shift: ext_ref[HALO + r0 + shift + lo:HALO + r0 + shift + hi, cols]
        prev, nxt = at(-1), at(1)
        if prev_mask is not None:
            prev = prev * prev_mask
        if next_mask is not None:
            nxt = nxt * next_mask
        return w0 * prev + w1 * at(0) + w2 * nxt

    n = SEQ_ALIGN
    out = jnp.concatenate([rows(0, 8, prev_mask=not_first8), rows(8, n - 8), rows(n - 8, n, next_mask=not_last8)],
                          axis=0)
    return out + b_ref[:, cols]


def _ffn_body(x_ref, prev_ref, next_ref, g_ref, sc_ref, sh_ref, gate_ref, wa_ref, wg_ref,
              cwa_ref, cwg_ref, ba_ref, bg_ref, wd_ref, o_ref,
              h_scr, ua_scr, ug_scr, act_cur, act_new,
              *, tm, rows_lat, t_lat, t_ctx):
    j = pl.program_id(1)
    nj = pl.num_programs(1) - 1

    @pl.when(j == 0)
    def _():
        norm = lambda x: _modulated_rmsnorm(x, g_ref[...], sc_ref[...], sh_ref[...]).astype(BF16)
        h_scr[0:HALO, :] = norm(prev_ref[...])
        h_scr[HALO:HALO + tm, :] = norm(x_ref[...])
        h_scr[HALO + tm:, :] = norm(next_ref[...])
        act_new[...] = jnp.zeros_like(act_new)
        o_ref[...] = jnp.zeros_like(o_ref)

    @pl.when(j < nj)
    def _():
        act_cur[...] = act_new[...]
        subs = [slice(s, s + FFN_SUB) for s in range(0, wa_ref.shape[1], FFN_SUB)]
        for cs in subs:
            ua_scr[:, cs] = _dot(h_scr[...], wa_ref[:, cs])
            ug_scr[:, cs] = _dot(h_scr[...], wg_ref[:, cs])
        blocks = range(0, tm, SEQ_ALIGN)
        edges = [_edge_masks(pl.program_id(0) * tm + r0, rows_lat, t_lat, t_ctx) for r0 in blocks]
        for cs in subs:
            for r0, edge in zip(blocks, edges):
                ua = _conv3_block(ua_scr, cwa_ref, ba_ref, cs, r0, *edge)
                ug = _conv3_block(ug_scr, cwg_ref, bg_ref, cs, r0, *edge)
                act_new[r0:r0 + SEQ_ALIGN, cs] = (ug * jax.nn.sigmoid(ug) * ua).astype(BF16)
        o_ref[...] += _dot(act_cur[...], wd_ref[...])

    @pl.when(j == nj)
    def _():
        o_ref[...] = x_ref[...] + gate_ref[...] * (o_ref[...] + _dot(act_new[...], wd_ref[...]))


def _conv_ffn(x, g, scale, shift, gate, wa, wg, cwa, cwg, ba, bg, wd, layer, seg, rows_lat, t_lat, t_ctx):
    rows, d = x.shape
    tm = next(t for t in (1024, 512, 256) if rows % t == 0 and rows_lat % t == 0 and seg[1] % t == 0)
    tf = FFN_TILE
    nh = tm // HALO
    last = rows // HALO - 1
    nj = D_FF_PAD // tf
    seg = _segment_index(seg, tm)
    col = lambda r: pl.BlockSpec((r, tf), lambda i, j: (0, jnp.minimum(j, nj - 1)))
    up = pl.BlockSpec((None, d, tf), lambda i, j: (layer, 0, jnp.minimum(j, nj - 1)))
    return pl.pallas_call(
        functools.partial(_ffn_body, tm=tm, rows_lat=rows_lat, t_lat=t_lat, t_ctx=t_ctx),
        grid=(rows // tm, nj + 1),
        in_specs=[pl.BlockSpec((tm, d), lambda i, j: (i, 0), pipeline_mode=pl.Buffered(1)),
                  pl.BlockSpec((HALO, d), lambda i, j: (jnp.maximum(i * nh - 1, 0), 0)),
                  pl.BlockSpec((HALO, d), lambda i, j: (jnp.minimum((i + 1) * nh, last), 0)),
                  pl.BlockSpec((1, d), lambda i, j: (0, 0)),
                  pl.BlockSpec((None, 1, d), seg), pl.BlockSpec((None, 1, d), seg), pl.BlockSpec((None, 1, d), seg),
                  up, up, col(3), col(3), col(1), col(1),
                  pl.BlockSpec((None, tf, d), lambda i, j: (layer, jnp.maximum(j - 1, 0), 0))],
        out_specs=pl.BlockSpec((tm, d), lambda i, j: (i, 0)),
        out_shape=jax.ShapeDtypeStruct(x.shape, F32),
        scratch_shapes=[pltpu.VMEM((tm + 2 * HALO, d), BF16),
                        pltpu.VMEM((tm + 2 * HALO, tf), F32),
                        pltpu.VMEM((tm + 2 * HALO, tf), F32),
                        pltpu.VMEM((tm, tf), BF16),
                        pltpu.VMEM((tm, tf), BF16)],
        compiler_params=_params(("parallel", "arbitrary"), 56),
        name="conv_ffn",
    )(x, x, x, g, scale, shift, gate, wa, wg, cwa, cwg, ba, bg, wd)


def _final_norm_body(x_ref, g_ref, o_ref):
    x = x_ref[...]
    ms = jnp.mean(x * x, axis=-1, keepdims=True)
    o_ref[...] = x * lax.rsqrt(ms + NORM_EPS) * g_ref[...]


def _final_norm(x, g, row0, nrows):
    tm = 256
    b0 = row0 // tm
    d = x.shape[1]
    return pl.pallas_call(
        _final_norm_body,
        grid=(nrows // tm,),
        in_specs=[pl.BlockSpec((tm, d), lambda i: (b0 + i, 0)), _resident((1, d))],
        out_specs=pl.BlockSpec((tm, d), lambda i: (i, 0)),
        out_shape=jax.ShapeDtypeStruct((nrows, d), F32),
        compiler_params=_params(("parallel",), 32),
        name="final_norm",
    )(x, g)


def _lora_weight(w2, a2, g2):
    c = RWKV_WIDTH
    wl = jnp.zeros((LORA_IN, 5 * c), F32)
    wl = wl.at[:DECAY_LORA, 0:c].set(w2[0]).at[:DECAY_LORA, c:2 * c].set(w2[1])
    wl = wl.at[DECAY_LORA:DECAY_LORA + ICLR_LORA, 2 * c:3 * c].set(a2[0])
    wl = wl.at[DECAY_LORA:DECAY_LORA + ICLR_LORA, 3 * c:4 * c].set(a2[1])
    return wl.at[DECAY_LORA + ICLR_LORA:, 4 * c:].set(g2)


LANES = 128


def _split_up_body(a_ref, g_ref, oa_ref, og_ref, *, nvalid):
    keep = pl.program_id(1) < nvalid
    oa_ref[...] = jnp.where(keep, a_ref[...], 0.0).astype(BF16)
    og_ref[...] = jnp.where(keep, g_ref[...], 0.0).astype(BF16)


def _split_ffn_up(ffn_up):
    depth, d, _ = ffn_up.shape
    nvalid = D_FF // LANES
    src = lambda half: pl.BlockSpec((None, d, LANES),
                                    lambda l, j: (l, 0, half * nvalid + jnp.minimum(j, nvalid - 1)))
    dst = pl.BlockSpec((None, d, LANES), lambda l, j: (l, 0, j))
    shape = jax.ShapeDtypeStruct((depth, d, D_FF_PAD), BF16)
    return pl.pallas_call(
        functools.partial(_split_up_body, nvalid=nvalid),
        grid=(depth, D_FF_PAD // LANES),
        in_specs=[src(0), src(1)], out_specs=[dst, dst], out_shape=[shape, shape],
        compiler_params=_params(("parallel", "parallel"), 32),
        name="split_ffn_up",
    )(ffn_up, ffn_up)


def _pad_down_body(w_ref, o_ref, *, tk):
    row = pl.program_id(1) * tk + lax.broadcasted_iota(jnp.int32, (tk, 1), 0)
    o_ref[...] = jnp.where(row < D_FF, w_ref[...], 0.0).astype(BF16)


def _pad_ffn_down(ffn_down):
    depth, _, d = ffn_down.shape
    tk = FFN_TILE
    return pl.pallas_call(
        functools.partial(_pad_down_body, tk=tk),
        grid=(depth, D_FF_PAD // tk),
        in_specs=[pl.BlockSpec((None, tk, d), lambda l, j: (l, j, 0))],
        out_specs=pl.BlockSpec((None, tk, d), lambda l, j: (l, j, 0)),
        out_shape=jax.ShapeDtypeStruct((depth, D_FF_PAD, d), BF16),
        compiler_params=_params(("parallel", "parallel"), 32),
        name="pad_ffn_down",
    )(ffn_down)


def _pad_cols(w):
    return jnp.pad(w, ((0, 0), (0, D_FF_PAD - D_FF)))


def _forward(x_lat, x_ctx, cache_k, cache_v, state, c, c_ctx, w_ada, b_ada, norm1_g, norm2_g, w_in, w_out,
             q_norm_g, k_norm_g, rw_conv, rw_w0, rw_w2, rw_a0, rw_a2, rw_g2, rw_kk, rw_ka, rw_rk,
             rw_lnx_g, rw_lnx_b, ffn_up, ffn_conv_w, ffn_conv_b, ffn_down, final_norm_g):
    n_lat, t_lat, d = x_lat.shape
    n_ctx, t_ctx, _ = x_ctx.shape
    depth = w_ada.shape[0]
    past = cache_k.shape[2]
    rows_lat, rows_ctx = n_lat * t_lat, n_ctx * t_ctx
    assert n_lat < MOD_ROWS
    assert t_lat % SEQ_ALIGN == 0 and t_ctx % SEQ_ALIGN == 0 and t_lat % GRID_W == 0

    cvec = jnp.concatenate([c, jnp.broadcast_to(c_ctx[None, :], (MOD_ROWS - n_lat, d))], axis=0)
    mod = _adaln_mod(cvec, w_ada, b_ada).reshape(depth, MOD_ROWS, 6, 1, d)

    ffn_wa, ffn_wg = _split_ffn_up(ffn_up)
    ffn_wd = _pad_ffn_down(ffn_down)
    rope_tabs = _rope_tables(t_lat)
    time_lat, chan_tab = _dft_tables(t_lat)
    time_ctx, _ = _dft_tables(t_ctx)
    zero_state = jnp.zeros((n_ctx, 1, 2, RWKV_HEADS, RWKV_N, RWKV_N), F32)
    row2 = lambda a: a.reshape(1, -1)

    cache = (cache_k.reshape(n_lat, depth, past, KV_WIDTH), cache_v.reshape(n_lat, depth, past, KV_WIDTH))

    def trunk_layer(x, l, latent):
        nseq, t = (n_lat, t_lat) if latent else (n_ctx, t_ctx)
        rows = nseq * t
        seg = (0, t_lat) if latent else (n_lat, rows)
        region_rows_lat = rows if latent else 0
        shift1, scale1, gate1, shift2, scale2, gate2 = (mod[l, :, i] for i in range(6))
        f, q, k, v, rw = _in_proj(x, row2(norm1_g[l]), scale1, shift1, w_in_b, l, seg)

        f_out = _fourier_mix(f, time_lat if latent else time_ctx, chan_tab, 0, nseq, t)

        gq, gk = row2(q_norm_g[l]), row2(k_norm_g[l])
        if latent:
            qb, kb, vb = _qkv_prep(q, k, v, gq, gk, 0, rows, rope_tabs)
            a_out = _attention(qb, kb, vb, nseq, t, cache[0], cache[1], l)
            k_norm = None
        else:
            qb, kb, vb, k_norm = _qkv_prep(q, k, v, gq, gk, 0, rows)
            a_out = _attention(qb, kb, vb, nseq, t)

        r_, v_, kk, g_, bonus, lw, km, b_ = _rwkv_prep(
            rw, rw_conv[l], lora_w[l], rw_w0[l], rw_a0[l],
            row2(rw_kk[l]), row2(rw_ka[l]), row2(rw_rk[l]), region_rows_lat, t_lat, t_ctx)
        yf, yb, s_fin = _rwkv_scan(r_, v_, kk, lw, km, b_, state if latent else zero_state, l if latent else 0,
                                   0, nseq, t)
        r_out = _rwkv_out(yf, yb, bonus, g_, 0, row2(rw_lnx_g[l]), row2(rw_lnx_b[l]))

        x = _out_proj(f_out, a_out, r_out, x, gate1, w_out_b, l, seg)
        x = _conv_ffn(x, row2(norm2_g[l]), scale2, shift2, gate2, ffn_wa, ffn_wg,
                      _pad_cols(ffn_conv_w[l][:, :D_FF]), _pad_cols(ffn_conv_w[l][:, D_FF:]),
                      _pad_cols(row2(ffn_conv_b[l][:D_FF])), _pad_cols(row2(ffn_conv_b[l][D_FF:])),
                      ffn_wd, l, seg, region_rows_lat, t_lat, t_ctx)
        return x, k_norm, v, s_fin

    w_in_b, w_out_b = w_in.astype(BF16), w_out.astype(BF16)
    lora_w = [_lora_weight(rw_w2[l], rw_a2[l], rw_g2[l]).astype(BF16) for l in range(depth)]
    xs, xc = x_lat.reshape(rows_lat, d), x_ctx.reshape(rows_ctx, d)
    new_k, new_v, new_s = [], [], []
    for l in range(depth):
        xc, k_norm, v_ctx, s_ctx = trunk_layer(xc, l, latent=False)
        new_k.append(k_norm.reshape(n_ctx, t_ctx, ATTN_KV_HEADS, HEAD_DIM))
        new_v.append(v_ctx.reshape(n_ctx, t_ctx, ATTN_KV_HEADS, HEAD_DIM))
        new_s.append(s_ctx)
        xs, _, _, _ = trunk_layer(xs, l, latent=True)

    fg = row2(final_norm_g)
    y_lat = _final_norm(xs, fg, 0, rows_lat).reshape(n_lat, t_lat, d)
    y_ctx = _final_norm(xc, fg, 0, rows_ctx).reshape(n_ctx, t_ctx, d)
    return (y_ctx, y_lat, jnp.stack(new_k, axis=1), jnp.stack(new_v, axis=1), jnp.stack(new_s, axis=1))


def kernel(x_prompt, x_sample, cache_attn_k, cache_attn_v, state_rwkv, c, c_ctx, w_ada, b_ada, norm1_g, norm2_g, w_in, w_out, q_norm_g, k_norm_g, rw_conv, rw_w0, rw_w2, rw_a0, rw_a2, rw_g2, rw_kk, rw_ka, rw_rk, rw_lnx_g, rw_lnx_b, ffn_up, ffn_conv_w, ffn_conv_b, ffn_down, final_norm_g):
    return _forward(x_sample, x_prompt, cache_attn_k, cache_attn_v, state_rwkv, c, c_ctx, w_ada, b_ada,
                    norm1_g, norm2_g, w_in, w_out, q_norm_g, k_norm_g, rw_conv, rw_w0, rw_w2, rw_a0, rw_a2,
                    rw_g2, rw_kk, rw_ka, rw_rk, rw_lnx_g, rw_lnx_b, ffn_up, ffn_conv_w, ffn_conv_b, ffn_down,
                    final_norm_g)
```

```python
import functools
import math

import jax
import jax.numpy as jnp
import numpy as np
from jax import lax
from jax.experimental import pallas as pl
from jax.experimental.pallas import tpu as pltpu

D_MODEL = 2048
GRID_W = 64
HEAD_DIM = 128
ATTN_HEADS = 8
ATTN_KV_HEADS = 2
KV_GROUP = ATTN_HEADS // ATTN_KV_HEADS
ATTN_WIDTH = ATTN_HEADS * HEAD_DIM
KV_WIDTH = ATTN_KV_HEADS * HEAD_DIM
FOURIER_WIDTH = 512
FOURIER_GROUPS = 4
FOURIER_GROUP_WIDTH = FOURIER_WIDTH // FOURIER_GROUPS
RWKV_WIDTH = 512
RWKV_N = 64
RWKV_HEADS = RWKV_WIDTH // RWKV_N
DECAY_LORA = 64
ICLR_LORA = 64
GATE_LORA = 128
LORA_IN = DECAY_LORA + ICLR_LORA + GATE_LORA
RWKV_IN = 3 * RWKV_WIDTH + LORA_IN
IN_WIDTH = FOURIER_WIDTH + ATTN_WIDTH + 2 * KV_WIDTH + RWKV_IN
D_FF = 5504
ROPE_THETA = 10000.0
NORM_EPS = 1e-6
GN_EPS = 64e-5

MOD_ROWS = 16
FFN_TILE = 512
FFN_SUB = 256
D_FF_PAD = -(-D_FF // FFN_TILE) * FFN_TILE
HALO = 16
CHUNK = 64
SEQ_ALIGN = 256
MIB = 2 ** 20

F32 = jnp.float32
BF16 = jnp.bfloat16


def _params(semantics, vmem_mib):
    return pltpu.CompilerParams(dimension_semantics=semantics, vmem_limit_bytes=vmem_mib * MIB)


def _resident(shape):
    return pl.BlockSpec(shape, lambda *_: (0,) * len(shape), pipeline_mode=pl.Buffered(1))


def _resident_layer(stacked_shape, layer):
    shape = stacked_shape[1:]
    return pl.BlockSpec((None,) + tuple(shape), lambda *_: (layer,) + (0,) * len(shape),
                        pipeline_mode=pl.Buffered(1))


def _dot(a, b):
    return jnp.dot(a, b, preferred_element_type=F32)


def _dot_nt(a, b):
    return lax.dot_general(a, b, (((1,), (1,)), ((), ())), preferred_element_type=F32)


def _dot_tn(a, b):
    return lax.dot_general(a, b, (((0,), (0,)), ((), ())), preferred_element_type=F32)


def _modulated_rmsnorm(x, g, scale, shift):
    ms = jnp.mean(x * x, axis=-1, keepdims=True)
    return (x * lax.rsqrt(ms + NORM_EPS) * g) * (1.0 + scale) + shift


def _mod_body(c_ref, w_ref, b_ref, o_ref):
    c = c_ref[...]
    s = (c * jax.nn.sigmoid(c)).astype(BF16)
    o_ref[...] = _dot(s, w_ref[...].astype(BF16)) + b_ref[...]


def _adaln_mod(cvec, w_ada, b_ada):
    depth, d, n = w_ada.shape
    tn = 1024
    return pl.pallas_call(
        _mod_body,
        grid=(depth, n // tn),
        in_specs=[pl.BlockSpec((MOD_ROWS, d), lambda l, j: (0, 0)),
                  pl.BlockSpec((None, d, tn), lambda l, j: (l, 0, j)),
                  pl.BlockSpec((None, 1, tn), lambda l, j: (l, 0, j))],
        out_specs=pl.BlockSpec((None, MOD_ROWS, tn), lambda l, j: (l, 0, j)),
        out_shape=jax.ShapeDtypeStruct((depth, MOD_ROWS, n), F32),
        compiler_params=_params(("parallel", "parallel"), 40),
        name="adaln_mod",
    )(cvec, w_ada, b_ada.reshape(depth, 1, n))


Q_COL = FOURIER_WIDTH
K_COL = Q_COL + ATTN_WIDTH
V_COL = K_COL + KV_WIDTH
RW_COL = V_COL + KV_WIDTH


def _inproj_body(*refs, rope):
    x_ref, g_ref, sc_ref, sh_ref, w_ref, gq_ref, gk_ref = refs[:7]
    if rope:
        cos_ref, sin_ref, f_ref, q_ref, k_ref, v_ref, rw_ref = refs[7:]
        cos, sin = cos_ref[...], sin_ref[...]
        lane = lax.broadcasted_iota(jnp.int32, (1, HEAD_DIM), 1)
        low = (lane % (HEAD_DIM // 2)) < (HEAD_DIM // 4)
    else:
        f_ref, q_ref, k_ref, v_ref, rw_ref, kn_ref, vf_ref = refs[7:]

    h = _modulated_rmsnorm(x_ref[...], g_ref[...], sc_ref[...], sh_ref[...]).astype(BF16)
    proj = lambda start, width: _dot(h, w_ref[:, start:start + width])
    q, k, v = proj(Q_COL, ATTN_WIDTH), proj(K_COL, KV_WIDTH), proj(V_COL, KV_WIDTH)
    f_ref[...] = proj(0, FOURIER_WIDTH).astype(BF16)
    rw_ref[...] = proj(RW_COL, RWKV_IN)

    def head_norm(xh, g):
        ms = jnp.mean(xh * xh, axis=-1, keepdims=True)
        return xh * lax.rsqrt(ms + NORM_EPS) * g

    def rotate(xh):
        partner = jnp.where(low, pltpu.roll(xh, HEAD_DIM - HEAD_DIM // 4, 1), pltpu.roll(xh, HEAD_DIM // 4, 1))
        return xh * cos + partner * sin

    scale = HEAD_DIM ** -0.5
    for hd in range(ATTN_HEADS):
        sl = slice(hd * HEAD_DIM, (hd + 1) * HEAD_DIM)
        qh = head_norm(q[:, sl], gq_ref[...])
        if rope:
            qh = rotate(qh)
        q_ref[:, sl] = (qh * scale).astype(BF16)
    for j in range(ATTN_KV_HEADS):
        sl = slice(j * HEAD_DIM, (j + 1) * HEAD_DIM)
        kh = head_norm(k[:, sl], gk_ref[...])
        if rope:
            kh = rotate(kh)
        else:
            kn_ref[:, sl] = kh
        k_ref[:, sl] = kh.astype(BF16)
    v_ref[...] = v.astype(BF16)
    if not rope:
        vf_ref[...] = v


def _segment_index(seg, tm):
    seg0, seg_rows = seg
    assert seg_rows % tm == 0
    return lambda i, *_: (seg0 + i * tm // seg_rows, 0, 0)


def _in_proj(x, g, scale, shift, w, layer, seg, gq, gk, rope_tabs=None):
    rows, d = x.shape
    tm = 512 if rows % 512 == 0 else 256
    seg = _segment_index(seg, tm)
    rope = rope_tabs is not None
    in_specs = [pl.BlockSpec((tm, d), lambda i: (i, 0)),
                _resident((1, d)),
                pl.BlockSpec((None, 1, d), seg),
                pl.BlockSpec((None, 1, d), seg),
                _resident_layer(w.shape, layer),
                _resident((1, HEAD_DIM)), _resident((1, HEAD_DIM))]
    args = [x, g, scale, shift, w, gq, gk]
    outs = [(FOURIER_WIDTH, BF16), (ATTN_WIDTH, BF16), (KV_WIDTH, BF16), (KV_WIDTH, BF16), (RWKV_IN, F32)]
    if rope:
        nt = rope_tabs[0].shape[0] // tm
        in_specs += [pl.BlockSpec((tm, HEAD_DIM), lambda i: (i % nt, 0))] * 2
        args += list(rope_tabs)
    else:
        outs += [(KV_WIDTH, F32), (KV_WIDTH, F32)]
    return pl.pallas_call(
        functools.partial(_inproj_body, rope=rope),
        grid=(rows // tm,),
        in_specs=in_specs,
        out_specs=[pl.BlockSpec((tm, width), lambda i: (i, 0)) for width, _ in outs],
        out_shape=[jax.ShapeDtypeStruct((rows, width), dt) for width, dt in outs],
        compiler_params=_params(("parallel",), 56),
        name="in_proj_rope" if rope else "in_proj",
    )(*args)


def _dft_tables(t):
    def angles(n):
        i = np.arange(n, dtype=np.int64)
        return (2.0 * math.pi / n) * ((i[:, None] * i[None, :]) % n)
    at = angles(t)
    time_tab = np.concatenate([np.cos(at), -np.sin(at)], axis=1).astype(np.float32)
    ac = angles(FOURIER_GROUP_WIDTH)
    eye = np.eye(FOURIER_GROUPS)
    chan_tab = np.concatenate([np.kron(eye, np.cos(ac)), np.kron(eye, np.sin(ac))], axis=1).astype(np.float32)
    return jnp.asarray(time_tab).astype(BF16), jnp.asarray(chan_tab).astype(BF16)


def _fourier_body(u_ref, ct_ref, cc_ref, o_ref, ab_scr, *, t, norm):
    @pl.when(pl.program_id(1) == 0)
    def _():
        ab = _dot(u_ref[...], cc_ref[...])
        ab_scr[0:t, :] = ab[:, :FOURIER_WIDTH].astype(BF16)
        ab_scr[t:2 * t, :] = ab[:, FOURIER_WIDTH:].astype(BF16)
    o_ref[...] = (_dot(ct_ref[...], ab_scr[...]) * norm).astype(o_ref.dtype)


def _fourier_mix(u, time_tab, chan_tab, row0, nseq, t):
    tm = min(t, 512)
    nt = t // tm
    seq0 = row0 // t
    return pl.pallas_call(
        functools.partial(_fourier_body, t=t, norm=1.0 / math.sqrt(t * FOURIER_GROUP_WIDTH)),
        grid=(nseq, nt),
        in_specs=[pl.BlockSpec((t, FOURIER_WIDTH), lambda b, i: (seq0 + b, 0)),
                  pl.BlockSpec((tm, 2 * t), lambda b, i: (i, 0)),
                  _resident(chan_tab.shape)],
        out_specs=pl.BlockSpec((tm, FOURIER_WIDTH), lambda b, i: (b * nt + i, 0)),
        out_shape=jax.ShapeDtypeStruct((nseq * t, FOURIER_WIDTH), BF16),
        scratch_shapes=[pltpu.VMEM((2 * t, FOURIER_WIDTH), BF16)],
        compiler_params=_params(("parallel", "arbitrary"), 40),
        name=f"fourier_mix_t{t}",
    )(u, time_tab, chan_tab)


def _rope_tables(t):
    pos = jnp.arange(t, dtype=jnp.int32)
    rows = (pos // GRID_W).astype(F32)
    cols = (pos % GRID_W).astype(F32)
    half = HEAD_DIM // 2
    inv = 1.0 / (ROPE_THETA ** (jnp.arange(0, half, 2, dtype=F32) / half))
    def tab(p):
        ang = p[:, None] * inv[None, :]
        return (jnp.concatenate([jnp.cos(ang), jnp.cos(ang)], -1),
                jnp.concatenate([-jnp.sin(ang), jnp.sin(ang)], -1))
    cr, sr = tab(rows)
    cc, sc = tab(cols)
    return jnp.concatenate([cr, cc], -1), jnp.concatenate([sr, sc], -1)


def _attn_body(*refs, cached):
    if cached:
        q_ref, k_ref, v_ref, kc_ref, vc_ref, o_ref = refs
    else:
        q_ref, k_ref, v_ref, o_ref = refs
    for j in range(ATTN_KV_HEADS):
        kv = slice(j * HEAD_DIM, (j + 1) * HEAD_DIM)
        kj, vj = k_ref[:, kv], v_ref[:, kv]
        if cached:
            kc, vc = kc_ref[:, kv].astype(BF16), vc_ref[:, kv].astype(BF16)
        for g in range(KV_GROUP):
            sl = slice((j * KV_GROUP + g) * HEAD_DIM, (j * KV_GROUP + g + 1) * HEAD_DIM)
            qh = q_ref[:, sl]
            s = _dot_nt(qh, kj)
            m = jnp.max(s, axis=-1, keepdims=True)
            if cached:
                sc = _dot_nt(qh, kc)
                m = jnp.maximum(m, jnp.max(sc, axis=-1, keepdims=True))
            p = jnp.exp(s - m)
            l = jnp.sum(p, axis=-1, keepdims=True)
            acc = _dot(p.astype(BF16), vj)
            if cached:
                pc = jnp.exp(sc - m)
                l = l + jnp.sum(pc, axis=-1, keepdims=True)
                acc = acc + _dot(pc.astype(BF16), vc)
            o_ref[:, sl] = (acc / l).astype(o_ref.dtype)


def _attention(q, k, v, nseq, t, cache_k=None, cache_v=None, layer=0):
    tq = 256
    nq = t // tq
    cached = cache_k is not None
    in_specs = [pl.BlockSpec((tq, ATTN_WIDTH), lambda b, i: (b * nq + i, 0)),
                pl.BlockSpec((t, KV_WIDTH), lambda b, i: (b, 0)),
                pl.BlockSpec((t, KV_WIDTH), lambda b, i: (b, 0))]
    args = [q, k, v]
    if cached:
        past = cache_k.shape[2]
        in_specs += [pl.BlockSpec((None, None, past, KV_WIDTH), lambda b, i: (b, layer, 0, 0))] * 2
        args += [cache_k, cache_v]
    return pl.pallas_call(
        functools.partial(_attn_body, cached=cached),
        grid=(nseq, nq),
        in_specs=in_specs,
        out_specs=pl.BlockSpec((tq, ATTN_WIDTH), lambda b, i: (b * nq + i, 0)),
        out_shape=jax.ShapeDtypeStruct((nseq * t, ATTN_WIDTH), BF16),
        compiler_params=_params(("parallel", "parallel"), 48),
        name="attention_cached" if cached else "attention",
    )(*args)


def _boundary_masks(row_start, tm, rows_lat, t_lat, t_ctx):
    in_lat = row_start < rows_lat
    length = jnp.where(in_lat, t_lat, t_ctx)
    base = jnp.where(in_lat, lax.rem(row_start, t_lat), lax.rem(row_start - rows_lat, t_ctx))
    pos = base + lax.broadcasted_iota(jnp.int32, (tm, 1), 0)
    first = pos == 0
    last = pos == length - 1
    for k in range(1, tm // min(t_lat, t_ctx) + 1):
        first = jnp.logical_or(first, pos == k * length)
        last = jnp.logical_or(last, pos == (k + 1) * length - 1)
    return 1.0 - first.astype(F32), 1.0 - last.astype(F32)


def _conv3(ext_ref, w_ref, tm, not_first, not_last, cols=slice(None)):
    return (w_ref[0:1, cols] * (ext_ref[HALO - 1:HALO - 1 + tm, cols] * not_first)
            + w_ref[1:2, cols] * ext_ref[HALO:HALO + tm, cols]
            + w_ref[2:3, cols] * (ext_ref[HALO + 1:HALO + 1 + tm, cols] * not_last))


def _head_sum_matrix():
    i = lax.broadcasted_iota(jnp.int32, (RWKV_WIDTH, RWKV_WIDTH), 0) // RWKV_N
    j = lax.broadcasted_iota(jnp.int32, (RWKV_WIDTH, RWKV_WIDTH), 1) // RWKV_N
    return (i == j).astype(BF16)


def _bf16_pieces(x, n):
    pieces = []
    for _ in range(n - 1):
        p = x.astype(BF16)
        pieces.append(p)
        x = x - p.astype(F32)
    return pieces + [x.astype(BF16)]


def _head_sums(x, ones_bd):
    hi, lo = _bf16_pieces(x, 2)
    return _dot(hi, ones_bd) + _dot(lo, ones_bd)


def _rwkv_prep_body(rw_ref, prev_ref, next_ref, cw_ref, wl_ref, w0_ref, a0_ref, kks_ref, ka_ref, rk_ref,
                    r_ref, v_ref, kk_ref, g_ref, bonus_ref, lw_ref, km_ref, b_ref, ext_scr,
                    *, tm, rows_lat, t_lat, t_ctx):
    ext_scr[0:HALO, :] = prev_ref[...]
    ext_scr[HALO:HALO + tm, :] = rw_ref[...]
    ext_scr[HALO + tm:, :] = next_ref[...]
    not_first, not_last = _boundary_masks(pl.program_id(0) * tm, tm, rows_lat, t_lat, t_ctx)
    z = _conv3(ext_scr, cw_ref, tm, not_first, not_last)
    c = RWKV_WIDTH
    r, k, v = z[:, :c], z[:, c:2 * c], z[:, 2 * c:3 * c]
    zl = z[:, 3 * c:]
    lane = lax.broadcasted_iota(jnp.int32, (1, LORA_IN), 1)
    lora_in = jnp.where(lane < DECAY_LORA, jnp.tanh(zl),
                        jnp.where(lane < DECAY_LORA + ICLR_LORA, zl, jax.nn.sigmoid(zl)))
    lora = _dot(lora_in.astype(BF16), wl_ref[...])
    ones_bd = _head_sum_matrix()
    kk = k * kks_ref[...]
    kk = kk * lax.rsqrt(_head_sums(kk * kk, ones_bd) + 1e-12)
    r_ref[...] = r
    v_ref[...] = v
    kk_ref[...] = kk
    g_ref[...] = lora[:, 4 * c:5 * c]
    kmod_sum = jnp.zeros_like(k)
    for d in range(2):
        wpre = w0_ref[d:d + 1, :] + lora[:, d * c:(d + 1) * c]
        w = -jax.nn.softplus(-wpre) - 0.5
        lw_ref[d] = -jnp.exp(w)
        a = jax.nn.sigmoid(a0_ref[d:d + 1, :] + lora[:, (2 + d) * c:(3 + d) * c])
        kmod = k * (1.0 + (a - 1.0) * ka_ref[...])
        km_ref[d] = kmod
        b_ref[d] = kk * a
        kmod_sum = kmod_sum + kmod
    bonus_ref[...] = _head_sums(r * kmod_sum * rk_ref[...], ones_bd) * v


def _rwkv_prep(rw, conv_w, lora_w, w0, a0, kk_scale, ka, rk, rows_lat, t_lat, t_ctx):
    rows = rw.shape[0]
    tm = 256
    nh = tm // HALO
    last = rows // HALO - 1
    c = RWKV_WIDTH
    one = lambda: pl.BlockSpec((tm, c), lambda i: (i, 0))
    two = lambda: pl.BlockSpec((2, tm, c), lambda i: (0, i, 0))
    return pl.pallas_call(
        functools.partial(_rwkv_prep_body, tm=tm, rows_lat=rows_lat, t_lat=t_lat, t_ctx=t_ctx),
        grid=(rows // tm,),
        in_specs=[pl.BlockSpec((tm, RWKV_IN), lambda i: (i, 0)),
                  pl.BlockSpec((HALO, RWKV_IN), lambda i: (jnp.maximum(i * nh - 1, 0), 0)),
                  pl.BlockSpec((HALO, RWKV_IN), lambda i: (jnp.minimum((i + 1) * nh, last), 0)),
                  _resident(conv_w.shape), _resident(lora_w.shape), _resident(w0.shape), _resident(a0.shape),
                  _resident(kk_scale.shape), _resident(ka.shape), _resident(rk.shape)],
        out_specs=[one(), one(), one(), one(), one(), two(), two(), two()],
        out_shape=[jax.ShapeDtypeStruct((rows, c), F32)] * 5 + [jax.ShapeDtypeStruct((2, rows, c), F32)] * 3,
        scratch_shapes=[pltpu.VMEM((tm + 2 * HALO, RWKV_IN), F32)],
        compiler_params=_params(("parallel",), 48),
        name="rwkv_prep",
    )(rw, rw, rw, conv_w, lora_w, w0, a0, kk_scale, ka, rk)


def _scan_direction_operands(r_ref, v_ref, kk_ref, lw_ref, km_ref, b_ref, backward):
    n = CHUNK
    row = lax.broadcasted_iota(jnp.int32, (n, n), 0)
    col = lax.broadcasted_iota(jnp.int32, (n, n), 1)
    upto = (col >= row) if backward else (col <= row)
    lw = lw_ref[...]
    tri = upto.astype(BF16)
    cs = sum(_dot(tri, piece) for piece in _bf16_pieces(lw, 3))
    tot = jnp.sum(lw, axis=0, keepdims=True)
    grow = jnp.exp(-cs)
    to_end = jnp.exp(tot - cs)
    b = b_ref[...]
    km = km_ref[...]
    return dict(
        kkt=(kk_ref[...] * jnp.exp(cs - lw)).astype(BF16),
        rt=(r_ref[...] * jnp.exp(cs)).astype(BF16),
        bt=(b * grow).astype(BF16), kt=(km * grow).astype(BF16),
        bh=(b * to_end).astype(BF16), kh=(km * to_end).astype(BF16),
        vb=v_ref[...].astype(BF16), w_all=jnp.exp(tot))


def _rwkv_scan_body(rf_ref, vf_ref, kkf_ref, lwf_ref, kmf_ref, bf_ref,
                    rb_ref, vb_ref, kkb_ref, lwb_ref, kmb_ref, bb_ref, s0_ref,
                    yf_ref, yb_ref, sfin_ref, s_scr):
    c = pl.program_id(1)

    @pl.when(c == 0)
    def _():
        s_scr[...] = s0_ref[...]

    n = CHUNK
    row = lax.broadcasted_iota(jnp.int32, (2 * n, 2 * n), 0)
    col = lax.broadcasted_iota(jnp.int32, (2 * n, 2 * n), 1)
    t_idx, s_idx, read_rows = row % n, col % n, row >= n
    eye = (lax.broadcasted_iota(jnp.int32, (n, n), 0) == lax.broadcasted_iota(jnp.int32, (n, n), 1)).astype(F32)
    ops = (_scan_direction_operands(rf_ref, vf_ref, kkf_ref, lwf_ref, kmf_ref, bf_ref, False),
           _scan_direction_operands(rb_ref, vb_ref, kkb_ref, lwb_ref, kmb_ref, bb_ref, True))
    same_step = jnp.logical_and(read_rows, s_idx == t_idx)
    masks = (jnp.logical_or(s_idx < t_idx, same_step), jnp.logical_or(s_idx > t_idx, same_step))
    y_refs = (yf_ref, yb_ref)
    chains = [(d, h) for d in range(2) for h in range(RWKV_HEADS)]
    head = lambda d, name, h: ops[d][name][:, h * RWKV_N:(h + 1) * RWKV_N]
    cat = lambda a, b: jnp.concatenate([a, b], axis=0)

    lhs = [cat(head(d, "kkt", h), head(d, "rt", h)) for d, h in chains]
    coef = [jnp.where(masks[d], _dot_nt(l, cat(head(d, "bt", h), head(d, "kt", h))), 0.0)
            for l, (d, h) in zip(lhs, chains)]
    n_mat = [a[:n, :n] for a in coef]
    state = [s_scr[d, h] for d, h in chains]
    state_b = [s.astype(BF16) for s in state]
    read = [_dot_nt(l, sb) for l, sb in zip(lhs, state_b)]
    akv = [_dot(a[:n, n:].astype(BF16), head(d, "vb", h)) for a, (d, h) in zip(coef, chains)]

    x = [eye - m for m in n_mat]
    p = [_dot(m.astype(BF16), m.astype(BF16)) for m in n_mat]
    steps = int(math.log2(n)) - 1
    for s in range(steps):
        pb = [q.astype(BF16) for q in p]
        x = [xi + _dot(xi.astype(BF16), q) for xi, q in zip(x, pb)]
        if s + 1 < steps:
            p = [_dot(q, q) for q in pb]

    u = [_dot(xi.astype(BF16), (-(rd[:n] + ak)).astype(BF16)) for xi, rd, ak in zip(x, read, akv)]
    uv = [cat(ui.astype(BF16), head(d, "vb", h)) for ui, (d, h) in zip(u, chains)]
    for (d, h), a, rd, uvi, s in zip(chains, coef, read, uv, state):
        sl = slice(h * RWKV_N, (h + 1) * RWKV_N)
        y_refs[d][:, sl] = rd[n:] + _dot(a[n:].astype(BF16), uvi)
        s_scr[d, h] = s * ops[d]["w_all"][:, sl] + _dot_tn(uvi, cat(head(d, "bh", h), head(d, "kh", h)))

    @pl.when(c == pl.num_programs(1) - 1)
    def _():
        sfin_ref[...] = s_scr[...]


def _rwkv_scan(r, v, kk, lw, km, b, states, layer, row0, nseq, t):
    n = CHUNK
    nc = t // n
    c0 = row0 // n
    c = RWKV_WIDTH
    fwd = lambda bb, cc: c0 + bb * nc + cc
    bwd = lambda bb, cc: c0 + bb * nc + nc - 1 - cc
    one = lambda chunk: pl.BlockSpec((n, c), lambda bb, cc: (chunk(bb, cc), 0))
    two = lambda chunk, d: pl.BlockSpec((None, n, c), lambda bb, cc: (d, chunk(bb, cc), 0))
    state = pl.BlockSpec((None, 2, RWKV_HEADS, RWKV_N, RWKV_N), lambda bb, cc: (bb, 0, 0, 0, 0))
    state_in = pl.BlockSpec((None, None, 2, RWKV_HEADS, RWKV_N, RWKV_N), lambda bb, cc: (bb, layer, 0, 0, 0, 0))
    y_shape = jax.ShapeDtypeStruct((nseq * t, c), F32)
    return pl.pallas_call(
        _rwkv_scan_body,
        grid=(nseq, nc),
        in_specs=[one(fwd), one(fwd), one(fwd), two(fwd, 0), two(fwd, 0), two(fwd, 0),
                  one(bwd), one(bwd), one(bwd), two(bwd, 1), two(bwd, 1), two(bwd, 1), state_in],
        out_specs=[pl.BlockSpec((n, c), lambda bb, cc: (bb * nc + cc, 0)),
                   pl.BlockSpec((n, c), lambda bb, cc: (bb * nc + nc - 1 - cc, 0)), state],
        out_shape=[y_shape, y_shape, jax.ShapeDtypeStruct((nseq, 2, RWKV_HEADS, RWKV_N, RWKV_N), F32)],
        scratch_shapes=[pltpu.VMEM((2, RWKV_HEADS, RWKV_N, RWKV_N), F32)],
        compiler_params=_params(("parallel", "arbitrary"), 32),
        name=f"rwkv_scan_t{t}",
    )(r, v, kk, lw, km, b, r, v, kk, lw, km, b, states)


def _outproj_body(f_ref, a_ref, yf_ref, yb_ref, bonus_ref, g_ref, lg_ref, lb_ref, x_ref, gate_ref, w_ref, o_ref):
    f0, a0, r0 = 0, FOURIER_WIDTH, FOURIER_WIDTH + ATTN_WIDTH
    mix = _dot(f_ref[...], w_ref[f0:a0, :]) + _dot(a_ref[...], w_ref[a0:r0, :])
    ones_bd = _head_sum_matrix()
    y = yf_ref[...] + yb_ref[...]
    mu = _head_sums(y, ones_bd) * (1.0 / RWKV_N)
    yc = y - mu
    var = _head_sums(yc * yc, ones_bd) * (1.0 / RWKV_N)
    yn = yc * lax.rsqrt(var + GN_EPS) * lg_ref[...] + lb_ref[...]
    r = ((yn + bonus_ref[...]) * g_ref[...]).astype(BF16)
    mix = mix + _dot(r, w_ref[r0:, :])
    o_ref[...] = x_ref[...] + gate_ref[...] * mix


def _out_proj(f, a, yf, yb, bonus, g, ln_g, ln_b, x, gate, w, layer, seg):
    rows, d = x.shape
    tm = 512 if rows % 512 == 0 else 256
    row = lambda arr: pl.BlockSpec((tm, arr.shape[1]), lambda i: (i, 0))
    return pl.pallas_call(
        _outproj_body,
        grid=(rows // tm,),
        in_specs=[row(f), row(a), row(yf), row(yb), row(bonus), row(g),
                  _resident(ln_g.shape), _resident(ln_b.shape), row(x),
                  pl.BlockSpec((None, 1, d), _segment_index(seg, tm)),
                  _resident_layer(w.shape, layer)],
        out_specs=row(x),
        out_shape=jax.ShapeDtypeStruct(x.shape, F32),
        compiler_params=_params(("parallel",), 48),
        name="out_proj",
    )(f, a, yf, yb, bonus, g, ln_g, ln_b, x, gate, w)


def _edge_masks(block_row, rows_lat, t_lat, t_ctx):
    in_lat = block_row < rows_lat
    length = jnp.where(in_lat, t_lat, t_ctx)
    off = jnp.where(in_lat, block_row, block_row - rows_lat)
    starts = (lax.rem(off, length) == 0).astype(F32)
    ends = (lax.rem(off + SEQ_ALIGN, length) == 0).astype(F32)
    sub = lax.broadcasted_iota(jnp.int32, (8, 1), 0)
    return 1.0 - starts * (sub == 0).astype(F32), 1.0 - ends * (sub == 7).astype(F32)


def _conv3_block(ext_ref, w_ref, b_ref, cols, r0, not_first8, not_last8):
    w0, w1, w2 = w_ref[0:1, cols], w_ref[1:2, cols], w_ref[2:3, cols]

    def rows(lo, hi, prev_mask=None, next_mask=None):
        at = lambda shift: ext_ref[HALO + r0 + shift + lo:HALO + r0 + shift + hi, cols]
        prev, nxt = at(-1), at(1)
        if prev_mask is not None:
            prev = prev * prev_mask
        if next_mask is not None:
            nxt = nxt * next_mask
        return w0 * prev + w1 * at(0) + w2 * nxt

    n = SEQ_ALIGN
    out = jnp.concatenate([rows(0, 8, prev_mask=not_first8), rows(8, n - 8), rows(n - 8, n, next_mask=not_last8)],
                          axis=0)
    return out + b_ref[:, cols]


def _ffn_body(x_ref, prev_ref, next_ref, g_ref, sc_ref, sh_ref, gate_ref, wa_ref, wg_ref,
              cwa_ref, cwg_ref, ba_ref, bg_ref, wd_ref, fg_ref, o_ref,
              h_scr, ua_scr, ug_scr, act_cur, act_new,
              *, tm, rows_lat, t_lat, t_ctx, final_norm):
    j = pl.program_id(1)
    nj = pl.num_programs(1) - 1

    @pl.when(j == 0)
    def _():
        norm = lambda x: _modulated_rmsnorm(x, g_ref[...], sc_ref[...], sh_ref[...]).astype(BF16)
        h_scr[0:HALO, :] = norm(prev_ref[...])
        h_scr[HALO:HALO + tm, :] = norm(x_ref[...])
        h_scr[HALO + tm:, :] = norm(next_ref[...])
        act_new[...] = jnp.zeros_like(act_new)
        o_ref[...] = jnp.zeros_like(o_ref)

    @pl.when(j < nj)
    def _():
        act_cur[...] = act_new[...]
        subs = [slice(s, s + FFN_SUB) for s in range(0, wa_ref.shape[1], FFN_SUB)]
        for cs in subs:
            ua_scr[:, cs] = _dot(h_scr[...], wa_ref[:, cs])
            ug_scr[:, cs] = _dot(h_scr[...], wg_ref[:, cs])
        blocks = range(0, tm, SEQ_ALIGN)
        edges = [_edge_masks(pl.program_id(0) * tm + r0, rows_lat, t_lat, t_ctx) for r0 in blocks]
        for cs in subs:
            for r0, edge in zip(blocks, edges):
                ua = _conv3_block(ua_scr, cwa_ref, ba_ref, cs, r0, *edge)
                ug = _conv3_block(ug_scr, cwg_ref, bg_ref, cs, r0, *edge)
                act_new[r0:r0 + SEQ_ALIGN, cs] = (ug * jax.nn.sigmoid(ug) * ua).astype(BF16)
        o_ref[...] += _dot(act_cur[...], wd_ref[...])

    @pl.when(j == nj)
    def _():
        out = x_ref[...] + gate_ref[...] * (o_ref[...] + _dot(act_new[...], wd_ref[...]))
        if final_norm:
            out = out * lax.rsqrt(jnp.mean(out * out, axis=-1, keepdims=True) + NORM_EPS) * fg_ref[...]
        o_ref[...] = out


def _conv_ffn(x, g, scale, shift, gate, wa, wg, cwa, cwg, ba, bg, wd, final_g, layer, seg, rows_lat, t_lat, t_ctx,
              final_norm):
    rows, d = x.shape
    tm = next(t for t in (1024, 512, 256) if rows % t == 0 and rows_lat % t == 0 and seg[1] % t == 0)
    tf = FFN_TILE
    nh = tm // HALO
    last = rows // HALO - 1
    nj = D_FF_PAD // tf
    seg = _segment_index(seg, tm)
    col = lambda r: pl.BlockSpec((r, tf), lambda i, j: (0, jnp.minimum(j, nj - 1)))
    up = pl.BlockSpec((None, d, tf), lambda i, j: (layer, 0, jnp.minimum(j, nj - 1)))
    return pl.pallas_call(
        functools.partial(_ffn_body, tm=tm, rows_lat=rows_lat, t_lat=t_lat, t_ctx=t_ctx, final_norm=final_norm),
        grid=(rows // tm, nj + 1),
        in_specs=[pl.BlockSpec((tm, d), lambda i, j: (i, 0), pipeline_mode=pl.Buffered(1)),
                  pl.BlockSpec((HALO, d), lambda i, j: (jnp.maximum(i * nh - 1, 0), 0)),
                  pl.BlockSpec((HALO, d), lambda i, j: (jnp.minimum((i + 1) * nh, last), 0)),
                  pl.BlockSpec((1, d), lambda i, j: (0, 0)),
                  pl.BlockSpec((None, 1, d), seg), pl.BlockSpec((None, 1, d), seg), pl.BlockSpec((None, 1, d), seg),
                  up, up, col(3), col(3), col(1), col(1),
                  pl.BlockSpec((None, tf, d), lambda i, j: (layer, jnp.maximum(j - 1, 0), 0)),
                  pl.BlockSpec((1, d), lambda i, j: (0, 0))],
        out_specs=pl.BlockSpec((tm, d), lambda i, j: (i, 0)),
        out_shape=jax.ShapeDtypeStruct(x.shape, F32),
        scratch_shapes=[pltpu.VMEM((tm + 2 * HALO, d), BF16),
                        pltpu.VMEM((tm + 2 * HALO, tf), F32),
                        pltpu.VMEM((tm + 2 * HALO, tf), F32),
                        pltpu.VMEM((tm, tf), BF16),
                        pltpu.VMEM((tm, tf), BF16)],
        compiler_params=_params(("parallel", "arbitrary"), 56),
        name="conv_ffn",
    )(x, x, x, g, scale, shift, gate, wa, wg, cwa, cwg, ba, bg, wd, final_g)


def _lora_weight(w2, a2, g2):
    c = RWKV_WIDTH
    wl = jnp.zeros((LORA_IN, 5 * c), F32)
    wl = wl.at[:DECAY_LORA, 0:c].set(w2[0]).at[:DECAY_LORA, c:2 * c].set(w2[1])
    wl = wl.at[DECAY_LORA:DECAY_LORA + ICLR_LORA, 2 * c:3 * c].set(a2[0])
    wl = wl.at[DECAY_LORA:DECAY_LORA + ICLR_LORA, 3 * c:4 * c].set(a2[1])
    return wl.at[DECAY_LORA + ICLR_LORA:, 4 * c:].set(g2)


LANES = 128


def _split_up_body(a_ref, g_ref, oa_ref, og_ref, *, nvalid):
    keep = pl.program_id(1) < nvalid
    oa_ref[...] = jnp.where(keep, a_ref[...], 0.0).astype(BF16)
    og_ref[...] = jnp.where(keep, g_ref[...], 0.0).astype(BF16)


def _split_ffn_up(ffn_up):
    depth, d, _ = ffn_up.shape
    nvalid = D_FF // LANES
    src = lambda half: pl.BlockSpec((None, d, LANES),
                                    lambda l, j: (l, 0, half * nvalid + jnp.minimum(j, nvalid - 1)))
    dst = pl.BlockSpec((None, d, LANES), lambda l, j: (l, 0, j))
    shape = jax.ShapeDtypeStruct((depth, d, D_FF_PAD), BF16)
    return pl.pallas_call(
        functools.partial(_split_up_body, nvalid=nvalid),
        grid=(depth, D_FF_PAD // LANES),
        in_specs=[src(0), src(1)], out_specs=[dst, dst], out_shape=[shape, shape],
        compiler_params=_params(("parallel", "parallel"), 32),
        name="split_ffn_up",
    )(ffn_up, ffn_up)


def _pad_down_body(w_ref, o_ref, *, tk):
    row = pl.program_id(1) * tk + lax.broadcasted_iota(jnp.int32, (tk, 1), 0)
    o_ref[...] = jnp.where(row < D_FF, w_ref[...], 0.0).astype(BF16)


def _pad_ffn_down(ffn_down):
    depth, _, d = ffn_down.shape
    tk = FFN_TILE
    return pl.pallas_call(
        functools.partial(_pad_down_body, tk=tk),
        grid=(depth, D_FF_PAD // tk),
        in_specs=[pl.BlockSpec((None, tk, d), lambda l, j: (l, j, 0))],
        out_specs=pl.BlockSpec((None, tk, d), lambda l, j: (l, j, 0)),
        out_shape=jax.ShapeDtypeStruct((depth, D_FF_PAD, d), BF16),
        compiler_params=_params(("parallel", "parallel"), 32),
        name="pad_ffn_down",
    )(ffn_down)


def _pad_cols(w):
    return jnp.pad(w, ((0, 0), (0, D_FF_PAD - D_FF)))


def _forward(x_lat, x_ctx, cache_k, cache_v, state, c, c_ctx, w_ada, b_ada, norm1_g, norm2_g, w_in, w_out,
             q_norm_g, k_norm_g, rw_conv, rw_w0, rw_w2, rw_a0, rw_a2, rw_g2, rw_kk, rw_ka, rw_rk,
             rw_lnx_g, rw_lnx_b, ffn_up, ffn_conv_w, ffn_conv_b, ffn_down, final_norm_g):
    n_lat, t_lat, d = x_lat.shape
    n_ctx, t_ctx, _ = x_ctx.shape
    depth = w_ada.shape[0]
    past = cache_k.shape[2]
    rows_lat, rows_ctx = n_lat * t_lat, n_ctx * t_ctx
    assert n_lat < MOD_ROWS
    assert t_lat % SEQ_ALIGN == 0 and t_ctx % SEQ_ALIGN == 0 and t_lat % GRID_W == 0

    cvec = jnp.concatenate([c, jnp.broadcast_to(c_ctx[None, :], (MOD_ROWS - n_lat, d))], axis=0)
    mod = _adaln_mod(cvec, w_ada, b_ada).reshape(depth, MOD_ROWS, 6, 1, d)

    ffn_wa, ffn_wg = _split_ffn_up(ffn_up)
    ffn_wd = _pad_ffn_down(ffn_down)
    rope_tabs = _rope_tables(t_lat)
    time_lat, chan_tab = _dft_tables(t_lat)
    time_ctx, _ = _dft_tables(t_ctx)
    zero_state = jnp.zeros((n_ctx, 1, 2, RWKV_HEADS, RWKV_N, RWKV_N), F32)
    row2 = lambda a: a.reshape(1, -1)

    cache = (cache_k.reshape(n_lat, depth, past, KV_WIDTH), cache_v.reshape(n_lat, depth, past, KV_WIDTH))

    def trunk_layer(x, l, latent):
        nseq, t = (n_lat, t_lat) if latent else (n_ctx, t_ctx)
        rows = nseq * t
        seg = (0, t_lat) if latent else (n_lat, rows)
        region_rows_lat = rows if latent else 0
        shift1, scale1, gate1, shift2, scale2, gate2 = (mod[l, :, i] for i in range(6))
        in_args = (x, row2(norm1_g[l]), scale1, shift1, w_in_b, l, seg, row2(q_norm_g[l]), row2(k_norm_g[l]))
        if latent:
            f, qb, kb, vb, rw = _in_proj(*in_args, rope_tabs)
            a_out = _attention(qb, kb, vb, nseq, t, cache[0], cache[1], l)
            k_norm = v = None
        else:
            f, qb, kb, vb, rw, k_norm, v = _in_proj(*in_args)
            a_out = _attention(qb, kb, vb, nseq, t)
        f_out = _fourier_mix(f, time_lat if latent else time_ctx, chan_tab, 0, nseq, t)

        r_, v_, kk, g_, bonus, lw, km, b_ = _rwkv_prep(
            rw, rw_conv[l], lora_w[l], rw_w0[l], rw_a0[l],
            row2(rw_kk[l]), row2(rw_ka[l]), row2(rw_rk[l]), region_rows_lat, t_lat, t_ctx)
        yf, yb, s_fin = _rwkv_scan(r_, v_, kk, lw, km, b_, state if latent else zero_state, l if latent else 0,
                                   0, nseq, t)
        x = _out_proj(f_out, a_out, yf, yb, bonus, g_, row2(rw_lnx_g[l]), row2(rw_lnx_b[l]), x, gate1,
                      w_out_b, l, seg)
        x = _conv_ffn(x, row2(norm2_g[l]), scale2, shift2, gate2, ffn_wa, ffn_wg,
                      _pad_cols(ffn_conv_w[l][:, :D_FF]), _pad_cols(ffn_conv_w[l][:, D_FF:]),
                      _pad_cols(row2(ffn_conv_b[l][:D_FF])), _pad_cols(row2(ffn_conv_b[l][D_FF:])),
                      ffn_wd, row2(final_norm_g), l, seg, region_rows_lat, t_lat, t_ctx, final_norm=l == depth - 1)
        return x, k_norm, v, s_fin

    w_in_b, w_out_b = w_in.astype(BF16), w_out.astype(BF16)
    lora_w = [_lora_weight(rw_w2[l], rw_a2[l], rw_g2[l]).astype(BF16) for l in range(depth)]
    xs, xc = x_lat.reshape(rows_lat, d), x_ctx.reshape(rows_ctx, d)
    new_k, new_v, new_s = [], [], []
    for l in range(depth):
        xc, k_norm, v_ctx, s_ctx = trunk_layer(xc, l, latent=False)
        new_k.append(k_norm.reshape(n_ctx, t_ctx, ATTN_KV_HEADS, HEAD_DIM))
        new_v.append(v_ctx.reshape(n_ctx, t_ctx, ATTN_KV_HEADS, HEAD_DIM))
        new_s.append(s_ctx)
        xs, _, _, _ = trunk_layer(xs, l, latent=True)

    y_lat, y_ctx = xs.reshape(n_lat, t_lat, d), xc.reshape(n_ctx, t_ctx, d)
    return (y_ctx, y_lat, jnp.stack(new_k, axis=1), jnp.stack(new_v, axis=1), jnp.stack(new_s, axis=1))


def kernel(x_prompt, x_sample, cache_attn_k, cache_attn_v, state_rwkv, c, c_ctx, w_ada, b_ada, norm1_g, norm2_g, w_in, w_out, q_norm_g, k_norm_g, rw_conv, rw_w0, rw_w2, rw_a0, rw_a2, rw_g2, rw_kk, rw_ka, rw_rk, rw_lnx_g, rw_lnx_b, ffn_up, ffn_conv_w, ffn_conv_b, ffn_down, final_norm_g):
    return _forward(x_sample, x_prompt, cache_attn_k, cache_attn_v, state_rwkv, c, c_ctx, w_ada, b_ada,
                    norm1_g, norm2_g, w_in, w_out, q_norm_g, k_norm_g, rw_conv, rw_w0, rw_w2, rw_a0, rw_a2,
                    rw_g2, rw_kk, rw_ka, rw_rk, rw_lnx_g, rw_lnx_b, ffn_up, ffn_conv_w, ffn_conv_b, ffn_down,
                    final_norm_g)
```

```python
import functools
import math

import jax
import jax.numpy as jnp
import numpy as np
from jax import lax
from jax.experimental import pallas as pl
from jax.experimental.pallas import tpu as pltpu

D_MODEL = 2048
GRID_W = 64
HEAD_DIM = 128
ATTN_HEADS = 8
ATTN_KV_HEADS = 2
KV_GROUP = ATTN_HEADS // ATTN_KV_HEADS
ATTN_WIDTH = ATTN_HEADS * HEAD_DIM
KV_WIDTH = ATTN_KV_HEADS * HEAD_DIM
FOURIER_WIDTH = 512
FOURIER_GROUPS = 4
FOURIER_GROUP_WIDTH = FOURIER_WIDTH // FOURIER_GROUPS
RWKV_WIDTH = 512
RWKV_N = 64
RWKV_HEADS = RWKV_WIDTH // RWKV_N
DECAY_LORA = 64
ICLR_LORA = 64
GATE_LORA = 128
LORA_IN = DECAY_LORA + ICLR_LORA + GATE_LORA
RWKV_IN = 3 * RWKV_WIDTH + LORA_IN
IN_WIDTH = FOURIER_WIDTH + ATTN_WIDTH + 2 * KV_WIDTH + RWKV_IN
D_FF = 5504
ROPE_THETA = 10000.0
NORM_EPS = 1e-6
GN_EPS = 64e-5

MOD_ROWS = 16
FFN_TILE = 512
FFN_SUB = 256
D_FF_PAD = -(-D_FF // FFN_TILE) * FFN_TILE
HALO = 16
CHUNK = 64
SEQ_ALIGN = 256
MIB = 2 ** 20

F32 = jnp.float32
BF16 = jnp.bfloat16


def _params(semantics, vmem_mib):
    return pltpu.CompilerParams(dimension_semantics=semantics, vmem_limit_bytes=vmem_mib * MIB)


def _resident(shape):
    return pl.BlockSpec(shape, lambda *_: (0,) * len(shape), pipeline_mode=pl.Buffered(1))


def _resident_layer(stacked_shape, layer):
    shape = stacked_shape[1:]
    return pl.BlockSpec((None,) + tuple(shape), lambda *_: (layer,) + (0,) * len(shape),
                        pipeline_mode=pl.Buffered(1))


def _dot(a, b):
    return jnp.dot(a, b, preferred_element_type=F32)


def _dot_nt(a, b):
    return lax.dot_general(a, b, (((1,), (1,)), ((), ())), preferred_element_type=F32)


def _dot_tn(a, b):
    return lax.dot_general(a, b, (((0,), (0,)), ((), ())), preferred_element_type=F32)


def _modulated_rmsnorm(x, g, scale, shift):
    ms = jnp.mean(x * x, axis=-1, keepdims=True)
    return (x * lax.rsqrt(ms + NORM_EPS) * g) * (1.0 + scale) + shift


def _mod_body(c_ref, w_ref, b_ref, o_ref):
    c = c_ref[...]
    s = (c * jax.nn.sigmoid(c)).astype(BF16)
    o_ref[...] = _dot(s, w_ref[...].astype(BF16)) + b_ref[...]


def _adaln_mod(cvec, w_ada, b_ada):
    depth, d, n = w_ada.shape
    tn = 1024
    return pl.pallas_call(
        _mod_body,
        grid=(depth, n // tn),
        in_specs=[pl.BlockSpec((MOD_ROWS, d), lambda l, j: (0, 0)),
                  pl.BlockSpec((None, d, tn), lambda l, j: (l, 0, j)),
                  pl.BlockSpec((None, 1, tn), lambda l, j: (l, 0, j))],
        out_specs=pl.BlockSpec((None, MOD_ROWS, tn), lambda l, j: (l, 0, j)),
        out_shape=jax.ShapeDtypeStruct((depth, MOD_ROWS, n), F32),
        compiler_params=_params(("parallel", "parallel"), 40),
        name="adaln_mod",
    )(cvec, w_ada, b_ada.reshape(depth, 1, n))


Q_COL = FOURIER_WIDTH
K_COL = Q_COL + ATTN_WIDTH
V_COL = K_COL + KV_WIDTH
RW_COL = V_COL + KV_WIDTH


def _inproj_body(*refs, rope):
    x_ref, g_ref, sc_ref, sh_ref, w_ref, gq_ref, gk_ref = refs[:7]
    if rope:
        cos_ref, sin_ref, f_ref, q_ref, k_ref, v_ref, rw_ref = refs[7:]
        cos, sin = cos_ref[...], sin_ref[...]
        lane = lax.broadcasted_iota(jnp.int32, (1, HEAD_DIM), 1)
        low = (lane % (HEAD_DIM // 2)) < (HEAD_DIM // 4)
    else:
        f_ref, q_ref, k_ref, v_ref, rw_ref, kn_ref, vf_ref = refs[7:]

    h = _modulated_rmsnorm(x_ref[...], g_ref[...], sc_ref[...], sh_ref[...]).astype(BF16)
    proj = lambda start, width: _dot(h, w_ref[:, start:start + width])
    q, k, v = proj(Q_COL, ATTN_WIDTH), proj(K_COL, KV_WIDTH), proj(V_COL, KV_WIDTH)
    f_ref[...] = proj(0, FOURIER_WIDTH).astype(BF16)
    rw_ref[...] = proj(RW_COL, RWKV_IN)

    def head_norm(xh, g):
        ms = jnp.mean(xh * xh, axis=-1, keepdims=True)
        return xh * lax.rsqrt(ms + NORM_EPS) * g

    def rotate(xh):
        partner = jnp.where(low, pltpu.roll(xh, HEAD_DIM - HEAD_DIM // 4, 1), pltpu.roll(xh, HEAD_DIM // 4, 1))
        return xh * cos + partner * sin

    scale = HEAD_DIM ** -0.5
    for hd in range(ATTN_HEADS):
        sl = slice(hd * HEAD_DIM, (hd + 1) * HEAD_DIM)
        qh = head_norm(q[:, sl], gq_ref[...])
        if rope:
            qh = rotate(qh)
        q_ref[:, sl] = (qh * scale).astype(BF16)
    for j in range(ATTN_KV_HEADS):
        sl = slice(j * HEAD_DIM, (j + 1) * HEAD_DIM)
        kh = head_norm(k[:, sl], gk_ref[...])
        if rope:
            kh = rotate(kh)
        else:
            kn_ref[:, sl] = kh
        k_ref[:, sl] = kh.astype(BF16)
    v_ref[...] = v.astype(BF16)
    if not rope:
        vf_ref[...] = v


def _segment_index(seg, tm):
    seg0, seg_rows = seg
    assert seg_rows % tm == 0
    return lambda i, *_: (seg0 + i * tm // seg_rows, 0, 0)


def _in_proj(x, g, scale, shift, w, layer, seg, gq, gk, rope_tabs=None):
    rows, d = x.shape
    tm = 512 if rows % 512 == 0 else 256
    seg = _segment_index(seg, tm)
    rope = rope_tabs is not None
    in_specs = [pl.BlockSpec((tm, d), lambda i: (i, 0)),
                _resident((1, d)),
                pl.BlockSpec((None, 1, d), seg),
                pl.BlockSpec((None, 1, d), seg),
                _resident_layer(w.shape, layer),
                _resident((1, HEAD_DIM)), _resident((1, HEAD_DIM))]
    args = [x, g, scale, shift, w, gq, gk]
    outs = [(FOURIER_WIDTH, BF16), (ATTN_WIDTH, BF16), (KV_WIDTH, BF16), (KV_WIDTH, BF16), (RWKV_IN, F32)]
    if rope:
        nt = rope_tabs[0].shape[0] // tm
        in_specs += [pl.BlockSpec((tm, HEAD_DIM), lambda i: (i % nt, 0))] * 2
        args += list(rope_tabs)
    else:
        outs += [(KV_WIDTH, F32), (KV_WIDTH, F32)]
    return pl.pallas_call(
        functools.partial(_inproj_body, rope=rope),
        grid=(rows // tm,),
        in_specs=in_specs,
        out_specs=[pl.BlockSpec((tm, width), lambda i: (i, 0)) for width, _ in outs],
        out_shape=[jax.ShapeDtypeStruct((rows, width), dt) for width, dt in outs],
        compiler_params=_params(("parallel",), 56),
        name="in_proj_rope" if rope else "in_proj",
    )(*args)


def _dft_tables(t):
    def angles(n):
        i = np.arange(n, dtype=np.int64)
        return (2.0 * math.pi / n) * ((i[:, None] * i[None, :]) % n)
    at = angles(t)
    time_tab = np.concatenate([np.cos(at), -np.sin(at)], axis=1).astype(np.float32)
    ac = angles(FOURIER_GROUP_WIDTH)
    eye = np.eye(FOURIER_GROUPS)
    chan_tab = np.concatenate([np.kron(eye, np.cos(ac)), np.kron(eye, np.sin(ac))], axis=1).astype(np.float32)
    return jnp.asarray(time_tab).astype(BF16), jnp.asarray(chan_tab).astype(BF16)


def _fourier_body(u_ref, ct_ref, cc_ref, o_ref, ab_scr, *, t, norm):
    @pl.when(pl.program_id(1) == 0)
    def _():
        ab = _dot(u_ref[...], cc_ref[...])
        ab_scr[0:t, :] = ab[:, :FOURIER_WIDTH].astype(BF16)
        ab_scr[t:2 * t, :] = ab[:, FOURIER_WIDTH:].astype(BF16)
    o_ref[...] = (_dot(ct_ref[...], ab_scr[...]) * norm).astype(o_ref.dtype)


def _fourier_mix(u, time_tab, chan_tab, row0, nseq, t):
    tm = min(t, 512)
    nt = t // tm
    seq0 = row0 // t
    return pl.pallas_call(
        functools.partial(_fourier_body, t=t, norm=1.0 / math.sqrt(t * FOURIER_GROUP_WIDTH)),
        grid=(nseq, nt),
        in_specs=[pl.BlockSpec((t, FOURIER_WIDTH), lambda b, i: (seq0 + b, 0)),
                  pl.BlockSpec((tm, 2 * t), lambda b, i: (i, 0)),
                  _resident(chan_tab.shape)],
        out_specs=pl.BlockSpec((tm, FOURIER_WIDTH), lambda b, i: (b * nt + i, 0)),
        out_shape=jax.ShapeDtypeStruct((nseq * t, FOURIER_WIDTH), BF16),
        scratch_shapes=[pltpu.VMEM((2 * t, FOURIER_WIDTH), BF16)],
        compiler_params=_params(("parallel", "arbitrary"), 40),
        name=f"fourier_mix_t{t}",
    )(u, time_tab, chan_tab)


def _rope_tables(t):
    pos = jnp.arange(t, dtype=jnp.int32)
    rows = (pos // GRID_W).astype(F32)
    cols = (pos % GRID_W).astype(F32)
    half = HEAD_DIM // 2
    inv = 1.0 / (ROPE_THETA ** (jnp.arange(0, half, 2, dtype=F32) / half))
    def tab(p):
        ang = p[:, None] * inv[None, :]
        return (jnp.concatenate([jnp.cos(ang), jnp.cos(ang)], -1),
                jnp.concatenate([-jnp.sin(ang), jnp.sin(ang)], -1))
    cr, sr = tab(rows)
    cc, sc = tab(cols)
    return jnp.concatenate([cr, cc], -1), jnp.concatenate([sr, sc], -1)


def _attn_body(*refs, cached):
    if cached:
        q_ref, k_ref, v_ref, kc_ref, vc_ref, o_ref = refs
    else:
        q_ref, k_ref, v_ref, o_ref = refs
    for j in range(ATTN_KV_HEADS):
        kv = slice(j * HEAD_DIM, (j + 1) * HEAD_DIM)
        kj, vj = k_ref[:, kv], v_ref[:, kv]
        if cached:
            kc, vc = kc_ref[:, kv].astype(BF16), vc_ref[:, kv].astype(BF16)
        for g in range(KV_GROUP):
            sl = slice((j * KV_GROUP + g) * HEAD_DIM, (j * KV_GROUP + g + 1) * HEAD_DIM)
            qh = q_ref[:, sl]
            s = _dot_nt(qh, kj)
            m = jnp.max(s, axis=-1, keepdims=True)
            if cached:
                sc = _dot_nt(qh, kc)
                m = jnp.maximum(m, jnp.max(sc, axis=-1, keepdims=True))
            p = jnp.exp(s - m)
            l = jnp.sum(p, axis=-1, keepdims=True)
            acc = _dot(p.astype(BF16), vj)
            if cached:
                pc = jnp.exp(sc - m)
                l = l + jnp.sum(pc, axis=-1, keepdims=True)
                acc = acc + _dot(pc.astype(BF16), vc)
            o_ref[:, sl] = (acc / l).astype(o_ref.dtype)


def _attention(q, k, v, nseq, t, cache_k=None, cache_v=None, layer=0):
    tq = 256
    nq = t // tq
    cached = cache_k is not None
    in_specs = [pl.BlockSpec((tq, ATTN_WIDTH), lambda b, i: (b * nq + i, 0)),
                pl.BlockSpec((t, KV_WIDTH), lambda b, i: (b, 0)),
                pl.BlockSpec((t, KV_WIDTH), lambda b, i: (b, 0))]
    args = [q, k, v]
    if cached:
        past = cache_k.shape[2]
        in_specs += [pl.BlockSpec((None, None, past, KV_WIDTH), lambda b, i: (b, layer, 0, 0))] * 2
        args += [cache_k, cache_v]
    return pl.pallas_call(
        functools.partial(_attn_body, cached=cached),
        grid=(nseq, nq),
        in_specs=in_specs,
        out_specs=pl.BlockSpec((tq, ATTN_WIDTH), lambda b, i: (b * nq + i, 0)),
        out_shape=jax.ShapeDtypeStruct((nseq * t, ATTN_WIDTH), BF16),
        compiler_params=_params(("parallel", "parallel"), 48),
        name="attention_cached" if cached else "attention",
    )(*args)


def _head_sum_matrix():
    i = lax.broadcasted_iota(jnp.int32, (RWKV_WIDTH, RWKV_WIDTH), 0) // RWKV_N
    j = lax.broadcasted_iota(jnp.int32, (RWKV_WIDTH, RWKV_WIDTH), 1) // RWKV_N
    return (i == j).astype(BF16)


def _bf16_pieces(x, n):
    pieces = []
    for _ in range(n - 1):
        p = x.astype(BF16)
        pieces.append(p)
        x = x - p.astype(F32)
    return pieces + [x.astype(BF16)]


def _head_sums(x, ones_bd):
    hi, lo = _bf16_pieces(x, 2)
    return _dot(hi, ones_bd) + _dot(lo, ones_bd)


def _rwkv_prep_body(rw_ref, prev_ref, next_ref, cw_ref, wl_ref, w0_ref, a0_ref, kks_ref, ka_ref, rk_ref,
                    r_ref, v_ref, kk_ref, g_ref, bonus_ref, lw_ref, km_ref, b_ref, ext_scr,
                    *, tm, rows_lat, t_lat, t_ctx):
    ext_scr[0:HALO, :] = prev_ref[...]
    ext_scr[HALO:HALO + tm, :] = rw_ref[...]
    ext_scr[HALO + tm:, :] = next_ref[...]
    edges = _edge_masks(pl.program_id(0) * tm, rows_lat, t_lat, t_ctx)
    z = _conv3_block(ext_scr, cw_ref, None, slice(None), 0, *edges)
    c = RWKV_WIDTH
    r, k, v = z[:, :c], z[:, c:2 * c], z[:, 2 * c:3 * c]
    zl = z[:, 3 * c:]
    lane = lax.broadcasted_iota(jnp.int32, (1, LORA_IN), 1)
    lora_in = jnp.where(lane < DECAY_LORA, jnp.tanh(zl),
                        jnp.where(lane < DECAY_LORA + ICLR_LORA, zl, jax.nn.sigmoid(zl)))
    lora = _dot(lora_in.astype(BF16), wl_ref[...])
    ones_bd = _head_sum_matrix()
    kk = k * kks_ref[...]
    kk = kk * lax.rsqrt(_head_sums(kk * kk, ones_bd) + 1e-12)
    r_ref[...] = r
    v_ref[...] = v
    kk_ref[...] = kk
    g_ref[...] = lora[:, 4 * c:5 * c]
    kmod_sum = jnp.zeros_like(k)
    for d in range(2):
        wpre = w0_ref[d:d + 1, :] + lora[:, d * c:(d + 1) * c]
        lw_ref[d] = -math.exp(-0.5) * jax.nn.sigmoid(wpre)
        a = jax.nn.sigmoid(a0_ref[d:d + 1, :] + lora[:, (2 + d) * c:(3 + d) * c])
        kmod = k * (1.0 + (a - 1.0) * ka_ref[...])
        km_ref[d] = kmod
        b_ref[d] = kk * a
        kmod_sum = kmod_sum + kmod
    bonus_ref[...] = _head_sums(r * kmod_sum * rk_ref[...], ones_bd) * v


def _rwkv_prep(rw, conv_w, lora_w, w0, a0, kk_scale, ka, rk, rows_lat, t_lat, t_ctx):
    rows = rw.shape[0]
    tm = 256
    nh = tm // HALO
    last = rows // HALO - 1
    c = RWKV_WIDTH
    one = lambda: pl.BlockSpec((tm, c), lambda i: (i, 0))
    two = lambda: pl.BlockSpec((2, tm, c), lambda i: (0, i, 0))
    return pl.pallas_call(
        functools.partial(_rwkv_prep_body, tm=tm, rows_lat=rows_lat, t_lat=t_lat, t_ctx=t_ctx),
        grid=(rows // tm,),
        in_specs=[pl.BlockSpec((tm, RWKV_IN), lambda i: (i, 0)),
                  pl.BlockSpec((HALO, RWKV_IN), lambda i: (jnp.maximum(i * nh - 1, 0), 0)),
                  pl.BlockSpec((HALO, RWKV_IN), lambda i: (jnp.minimum((i + 1) * nh, last), 0)),
                  _resident(conv_w.shape), _resident(lora_w.shape), _resident(w0.shape), _resident(a0.shape),
                  _resident(kk_scale.shape), _resident(ka.shape), _resident(rk.shape)],
        out_specs=[one(), one(), one(), one(), one(), two(), two(), two()],
        out_shape=[jax.ShapeDtypeStruct((rows, c), F32)] * 5 + [jax.ShapeDtypeStruct((2, rows, c), F32)] * 3,
        scratch_shapes=[pltpu.VMEM((tm + 2 * HALO, RWKV_IN), F32)],
        compiler_params=_params(("parallel",), 48),
        name="rwkv_prep",
    )(rw, rw, rw, conv_w, lora_w, w0, a0, kk_scale, ka, rk)


def _scan_direction_operands(r_ref, v_ref, kk_ref, lw_ref, km_ref, b_ref, backward):
    n = CHUNK
    row = lax.broadcasted_iota(jnp.int32, (n, n), 0)
    col = lax.broadcasted_iota(jnp.int32, (n, n), 1)
    upto = (col >= row) if backward else (col <= row)
    lw = lw_ref[...]
    tri = upto.astype(BF16)
    cs = sum(_dot(tri, piece) for piece in _bf16_pieces(lw, 3))
    tot = jnp.sum(lw, axis=0, keepdims=True)
    grow = jnp.exp(-cs)
    to_end = jnp.exp(tot - cs)
    b = b_ref[...]
    km = km_ref[...]
    return dict(
        kkt=(kk_ref[...] * jnp.exp(cs - lw)).astype(BF16),
        rt=(r_ref[...] * jnp.exp(cs)).astype(BF16),
        bt=(b * grow).astype(BF16), kt=(km * grow).astype(BF16),
        bh=(b * to_end).astype(BF16), kh=(km * to_end).astype(BF16),
        vb=v_ref[...].astype(BF16), w_all=jnp.exp(tot))


def _rwkv_scan_body(rf_ref, vf_ref, kkf_ref, lwf_ref, kmf_ref, bf_ref,
                    rb_ref, vb_ref, kkb_ref, lwb_ref, kmb_ref, bb_ref, s0_ref,
                    yf_ref, yb_ref, sfin_ref, s_scr):
    c = pl.program_id(1)

    @pl.when(c == 0)
    def _():
        s_scr[...] = s0_ref[...]

    n = CHUNK
    row = lax.broadcasted_iota(jnp.int32, (2 * n, 2 * n), 0)
    col = lax.broadcasted_iota(jnp.int32, (2 * n, 2 * n), 1)
    t_idx, s_idx, read_rows = row % n, col % n, row >= n
    eye = (lax.broadcasted_iota(jnp.int32, (n, n), 0) == lax.broadcasted_iota(jnp.int32, (n, n), 1)).astype(F32)
    ops = (_scan_direction_operands(rf_ref, vf_ref, kkf_ref, lwf_ref, kmf_ref, bf_ref, False),
           _scan_direction_operands(rb_ref, vb_ref, kkb_ref, lwb_ref, kmb_ref, bb_ref, True))
    same_step = jnp.logical_and(read_rows, s_idx == t_idx)
    masks = (jnp.logical_or(s_idx < t_idx, same_step), jnp.logical_or(s_idx > t_idx, same_step))
    y_refs = (yf_ref, yb_ref)
    chains = [(d, h) for d in range(2) for h in range(RWKV_HEADS)]
    head = lambda d, name, h: ops[d][name][:, h * RWKV_N:(h + 1) * RWKV_N]
    cat = lambda a, b: jnp.concatenate([a, b], axis=0)

    lhs = [cat(head(d, "kkt", h), head(d, "rt", h)) for d, h in chains]
    coef = [jnp.where(masks[d], _dot_nt(l, cat(head(d, "bt", h), head(d, "kt", h))), 0.0)
            for l, (d, h) in zip(lhs, chains)]
    n_mat = [a[:n, :n] for a in coef]
    state = [s_scr[d, h] for d, h in chains]
    state_b = [s.astype(BF16) for s in state]
    read = [_dot_nt(l, sb) for l, sb in zip(lhs, state_b)]
    akv = [_dot(a[:n, n:].astype(BF16), head(d, "vb", h)) for a, (d, h) in zip(coef, chains)]

    x = [eye - m for m in n_mat]
    p = [_dot(m.astype(BF16), m.astype(BF16)) for m in n_mat]
    steps = int(math.log2(n)) - 1
    for s in range(steps):
        pb = [q.astype(BF16) for q in p]
        x = [xi + _dot(xi.astype(BF16), q) for xi, q in zip(x, pb)]
        if s + 1 < steps:
            p = [_dot(q, q) for q in pb]

    u = [_dot(xi.astype(BF16), (-(rd[:n] + ak)).astype(BF16)) for xi, rd, ak in zip(x, read, akv)]
    uv = [cat(ui.astype(BF16), head(d, "vb", h)) for ui, (d, h) in zip(u, chains)]
    for (d, h), a, rd, uvi, s in zip(chains, coef, read, uv, state):
        sl = slice(h * RWKV_N, (h + 1) * RWKV_N)
        y_refs[d][:, sl] = rd[n:] + _dot(a[n:].astype(BF16), uvi)
        s_scr[d, h] = s * ops[d]["w_all"][:, sl] + _dot_tn(uvi, cat(head(d, "bh", h), head(d, "kh", h)))

    @pl.when(c == pl.num_programs(1) - 1)
    def _():
        sfin_ref[...] = s_scr[...]


def _rwkv_scan(r, v, kk, lw, km, b, states, layer, row0, nseq, t):
    n = CHUNK
    nc = t // n
    c0 = row0 // n
    c = RWKV_WIDTH
    fwd = lambda bb, cc: c0 + bb * nc + cc
    bwd = lambda bb, cc: c0 + bb * nc + nc - 1 - cc
    one = lambda chunk: pl.BlockSpec((n, c), lambda bb, cc: (chunk(bb, cc), 0))
    two = lambda chunk, d: pl.BlockSpec((None, n, c), lambda bb, cc: (d, chunk(bb, cc), 0))
    state = pl.BlockSpec((None, 2, RWKV_HEADS, RWKV_N, RWKV_N), lambda bb, cc: (bb, 0, 0, 0, 0))
    state_in = pl.BlockSpec((None, None, 2, RWKV_HEADS, RWKV_N, RWKV_N), lambda bb, cc: (bb, layer, 0, 0, 0, 0))
    y_shape = jax.ShapeDtypeStruct((nseq * t, c), F32)
    return pl.pallas_call(
        _rwkv_scan_body,
        grid=(nseq, nc),
        in_specs=[one(fwd), one(fwd), one(fwd), two(fwd, 0), two(fwd, 0), two(fwd, 0),
                  one(bwd), one(bwd), one(bwd), two(bwd, 1), two(bwd, 1), two(bwd, 1), state_in],
        out_specs=[pl.BlockSpec((n, c), lambda bb, cc: (bb * nc + cc, 0)),
                   pl.BlockSpec((n, c), lambda bb, cc: (bb * nc + nc - 1 - cc, 0)), state],
        out_shape=[y_shape, y_shape, jax.ShapeDtypeStruct((nseq, 2, RWKV_HEADS, RWKV_N, RWKV_N), F32)],
        scratch_shapes=[pltpu.VMEM((2, RWKV_HEADS, RWKV_N, RWKV_N), F32)],
        compiler_params=_params(("parallel", "arbitrary"), 32),
        name=f"rwkv_scan_t{t}",
    )(r, v, kk, lw, km, b, r, v, kk, lw, km, b, states)


def _outproj_body(f_ref, a_ref, yf_ref, yb_ref, bonus_ref, g_ref, lg_ref, lb_ref, x_ref, gate_ref, w_ref, o_ref):
    f0, a0, r0 = 0, FOURIER_WIDTH, FOURIER_WIDTH + ATTN_WIDTH
    mix = _dot(f_ref[...], w_ref[f0:a0, :]) + _dot(a_ref[...], w_ref[a0:r0, :])
    ones_bd = _head_sum_matrix()
    y = yf_ref[...] + yb_ref[...]
    mu = _head_sums(y, ones_bd) * (1.0 / RWKV_N)
    yc = y - mu
    var = _head_sums(yc * yc, ones_bd) * (1.0 / RWKV_N)
    yn = yc * lax.rsqrt(var + GN_EPS) * lg_ref[...] + lb_ref[...]
    r = ((yn + bonus_ref[...]) * g_ref[...]).astype(BF16)
    mix = mix + _dot(r, w_ref[r0:, :])
    o_ref[...] = x_ref[...] + gate_ref[...] * mix


def _out_proj(f, a, yf, yb, bonus, g, ln_g, ln_b, x, gate, w, layer, seg):
    rows, d = x.shape
    tm = 512 if rows % 512 == 0 else 256
    row = lambda arr: pl.BlockSpec((tm, arr.shape[1]), lambda i: (i, 0))
    return pl.pallas_call(
        _outproj_body,
        grid=(rows // tm,),
        in_specs=[row(f), row(a), row(yf), row(yb), row(bonus), row(g),
                  _resident(ln_g.shape), _resident(ln_b.shape), row(x),
                  pl.BlockSpec((None, 1, d), _segment_index(seg, tm)),
                  _resident_layer(w.shape, layer)],
        out_specs=row(x),
        out_shape=jax.ShapeDtypeStruct(x.shape, F32),
        compiler_params=_params(("parallel",), 48),
        name="out_proj",
    )(f, a, yf, yb, bonus, g, ln_g, ln_b, x, gate, w)


def _edge_masks(block_row, rows_lat, t_lat, t_ctx):
    in_lat = block_row < rows_lat
    length = jnp.where(in_lat, t_lat, t_ctx)
    off = jnp.where(in_lat, block_row, block_row - rows_lat)
    starts = (lax.rem(off, length) == 0).astype(F32)
    ends = (lax.rem(off + SEQ_ALIGN, length) == 0).astype(F32)
    sub = lax.broadcasted_iota(jnp.int32, (8, 1), 0)
    return 1.0 - starts * (sub == 0).astype(F32), 1.0 - ends * (sub == 7).astype(F32)


def _conv3_block(ext_ref, w_ref, b_ref, cols, r0, not_first8, not_last8):
    w0, w1, w2 = w_ref[0:1, cols], w_ref[1:2, cols], w_ref[2:3, cols]

    def rows(lo, hi, prev_mask=None, next_mask=None):
        at = lambda shift: ext_ref[HALO + r0 + shift + lo:HALO + r0 + shift + hi, :]
        prev, nxt = at(-1), at(1)
        if prev_mask is not None:
            prev = prev * prev_mask
        if next_mask is not None:
            nxt = nxt * next_mask
        return w0 * prev + w1 * at(0) + w2 * nxt

    n = SEQ_ALIGN
    out = jnp.concatenate([rows(0, 8, prev_mask=not_first8), rows(8, n - 8), rows(n - 8, n, next_mask=not_last8)],
                          axis=0)
    return out if b_ref is None else out + b_ref[:, cols]


def _ffn_body(x_ref, prev_ref, next_ref, g_ref, sc_ref, sh_ref, gate_ref, wa_ref, wg_ref,
              cwa_ref, cwg_ref, ba_ref, bg_ref, wd_ref, fg_ref, o_ref,
              h_scr, ua0_scr, ua1_scr, ug0_scr, ug1_scr, act_cur, act_new,
              *, tm, rows_lat, t_lat, t_ctx, final_norm):
    j = pl.program_id(1)
    nj = pl.num_programs(1) - 1

    @pl.when(j == 0)
    def _():
        norm = lambda x: _modulated_rmsnorm(x, g_ref[...], sc_ref[...], sh_ref[...]).astype(BF16)
        h_scr[0:HALO, :] = norm(prev_ref[...])
        h_scr[HALO:HALO + tm, :] = norm(x_ref[...])
        h_scr[HALO + tm:, :] = norm(next_ref[...])
        act_new[...] = jnp.zeros_like(act_new)
        o_ref[...] = jnp.zeros_like(o_ref)

    subs = [slice(s, s + FFN_SUB) for s in range(0, wa_ref.shape[1], FFN_SUB)]
    blocks = range(0, tm, SEQ_ALIGN)

    u_scr = ((ua0_scr, ug0_scr), (ua1_scr, ug1_scr))
    assert len(subs) == len(u_scr)

    def up_project(s):
        ua_scr, ug_scr = u_scr[s]
        ua_scr[...] = _dot(h_scr[...], wa_ref[:, subs[s]])
        ug_scr[...] = _dot(h_scr[...], wg_ref[:, subs[s]])

    def conv_gate(s):
        ua_scr, ug_scr = u_scr[s]
        for r0 in blocks:
            edge = _edge_masks(pl.program_id(0) * tm + r0, rows_lat, t_lat, t_ctx)
            ua = _conv3_block(ua_scr, cwa_ref, ba_ref, subs[s], r0, *edge)
            ug = _conv3_block(ug_scr, cwg_ref, bg_ref, subs[s], r0, *edge)
            act_new[r0:r0 + SEQ_ALIGN, subs[s]] = (ug * jax.nn.sigmoid(ug) * ua).astype(BF16)

    @pl.when(j < nj)
    def _():
        act_cur[...] = act_new[...]
        up_project(0)
        up_project(1)
        conv_gate(0)
        conv_gate(1)
        o_ref[...] += _dot(act_cur[...], wd_ref[...])

    @pl.when(j == nj)
    def _():
        out = x_ref[...] + gate_ref[...] * (o_ref[...] + _dot(act_new[...], wd_ref[...]))
        if final_norm:
            out = out * lax.rsqrt(jnp.mean(out * out, axis=-1, keepdims=True) + NORM_EPS) * fg_ref[...]
        o_ref[...] = out


def _conv_ffn(x, g, scale, shift, gate, wa, wg, cwa, cwg, ba, bg, wd, final_g, layer, seg, rows_lat, t_lat, t_ctx,
              final_norm):
    rows, d = x.shape
    tm = next(t for t in (1024, 512, 256) if rows % t == 0 and rows_lat % t == 0 and seg[1] % t == 0)
    tf = FFN_TILE
    nh = tm // HALO
    last = rows // HALO - 1
    nj = D_FF_PAD // tf
    seg = _segment_index(seg, tm)
    col = lambda r: pl.BlockSpec((r, tf), lambda i, j: (0, jnp.minimum(j, nj - 1)))
    up = pl.BlockSpec((None, d, tf), lambda i, j: (layer, 0, jnp.minimum(j, nj - 1)))
    return pl.pallas_call(
        functools.partial(_ffn_body, tm=tm, rows_lat=rows_lat, t_lat=t_lat, t_ctx=t_ctx, final_norm=final_norm),
        grid=(rows // tm, nj + 1),
        in_specs=[pl.BlockSpec((tm, d), lambda i, j: (i, 0), pipeline_mode=pl.Buffered(1)),
                  pl.BlockSpec((HALO, d), lambda i, j: (jnp.maximum(i * nh - 1, 0), 0)),
                  pl.BlockSpec((HALO, d), lambda i, j: (jnp.minimum((i + 1) * nh, last), 0)),
                  pl.BlockSpec((1, d), lambda i, j: (0, 0)),
                  pl.BlockSpec((None, 1, d), seg), pl.BlockSpec((None, 1, d), seg), pl.BlockSpec((None, 1, d), seg),
                  up, up, col(3), col(3), col(1), col(1),
                  pl.BlockSpec((None, tf, d), lambda i, j: (layer, jnp.maximum(j - 1, 0), 0)),
                  pl.BlockSpec((1, d), lambda i, j: (0, 0))],
        out_specs=pl.BlockSpec((tm, d), lambda i, j: (i, 0)),
        out_shape=jax.ShapeDtypeStruct(x.shape, F32),
        scratch_shapes=[pltpu.VMEM((tm + 2 * HALO, d), BF16),
                        pltpu.VMEM((tm + 2 * HALO, FFN_SUB), F32),
                        pltpu.VMEM((tm + 2 * HALO, FFN_SUB), F32),
                        pltpu.VMEM((tm + 2 * HALO, FFN_SUB), F32),
                        pltpu.VMEM((tm + 2 * HALO, FFN_SUB), F32),
                        pltpu.VMEM((tm, tf), BF16),
                        pltpu.VMEM((tm, tf), BF16)],
        compiler_params=_params(("parallel", "arbitrary"), 56),
        name="conv_ffn",
    )(x, x, x, g, scale, shift, gate, wa, wg, cwa, cwg, ba, bg, wd, final_g)


def _lora_weight(w2, a2, g2):
    c = RWKV_WIDTH
    wl = jnp.zeros((LORA_IN, 5 * c), F32)
    wl = wl.at[:DECAY_LORA, 0:c].set(w2[0]).at[:DECAY_LORA, c:2 * c].set(w2[1])
    wl = wl.at[DECAY_LORA:DECAY_LORA + ICLR_LORA, 2 * c:3 * c].set(a2[0])
    wl = wl.at[DECAY_LORA:DECAY_LORA + ICLR_LORA, 3 * c:4 * c].set(a2[1])
    return wl.at[DECAY_LORA + ICLR_LORA:, 4 * c:].set(g2)


LANES = 128


def _split_up_body(a_ref, g_ref, oa_ref, og_ref, *, nvalid):
    keep = pl.program_id(1) < nvalid
    oa_ref[...] = jnp.where(keep, a_ref[...], 0.0).astype(BF16)
    og_ref[...] = jnp.where(keep, g_ref[...], 0.0).astype(BF16)


def _split_ffn_up(ffn_up):
    depth, d, _ = ffn_up.shape
    nvalid = D_FF // LANES
    src = lambda half: pl.BlockSpec((None, d, LANES),
                                    lambda l, j: (l, 0, half * nvalid + jnp.minimum(j, nvalid - 1)))
    dst = pl.BlockSpec((None, d, LANES), lambda l, j: (l, 0, j))
    shape = jax.ShapeDtypeStruct((depth, d, D_FF_PAD), BF16)
    return pl.pallas_call(
        functools.partial(_split_up_body, nvalid=nvalid),
        grid=(depth, D_FF_PAD // LANES),
        in_specs=[src(0), src(1)], out_specs=[dst, dst], out_shape=[shape, shape],
        compiler_params=_params(("parallel", "parallel"), 32),
        name="split_ffn_up",
    )(ffn_up, ffn_up)


def _pad_down_body(w_ref, o_ref, *, tk):
    row = pl.program_id(1) * tk + lax.broadcasted_iota(jnp.int32, (tk, 1), 0)
    o_ref[...] = jnp.where(row < D_FF, w_ref[...], 0.0).astype(BF16)


def _pad_ffn_down(ffn_down):
    depth, _, d = ffn_down.shape
    tk = FFN_TILE
    return pl.pallas_call(
        functools.partial(_pad_down_body, tk=tk),
        grid=(depth, D_FF_PAD // tk),
        in_specs=[pl.BlockSpec((None, tk, d), lambda l, j: (l, j, 0))],
        out_specs=pl.BlockSpec((None, tk, d), lambda l, j: (l, j, 0)),
        out_shape=jax.ShapeDtypeStruct((depth, D_FF_PAD, d), BF16),
        compiler_params=_params(("parallel", "parallel"), 32),
        name="pad_ffn_down",
    )(ffn_down)


def _pad_cols(w):
    return jnp.pad(w, ((0, 0), (0, D_FF_PAD - D_FF)))


def _forward(x_lat, x_ctx, cache_k, cache_v, state, c, c_ctx, w_ada, b_ada, norm1_g, norm2_g, w_in, w_out,
             q_norm_g, k_norm_g, rw_conv, rw_w0, rw_w2, rw_a0, rw_a2, rw_g2, rw_kk, rw_ka, rw_rk,
             rw_lnx_g, rw_lnx_b, ffn_up, ffn_conv_w, ffn_conv_b, ffn_down, final_norm_g):
    n_lat, t_lat, d = x_lat.shape
    n_ctx, t_ctx, _ = x_ctx.shape
    depth = w_ada.shape[0]
    past = cache_k.shape[2]
    rows_lat, rows_ctx = n_lat * t_lat, n_ctx * t_ctx
    assert n_lat < MOD_ROWS
    assert t_lat % SEQ_ALIGN == 0 and t_ctx % SEQ_ALIGN == 0 and t_lat % GRID_W == 0

    cvec = jnp.concatenate([c, jnp.broadcast_to(c_ctx[None, :], (MOD_ROWS - n_lat, d))], axis=0)
    mod = _adaln_mod(cvec, w_ada, b_ada).reshape(depth, MOD_ROWS, 6, 1, d)

    ffn_wa, ffn_wg = _split_ffn_up(ffn_up)
    ffn_wd = _pad_ffn_down(ffn_down)
    rope_tabs = _rope_tables(t_lat)
    time_lat, chan_tab = _dft_tables(t_lat)
    time_ctx, _ = _dft_tables(t_ctx)
    zero_state = jnp.zeros((n_ctx, 1, 2, RWKV_HEADS, RWKV_N, RWKV_N), F32)
    row2 = lambda a: a.reshape(1, -1)

    cache = (cache_k.reshape(n_lat, depth, past, KV_WIDTH), cache_v.reshape(n_lat, depth, past, KV_WIDTH))

    def trunk_layer(x, l, latent):
        nseq, t = (n_lat, t_lat) if latent else (n_ctx, t_ctx)
        rows = nseq * t
        seg = (0, t_lat) if latent else (n_lat, rows)
        region_rows_lat = rows if latent else 0
        shift1, scale1, gate1, shift2, scale2, gate2 = (mod[l, :, i] for i in range(6))
        in_args = (x, row2(norm1_g[l]), scale1, shift1, w_in_b, l, seg, row2(q_norm_g[l]), row2(k_norm_g[l]))
        if latent:
            f, qb, kb, vb, rw = _in_proj(*in_args, rope_tabs)
            a_out = _attention(qb, kb, vb, nseq, t, cache[0], cache[1], l)
            k_norm = v = None
        else:
            f, qb, kb, vb, rw, k_norm, v = _in_proj(*in_args)
            a_out = _attention(qb, kb, vb, nseq, t)
        f_out = _fourier_mix(f, time_lat if latent else time_ctx, chan_tab, 0, nseq, t)

        r_, v_, kk, g_, bonus, lw, km, b_ = _rwkv_prep(
            rw, rw_conv[l], lora_w[l], rw_w0[l], rw_a0[l],
            row2(rw_kk[l]), row2(rw_ka[l]), row2(rw_rk[l]), region_rows_lat, t_lat, t_ctx)
        yf, yb, s_fin = _rwkv_scan(r_, v_, kk, lw, km, b_, state if latent else zero_state, l if latent else 0,
                                   0, nseq, t)
        x = _out_proj(f_out, a_out, yf, yb, bonus, g_, row2(rw_lnx_g[l]), row2(rw_lnx_b[l]), x, gate1,
                      w_out_b, l, seg)
        x = _conv_ffn(x, row2(norm2_g[l]), scale2, shift2, gate2, ffn_wa, ffn_wg,
                      _pad_cols(ffn_conv_w[l][:, :D_FF]), _pad_cols(ffn_conv_w[l][:, D_FF:]),
                      _pad_cols(row2(ffn_conv_b[l][:D_FF])), _pad_cols(row2(ffn_conv_b[l][D_FF:])),
                      ffn_wd, row2(final_norm_g), l, seg, region_rows_lat, t_lat, t_ctx, final_norm=l == depth - 1)
        return x, k_norm, v, s_fin

    w_in_b, w_out_b = w_in.astype(BF16), w_out.astype(BF16)
    lora_w = [_lora_weight(rw_w2[l], rw_a2[l], rw_g2[l]).astype(BF16) for l in range(depth)]
    xs, xc = x_lat.reshape(rows_lat, d), x_ctx.reshape(rows_ctx, d)
    new_k, new_v, new_s = [], [], []
    for l in range(depth):
        xc, k_norm, v_ctx, s_ctx = trunk_layer(xc, l, latent=False)
        new_k.append(k_norm.reshape(n_ctx, t_ctx, ATTN_KV_HEADS, HEAD_DIM))
        new_v.append(v_ctx.reshape(n_ctx, t_ctx, ATTN_KV_HEADS, HEAD_DIM))
        new_s.append(s_ctx)
        xs, _, _, _ = trunk_layer(xs, l, latent=True)

    y_lat, y_ctx = xs.reshape(n_lat, t_lat, d), xc.reshape(n_ctx, t_ctx, d)
    return (y_ctx, y_lat, jnp.stack(new_k, axis=1), jnp.stack(new_v, axis=1), jnp.stack(new_s, axis=1))


def kernel(x_prompt, x_sample, cache_attn_k, cache_attn_v, state_rwkv, c, c_ctx, w_ada, b_ada, norm1_g, norm2_g, w_in, w_out, q_norm_g, k_norm_g, rw_conv, rw_w0, rw_w2, rw_a0, rw_a2, rw_g2, rw_kk, rw_ka, rw_rk, rw_lnx_g, rw_lnx_b, ffn_up, ffn_conv_w, ffn_conv_b, ffn_down, final_norm_g):
    return _forward(x_sample, x_prompt, cache_attn_k, cache_attn_v, state_rwkv, c, c_ctx, w_ada, b_ada,
                    norm1_g, norm2_g, w_in, w_out, q_norm_g, k_norm_g, rw_conv, rw_w0, rw_w2, rw_a0, rw_a2,
                    rw_g2, rw_kk, rw_ka, rw_rk, rw_lnx_g, rw_lnx_b, ffn_up, ffn_conv_w, ffn_conv_b, ffn_down,
                    final_norm_g)
```

```python
import functools
import math

import jax
import jax.numpy as jnp
import numpy as np
from jax import lax
from jax.experimental import pallas as pl
from jax.experimental.pallas import tpu as pltpu

D_MODEL = 2048
GRID_W = 64
HEAD_DIM = 128
ATTN_HEADS = 8
ATTN_KV_HEADS = 2
KV_GROUP = ATTN_HEADS // ATTN_KV_HEADS
ATTN_WIDTH = ATTN_HEADS * HEAD_DIM
KV_WIDTH = ATTN_KV_HEADS * HEAD_DIM
FOURIER_WIDTH = 512
FOURIER_GROUPS = 4
FOURIER_GROUP_WIDTH = FOURIER_WIDTH // FOURIER_GROUPS
RWKV_WIDTH = 512
RWKV_N = 64
RWKV_HEADS = RWKV_WIDTH // RWKV_N
DECAY_LORA = 64
ICLR_LORA = 64
GATE_LORA = 128
LORA_IN = DECAY_LORA + ICLR_LORA + GATE_LORA
RWKV_IN = 3 * RWKV_WIDTH + LORA_IN
IN_WIDTH = FOURIER_WIDTH + ATTN_WIDTH + 2 * KV_WIDTH + RWKV_IN
D_FF = 5504
ROPE_THETA = 10000.0
NORM_EPS = 1e-6
GN_EPS = 64e-5

MOD_ROWS = 16
FFN_TILE = 512
FFN_SUB = 256
D_FF_PAD = -(-D_FF // FFN_TILE) * FFN_TILE
HALO = 16
CHUNK = 64
SEQ_ALIGN = 256
MIB = 2 ** 20

F32 = jnp.float32
BF16 = jnp.bfloat16


def _params(semantics, vmem_mib):
    return pltpu.CompilerParams(dimension_semantics=semantics, vmem_limit_bytes=vmem_mib * MIB)


def _resident(shape):
    return pl.BlockSpec(shape, lambda *_: (0,) * len(shape), pipeline_mode=pl.Buffered(1))


def _resident_layer(stacked_shape, layer):
    shape = stacked_shape[1:]
    return pl.BlockSpec((None,) + tuple(shape), lambda *_: (layer,) + (0,) * len(shape),
                        pipeline_mode=pl.Buffered(1))


def _dot(a, b):
    return jnp.dot(a, b, preferred_element_type=F32)


def _dot_nt(a, b):
    return lax.dot_general(a, b, (((1,), (1,)), ((), ())), preferred_element_type=F32)


def _dot_tn(a, b):
    return lax.dot_general(a, b, (((0,), (0,)), ((), ())), preferred_element_type=F32)


def _modulated_rmsnorm(x, g, scale, shift):
    ms = jnp.mean(x * x, axis=-1, keepdims=True)
    return (x * lax.rsqrt(ms + NORM_EPS) * g) * (1.0 + scale) + shift


def _mod_body(c_ref, w_ref, b_ref, o_ref):
    c = c_ref[...]
    s = (c * jax.nn.sigmoid(c)).astype(BF16)
    o_ref[...] = _dot(s, w_ref[...].astype(BF16)) + b_ref[...]


def _adaln_mod(cvec, w_ada, b_ada):
    depth, d, n = w_ada.shape
    tn = 1024
    return pl.pallas_call(
        _mod_body,
        grid=(depth, n // tn),
        in_specs=[pl.BlockSpec((MOD_ROWS, d), lambda l, j: (0, 0)),
                  pl.BlockSpec((None, d, tn), lambda l, j: (l, 0, j)),
                  pl.BlockSpec((None, 1, tn), lambda l, j: (l, 0, j))],
        out_specs=pl.BlockSpec((None, MOD_ROWS, tn), lambda l, j: (l, 0, j)),
        out_shape=jax.ShapeDtypeStruct((depth, MOD_ROWS, n), F32),
        compiler_params=_params(("parallel", "parallel"), 40),
        name="adaln_mod",
    )(cvec, w_ada, b_ada.reshape(depth, 1, n))


Q_COL = FOURIER_WIDTH
K_COL = Q_COL + ATTN_WIDTH
V_COL = K_COL + KV_WIDTH
RW_COL = V_COL + KV_WIDTH


def _inproj_body(*refs, rope):
    x_ref, g_ref, sc_ref, sh_ref, w_ref, gq_ref, gk_ref = refs[:7]
    if rope:
        cos_ref, sin_ref, f_ref, q_ref, k_ref, v_ref, rw_ref = refs[7:]
        cos, sin = cos_ref[...], sin_ref[...]
        lane = lax.broadcasted_iota(jnp.int32, (1, HEAD_DIM), 1)
        low = (lane % (HEAD_DIM // 2)) < (HEAD_DIM // 4)
    else:
        f_ref, q_ref, k_ref, v_ref, rw_ref, kn_ref, vf_ref = refs[7:]

    h = _modulated_rmsnorm(x_ref[...], g_ref[...], sc_ref[...], sh_ref[...]).astype(BF16)
    proj = lambda start, width: _dot(h, w_ref[:, start:start + width])
    q, k, v = proj(Q_COL, ATTN_WIDTH), proj(K_COL, KV_WIDTH), proj(V_COL, KV_WIDTH)
    f_ref[...] = proj(0, FOURIER_WIDTH).astype(BF16)
    rw_ref[...] = proj(RW_COL, RWKV_IN)

    def head_norm(xh, g):
        ms = jnp.mean(xh * xh, axis=-1, keepdims=True)
        return xh * lax.rsqrt(ms + NORM_EPS) * g

    def rotate(xh):
        partner = jnp.where(low, pltpu.roll(xh, HEAD_DIM - HEAD_DIM // 4, 1), pltpu.roll(xh, HEAD_DIM // 4, 1))
        return xh * cos + partner * sin

    scale = HEAD_DIM ** -0.5
    for hd in range(ATTN_HEADS):
        sl = slice(hd * HEAD_DIM, (hd + 1) * HEAD_DIM)
        qh = head_norm(q[:, sl], gq_ref[...])
        if rope:
            qh = rotate(qh)
        q_ref[:, sl] = (qh * scale).astype(BF16)
    for j in range(ATTN_KV_HEADS):
        sl = slice(j * HEAD_DIM, (j + 1) * HEAD_DIM)
        kh = head_norm(k[:, sl], gk_ref[...])
        if rope:
            kh = rotate(kh)
        else:
            kn_ref[:, sl] = kh
        k_ref[:, sl] = kh.astype(BF16)
    v_ref[...] = v.astype(BF16)
    if not rope:
        vf_ref[...] = v


def _segment_index(seg, tm):
    seg0, seg_rows = seg
    assert seg_rows % tm == 0
    return lambda i, *_: (seg0 + i * tm // seg_rows, 0, 0)


def _in_proj(x, g, scale, shift, w, layer, seg, gq, gk, rope_tabs=None):
    rows, d = x.shape
    tm = 512 if rows % 512 == 0 else 256
    seg = _segment_index(seg, tm)
    rope = rope_tabs is not None
    in_specs = [pl.BlockSpec((tm, d), lambda i: (i, 0)),
                _resident((1, d)),
                pl.BlockSpec((None, 1, d), seg),
                pl.BlockSpec((None, 1, d), seg),
                _resident_layer(w.shape, layer),
                _resident((1, HEAD_DIM)), _resident((1, HEAD_DIM))]
    args = [x, g, scale, shift, w, gq, gk]
    outs = [(FOURIER_WIDTH, BF16), (ATTN_WIDTH, BF16), (KV_WIDTH, BF16), (KV_WIDTH, BF16), (RWKV_IN, F32)]
    if rope:
        nt = rope_tabs[0].shape[0] // tm
        in_specs += [pl.BlockSpec((tm, HEAD_DIM), lambda i: (i % nt, 0))] * 2
        args += list(rope_tabs)
    else:
        outs += [(KV_WIDTH, F32), (KV_WIDTH, F32)]
    return pl.pallas_call(
        functools.partial(_inproj_body, rope=rope),
        grid=(rows // tm,),
        in_specs=in_specs,
        out_specs=[pl.BlockSpec((tm, width), lambda i: (i, 0)) for width, _ in outs],
        out_shape=[jax.ShapeDtypeStruct((rows, width), dt) for width, dt in outs],
        compiler_params=_params(("parallel",), 56),
        name="in_proj_rope" if rope else "in_proj",
    )(*args)


def _dft_tables(t):
    def angles(n):
        i = np.arange(n, dtype=np.int64)
        return (2.0 * math.pi / n) * ((i[:, None] * i[None, :]) % n)
    at = angles(t)
    time_tab = np.concatenate([np.cos(at), -np.sin(at)], axis=1).astype(np.float32)
    ac = angles(FOURIER_GROUP_WIDTH)
    eye = np.eye(FOURIER_GROUPS)
    chan_tab = np.concatenate([np.kron(eye, np.cos(ac)), np.kron(eye, np.sin(ac))], axis=1).astype(np.float32)
    return jnp.asarray(time_tab).astype(BF16), jnp.asarray(chan_tab).astype(BF16)


def _fourier_body(u_ref, ct_ref, cc_ref, o_ref, ab_scr, *, t, norm):
    @pl.when(pl.program_id(1) == 0)
    def _():
        ab = _dot(u_ref[...], cc_ref[...])
        ab_scr[0:t, :] = ab[:, :FOURIER_WIDTH].astype(BF16)
        ab_scr[t:2 * t, :] = ab[:, FOURIER_WIDTH:].astype(BF16)
    o_ref[...] = (_dot(ct_ref[...], ab_scr[...]) * norm).astype(o_ref.dtype)


def _fourier_mix(u, time_tab, chan_tab, row0, nseq, t):
    tm = min(t, 512)
    nt = t // tm
    seq0 = row0 // t
    return pl.pallas_call(
        functools.partial(_fourier_body, t=t, norm=1.0 / math.sqrt(t * FOURIER_GROUP_WIDTH)),
        grid=(nseq, nt),
        in_specs=[pl.BlockSpec((t, FOURIER_WIDTH), lambda b, i: (seq0 + b, 0)),
                  pl.BlockSpec((tm, 2 * t), lambda b, i: (i, 0)),
                  _resident(chan_tab.shape)],
        out_specs=pl.BlockSpec((tm, FOURIER_WIDTH), lambda b, i: (b * nt + i, 0)),
        out_shape=jax.ShapeDtypeStruct((nseq * t, FOURIER_WIDTH), BF16),
        scratch_shapes=[pltpu.VMEM((2 * t, FOURIER_WIDTH), BF16)],
        compiler_params=_params(("parallel", "arbitrary"), 40),
        name=f"fourier_mix_t{t}",
    )(u, time_tab, chan_tab)


def _rope_tables(t):
    pos = jnp.arange(t, dtype=jnp.int32)
    rows = (pos // GRID_W).astype(F32)
    cols = (pos % GRID_W).astype(F32)
    half = HEAD_DIM // 2
    inv = 1.0 / (ROPE_THETA ** (jnp.arange(0, half, 2, dtype=F32) / half))
    def tab(p):
        ang = p[:, None] * inv[None, :]
        return (jnp.concatenate([jnp.cos(ang), jnp.cos(ang)], -1),
                jnp.concatenate([-jnp.sin(ang), jnp.sin(ang)], -1))
    cr, sr = tab(rows)
    cc, sc = tab(cols)
    return jnp.concatenate([cr, cc], -1), jnp.concatenate([sr, sc], -1)


def _attn_body(*refs, cached):
    if cached:
        q_ref, k_ref, v_ref, kc_ref, vc_ref, o_ref = refs
    else:
        q_ref, k_ref, v_ref, o_ref = refs
    for j in range(ATTN_KV_HEADS):
        kv = slice(j * HEAD_DIM, (j + 1) * HEAD_DIM)
        kj, vj = k_ref[:, kv], v_ref[:, kv]
        if cached:
            kc, vc = kc_ref[:, kv].astype(BF16), vc_ref[:, kv].astype(BF16)
        for g in range(KV_GROUP):
            sl = slice((j * KV_GROUP + g) * HEAD_DIM, (j * KV_GROUP + g + 1) * HEAD_DIM)
            qh = q_ref[:, sl]
            s = _dot_nt(qh, kj)
            m = jnp.max(s, axis=-1, keepdims=True)
            if cached:
                sc = _dot_nt(qh, kc)
                m = jnp.maximum(m, jnp.max(sc, axis=-1, keepdims=True))
            p = jnp.exp(s - m)
            l = jnp.sum(p, axis=-1, keepdims=True)
            acc = _dot(p.astype(BF16), vj)
            if cached:
                pc = jnp.exp(sc - m)
                l = l + jnp.sum(pc, axis=-1, keepdims=True)
                acc = acc + _dot(pc.astype(BF16), vc)
            o_ref[:, sl] = (acc / l).astype(o_ref.dtype)


def _attention(q, k, v, nseq, t, cache_k=None, cache_v=None, layer=0):
    tq = 256
    nq = t // tq
    cached = cache_k is not None
    in_specs = [pl.BlockSpec((tq, ATTN_WIDTH), lambda b, i: (b * nq + i, 0)),
                pl.BlockSpec((t, KV_WIDTH), lambda b, i: (b, 0)),
                pl.BlockSpec((t, KV_WIDTH), lambda b, i: (b, 0))]
    args = [q, k, v]
    if cached:
        past = cache_k.shape[2]
        in_specs += [pl.BlockSpec((None, None, past, KV_WIDTH), lambda b, i: (b, layer, 0, 0))] * 2
        args += [cache_k, cache_v]
    return pl.pallas_call(
        functools.partial(_attn_body, cached=cached),
        grid=(nseq, nq),
        in_specs=in_specs,
        out_specs=pl.BlockSpec((tq, ATTN_WIDTH), lambda b, i: (b * nq + i, 0)),
        out_shape=jax.ShapeDtypeStruct((nseq * t, ATTN_WIDTH), BF16),
        compiler_params=_params(("parallel", "parallel"), 48),
        name="attention_cached" if cached else "attention",
    )(*args)


def _head_sum_matrix():
    i = lax.broadcasted_iota(jnp.int32, (RWKV_WIDTH, RWKV_WIDTH), 0) // RWKV_N
    j = lax.broadcasted_iota(jnp.int32, (RWKV_WIDTH, RWKV_WIDTH), 1) // RWKV_N
    return (i == j).astype(BF16)


def _bf16_pieces(x, n):
    pieces = []
    for _ in range(n - 1):
        p = x.astype(BF16)
        pieces.append(p)
        x = x - p.astype(F32)
    return pieces + [x.astype(BF16)]


def _head_sums(x, ones_bd):
    hi, lo = _bf16_pieces(x, 2)
    return _dot(hi, ones_bd) + _dot(lo, ones_bd)


def _rwkv_prep_body(rw_ref, prev_ref, next_ref, cw_ref, wl_ref, w0_ref, a0_ref, kks_ref, ka_ref, rk_ref,
                    r_ref, v_ref, kk_ref, g_ref, bonus_ref, lw_ref, km_ref, b_ref, ext_scr,
                    *, tm, rows_lat, t_lat, t_ctx):
    ext_scr[0:HALO, :] = prev_ref[...]
    ext_scr[HALO:HALO + tm, :] = rw_ref[...]
    ext_scr[HALO + tm:, :] = next_ref[...]
    edges = _edge_masks(pl.program_id(0) * tm, rows_lat, t_lat, t_ctx)
    z = _conv3_block(ext_scr, cw_ref, None, slice(None), 0, *edges)
    c = RWKV_WIDTH
    r, k, v = z[:, :c], z[:, c:2 * c], z[:, 2 * c:3 * c]
    zl = z[:, 3 * c:]
    lane = lax.broadcasted_iota(jnp.int32, (1, LORA_IN), 1)
    lora_in = jnp.where(lane < DECAY_LORA, jnp.tanh(zl),
                        jnp.where(lane < DECAY_LORA + ICLR_LORA, zl, jax.nn.sigmoid(zl)))
    lora = _dot(lora_in.astype(BF16), wl_ref[...])
    ones_bd = _head_sum_matrix()
    kk = k * kks_ref[...]
    kk = kk * lax.rsqrt(_head_sums(kk * kk, ones_bd) + 1e-12)
    r_ref[...] = r
    v_ref[...] = v
    kk_ref[...] = kk
    g_ref[...] = lora[:, 4 * c:5 * c]
    kmod_sum = jnp.zeros_like(k)
    for d in range(2):
        wpre = w0_ref[d:d + 1, :] + lora[:, d * c:(d + 1) * c]
        lw_ref[d] = -math.exp(-0.5) * jax.nn.sigmoid(wpre)
        a = jax.nn.sigmoid(a0_ref[d:d + 1, :] + lora[:, (2 + d) * c:(3 + d) * c])
        kmod = k * (1.0 + (a - 1.0) * ka_ref[...])
        km_ref[d] = kmod
        b_ref[d] = kk * a
        kmod_sum = kmod_sum + kmod
    bonus_ref[...] = _head_sums(r * kmod_sum * rk_ref[...], ones_bd) * v


def _rwkv_prep(rw, conv_w, lora_w, w0, a0, kk_scale, ka, rk, rows_lat, t_lat, t_ctx):
    rows = rw.shape[0]
    tm = 256
    nh = tm // HALO
    last = rows // HALO - 1
    c = RWKV_WIDTH
    one = lambda: pl.BlockSpec((tm, c), lambda i: (i, 0))
    two = lambda: pl.BlockSpec((2, tm, c), lambda i: (0, i, 0))
    return pl.pallas_call(
        functools.partial(_rwkv_prep_body, tm=tm, rows_lat=rows_lat, t_lat=t_lat, t_ctx=t_ctx),
        grid=(rows // tm,),
        in_specs=[pl.BlockSpec((tm, RWKV_IN), lambda i: (i, 0)),
                  pl.BlockSpec((HALO, RWKV_IN), lambda i: (jnp.maximum(i * nh - 1, 0), 0)),
                  pl.BlockSpec((HALO, RWKV_IN), lambda i: (jnp.minimum((i + 1) * nh, last), 0)),
                  _resident(conv_w.shape), _resident(lora_w.shape), _resident(w0.shape), _resident(a0.shape),
                  _resident(kk_scale.shape), _resident(ka.shape), _resident(rk.shape)],
        out_specs=[one(), one(), one(), one(), one(), two(), two(), two()],
        out_shape=[jax.ShapeDtypeStruct((rows, c), F32)] * 5 + [jax.ShapeDtypeStruct((2, rows, c), F32)] * 3,
        scratch_shapes=[pltpu.VMEM((tm + 2 * HALO, RWKV_IN), F32)],
        compiler_params=_params(("parallel",), 48),
        name="rwkv_prep",
    )(rw, rw, rw, conv_w, lora_w, w0, a0, kk_scale, ka, rk)


def _scan_direction_operands(r_ref, v_ref, kk_ref, lw_ref, km_ref, b_ref, backward):
    n = CHUNK
    row = lax.broadcasted_iota(jnp.int32, (n, n), 0)
    col = lax.broadcasted_iota(jnp.int32, (n, n), 1)
    upto = (col >= row) if backward else (col <= row)
    lw = lw_ref[...]
    tri = upto.astype(BF16)
    cs = sum(_dot(tri, piece) for piece in _bf16_pieces(lw, 3))
    tot = jnp.sum(lw, axis=0, keepdims=True)
    grow = jnp.exp(-cs)
    to_end = jnp.exp(tot - cs)
    b = b_ref[...]
    km = km_ref[...]
    return dict(
        kkt=(kk_ref[...] * jnp.exp(cs - lw)).astype(BF16),
        rt=(r_ref[...] * jnp.exp(cs)).astype(BF16),
        bt=(b * grow).astype(BF16), kt=(km * grow).astype(BF16),
        bh=(b * to_end).astype(BF16), kh=(km * to_end).astype(BF16),
        vb=v_ref[...].astype(BF16), w_all=jnp.exp(tot))


def _rwkv_scan_body(rf_ref, vf_ref, kkf_ref, lwf_ref, kmf_ref, bf_ref,
                    rb_ref, vb_ref, kkb_ref, lwb_ref, kmb_ref, bb_ref, s0_ref,
                    yf_ref, yb_ref, sfin_ref, s_scr):
    c = pl.program_id(1)

    @pl.when(c == 0)
    def _():
        s_scr[...] = s0_ref[...]

    n = CHUNK
    row = lax.broadcasted_iota(jnp.int32, (2 * n, 2 * n), 0)
    col = lax.broadcasted_iota(jnp.int32, (2 * n, 2 * n), 1)
    t_idx, s_idx, read_rows = row % n, col % n, row >= n
    eye = (lax.broadcasted_iota(jnp.int32, (n, n), 0) == lax.broadcasted_iota(jnp.int32, (n, n), 1)).astype(F32)
    ops = (_scan_direction_operands(rf_ref, vf_ref, kkf_ref, lwf_ref, kmf_ref, bf_ref, False),
           _scan_direction_operands(rb_ref, vb_ref, kkb_ref, lwb_ref, kmb_ref, bb_ref, True))
    same_step = jnp.logical_and(read_rows, s_idx == t_idx)
    masks = (jnp.logical_or(s_idx < t_idx, same_step), jnp.logical_or(s_idx > t_idx, same_step))
    y_refs = (yf_ref, yb_ref)
    chains = [(d, h) for d in range(2) for h in range(RWKV_HEADS)]
    head = lambda d, name, h: ops[d][name][:, h * RWKV_N:(h + 1) * RWKV_N]
    cat = lambda a, b: jnp.concatenate([a, b], axis=0)

    lhs = [cat(head(d, "kkt", h), head(d, "rt", h)) for d, h in chains]
    coef = [jnp.where(masks[d], _dot_nt(l, cat(head(d, "bt", h), head(d, "kt", h))), 0.0)
            for l, (d, h) in zip(lhs, chains)]
    n_mat = [a[:n, :n] for a in coef]
    state = [s_scr[d, h] for d, h in chains]
    state_b = [s.astype(BF16) for s in state]
    read = [_dot_nt(l, sb) for l, sb in zip(lhs, state_b)]
    akv = [_dot(a[:n, n:].astype(BF16), head(d, "vb", h)) for a, (d, h) in zip(coef, chains)]

    x = [eye - m for m in n_mat]
    p = [_dot(m.astype(BF16), m.astype(BF16)) for m in n_mat]
    steps = int(math.log2(n)) - 1
    for s in range(steps):
        pb = [q.astype(BF16) for q in p]
        x = [xi + _dot(xi.astype(BF16), q) for xi, q in zip(x, pb)]
        if s + 1 < steps:
            p = [_dot(q, q) for q in pb]

    u = [_dot(xi.astype(BF16), (-(rd[:n] + ak)).astype(BF16)) for xi, rd, ak in zip(x, read, akv)]
    uv = [cat(ui.astype(BF16), head(d, "vb", h)) for ui, (d, h) in zip(u, chains)]
    for (d, h), a, rd, uvi, s in zip(chains, coef, read, uv, state):
        sl = slice(h * RWKV_N, (h + 1) * RWKV_N)
        y_refs[d][:, sl] = rd[n:] + _dot(a[n:].astype(BF16), uvi)
        s_scr[d, h] = s * ops[d]["w_all"][:, sl] + _dot_tn(uvi, cat(head(d, "bh", h), head(d, "kh", h)))

    @pl.when(c == pl.num_programs(1) - 1)
    def _():
        sfin_ref[...] = s_scr[...]


def _rwkv_scan(r, v, kk, lw, km, b, states, layer, row0, nseq, t):
    n = CHUNK
    nc = t // n
    c0 = row0 // n
    c = RWKV_WIDTH
    fwd = lambda bb, cc: c0 + bb * nc + cc
    bwd = lambda bb, cc: c0 + bb * nc + nc - 1 - cc
    one = lambda chunk: pl.BlockSpec((n, c), lambda bb, cc: (chunk(bb, cc), 0))
    two = lambda chunk, d: pl.BlockSpec((None, n, c), lambda bb, cc: (d, chunk(bb, cc), 0))
    state = pl.BlockSpec((None, 2, RWKV_HEADS, RWKV_N, RWKV_N), lambda bb, cc: (bb, 0, 0, 0, 0))
    state_in = pl.BlockSpec((None, None, 2, RWKV_HEADS, RWKV_N, RWKV_N), lambda bb, cc: (bb, layer, 0, 0, 0, 0))
    y_shape = jax.ShapeDtypeStruct((nseq * t, c), F32)
    return pl.pallas_call(
        _rwkv_scan_body,
        grid=(nseq, nc),
        in_specs=[one(fwd), one(fwd), one(fwd), two(fwd, 0), two(fwd, 0), two(fwd, 0),
                  one(bwd), one(bwd), one(bwd), two(bwd, 1), two(bwd, 1), two(bwd, 1), state_in],
        out_specs=[pl.BlockSpec((n, c), lambda bb, cc: (bb * nc + cc, 0)),
                   pl.BlockSpec((n, c), lambda bb, cc: (bb * nc + nc - 1 - cc, 0)), state],
        out_shape=[y_shape, y_shape, jax.ShapeDtypeStruct((nseq, 2, RWKV_HEADS, RWKV_N, RWKV_N), F32)],
        scratch_shapes=[pltpu.VMEM((2, RWKV_HEADS, RWKV_N, RWKV_N), F32)],
        compiler_params=_params(("parallel", "arbitrary"), 32),
        name=f"rwkv_scan_t{t}",
    )(r, v, kk, lw, km, b, r, v, kk, lw, km, b, states)


def _outproj_body(f_ref, a_ref, yf_ref, yb_ref, bonus_ref, g_ref, lg_ref, lb_ref, x_ref, gate_ref, w_ref, o_ref):
    f0, a0, r0 = 0, FOURIER_WIDTH, FOURIER_WIDTH + ATTN_WIDTH
    mix = _dot(f_ref[...], w_ref[f0:a0, :]) + _dot(a_ref[...], w_ref[a0:r0, :])
    ones_bd = _head_sum_matrix()
    y = yf_ref[...] + yb_ref[...]
    mu = _head_sums(y, ones_bd) * (1.0 / RWKV_N)
    yc = y - mu
    var = _head_sums(yc * yc, ones_bd) * (1.0 / RWKV_N)
    yn = yc * lax.rsqrt(var + GN_EPS) * lg_ref[...] + lb_ref[...]
    r = ((yn + bonus_ref[...]) * g_ref[...]).astype(BF16)
    mix = mix + _dot(r, w_ref[r0:, :])
    o_ref[...] = x_ref[...] + gate_ref[...] * mix


def _out_proj(f, a, yf, yb, bonus, g, ln_g, ln_b, x, gate, w, layer, seg):
    rows, d = x.shape
    tm = 512 if rows % 512 == 0 else 256
    row = lambda arr: pl.BlockSpec((tm, arr.shape[1]), lambda i: (i, 0))
    return pl.pallas_call(
        _outproj_body,
        grid=(rows // tm,),
        in_specs=[row(f), row(a), row(yf), row(yb), row(bonus), row(g),
                  _resident(ln_g.shape), _resident(ln_b.shape), row(x),
                  pl.BlockSpec((None, 1, d), _segment_index(seg, tm)),
                  _resident_layer(w.shape, layer)],
        out_specs=row(x),
        out_shape=jax.ShapeDtypeStruct(x.shape, F32),
        compiler_params=_params(("parallel",), 48),
        name="out_proj",
    )(f, a, yf, yb, bonus, g, ln_g, ln_b, x, gate, w)


def _edge_masks(block_row, rows_lat, t_lat, t_ctx):
    in_lat = block_row < rows_lat
    length = jnp.where(in_lat, t_lat, t_ctx)
    off = jnp.where(in_lat, block_row, block_row - rows_lat)
    starts = (lax.rem(off, length) == 0).astype(F32)
    ends = (lax.rem(off + SEQ_ALIGN, length) == 0).astype(F32)
    sub = lax.broadcasted_iota(jnp.int32, (8, 1), 0)
    return 1.0 - starts * (sub == 0).astype(F32), 1.0 - ends * (sub == 7).astype(F32)


def _conv3_block(ext_ref, w_ref, b_ref, cols, r0, not_first8, not_last8):
    w0, w1, w2 = w_ref[0:1, cols], w_ref[1:2, cols], w_ref[2:3, cols]

    def rows(lo, hi, prev_mask=None, next_mask=None):
        at = lambda shift: ext_ref[HALO + r0 + shift + lo:HALO + r0 + shift + hi, :]
        prev, nxt = at(-1), at(1)
        if prev_mask is not None:
            prev = prev * prev_mask
        if next_mask is not None:
            nxt = nxt * next_mask
        return w0 * prev + w1 * at(0) + w2 * nxt

    n = SEQ_ALIGN
    out = jnp.concatenate([rows(0, 8, prev_mask=not_first8), rows(8, n - 8), rows(n - 8, n, next_mask=not_last8)],
                          axis=0)
    return out if b_ref is None else out + b_ref[:, cols]


def _ffn_body(x_ref, prev_ref, next_ref, g_ref, sc_ref, sh_ref, gate_ref, wa_ref, wg_ref,
              cwa_ref, cwg_ref, ba_ref, bg_ref, wd_ref, fg_ref, o_ref,
              h_scr, ua0_scr, ua1_scr, ug0_scr, ug1_scr, act_cur, act_new,
              *, tm, rows_lat, t_lat, t_ctx, final_norm):
    j = pl.program_id(1)
    nj = pl.num_programs(1) - 1

    @pl.when(j == 0)
    def _():
        norm = lambda x: _modulated_rmsnorm(x, g_ref[...], sc_ref[...], sh_ref[...]).astype(BF16)
        h_scr[0:HALO, :] = norm(prev_ref[...])
        h_scr[HALO:HALO + tm, :] = norm(x_ref[...])
        h_scr[HALO + tm:, :] = norm(next_ref[...])
        act_new[...] = jnp.zeros_like(act_new)
        o_ref[...] = jnp.zeros_like(o_ref)

    subs = [slice(s, s + FFN_SUB) for s in range(0, wa_ref.shape[1], FFN_SUB)]
    blocks = range(0, tm, SEQ_ALIGN)

    u_scr = ((ua0_scr, ug0_scr), (ua1_scr, ug1_scr))
    assert len(subs) == len(u_scr)

    def up_project(s):
        ua_scr, ug_scr = u_scr[s]
        ua_scr[...] = _dot(h_scr[...], wa_ref[:, subs[s]])
        ug_scr[...] = _dot(h_scr[...], wg_ref[:, subs[s]])

    def conv_gate(s):
        ua_scr, ug_scr = u_scr[s]
        for r0 in blocks:
            edge = _edge_masks(pl.program_id(0) * tm + r0, rows_lat, t_lat, t_ctx)
            ua = _conv3_block(ua_scr, cwa_ref, ba_ref, subs[s], r0, *edge)
            ug = _conv3_block(ug_scr, cwg_ref, bg_ref, subs[s], r0, *edge)
            act_new[r0:r0 + SEQ_ALIGN, subs[s]] = (ug * jax.nn.sigmoid(ug) * ua).astype(BF16)

    @pl.when(j < nj)
    def _():
        act_cur[...] = act_new[...]
        up_project(0)
        up_project(1)
        conv_gate(0)
        conv_gate(1)
        o_ref[...] += _dot(act_cur[...], wd_ref[...])

    @pl.when(j == nj)
    def _():
        out = x_ref[...] + gate_ref[...] * (o_ref[...] + _dot(act_new[...], wd_ref[...]))
        if final_norm:
            out = out * lax.rsqrt(jnp.mean(out * out, axis=-1, keepdims=True) + NORM_EPS) * fg_ref[...]
        o_ref[...] = out


def _conv_ffn(x, g, scale, shift, gate, wa, wg, cwa, cwg, ba, bg, wd, final_g, layer, seg, rows_lat, t_lat, t_ctx,
              final_norm):
    rows, d = x.shape
    tm = next(t for t in (1024, 512, 256) if rows % t == 0 and rows_lat % t == 0 and seg[1] % t == 0)
    tf = FFN_TILE
    nh = tm // HALO
    last = rows // HALO - 1
    nj = D_FF_PAD // tf
    seg = _segment_index(seg, tm)
    col = lambda r: pl.BlockSpec((r, tf), lambda i, j: (0, jnp.minimum(j, nj - 1)))
    up = pl.BlockSpec((None, None, d, tf), lambda i, j: (layer, jnp.minimum(j, nj - 1), 0, 0))
    return pl.pallas_call(
        functools.partial(_ffn_body, tm=tm, rows_lat=rows_lat, t_lat=t_lat, t_ctx=t_ctx, final_norm=final_norm),
        grid=(rows // tm, nj + 1),
        in_specs=[pl.BlockSpec((tm, d), lambda i, j: (i, 0), pipeline_mode=pl.Buffered(1)),
                  pl.BlockSpec((HALO, d), lambda i, j: (jnp.maximum(i * nh - 1, 0), 0)),
                  pl.BlockSpec((HALO, d), lambda i, j: (jnp.minimum((i + 1) * nh, last), 0)),
                  pl.BlockSpec((1, d), lambda i, j: (0, 0)),
                  pl.BlockSpec((None, 1, d), seg), pl.BlockSpec((None, 1, d), seg), pl.BlockSpec((None, 1, d), seg),
                  up, up, col(3), col(3), col(1), col(1),
                  pl.BlockSpec((None, tf, d), lambda i, j: (layer, jnp.maximum(j - 1, 0), 0)),
                  pl.BlockSpec((1, d), lambda i, j: (0, 0))],
        out_specs=pl.BlockSpec((tm, d), lambda i, j: (i, 0)),
        out_shape=jax.ShapeDtypeStruct(x.shape, F32),
        scratch_shapes=[pltpu.VMEM((tm + 2 * HALO, d), BF16),
                        pltpu.VMEM((tm + 2 * HALO, FFN_SUB), F32),
                        pltpu.VMEM((tm + 2 * HALO, FFN_SUB), F32),
                        pltpu.VMEM((tm + 2 * HALO, FFN_SUB), F32),
                        pltpu.VMEM((tm + 2 * HALO, FFN_SUB), F32),
                        pltpu.VMEM((tm, tf), BF16),
                        pltpu.VMEM((tm, tf), BF16)],
        compiler_params=_params(("parallel", "arbitrary"), 56),
        name="conv_ffn",
    )(x, x, x, g, scale, shift, gate, wa, wg, cwa, cwg, ba, bg, wd, final_g)


def _lora_weight(w2, a2, g2):
    c = RWKV_WIDTH
    wl = jnp.zeros((LORA_IN, 5 * c), F32)
    wl = wl.at[:DECAY_LORA, 0:c].set(w2[0]).at[:DECAY_LORA, c:2 * c].set(w2[1])
    wl = wl.at[DECAY_LORA:DECAY_LORA + ICLR_LORA, 2 * c:3 * c].set(a2[0])
    wl = wl.at[DECAY_LORA:DECAY_LORA + ICLR_LORA, 3 * c:4 * c].set(a2[1])
    return wl.at[DECAY_LORA + ICLR_LORA:, 4 * c:].set(g2)


LANES = 128


def _split_up_body(a_ref, g_ref, oa_ref, og_ref, *, nvalid):
    keep = pl.program_id(1) < nvalid
    oa_ref[...] = jnp.where(keep, a_ref[...], 0.0).astype(BF16)
    og_ref[...] = jnp.where(keep, g_ref[...], 0.0).astype(BF16)


def _split_ffn_up(ffn_up):
    depth, d, _ = ffn_up.shape
    nvalid = D_FF // LANES
    per_tile = FFN_TILE // LANES
    src = lambda half: pl.BlockSpec((None, d, LANES),
                                    lambda l, j: (l, 0, half * nvalid + jnp.minimum(j, nvalid - 1)))
    dst = pl.BlockSpec((None, None, d, LANES), lambda l, j: (l, j // per_tile, 0, j % per_tile))
    shape = jax.ShapeDtypeStruct((depth, D_FF_PAD // FFN_TILE, d, FFN_TILE), BF16)
    return pl.pallas_call(
        functools.partial(_split_up_body, nvalid=nvalid),
        grid=(depth, D_FF_PAD // LANES),
        in_specs=[src(0), src(1)], out_specs=[dst, dst], out_shape=[shape, shape],
        compiler_params=_params(("parallel", "parallel"), 32),
        name="split_ffn_up",
    )(ffn_up, ffn_up)


def _pad_down_body(w_ref, o_ref, *, tk):
    row = pl.program_id(1) * tk + lax.broadcasted_iota(jnp.int32, (tk, 1), 0)
    o_ref[...] = jnp.where(row < D_FF, w_ref[...], 0.0).astype(BF16)


def _pad_ffn_down(ffn_down):
    depth, _, d = ffn_down.shape
    tk = FFN_TILE
    return pl.pallas_call(
        functools.partial(_pad_down_body, tk=tk),
        grid=(depth, D_FF_PAD // tk),
        in_specs=[pl.BlockSpec((None, tk, d), lambda l, j: (l, j, 0))],
        out_specs=pl.BlockSpec((None, tk, d), lambda l, j: (l, j, 0)),
        out_shape=jax.ShapeDtypeStruct((depth, D_FF_PAD, d), BF16),
        compiler_params=_params(("parallel", "parallel"), 32),
        name="pad_ffn_down",
    )(ffn_down)


def _pad_cols(w):
    return jnp.pad(w, ((0, 0), (0, D_FF_PAD - D_FF)))


def _forward(x_lat, x_ctx, cache_k, cache_v, state, c, c_ctx, w_ada, b_ada, norm1_g, norm2_g, w_in, w_out,
             q_norm_g, k_norm_g, rw_conv, rw_w0, rw_w2, rw_a0, rw_a2, rw_g2, rw_kk, rw_ka, rw_rk,
             rw_lnx_g, rw_lnx_b, ffn_up, ffn_conv_w, ffn_conv_b, ffn_down, final_norm_g):
    n_lat, t_lat, d = x_lat.shape
    n_ctx, t_ctx, _ = x_ctx.shape
    depth = w_ada.shape[0]
    past = cache_k.shape[2]
    rows_lat, rows_ctx = n_lat * t_lat, n_ctx * t_ctx
    assert n_lat < MOD_ROWS
    assert t_lat % SEQ_ALIGN == 0 and t_ctx % SEQ_ALIGN == 0 and t_lat % GRID_W == 0

    cvec = jnp.concatenate([c, jnp.broadcast_to(c_ctx[None, :], (MOD_ROWS - n_lat, d))], axis=0)
    mod = _adaln_mod(cvec, w_ada, b_ada).reshape(depth, MOD_ROWS, 6, 1, d)

    ffn_wa, ffn_wg = _split_ffn_up(ffn_up)
    ffn_wd = _pad_ffn_down(ffn_down)
    rope_tabs = _rope_tables(t_lat)
    time_lat, chan_tab = _dft_tables(t_lat)
    time_ctx, _ = _dft_tables(t_ctx)
    zero_state = jnp.zeros((n_ctx, 1, 2, RWKV_HEADS, RWKV_N, RWKV_N), F32)
    row2 = lambda a: a.reshape(1, -1)

    cache = (cache_k.reshape(n_lat, depth, past, KV_WIDTH), cache_v.reshape(n_lat, depth, past, KV_WIDTH))

    def trunk_layer(x, l, latent):
        nseq, t = (n_lat, t_lat) if latent else (n_ctx, t_ctx)
        rows = nseq * t
        seg = (0, t_lat) if latent else (n_lat, rows)
        region_rows_lat = rows if latent else 0
        shift1, scale1, gate1, shift2, scale2, gate2 = (mod[l, :, i] for i in range(6))
        in_args = (x, row2(norm1_g[l]), scale1, shift1, w_in_b, l, seg, row2(q_norm_g[l]), row2(k_norm_g[l]))
        if latent:
            f, qb, kb, vb, rw = _in_proj(*in_args, rope_tabs)
            a_out = _attention(qb, kb, vb, nseq, t, cache[0], cache[1], l)
            k_norm = v = None
        else:
            f, qb, kb, vb, rw, k_norm, v = _in_proj(*in_args)
            a_out = _attention(qb, kb, vb, nseq, t)
        f_out = _fourier_mix(f, time_lat if latent else time_ctx, chan_tab, 0, nseq, t)

        r_, v_, kk, g_, bonus, lw, km, b_ = _rwkv_prep(
            rw, rw_conv[l], lora_w[l], rw_w0[l], rw_a0[l],
            row2(rw_kk[l]), row2(rw_ka[l]), row2(rw_rk[l]), region_rows_lat, t_lat, t_ctx)
        yf, yb, s_fin = _rwkv_scan(r_, v_, kk, lw, km, b_, state if latent else zero_state, l if latent else 0,
                                   0, nseq, t)
        x = _out_proj(f_out, a_out, yf, yb, bonus, g_, row2(rw_lnx_g[l]), row2(rw_lnx_b[l]), x, gate1,
                      w_out_b, l, seg)
        x = _conv_ffn(x, row2(norm2_g[l]), scale2, shift2, gate2, ffn_wa, ffn_wg,
                      _pad_cols(ffn_conv_w[l][:, :D_FF]), _pad_cols(ffn_conv_w[l][:, D_FF:]),
                      _pad_cols(row2(ffn_conv_b[l][:D_FF])), _pad_cols(row2(ffn_conv_b[l][D_FF:])),
                      ffn_wd, row2(final_norm_g), l, seg, region_rows_lat, t_lat, t_ctx, final_norm=l == depth - 1)
        return x, k_norm, v, s_fin

    w_in_b, w_out_b = w_in.astype(BF16), w_out.astype(BF16)
    lora_w = [_lora_weight(rw_w2[l], rw_a2[l], rw_g2[l]).astype(BF16) for l in range(depth)]
    xs, xc = x_lat.reshape(rows_lat, d), x_ctx.reshape(rows_ctx, d)
    new_k, new_v, new_s = [], [], []
    for l in range(depth):
        xc, k_norm, v_ctx, s_ctx = trunk_layer(xc, l, latent=False)
        new_k.append(k_norm.reshape(n_ctx, t_ctx, ATTN_KV_HEADS, HEAD_DIM))
        new_v.append(v_ctx.reshape(n_ctx, t_ctx, ATTN_KV_HEADS, HEAD_DIM))
        new_s.append(s_ctx)
        xs, _, _, _ = trunk_layer(xs, l, latent=True)

    y_lat, y_ctx = xs.reshape(n_lat, t_lat, d), xc.reshape(n_ctx, t_ctx, d)
    return (y_ctx, y_lat, jnp.stack(new_k, axis=1), jnp.stack(new_v, axis=1), jnp.stack(new_s, axis=1))


def kernel(x_prompt, x_sample, cache_attn_k, cache_attn_v, state_rwkv, c, c_ctx, w_ada, b_ada, norm1_g, norm2_g, w_in, w_out, q_norm_g, k_norm_g, rw_conv, rw_w0, rw_w2, rw_a0, rw_a2, rw_g2, rw_kk, rw_ka, rw_rk, rw_lnx_g, rw_lnx_b, ffn_up, ffn_conv_w, ffn_conv_b, ffn_down, final_norm_g):
    return _forward(x_sample, x_prompt, cache_attn_k, cache_attn_v, state_rwkv, c, c_ctx, w_ada, b_ada,
                    norm1_g, norm2_g, w_in, w_out, q_norm_g, k_norm_g, rw_conv, rw_w0, rw_w2, rw_a0, rw_a2,
                    rw_g2, rw_kk, rw_ka, rw_rk, rw_lnx_g, rw_lnx_b, ffn_up, ffn_conv_w, ffn_conv_b, ffn_down,
                    final_norm_g)
```

```python
import functools
import math

import jax
import jax.numpy as jnp
import numpy as np
from jax import lax
from jax.experimental import pallas as pl
from jax.experimental.pallas import tpu as pltpu

D_MODEL = 2048
GRID_W = 64
HEAD_DIM = 128
ATTN_HEADS = 8
ATTN_KV_HEADS = 2
KV_GROUP = ATTN_HEADS // ATTN_KV_HEADS
ATTN_WIDTH = ATTN_HEADS * HEAD_DIM
KV_WIDTH = ATTN_KV_HEADS * HEAD_DIM
FOURIER_WIDTH = 512
FOURIER_GROUPS = 4
FOURIER_GROUP_WIDTH = FOURIER_WIDTH // FOURIER_GROUPS
RWKV_WIDTH = 512
RWKV_N = 64
RWKV_HEADS = RWKV_WIDTH // RWKV_N
DECAY_LORA = 64
ICLR_LORA = 64
GATE_LORA = 128
LORA_IN = DECAY_LORA + ICLR_LORA + GATE_LORA
RWKV_IN = 3 * RWKV_WIDTH + LORA_IN
IN_WIDTH = FOURIER_WIDTH + ATTN_WIDTH + 2 * KV_WIDTH + RWKV_IN
D_FF = 5504
ROPE_THETA = 10000.0
NORM_EPS = 1e-6
GN_EPS = 64e-5

MOD_ROWS = 16
FFN_TILE = 512
FFN_SUB = 256
D_FF_PAD = -(-D_FF // FFN_TILE) * FFN_TILE
HALO = 16
CHUNK = 64
SEQ_ALIGN = 256
MIB = 2 ** 20

F32 = jnp.float32
BF16 = jnp.bfloat16


def _params(semantics, vmem_mib):
    return pltpu.CompilerParams(dimension_semantics=semantics, vmem_limit_bytes=vmem_mib * MIB)


def _resident(shape):
    return pl.BlockSpec(shape, lambda *_: (0,) * len(shape), pipeline_mode=pl.Buffered(1))


def _resident_layer(stacked_shape, layer):
    shape = stacked_shape[1:]
    return pl.BlockSpec((None,) + tuple(shape), lambda *_: (layer,) + (0,) * len(shape),
                        pipeline_mode=pl.Buffered(1))


def _dot(a, b):
    return jnp.dot(a, b, preferred_element_type=F32)


def _dot_nt(a, b):
    return lax.dot_general(a, b, (((1,), (1,)), ((), ())), preferred_element_type=F32)


def _dot_tn(a, b):
    return lax.dot_general(a, b, (((0,), (0,)), ((), ())), preferred_element_type=F32)


def _modulated_rmsnorm(x, g, scale, shift):
    ms = jnp.mean(x * x, axis=-1, keepdims=True)
    return (x * lax.rsqrt(ms + NORM_EPS) * g) * (1.0 + scale) + shift


def _mod_body(c_ref, w_ref, b_ref, o_ref):
    c = c_ref[...]
    s = (c * jax.nn.sigmoid(c)).astype(BF16)
    o_ref[...] = _dot(s, w_ref[...].astype(BF16)) + b_ref[...]


def _adaln_mod(cvec, w_ada, b_ada):
    depth, d, n = w_ada.shape
    tn = 1024
    return pl.pallas_call(
        _mod_body,
        grid=(depth, n // tn),
        in_specs=[pl.BlockSpec((MOD_ROWS, d), lambda l, j: (0, 0)),
                  pl.BlockSpec((None, d, tn), lambda l, j: (l, 0, j)),
                  pl.BlockSpec((None, 1, tn), lambda l, j: (l, 0, j))],
        out_specs=pl.BlockSpec((None, MOD_ROWS, tn), lambda l, j: (l, 0, j)),
        out_shape=jax.ShapeDtypeStruct((depth, MOD_ROWS, n), F32),
        compiler_params=_params(("parallel", "parallel"), 40),
        name="adaln_mod",
    )(cvec, w_ada, b_ada.reshape(depth, 1, n))


Q_COL = FOURIER_WIDTH
K_COL = Q_COL + ATTN_WIDTH
V_COL = K_COL + KV_WIDTH
RW_COL = V_COL + KV_WIDTH


def _inproj_body(*refs, rope):
    x_ref, g_ref, sc_ref, sh_ref, w_ref, gq_ref, gk_ref = refs[:7]
    if rope:
        cos_ref, sin_ref, f_ref, q_ref, k_ref, v_ref, rw_ref = refs[7:]
        cos, sin = cos_ref[...], sin_ref[...]
        lane = lax.broadcasted_iota(jnp.int32, (1, HEAD_DIM), 1)
        low = (lane % (HEAD_DIM // 2)) < (HEAD_DIM // 4)
    else:
        f_ref, q_ref, k_ref, v_ref, rw_ref, kn_ref, vf_ref = refs[7:]

    h = _modulated_rmsnorm(x_ref[...], g_ref[...], sc_ref[...], sh_ref[...]).astype(BF16)
    proj = lambda start, width: _dot(h, w_ref[:, start:start + width])
    q, k, v = proj(Q_COL, ATTN_WIDTH), proj(K_COL, KV_WIDTH), proj(V_COL, KV_WIDTH)
    f_ref[...] = proj(0, FOURIER_WIDTH).astype(BF16)
    rw_ref[...] = proj(RW_COL, RWKV_IN)

    def head_norm(xh, g):
        ms = jnp.mean(xh * xh, axis=-1, keepdims=True)
        return xh * lax.rsqrt(ms + NORM_EPS) * g

    def rotate(xh):
        partner = jnp.where(low, pltpu.roll(xh, HEAD_DIM - HEAD_DIM // 4, 1), pltpu.roll(xh, HEAD_DIM // 4, 1))
        return xh * cos + partner * sin

    scale = HEAD_DIM ** -0.5
    for hd in range(ATTN_HEADS):
        sl = slice(hd * HEAD_DIM, (hd + 1) * HEAD_DIM)
        qh = head_norm(q[:, sl], gq_ref[...])
        if rope:
            qh = rotate(qh)
        q_ref[:, sl] = (qh * scale).astype(BF16)
    for j in range(ATTN_KV_HEADS):
        sl = slice(j * HEAD_DIM, (j + 1) * HEAD_DIM)
        kh = head_norm(k[:, sl], gk_ref[...])
        if rope:
            kh = rotate(kh)
        else:
            kn_ref[:, sl] = kh
        k_ref[:, sl] = kh.astype(BF16)
    v_ref[...] = v.astype(BF16)
    if not rope:
        vf_ref[...] = v


def _segment_index(seg, tm):
    seg0, seg_rows = seg
    assert seg_rows % tm == 0
    return lambda i, *_: (seg0 + i * tm // seg_rows, 0, 0)


def _in_proj(x, g, scale, shift, w, layer, seg, gq, gk, rope_tabs=None):
    rows, d = x.shape
    tm = 512 if rows % 512 == 0 else 256
    seg = _segment_index(seg, tm)
    rope = rope_tabs is not None
    in_specs = [pl.BlockSpec((tm, d), lambda i: (i, 0)),
                _resident((1, d)),
                pl.BlockSpec((None, 1, d), seg),
                pl.BlockSpec((None, 1, d), seg),
                _resident_layer(w.shape, layer),
                _resident((1, HEAD_DIM)), _resident((1, HEAD_DIM))]
    args = [x, g, scale, shift, w, gq, gk]
    outs = [(FOURIER_WIDTH, BF16), (ATTN_WIDTH, BF16), (KV_WIDTH, BF16), (KV_WIDTH, BF16), (RWKV_IN, F32)]
    if rope:
        nt = rope_tabs[0].shape[0] // tm
        in_specs += [pl.BlockSpec((tm, HEAD_DIM), lambda i: (i % nt, 0))] * 2
        args += list(rope_tabs)
    else:
        outs += [(KV_WIDTH, F32), (KV_WIDTH, F32)]
    return pl.pallas_call(
        functools.partial(_inproj_body, rope=rope),
        grid=(rows // tm,),
        in_specs=in_specs,
        out_specs=[pl.BlockSpec((tm, width), lambda i: (i, 0)) for width, _ in outs],
        out_shape=[jax.ShapeDtypeStruct((rows, width), dt) for width, dt in outs],
        compiler_params=_params(("parallel",), 56),
        name="in_proj_rope" if rope else "in_proj",
    )(*args)


def _dft_tables(t):
    def angles(n):
        i = np.arange(n, dtype=np.int64)
        return (2.0 * math.pi / n) * ((i[:, None] * i[None, :]) % n)
    at = angles(t)
    time_tab = np.concatenate([np.cos(at), -np.sin(at)], axis=1).astype(np.float32)
    ac = angles(FOURIER_GROUP_WIDTH)
    eye = np.eye(FOURIER_GROUPS)
    chan_tab = np.concatenate([np.kron(eye, np.cos(ac)), np.kron(eye, np.sin(ac))], axis=1).astype(np.float32)
    return jnp.asarray(time_tab).astype(BF16), jnp.asarray(chan_tab).astype(BF16)


def _fourier_body(u_ref, ct_ref, cc_ref, o_ref, ab_scr, *, t, norm):
    @pl.when(pl.program_id(1) == 0)
    def _():
        ab = _dot(u_ref[...], cc_ref[...])
        ab_scr[0:t, :] = ab[:, :FOURIER_WIDTH].astype(BF16)
        ab_scr[t:2 * t, :] = ab[:, FOURIER_WIDTH:].astype(BF16)
    o_ref[...] = (_dot(ct_ref[...], ab_scr[...]) * norm).astype(o_ref.dtype)


def _fourier_mix(u, time_tab, chan_tab, row0, nseq, t):
    tm = min(t, 512)
    nt = t // tm
    seq0 = row0 // t
    return pl.pallas_call(
        functools.partial(_fourier_body, t=t, norm=1.0 / math.sqrt(t * FOURIER_GROUP_WIDTH)),
        grid=(nseq, nt),
        in_specs=[pl.BlockSpec((t, FOURIER_WIDTH), lambda b, i: (seq0 + b, 0)),
                  pl.BlockSpec((tm, 2 * t), lambda b, i: (i, 0)),
                  _resident(chan_tab.shape)],
        out_specs=pl.BlockSpec((tm, FOURIER_WIDTH), lambda b, i: (b * nt + i, 0)),
        out_shape=jax.ShapeDtypeStruct((nseq * t, FOURIER_WIDTH), BF16),
        scratch_shapes=[pltpu.VMEM((2 * t, FOURIER_WIDTH), BF16)],
        compiler_params=_params(("parallel", "arbitrary"), 40),
        name=f"fourier_mix_t{t}",
    )(u, time_tab, chan_tab)


def _rope_tables(t):
    pos = jnp.arange(t, dtype=jnp.int32)
    rows = (pos // GRID_W).astype(F32)
    cols = (pos % GRID_W).astype(F32)
    half = HEAD_DIM // 2
    inv = 1.0 / (ROPE_THETA ** (jnp.arange(0, half, 2, dtype=F32) / half))
    def tab(p):
        ang = p[:, None] * inv[None, :]
        return (jnp.concatenate([jnp.cos(ang), jnp.cos(ang)], -1),
                jnp.concatenate([-jnp.sin(ang), jnp.sin(ang)], -1))
    cr, sr = tab(rows)
    cc, sc = tab(cols)
    return jnp.concatenate([cr, cc], -1), jnp.concatenate([sr, sc], -1)


def _attn_body(*refs, cached):
    if cached:
        q_ref, k_ref, v_ref, kc_ref, vc_ref, o_ref = refs
    else:
        q_ref, k_ref, v_ref, o_ref = refs
    for j in range(ATTN_KV_HEADS):
        kv = slice(j * HEAD_DIM, (j + 1) * HEAD_DIM)
        kj, vj = k_ref[:, kv], v_ref[:, kv]
        if cached:
            kc, vc = kc_ref[:, kv].astype(BF16), vc_ref[:, kv].astype(BF16)
        for g in range(KV_GROUP):
            sl = slice((j * KV_GROUP + g) * HEAD_DIM, (j * KV_GROUP + g + 1) * HEAD_DIM)
            qh = q_ref[:, sl]
            s = _dot_nt(qh, kj)
            m = jnp.max(s, axis=-1, keepdims=True)
            if cached:
                sc = _dot_nt(qh, kc)
                m = jnp.maximum(m, jnp.max(sc, axis=-1, keepdims=True))
            p = jnp.exp(s - m)
            l = jnp.sum(p, axis=-1, keepdims=True)
            acc = _dot(p.astype(BF16), vj)
            if cached:
                pc = jnp.exp(sc - m)
                l = l + jnp.sum(pc, axis=-1, keepdims=True)
                acc = acc + _dot(pc.astype(BF16), vc)
            o_ref[:, sl] = (acc / l).astype(o_ref.dtype)


def _attention(q, k, v, nseq, t, cache_k=None, cache_v=None, layer=0):
    tq = 512 if t % 512 == 0 else 256
    nq = t // tq
    cached = cache_k is not None
    in_specs = [pl.BlockSpec((tq, ATTN_WIDTH), lambda b, i: (b * nq + i, 0)),
                pl.BlockSpec((t, KV_WIDTH), lambda b, i: (b, 0)),
                pl.BlockSpec((t, KV_WIDTH), lambda b, i: (b, 0))]
    args = [q, k, v]
    if cached:
        past = cache_k.shape[2]
        in_specs += [pl.BlockSpec((None, None, past, KV_WIDTH), lambda b, i: (b, layer, 0, 0))] * 2
        args += [cache_k, cache_v]
    return pl.pallas_call(
        functools.partial(_attn_body, cached=cached),
        grid=(nseq, nq),
        in_specs=in_specs,
        out_specs=pl.BlockSpec((tq, ATTN_WIDTH), lambda b, i: (b * nq + i, 0)),
        out_shape=jax.ShapeDtypeStruct((nseq * t, ATTN_WIDTH), BF16),
        compiler_params=_params(("parallel", "parallel"), 48),
        name="attention_cached" if cached else "attention",
    )(*args)


def _head_sum_matrix():
    i = lax.broadcasted_iota(jnp.int32, (RWKV_WIDTH, RWKV_WIDTH), 0) // RWKV_N
    j = lax.broadcasted_iota(jnp.int32, (RWKV_WIDTH, RWKV_WIDTH), 1) // RWKV_N
    return (i == j).astype(BF16)


def _bf16_pieces(x, n):
    pieces = []
    for _ in range(n - 1):
        p = x.astype(BF16)
        pieces.append(p)
        x = x - p.astype(F32)
    return pieces + [x.astype(BF16)]


def _head_sums(x, ones_bd):
    hi, lo = _bf16_pieces(x, 2)
    return _dot(hi, ones_bd) + _dot(lo, ones_bd)


def _rwkv_prep_body(rw_ref, prev_ref, next_ref, cw_ref, wl_ref, w0_ref, a0_ref, kks_ref, ka_ref, rk_ref,
                    r_ref, v_ref, kk_ref, g_ref, bonus_ref, lw_ref, km_ref, b_ref, ext_scr,
                    *, tm, rows_lat, t_lat, t_ctx):
    ext_scr[0:HALO, :] = prev_ref[...]
    ext_scr[HALO:HALO + tm, :] = rw_ref[...]
    ext_scr[HALO + tm:, :] = next_ref[...]
    edges = _edge_masks(pl.program_id(0) * tm, rows_lat, t_lat, t_ctx)
    z = _conv3_block(ext_scr, cw_ref, None, slice(None), 0, *edges)
    c = RWKV_WIDTH
    r, k, v = z[:, :c], z[:, c:2 * c], z[:, 2 * c:3 * c]
    zl = z[:, 3 * c:]
    lane = lax.broadcasted_iota(jnp.int32, (1, LORA_IN), 1)
    lora_in = jnp.where(lane < DECAY_LORA, jnp.tanh(zl),
                        jnp.where(lane < DECAY_LORA + ICLR_LORA, zl, jax.nn.sigmoid(zl)))
    lora = _dot(lora_in.astype(BF16), wl_ref[...])
    ones_bd = _head_sum_matrix()
    kk = k * kks_ref[...]
    kk = kk * lax.rsqrt(_head_sums(kk * kk, ones_bd) + 1e-12)
    r_ref[...] = r
    v_ref[...] = v
    kk_ref[...] = kk
    g_ref[...] = lora[:, 4 * c:5 * c]
    kmod_sum = jnp.zeros_like(k)
    for d in range(2):
        wpre = w0_ref[d:d + 1, :] + lora[:, d * c:(d + 1) * c]
        lw_ref[d] = -math.exp(-0.5) * jax.nn.sigmoid(wpre)
        a = jax.nn.sigmoid(a0_ref[d:d + 1, :] + lora[:, (2 + d) * c:(3 + d) * c])
        kmod = k * (1.0 + (a - 1.0) * ka_ref[...])
        km_ref[d] = kmod
        b_ref[d] = kk * a
        kmod_sum = kmod_sum + kmod
    bonus_ref[...] = _head_sums(r * kmod_sum * rk_ref[...], ones_bd) * v


def _rwkv_prep(rw, conv_w, lora_w, w0, a0, kk_scale, ka, rk, rows_lat, t_lat, t_ctx):
    rows = rw.shape[0]
    tm = 256
    nh = tm // HALO
    last = rows // HALO - 1
    c = RWKV_WIDTH
    one = lambda: pl.BlockSpec((tm, c), lambda i: (i, 0))
    two = lambda: pl.BlockSpec((2, tm, c), lambda i: (0, i, 0))
    return pl.pallas_call(
        functools.partial(_rwkv_prep_body, tm=tm, rows_lat=rows_lat, t_lat=t_lat, t_ctx=t_ctx),
        grid=(rows // tm,),
        in_specs=[pl.BlockSpec((tm, RWKV_IN), lambda i: (i, 0)),
                  pl.BlockSpec((HALO, RWKV_IN), lambda i: (jnp.maximum(i * nh - 1, 0), 0)),
                  pl.BlockSpec((HALO, RWKV_IN), lambda i: (jnp.minimum((i + 1) * nh, last), 0)),
                  _resident(conv_w.shape), _resident(lora_w.shape), _resident(w0.shape), _resident(a0.shape),
                  _resident(kk_scale.shape), _resident(ka.shape), _resident(rk.shape)],
        out_specs=[one(), one(), one(), one(), one(), two(), two(), two()],
        out_shape=[jax.ShapeDtypeStruct((rows, c), F32)] * 5 + [jax.ShapeDtypeStruct((2, rows, c), F32)] * 3,
        scratch_shapes=[pltpu.VMEM((tm + 2 * HALO, RWKV_IN), F32)],
        compiler_params=_params(("parallel",), 48),
        name="rwkv_prep",
    )(rw, rw, rw, conv_w, lora_w, w0, a0, kk_scale, ka, rk)


def _scan_direction_operands(r_ref, v_ref, kk_ref, lw_ref, km_ref, b_ref, backward):
    n = CHUNK
    row = lax.broadcasted_iota(jnp.int32, (n, n), 0)
    col = lax.broadcasted_iota(jnp.int32, (n, n), 1)
    upto = (col >= row) if backward else (col <= row)
    lw = lw_ref[...]
    tri = upto.astype(BF16)
    cs = sum(_dot(tri, piece) for piece in _bf16_pieces(lw, 3))
    tot = jnp.sum(lw, axis=0, keepdims=True)
    grow = jnp.exp(-cs)
    to_end = jnp.exp(tot - cs)
    b = b_ref[...]
    km = km_ref[...]
    return dict(
        kkt=(kk_ref[...] * jnp.exp(cs - lw)).astype(BF16),
        rt=(r_ref[...] * jnp.exp(cs)).astype(BF16),
        bt=(b * grow).astype(BF16), kt=(km * grow).astype(BF16),
        bh=(b * to_end).astype(BF16), kh=(km * to_end).astype(BF16),
        vb=v_ref[...].astype(BF16), w_all=jnp.exp(tot))


def _rwkv_scan_body(rf_ref, vf_ref, kkf_ref, lwf_ref, kmf_ref, bf_ref,
                    rb_ref, vb_ref, kkb_ref, lwb_ref, kmb_ref, bb_ref, s0_ref,
                    yf_ref, yb_ref, sfin_ref, s_scr):
    c = pl.program_id(1)

    @pl.when(c == 0)
    def _():
        s_scr[...] = s0_ref[...]

    n = CHUNK
    row = lax.broadcasted_iota(jnp.int32, (2 * n, 2 * n), 0)
    col = lax.broadcasted_iota(jnp.int32, (2 * n, 2 * n), 1)
    t_idx, s_idx, read_rows = row % n, col % n, row >= n
    eye = (lax.broadcasted_iota(jnp.int32, (n, n), 0) == lax.broadcasted_iota(jnp.int32, (n, n), 1)).astype(F32)
    ops = (_scan_direction_operands(rf_ref, vf_ref, kkf_ref, lwf_ref, kmf_ref, bf_ref, False),
           _scan_direction_operands(rb_ref, vb_ref, kkb_ref, lwb_ref, kmb_ref, bb_ref, True))
    same_step = jnp.logical_and(read_rows, s_idx == t_idx)
    masks = (jnp.logical_or(s_idx < t_idx, same_step), jnp.logical_or(s_idx > t_idx, same_step))
    y_refs = (yf_ref, yb_ref)
    chains = [(d, h) for d in range(2) for h in range(RWKV_HEADS)]
    head = lambda d, name, h: ops[d][name][:, h * RWKV_N:(h + 1) * RWKV_N]
    cat = lambda a, b: jnp.concatenate([a, b], axis=0)

    lhs = [cat(head(d, "kkt", h), head(d, "rt", h)) for d, h in chains]
    coef = [jnp.where(masks[d], _dot_nt(l, cat(head(d, "bt", h), head(d, "kt", h))), 0.0)
            for l, (d, h) in zip(lhs, chains)]
    n_mat = [a[:n, :n] for a in coef]
    state = [s_scr[d, h] for d, h in chains]
    state_b = [s.astype(BF16) for s in state]
    read = [_dot_nt(l, sb) for l, sb in zip(lhs, state_b)]
    akv = [_dot(a[:n, n:].astype(BF16), head(d, "vb", h)) for a, (d, h) in zip(coef, chains)]

    x = [eye - m for m in n_mat]
    p = [_dot(m.astype(BF16), m.astype(BF16)) for m in n_mat]
    steps = int(math.log2(n)) - 1
    for s in range(steps):
        pb = [q.astype(BF16) for q in p]
        x = [xi + _dot(xi.astype(BF16), q) for xi, q in zip(x, pb)]
        if s + 1 < steps:
            p = [_dot(q, q) for q in pb]

    u = [_dot(xi.astype(BF16), (-(rd[:n] + ak)).astype(BF16)) for xi, rd, ak in zip(x, read, akv)]
    uv = [cat(ui.astype(BF16), head(d, "vb", h)) for ui, (d, h) in zip(u, chains)]
    for (d, h), a, rd, uvi, s in zip(chains, coef, read, uv, state):
        sl = slice(h * RWKV_N, (h + 1) * RWKV_N)
        y_refs[d][:, sl] = rd[n:] + _dot(a[n:].astype(BF16), uvi)
        s_scr[d, h] = s * ops[d]["w_all"][:, sl] + _dot_tn(uvi, cat(head(d, "bh", h), head(d, "kh", h)))

    @pl.when(c == pl.num_programs(1) - 1)
    def _():
        sfin_ref[...] = s_scr[...]


def _rwkv_scan(r, v, kk, lw, km, b, states, layer, row0, nseq, t):
    n = CHUNK
    nc = t // n
    c0 = row0 // n
    c = RWKV_WIDTH
    fwd = lambda bb, cc: c0 + bb * nc + cc
    bwd = lambda bb, cc: c0 + bb * nc + nc - 1 - cc
    one = lambda chunk: pl.BlockSpec((n, c), lambda bb, cc: (chunk(bb, cc), 0))
    two = lambda chunk, d: pl.BlockSpec((None, n, c), lambda bb, cc: (d, chunk(bb, cc), 0))
    state = pl.BlockSpec((None, 2, RWKV_HEADS, RWKV_N, RWKV_N), lambda bb, cc: (bb, 0, 0, 0, 0))
    state_in = pl.BlockSpec((None, None, 2, RWKV_HEADS, RWKV_N, RWKV_N), lambda bb, cc: (bb, layer, 0, 0, 0, 0))
    y_shape = jax.ShapeDtypeStruct((nseq * t, c), F32)
    return pl.pallas_call(
        _rwkv_scan_body,
        grid=(nseq, nc),
        in_specs=[one(fwd), one(fwd), one(fwd), two(fwd, 0), two(fwd, 0), two(fwd, 0),
                  one(bwd), one(bwd), one(bwd), two(bwd, 1), two(bwd, 1), two(bwd, 1), state_in],
        out_specs=[pl.BlockSpec((n, c), lambda bb, cc: (bb * nc + cc, 0)),
                   pl.BlockSpec((n, c), lambda bb, cc: (bb * nc + nc - 1 - cc, 0)), state],
        out_shape=[y_shape, y_shape, jax.ShapeDtypeStruct((nseq, 2, RWKV_HEADS, RWKV_N, RWKV_N), F32)],
        scratch_shapes=[pltpu.VMEM((2, RWKV_HEADS, RWKV_N, RWKV_N), F32)],
        compiler_params=_params(("parallel", "arbitrary"), 32),
        name=f"rwkv_scan_t{t}",
    )(r, v, kk, lw, km, b, r, v, kk, lw, km, b, states)


def _outproj_body(f_ref, a_ref, yf_ref, yb_ref, bonus_ref, g_ref, lg_ref, lb_ref, x_ref, gate_ref, w_ref, o_ref,
                  mix_scr):
    a0, r0 = FOURIER_WIDTH, FOURIER_WIDTH + ATTN_WIDTH
    mix_scr[:, :a0] = f_ref[...]
    mix_scr[:, a0:r0] = a_ref[...]
    ones_bd = _head_sum_matrix()
    y = yf_ref[...] + yb_ref[...]
    mu = _head_sums(y, ones_bd) * (1.0 / RWKV_N)
    yc = y - mu
    var = _head_sums(yc * yc, ones_bd) * (1.0 / RWKV_N)
    yn = yc * lax.rsqrt(var + GN_EPS) * lg_ref[...] + lb_ref[...]
    mix_scr[:, r0:] = ((yn + bonus_ref[...]) * g_ref[...]).astype(BF16)
    o_ref[...] = x_ref[...] + gate_ref[...] * _dot(mix_scr[...], w_ref[...])


def _out_proj(f, a, yf, yb, bonus, g, ln_g, ln_b, x, gate, w, layer, seg):
    rows, d = x.shape
    tm = 512 if rows % 512 == 0 else 256
    row = lambda arr: pl.BlockSpec((tm, arr.shape[1]), lambda i: (i, 0))
    return pl.pallas_call(
        _outproj_body,
        grid=(rows // tm,),
        in_specs=[row(f), row(a), row(yf), row(yb), row(bonus), row(g),
                  _resident(ln_g.shape), _resident(ln_b.shape), row(x),
                  pl.BlockSpec((None, 1, d), _segment_index(seg, tm)),
                  _resident_layer(w.shape, layer)],
        out_specs=row(x),
        out_shape=jax.ShapeDtypeStruct(x.shape, F32),
        scratch_shapes=[pltpu.VMEM((tm, w.shape[1]), BF16)],
        compiler_params=_params(("parallel",), 48),
        name="out_proj",
    )(f, a, yf, yb, bonus, g, ln_g, ln_b, x, gate, w)


def _edge_masks(block_row, rows_lat, t_lat, t_ctx):
    in_lat = block_row < rows_lat
    length = jnp.where(in_lat, t_lat, t_ctx)
    off = jnp.where(in_lat, block_row, block_row - rows_lat)
    starts = (lax.rem(off, length) == 0).astype(F32)
    ends = (lax.rem(off + SEQ_ALIGN, length) == 0).astype(F32)
    sub = lax.broadcasted_iota(jnp.int32, (8, 1), 0)
    return 1.0 - starts * (sub == 0).astype(F32), 1.0 - ends * (sub == 7).astype(F32)


def _conv3_block(ext_ref, w_ref, b_ref, cols, r0, not_first8, not_last8):
    w0, w1, w2 = w_ref[0:1, cols], w_ref[1:2, cols], w_ref[2:3, cols]

    def rows(lo, hi, prev_mask=None, next_mask=None):
        at = lambda shift: ext_ref[HALO + r0 + shift + lo:HALO + r0 + shift + hi, :]
        prev, nxt = at(-1), at(1)
        if prev_mask is not None:
            prev = prev * prev_mask
        if next_mask is not None:
            nxt = nxt * next_mask
        return w0 * prev + w1 * at(0) + w2 * nxt

    n = SEQ_ALIGN
    out = jnp.concatenate([rows(0, 8, prev_mask=not_first8), rows(8, n - 8), rows(n - 8, n, next_mask=not_last8)],
                          axis=0)
    return out if b_ref is None else out + b_ref[:, cols]


def _ffn_body(x_ref, prev_ref, next_ref, g_ref, sc_ref, sh_ref, gate_ref, wa_ref, wg_ref,
              cwa_ref, cwg_ref, ba_ref, bg_ref, wd_ref, fg_ref, o_ref,
              h_scr, ua0_scr, ua1_scr, ug0_scr, ug1_scr, act_cur, act_new,
              *, tm, rows_lat, t_lat, t_ctx, final_norm):
    j = pl.program_id(1)
    nj = pl.num_programs(1) - 1

    @pl.when(j == 0)
    def _():
        norm = lambda x: _modulated_rmsnorm(x, g_ref[...], sc_ref[...], sh_ref[...]).astype(BF16)
        h_scr[0:HALO, :] = norm(prev_ref[...])
        h_scr[HALO:HALO + tm, :] = norm(x_ref[...])
        h_scr[HALO + tm:, :] = norm(next_ref[...])
        act_new[...] = jnp.zeros_like(act_new)
        o_ref[...] = jnp.zeros_like(o_ref)

    subs = [slice(s, s + FFN_SUB) for s in range(0, wa_ref.shape[1], FFN_SUB)]
    blocks = range(0, tm, SEQ_ALIGN)

    u_scr = ((ua0_scr, ug0_scr), (ua1_scr, ug1_scr))
    assert len(subs) == len(u_scr)

    def up_project(s):
        ua_scr, ug_scr = u_scr[s]
        ua_scr[...] = _dot(h_scr[...], wa_ref[:, subs[s]])
        ug_scr[...] = _dot(h_scr[...], wg_ref[:, subs[s]])

    def conv_gate(s):
        ua_scr, ug_scr = u_scr[s]
        for r0 in blocks:
            edge = _edge_masks(pl.program_id(0) * tm + r0, rows_lat, t_lat, t_ctx)
            ua = _conv3_block(ua_scr, cwa_ref, ba_ref, subs[s], r0, *edge)
            ug = _conv3_block(ug_scr, cwg_ref, bg_ref, subs[s], r0, *edge)
            act_new[r0:r0 + SEQ_ALIGN, subs[s]] = (ug * jax.nn.sigmoid(ug) * ua).astype(BF16)

    @pl.when(j < nj)
    def _():
        act_cur[...] = act_new[...]
        up_project(0)
        up_project(1)
        conv_gate(0)
        conv_gate(1)
        o_ref[...] += _dot(act_cur[...], wd_ref[...])

    @pl.when(j == nj)
    def _():
        out = x_ref[...] + gate_ref[...] * (o_ref[...] + _dot(act_new[...], wd_ref[...]))
        if final_norm:
            out = out * lax.rsqrt(jnp.mean(out * out, axis=-1, keepdims=True) + NORM_EPS) * fg_ref[...]
        o_ref[...] = out


def _conv_ffn(x, g, scale, shift, gate, wa, wg, cwa, cwg, ba, bg, wd, final_g, layer, seg, rows_lat, t_lat, t_ctx,
              final_norm):
    rows, d = x.shape
    tm = next(t for t in (1024, 512, 256) if rows % t == 0 and rows_lat % t == 0 and seg[1] % t == 0)
    tf = FFN_TILE
    nh = tm // HALO
    last = rows // HALO - 1
    nj = D_FF_PAD // tf
    seg = _segment_index(seg, tm)
    col = lambda r: pl.BlockSpec((r, tf), lambda i, j: (0, jnp.minimum(j, nj - 1)))
    up = pl.BlockSpec((None, None, d, tf), lambda i, j: (layer, jnp.minimum(j, nj - 1), 0, 0))
    return pl.pallas_call(
        functools.partial(_ffn_body, tm=tm, rows_lat=rows_lat, t_lat=t_lat, t_ctx=t_ctx, final_norm=final_norm),
        grid=(rows // tm, nj + 1),
        in_specs=[pl.BlockSpec((tm, d), lambda i, j: (i, 0), pipeline_mode=pl.Buffered(1)),
                  pl.BlockSpec((HALO, d), lambda i, j: (jnp.maximum(i * nh - 1, 0), 0)),
                  pl.BlockSpec((HALO, d), lambda i, j: (jnp.minimum((i + 1) * nh, last), 0)),
                  pl.BlockSpec((1, d), lambda i, j: (0, 0)),
                  pl.BlockSpec((None, 1, d), seg), pl.BlockSpec((None, 1, d), seg), pl.BlockSpec((None, 1, d), seg),
                  up, up, col(3), col(3), col(1), col(1),
                  pl.BlockSpec((None, tf, d), lambda i, j: (layer, jnp.maximum(j - 1, 0), 0)),
                  pl.BlockSpec((1, d), lambda i, j: (0, 0))],
        out_specs=pl.BlockSpec((tm, d), lambda i, j: (i, 0)),
        out_shape=jax.ShapeDtypeStruct(x.shape, F32),
        scratch_shapes=[pltpu.VMEM((tm + 2 * HALO, d), BF16),
                        pltpu.VMEM((tm + 2 * HALO, FFN_SUB), F32),
                        pltpu.VMEM((tm + 2 * HALO, FFN_SUB), F32),
                        pltpu.VMEM((tm + 2 * HALO, FFN_SUB), F32),
                        pltpu.VMEM((tm + 2 * HALO, FFN_SUB), F32),
                        pltpu.VMEM((tm, tf), BF16),
                        pltpu.VMEM((tm, tf), BF16)],
        compiler_params=_params(("parallel", "arbitrary"), 56),
        name="conv_ffn",
    )(x, x, x, g, scale, shift, gate, wa, wg, cwa, cwg, ba, bg, wd, final_g)


def _lora_weight(w2, a2, g2):
    c = RWKV_WIDTH
    wl = jnp.zeros((LORA_IN, 5 * c), F32)
    wl = wl.at[:DECAY_LORA, 0:c].set(w2[0]).at[:DECAY_LORA, c:2 * c].set(w2[1])
    wl = wl.at[DECAY_LORA:DECAY_LORA + ICLR_LORA, 2 * c:3 * c].set(a2[0])
    wl = wl.at[DECAY_LORA:DECAY_LORA + ICLR_LORA, 3 * c:4 * c].set(a2[1])
    return wl.at[DECAY_LORA + ICLR_LORA:, 4 * c:].set(g2)


LANES = 128


def _split_up_body(a_ref, g_ref, oa_ref, og_ref, *, nvalid):
    keep = pl.program_id(1) < nvalid
    oa_ref[...] = jnp.where(keep, a_ref[...], 0.0).astype(BF16)
    og_ref[...] = jnp.where(keep, g_ref[...], 0.0).astype(BF16)


def _split_ffn_up(ffn_up):
    depth, d, _ = ffn_up.shape
    nvalid = D_FF // LANES
    per_tile = FFN_TILE // LANES
    src = lambda half: pl.BlockSpec((None, d, LANES),
                                    lambda l, j: (l, 0, half * nvalid + jnp.minimum(j, nvalid - 1)))
    dst = pl.BlockSpec((None, None, d, LANES), lambda l, j: (l, j // per_tile, 0, j % per_tile))
    shape = jax.ShapeDtypeStruct((depth, D_FF_PAD // FFN_TILE, d, FFN_TILE), BF16)
    return pl.pallas_call(
        functools.partial(_split_up_body, nvalid=nvalid),
        grid=(depth, D_FF_PAD // LANES),
        in_specs=[src(0), src(1)], out_specs=[dst, dst], out_shape=[shape, shape],
        compiler_params=_params(("parallel", "parallel"), 32),
        name="split_ffn_up",
    )(ffn_up, ffn_up)


def _pad_down_body(w_ref, o_ref, *, tk):
    row = pl.program_id(1) * tk + lax.broadcasted_iota(jnp.int32, (tk, 1), 0)
    o_ref[...] = jnp.where(row < D_FF, w_ref[...], 0.0).astype(BF16)


def _pad_ffn_down(ffn_down):
    depth, _, d = ffn_down.shape
    tk = FFN_TILE
    return pl.pallas_call(
        functools.partial(_pad_down_body, tk=tk),
        grid=(depth, D_FF_PAD // tk),
        in_specs=[pl.BlockSpec((None, tk, d), lambda l, j: (l, j, 0))],
        out_specs=pl.BlockSpec((None, tk, d), lambda l, j: (l, j, 0)),
        out_shape=jax.ShapeDtypeStruct((depth, D_FF_PAD, d), BF16),
        compiler_params=_params(("parallel", "parallel"), 32),
        name="pad_ffn_down",
    )(ffn_down)


def _pad_cols(w):
    return jnp.pad(w, ((0, 0), (0, D_FF_PAD - D_FF)))


def _forward(x_lat, x_ctx, cache_k, cache_v, state, c, c_ctx, w_ada, b_ada, norm1_g, norm2_g, w_in, w_out,
             q_norm_g, k_norm_g, rw_conv, rw_w0, rw_w2, rw_a0, rw_a2, rw_g2, rw_kk, rw_ka, rw_rk,
             rw_lnx_g, rw_lnx_b, ffn_up, ffn_conv_w, ffn_conv_b, ffn_down, final_norm_g):
    n_lat, t_lat, d = x_lat.shape
    n_ctx, t_ctx, _ = x_ctx.shape
    depth = w_ada.shape[0]
    past = cache_k.shape[2]
    rows_lat, rows_ctx = n_lat * t_lat, n_ctx * t_ctx
    assert n_lat < MOD_ROWS
    assert t_lat % SEQ_ALIGN == 0 and t_ctx % SEQ_ALIGN == 0 and t_lat % GRID_W == 0

    cvec = jnp.concatenate([c, jnp.broadcast_to(c_ctx[None, :], (MOD_ROWS - n_lat, d))], axis=0)
    mod = _adaln_mod(cvec, w_ada, b_ada).reshape(depth, MOD_ROWS, 6, 1, d)

    ffn_wa, ffn_wg = _split_ffn_up(ffn_up)
    ffn_wd = _pad_ffn_down(ffn_down)
    rope_tabs = _rope_tables(t_lat)
    time_lat, chan_tab = _dft_tables(t_lat)
    time_ctx, _ = _dft_tables(t_ctx)
    zero_state = jnp.zeros((n_ctx, 1, 2, RWKV_HEADS, RWKV_N, RWKV_N), F32)
    row2 = lambda a: a.reshape(1, -1)

    cache = (cache_k.reshape(n_lat, depth, past, KV_WIDTH), cache_v.reshape(n_lat, depth, past, KV_WIDTH))

    def trunk_layer(x, l, latent):
        nseq, t = (n_lat, t_lat) if latent else (n_ctx, t_ctx)
        rows = nseq * t
        seg = (0, t_lat) if latent else (n_lat, rows)
        region_rows_lat = rows if latent else 0
        shift1, scale1, gate1, shift2, scale2, gate2 = (mod[l, :, i] for i in range(6))
        in_args = (x, row2(norm1_g[l]), scale1, shift1, w_in_b, l, seg, row2(q_norm_g[l]), row2(k_norm_g[l]))
        if latent:
            f, qb, kb, vb, rw = _in_proj(*in_args, rope_tabs)
            a_out = _attention(qb, kb, vb, nseq, t, cache[0], cache[1], l)
            k_norm = v = None
        else:
            f, qb, kb, vb, rw, k_norm, v = _in_proj(*in_args)
            a_out = _attention(qb, kb, vb, nseq, t)
        f_out = _fourier_mix(f, time_lat if latent else time_ctx, chan_tab, 0, nseq, t)

        r_, v_, kk, g_, bonus, lw, km, b_ = _rwkv_prep(
            rw, rw_conv[l], lora_w[l], rw_w0[l], rw_a0[l],
            row2(rw_kk[l]), row2(rw_ka[l]), row2(rw_rk[l]), region_rows_lat, t_lat, t_ctx)
        yf, yb, s_fin = _rwkv_scan(r_, v_, kk, lw, km, b_, state if latent else zero_state, l if latent else 0,
                                   0, nseq, t)
        x = _out_proj(f_out, a_out, yf, yb, bonus, g_, row2(rw_lnx_g[l]), row2(rw_lnx_b[l]), x, gate1,
                      w_out_b, l, seg)
        x = _conv_ffn(x, row2(norm2_g[l]), scale2, shift2, gate2, ffn_wa, ffn_wg,
                      _pad_cols(ffn_conv_w[l][:, :D_FF]), _pad_cols(ffn_conv_w[l][:, D_FF:]),
                      _pad_cols(row2(ffn_conv_b[l][:D_FF])), _pad_cols(row2(ffn_conv_b[l][D_FF:])),
                      ffn_wd, row2(final_norm_g), l, seg, region_rows_lat, t_lat, t_ctx, final_norm=l == depth - 1)
        return x, k_norm, v, s_fin

    w_in_b, w_out_b = w_in.astype(BF16), w_out.astype(BF16)
    lora_w = [_lora_weight(rw_w2[l], rw_a2[l], rw_g2[l]).astype(BF16) for l in range(depth)]
    xs, xc = x_lat.reshape(rows_lat, d), x_ctx.reshape(rows_ctx, d)
    new_k, new_v, new_s = [], [], []
    for l in range(depth):
        xc, k_norm, v_ctx, s_ctx = trunk_layer(xc, l, latent=False)
        new_k.append(k_norm.reshape(n_ctx, t_ctx, ATTN_KV_HEADS, HEAD_DIM))
        new_v.append(v_ctx.reshape(n_ctx, t_ctx, ATTN_KV_HEADS, HEAD_DIM))
        new_s.append(s_ctx)
        xs, _, _, _ = trunk_layer(xs, l, latent=True)

    y_lat, y_ctx = xs.reshape(n_lat, t_lat, d), xc.reshape(n_ctx, t_ctx, d)
    return (y_ctx, y_lat, jnp.stack(new_k, axis=1), jnp.stack(new_v, axis=1), jnp.stack(new_s, axis=1))


def kernel(x_prompt, x_sample, cache_attn_k, cache_attn_v, state_rwkv, c, c_ctx, w_ada, b_ada, norm1_g, norm2_g, w_in, w_out, q_norm_g, k_norm_g, rw_conv, rw_w0, rw_w2, rw_a0, rw_a2, rw_g2, rw_kk, rw_ka, rw_rk, rw_lnx_g, rw_lnx_b, ffn_up, ffn_conv_w, ffn_conv_b, ffn_down, final_norm_g):
    return _forward(x_sample, x_prompt, cache_attn_k, cache_attn_v, state_rwkv, c, c_ctx, w_ada, b_ada,
                    norm1_g, norm2_g, w_in, w_out, q_norm_g, k_norm_g, rw_conv, rw_w0, rw_w2, rw_a0, rw_a2,
                    rw_g2, rw_kk, rw_ka, rw_rk, rw_lnx_g, rw_lnx_b, ffn_up, ffn_conv_w, ffn_conv_b, ffn_down,
                    final_norm_g)
```

```python
import functools
import math

import jax
import jax.numpy as jnp
import numpy as np
from jax import lax
from jax.experimental import pallas as pl
from jax.experimental.pallas import tpu as pltpu

D_MODEL = 2048
GRID_W = 64
HEAD_DIM = 128
ATTN_HEADS = 8
ATTN_KV_HEADS = 2
KV_GROUP = ATTN_HEADS // ATTN_KV_HEADS
ATTN_WIDTH = ATTN_HEADS * HEAD_DIM
KV_WIDTH = ATTN_KV_HEADS * HEAD_DIM
FOURIER_WIDTH = 512
FOURIER_GROUPS = 4
FOURIER_GROUP_WIDTH = FOURIER_WIDTH // FOURIER_GROUPS
RWKV_WIDTH = 512
RWKV_N = 64
RWKV_HEADS = RWKV_WIDTH // RWKV_N
DECAY_LORA = 64
ICLR_LORA = 64
GATE_LORA = 128
LORA_IN = DECAY_LORA + ICLR_LORA + GATE_LORA
RWKV_IN = 3 * RWKV_WIDTH + LORA_IN
IN_WIDTH = FOURIER_WIDTH + ATTN_WIDTH + 2 * KV_WIDTH + RWKV_IN
D_FF = 5504
ROPE_THETA = 10000.0
NORM_EPS = 1e-6
GN_EPS = 64e-5

MOD_ROWS = 16
FFN_TILE = 512
FFN_SUB = 256
D_FF_PAD = -(-D_FF // FFN_TILE) * FFN_TILE
HALO = 16
CHUNK = 64
SEQ_ALIGN = 256
MIB = 2 ** 20

F32 = jnp.float32
BF16 = jnp.bfloat16


def _params(semantics, vmem_mib):
    return pltpu.CompilerParams(dimension_semantics=semantics, vmem_limit_bytes=vmem_mib * MIB)


def _resident(shape):
    return pl.BlockSpec(shape, lambda *_: (0,) * len(shape), pipeline_mode=pl.Buffered(1))


def _resident_layer(stacked_shape, layer):
    shape = stacked_shape[1:]
    return pl.BlockSpec((None,) + tuple(shape), lambda *_: (layer,) + (0,) * len(shape),
                        pipeline_mode=pl.Buffered(1))


def _dot(a, b):
    return jnp.dot(a, b, preferred_element_type=F32)


def _dot_nt(a, b):
    return lax.dot_general(a, b, (((1,), (1,)), ((), ())), preferred_element_type=F32)


def _dot_tn(a, b):
    return lax.dot_general(a, b, (((0,), (0,)), ((), ())), preferred_element_type=F32)


def _modulated_rmsnorm(x, g, scale, shift):
    ms = jnp.mean(x * x, axis=-1, keepdims=True)
    return (x * lax.rsqrt(ms + NORM_EPS) * g) * (1.0 + scale) + shift


def _mod_body(c_ref, w_ref, b_ref, o_ref):
    c = c_ref[...]
    s = (c * jax.nn.sigmoid(c)).astype(BF16)
    o_ref[...] = _dot(s, w_ref[...].astype(BF16)) + b_ref[...]


def _adaln_mod(cvec, w_ada, b_ada):
    depth, d, n = w_ada.shape
    tn = 1024
    return pl.pallas_call(
        _mod_body,
        grid=(depth, n // tn),
        in_specs=[pl.BlockSpec((MOD_ROWS, d), lambda l, j: (0, 0)),
                  pl.BlockSpec((None, d, tn), lambda l, j: (l, 0, j)),
                  pl.BlockSpec((None, 1, tn), lambda l, j: (l, 0, j))],
        out_specs=pl.BlockSpec((None, MOD_ROWS, tn), lambda l, j: (l, 0, j)),
        out_shape=jax.ShapeDtypeStruct((depth, MOD_ROWS, n), F32),
        compiler_params=_params(("parallel", "parallel"), 40),
        name="adaln_mod",
    )(cvec, w_ada, b_ada.reshape(depth, 1, n))


Q_COL = FOURIER_WIDTH
K_COL = Q_COL + ATTN_WIDTH
V_COL = K_COL + KV_WIDTH
RW_COL = V_COL + KV_WIDTH


def _inproj_body(*refs, rope):
    x_ref, g_ref, sc_ref, sh_ref, w_ref, gq_ref, gk_ref = refs[:7]
    if rope:
        cos_ref, sin_ref, f_ref, q_ref, k_ref, v_ref, rw_ref = refs[7:]
        cos, sin = cos_ref[...], sin_ref[...]
        lane = lax.broadcasted_iota(jnp.int32, (1, HEAD_DIM), 1)
        low = (lane % (HEAD_DIM // 2)) < (HEAD_DIM // 4)
    else:
        f_ref, q_ref, k_ref, v_ref, rw_ref, kn_ref, vf_ref = refs[7:]

    h = _modulated_rmsnorm(x_ref[...], g_ref[...], sc_ref[...], sh_ref[...]).astype(BF16)
    proj = lambda start, width: _dot(h, w_ref[:, start:start + width])
    q, k, v = proj(Q_COL, ATTN_WIDTH), proj(K_COL, KV_WIDTH), proj(V_COL, KV_WIDTH)
    f_ref[...] = proj(0, FOURIER_WIDTH).astype(BF16)
    rw_ref[...] = proj(RW_COL, RWKV_IN)

    def head_norm(xh, g):
        ms = jnp.mean(xh * xh, axis=-1, keepdims=True)
        return xh * lax.rsqrt(ms + NORM_EPS) * g

    def rotate(xh):
        partner = jnp.where(low, pltpu.roll(xh, HEAD_DIM - HEAD_DIM // 4, 1), pltpu.roll(xh, HEAD_DIM // 4, 1))
        return xh * cos + partner * sin

    scale = HEAD_DIM ** -0.5
    for hd in range(ATTN_HEADS):
        sl = slice(hd * HEAD_DIM, (hd + 1) * HEAD_DIM)
        qh = head_norm(q[:, sl], gq_ref[...])
        if rope:
            qh = rotate(qh)
        q_ref[:, sl] = (qh * scale).astype(BF16)
    for j in range(ATTN_KV_HEADS):
        sl = slice(j * HEAD_DIM, (j + 1) * HEAD_DIM)
        kh = head_norm(k[:, sl], gk_ref[...])
        if rope:
            kh = rotate(kh)
        else:
            kn_ref[:, sl] = kh
        k_ref[:, sl] = kh.astype(BF16)
    v_ref[...] = v.astype(BF16)
    if not rope:
        vf_ref[...] = v


def _segment_index(seg, tm):
    seg0, seg_rows = seg
    assert seg_rows % tm == 0
    return lambda i, *_: (seg0 + i * tm // seg_rows, 0, 0)


def _in_proj(x, g, scale, shift, w, layer, seg, gq, gk, rope_tabs=None):
    rows, d = x.shape
    tm = 512 if rows % 512 == 0 else 256
    seg = _segment_index(seg, tm)
    rope = rope_tabs is not None
    in_specs = [pl.BlockSpec((tm, d), lambda i: (i, 0)),
                _resident((1, d)),
                pl.BlockSpec((None, 1, d), seg),
                pl.BlockSpec((None, 1, d), seg),
                _resident_layer(w.shape, layer),
                _resident((1, HEAD_DIM)), _resident((1, HEAD_DIM))]
    args = [x, g, scale, shift, w, gq, gk]
    outs = [(FOURIER_WIDTH, BF16), (ATTN_WIDTH, BF16), (KV_WIDTH, BF16), (KV_WIDTH, BF16), (RWKV_IN, F32)]
    if rope:
        nt = rope_tabs[0].shape[0] // tm
        in_specs += [pl.BlockSpec((tm, HEAD_DIM), lambda i: (i % nt, 0))] * 2
        args += list(rope_tabs)
    else:
        outs += [(KV_WIDTH, F32), (KV_WIDTH, F32)]
    return pl.pallas_call(
        functools.partial(_inproj_body, rope=rope),
        grid=(rows // tm,),
        in_specs=in_specs,
        out_specs=[pl.BlockSpec((tm, width), lambda i: (i, 0)) for width, _ in outs],
        out_shape=[jax.ShapeDtypeStruct((rows, width), dt) for width, dt in outs],
        compiler_params=_params(("parallel",), 56),
        name="in_proj_rope" if rope else "in_proj",
    )(*args)


def _dft_tables(t):
    def angles(n):
        i = np.arange(n, dtype=np.int64)
        return (2.0 * math.pi / n) * ((i[:, None] * i[None, :]) % n)
    at = angles(t)
    time_tab = np.concatenate([np.cos(at), -np.sin(at)], axis=1).astype(np.float32)
    ac = angles(FOURIER_GROUP_WIDTH)
    eye = np.eye(FOURIER_GROUPS)
    chan_tab = np.concatenate([np.kron(eye, np.cos(ac)), np.kron(eye, np.sin(ac))], axis=1).astype(np.float32)
    return jnp.asarray(time_tab).astype(BF16), jnp.asarray(chan_tab).astype(BF16)


def _fourier_body(u_ref, ct_ref, cc_ref, o_ref, ab_scr, *, t, norm):
    @pl.when(pl.program_id(1) == 0)
    def _():
        ab = _dot(u_ref[...], cc_ref[...])
        ab_scr[0:t, :] = ab[:, :FOURIER_WIDTH].astype(BF16)
        ab_scr[t:2 * t, :] = ab[:, FOURIER_WIDTH:].astype(BF16)
    o_ref[...] = (_dot(ct_ref[...], ab_scr[...]) * norm).astype(o_ref.dtype)


def _fourier_mix(u, time_tab, chan_tab, row0, nseq, t):
    tm = min(t, 512)
    nt = t // tm
    seq0 = row0 // t
    return pl.pallas_call(
        functools.partial(_fourier_body, t=t, norm=1.0 / math.sqrt(t * FOURIER_GROUP_WIDTH)),
        grid=(nseq, nt),
        in_specs=[pl.BlockSpec((t, FOURIER_WIDTH), lambda b, i: (seq0 + b, 0)),
                  pl.BlockSpec((tm, 2 * t), lambda b, i: (i, 0)),
                  _resident(chan_tab.shape)],
        out_specs=pl.BlockSpec((tm, FOURIER_WIDTH), lambda b, i: (b * nt + i, 0)),
        out_shape=jax.ShapeDtypeStruct((nseq * t, FOURIER_WIDTH), BF16),
        scratch_shapes=[pltpu.VMEM((2 * t, FOURIER_WIDTH), BF16)],
        compiler_params=_params(("parallel", "arbitrary"), 40),
        name=f"fourier_mix_t{t}",
    )(u, time_tab, chan_tab)


def _rope_tables(t):
    pos = jnp.arange(t, dtype=jnp.int32)
    rows = (pos // GRID_W).astype(F32)
    cols = (pos % GRID_W).astype(F32)
    half = HEAD_DIM // 2
    inv = 1.0 / (ROPE_THETA ** (jnp.arange(0, half, 2, dtype=F32) / half))
    def tab(p):
        ang = p[:, None] * inv[None, :]
        return (jnp.concatenate([jnp.cos(ang), jnp.cos(ang)], -1),
                jnp.concatenate([-jnp.sin(ang), jnp.sin(ang)], -1))
    cr, sr = tab(rows)
    cc, sc = tab(cols)
    return jnp.concatenate([cr, cc], -1), jnp.concatenate([sr, sc], -1)


def _attn_body(*refs, cached):
    if cached:
        q_ref, k_ref, v_ref, kc_ref, vc_ref, o_ref = refs
    else:
        q_ref, k_ref, v_ref, o_ref = refs
    for j in range(ATTN_KV_HEADS):
        kv = slice(j * HEAD_DIM, (j + 1) * HEAD_DIM)
        kj, vj = k_ref[:, kv], v_ref[:, kv]
        if cached:
            kc, vc = kc_ref[:, kv].astype(BF16), vc_ref[:, kv].astype(BF16)
        for g in range(KV_GROUP):
            sl = slice((j * KV_GROUP + g) * HEAD_DIM, (j * KV_GROUP + g + 1) * HEAD_DIM)
            qh = q_ref[:, sl]
            s = _dot_nt(qh, kj)
            m = jnp.max(s, axis=-1, keepdims=True)
            if cached:
                sc = _dot_nt(qh, kc)
                m = jnp.maximum(m, jnp.max(sc, axis=-1, keepdims=True))
            p = jnp.exp(s - m)
            l = jnp.sum(p, axis=-1, keepdims=True)
            acc = _dot(p.astype(BF16), vj)
            if cached:
                pc = jnp.exp(sc - m)
                l = l + jnp.sum(pc, axis=-1, keepdims=True)
                acc = acc + _dot(pc.astype(BF16), vc)
            o_ref[:, sl] = (acc / l).astype(o_ref.dtype)


def _attention(q, k, v, nseq, t, cache_k=None, cache_v=None, layer=0):
    tq = 512 if t % 512 == 0 else 256
    nq = t // tq
    cached = cache_k is not None
    in_specs = [pl.BlockSpec((tq, ATTN_WIDTH), lambda b, i: (b * nq + i, 0)),
                pl.BlockSpec((t, KV_WIDTH), lambda b, i: (b, 0)),
                pl.BlockSpec((t, KV_WIDTH), lambda b, i: (b, 0))]
    args = [q, k, v]
    if cached:
        past = cache_k.shape[2]
        in_specs += [pl.BlockSpec((None, None, past, KV_WIDTH), lambda b, i: (b, layer, 0, 0))] * 2
        args += [cache_k, cache_v]
    return pl.pallas_call(
        functools.partial(_attn_body, cached=cached),
        grid=(nseq, nq),
        in_specs=in_specs,
        out_specs=pl.BlockSpec((tq, ATTN_WIDTH), lambda b, i: (b * nq + i, 0)),
        out_shape=jax.ShapeDtypeStruct((nseq * t, ATTN_WIDTH), BF16),
        compiler_params=_params(("parallel", "parallel"), 48),
        name="attention_cached" if cached else "attention",
    )(*args)


def _head_sum_matrix():
    i = lax.broadcasted_iota(jnp.int32, (RWKV_WIDTH, RWKV_WIDTH), 0) // RWKV_N
    j = lax.broadcasted_iota(jnp.int32, (RWKV_WIDTH, RWKV_WIDTH), 1) // RWKV_N
    return (i == j).astype(BF16)


def _bf16_pieces(x, n):
    pieces = []
    for _ in range(n - 1):
        p = x.astype(BF16)
        pieces.append(p)
        x = x - p.astype(F32)
    return pieces + [x.astype(BF16)]


def _head_sums(x, ones_bd):
    hi, lo = _bf16_pieces(x, 2)
    return _dot(hi, ones_bd) + _dot(lo, ones_bd)


def _rwkv_prep_body(rw_ref, prev_ref, next_ref, cw_ref, wl_ref, w0_ref, a0_ref, kks_ref, ka_ref, rk_ref,
                    r_ref, v_ref, kk_ref, g_ref, bonus_ref, lw_ref, km_ref, b_ref, ext_scr,
                    *, tm, rows_lat, t_lat, t_ctx):
    ext_scr[0:HALO, :] = prev_ref[...]
    ext_scr[HALO:HALO + tm, :] = rw_ref[...]
    ext_scr[HALO + tm:, :] = next_ref[...]
    edges = _edge_masks(pl.program_id(0) * tm, rows_lat, t_lat, t_ctx)
    z = _conv3_block(ext_scr, cw_ref, None, slice(None), 0, *edges)
    c = RWKV_WIDTH
    r, k, v = z[:, :c], z[:, c:2 * c], z[:, 2 * c:3 * c]
    zl = z[:, 3 * c:]
    lane = lax.broadcasted_iota(jnp.int32, (1, LORA_IN), 1)
    lora_in = jnp.where(lane < DECAY_LORA, jnp.tanh(zl),
                        jnp.where(lane < DECAY_LORA + ICLR_LORA, zl, jax.nn.sigmoid(zl)))
    lora = _dot(lora_in.astype(BF16), wl_ref[...])
    ones_bd = _head_sum_matrix()
    kk = k * kks_ref[...]
    kk = kk * lax.rsqrt(_head_sums(kk * kk, ones_bd) + 1e-12)
    r_ref[...] = r.astype(r_ref.dtype)
    v_ref[...] = v.astype(v_ref.dtype)
    kk_ref[...] = kk.astype(kk_ref.dtype)
    g_ref[...] = lora[:, 4 * c:5 * c]
    kmod_sum = jnp.zeros_like(k)
    for d in range(2):
        wpre = w0_ref[d:d + 1, :] + lora[:, d * c:(d + 1) * c]
        lw_ref[d] = -math.exp(-0.5) * jax.nn.sigmoid(wpre)
        a = jax.nn.sigmoid(a0_ref[d:d + 1, :] + lora[:, (2 + d) * c:(3 + d) * c])
        kmod = k * (1.0 + (a - 1.0) * ka_ref[...])
        km_ref[d] = kmod.astype(km_ref.dtype)
        b_ref[d] = (kk * a).astype(b_ref.dtype)
        kmod_sum = kmod_sum + kmod
    bonus_ref[...] = _head_sums(r * kmod_sum * rk_ref[...], ones_bd) * v


def _rwkv_prep(rw, conv_w, lora_w, w0, a0, kk_scale, ka, rk, rows_lat, t_lat, t_ctx):
    rows = rw.shape[0]
    tm = 256
    nh = tm // HALO
    last = rows // HALO - 1
    c = RWKV_WIDTH
    one = lambda: pl.BlockSpec((tm, c), lambda i: (i, 0))
    two = lambda: pl.BlockSpec((2, tm, c), lambda i: (0, i, 0))
    return pl.pallas_call(
        functools.partial(_rwkv_prep_body, tm=tm, rows_lat=rows_lat, t_lat=t_lat, t_ctx=t_ctx),
        grid=(rows // tm,),
        in_specs=[pl.BlockSpec((tm, RWKV_IN), lambda i: (i, 0)),
                  pl.BlockSpec((HALO, RWKV_IN), lambda i: (jnp.maximum(i * nh - 1, 0), 0)),
                  pl.BlockSpec((HALO, RWKV_IN), lambda i: (jnp.minimum((i + 1) * nh, last), 0)),
                  _resident(conv_w.shape), _resident(lora_w.shape), _resident(w0.shape), _resident(a0.shape),
                  _resident(kk_scale.shape), _resident(ka.shape), _resident(rk.shape)],
        out_specs=[one(), one(), one(), one(), one(), two(), two(), two()],
        out_shape=[jax.ShapeDtypeStruct((rows, c), dt) for dt in (BF16, BF16, BF16, F32, F32)]
        + [jax.ShapeDtypeStruct((2, rows, c), dt) for dt in (F32, BF16, BF16)],
        scratch_shapes=[pltpu.VMEM((tm + 2 * HALO, RWKV_IN), F32)],
        compiler_params=_params(("parallel",), 48),
        name="rwkv_prep",
    )(rw, rw, rw, conv_w, lora_w, w0, a0, kk_scale, ka, rk)


def _scan_direction_operands(r_ref, v_ref, kk_ref, lw_ref, km_ref, b_ref, backward):
    n = CHUNK
    row = lax.broadcasted_iota(jnp.int32, (n, n), 0)
    col = lax.broadcasted_iota(jnp.int32, (n, n), 1)
    upto = (col >= row) if backward else (col <= row)
    lw = lw_ref[...]
    tri = upto.astype(BF16)
    cs = sum(_dot(tri, piece) for piece in _bf16_pieces(lw, 3))
    tot = jnp.sum(lw, axis=0, keepdims=True)
    grow = jnp.exp(-cs)
    to_end = jnp.exp(tot - cs)
    b = b_ref[...]
    km = km_ref[...]
    return dict(
        kkt=(kk_ref[...] * jnp.exp(cs - lw)).astype(BF16),
        rt=(r_ref[...] * jnp.exp(cs)).astype(BF16),
        bt=(b * grow).astype(BF16), kt=(km * grow).astype(BF16),
        bh=(b * to_end).astype(BF16), kh=(km * to_end).astype(BF16),
        vb=v_ref[...].astype(BF16), w_all=jnp.exp(tot))


def _rwkv_scan_body(rf_ref, vf_ref, kkf_ref, lwf_ref, kmf_ref, bf_ref,
                    rb_ref, vb_ref, kkb_ref, lwb_ref, kmb_ref, bb_ref, s0_ref,
                    yf_ref, yb_ref, sfin_ref, s_scr):
    c = pl.program_id(1)

    @pl.when(c == 0)
    def _():
        s_scr[...] = s0_ref[...]

    n = CHUNK
    row = lax.broadcasted_iota(jnp.int32, (2 * n, 2 * n), 0)
    col = lax.broadcasted_iota(jnp.int32, (2 * n, 2 * n), 1)
    t_idx, s_idx, read_rows = row % n, col % n, row >= n
    eye = (lax.broadcasted_iota(jnp.int32, (n, n), 0) == lax.broadcasted_iota(jnp.int32, (n, n), 1)).astype(F32)
    ops = (_scan_direction_operands(rf_ref, vf_ref, kkf_ref, lwf_ref, kmf_ref, bf_ref, False),
           _scan_direction_operands(rb_ref, vb_ref, kkb_ref, lwb_ref, kmb_ref, bb_ref, True))
    same_step = jnp.logical_and(read_rows, s_idx == t_idx)
    masks = (jnp.logical_or(s_idx < t_idx, same_step), jnp.logical_or(s_idx > t_idx, same_step))
    y_refs = (yf_ref, yb_ref)
    chains = [(d, h) for d in range(2) for h in range(RWKV_HEADS)]
    head = lambda d, name, h: ops[d][name][:, h * RWKV_N:(h + 1) * RWKV_N]
    cat = lambda a, b: jnp.concatenate([a, b], axis=0)

    lhs = [cat(head(d, "kkt", h), head(d, "rt", h)) for d, h in chains]
    coef = [jnp.where(masks[d], _dot_nt(l, cat(head(d, "bt", h), head(d, "kt", h))), 0.0)
            for l, (d, h) in zip(lhs, chains)]
    n_mat = [a[:n, :n] for a in coef]
    state = [s_scr[d, h] for d, h in chains]
    state_b = [s.astype(BF16) for s in state]
    read = [_dot_nt(l, sb) for l, sb in zip(lhs, state_b)]
    akv = [_dot(a[:n, n:].astype(BF16), head(d, "vb", h)) for a, (d, h) in zip(coef, chains)]

    x = [eye - m for m in n_mat]
    p = [_dot(m.astype(BF16), m.astype(BF16)) for m in n_mat]
    steps = int(math.log2(n)) - 1
    for s in range(steps):
        pb = [q.astype(BF16) for q in p]
        x = [xi + _dot(xi.astype(BF16), q) for xi, q in zip(x, pb)]
        if s + 1 < steps:
            p = [_dot(q, q) for q in pb]

    u = [_dot(xi.astype(BF16), (-(rd[:n] + ak)).astype(BF16)) for xi, rd, ak in zip(x, read, akv)]
    uv = [cat(ui.astype(BF16), head(d, "vb", h)) for ui, (d, h) in zip(u, chains)]
    for (d, h), a, rd, uvi, s in zip(chains, coef, read, uv, state):
        sl = slice(h * RWKV_N, (h + 1) * RWKV_N)
        y_refs[d][:, sl] = rd[n:] + _dot(a[n:].astype(BF16), uvi)
        s_scr[d, h] = s * ops[d]["w_all"][:, sl] + _dot_tn(uvi, cat(head(d, "bh", h), head(d, "kh", h)))

    @pl.when(c == pl.num_programs(1) - 1)
    def _():
        sfin_ref[...] = s_scr[...]


def _rwkv_scan(r, v, kk, lw, km, b, states, layer, row0, nseq, t):
    n = CHUNK
    nc = t // n
    c0 = row0 // n
    c = RWKV_WIDTH
    fwd = lambda bb, cc: c0 + bb * nc + cc
    bwd = lambda bb, cc: c0 + bb * nc + nc - 1 - cc
    one = lambda chunk: pl.BlockSpec((n, c), lambda bb, cc: (chunk(bb, cc), 0))
    two = lambda chunk, d: pl.BlockSpec((None, n, c), lambda bb, cc: (d, chunk(bb, cc), 0))
    state = pl.BlockSpec((None, 2, RWKV_HEADS, RWKV_N, RWKV_N), lambda bb, cc: (bb, 0, 0, 0, 0))
    state_in = pl.BlockSpec((None, None, 2, RWKV_HEADS, RWKV_N, RWKV_N), lambda bb, cc: (bb, layer, 0, 0, 0, 0))
    y_shape = jax.ShapeDtypeStruct((nseq * t, c), F32)
    return pl.pallas_call(
        _rwkv_scan_body,
        grid=(nseq, nc),
        in_specs=[one(fwd), one(fwd), one(fwd), two(fwd, 0), two(fwd, 0), two(fwd, 0),
                  one(bwd), one(bwd), one(bwd), two(bwd, 1), two(bwd, 1), two(bwd, 1), state_in],
        out_specs=[pl.BlockSpec((n, c), lambda bb, cc: (bb * nc + cc, 0)),
                   pl.BlockSpec((n, c), lambda bb, cc: (bb * nc + nc - 1 - cc, 0)), state],
        out_shape=[y_shape, y_shape, jax.ShapeDtypeStruct((nseq, 2, RWKV_HEADS, RWKV_N, RWKV_N), F32)],
        scratch_shapes=[pltpu.VMEM((2, RWKV_HEADS, RWKV_N, RWKV_N), F32)],
        compiler_params=_params(("parallel", "arbitrary"), 32),
        name=f"rwkv_scan_t{t}",
    )(r, v, kk, lw, km, b, r, v, kk, lw, km, b, states)


def _outproj_body(f_ref, a_ref, yf_ref, yb_ref, bonus_ref, g_ref, lg_ref, lb_ref, x_ref, gate_ref, w_ref, o_ref,
                  mix_scr):
    a0, r0 = FOURIER_WIDTH, FOURIER_WIDTH + ATTN_WIDTH
    mix_scr[:, :a0] = f_ref[...]
    mix_scr[:, a0:r0] = a_ref[...]
    ones_bd = _head_sum_matrix()
    y = yf_ref[...] + yb_ref[...]
    mu = _head_sums(y, ones_bd) * (1.0 / RWKV_N)
    yc = y - mu
    var = _head_sums(yc * yc, ones_bd) * (1.0 / RWKV_N)
    yn = yc * lax.rsqrt(var + GN_EPS) * lg_ref[...] + lb_ref[...]
    mix_scr[:, r0:] = ((yn + bonus_ref[...]) * g_ref[...]).astype(BF16)
    o_ref[...] = x_ref[...] + gate_ref[...] * _dot(mix_scr[...], w_ref[...])


def _out_proj(f, a, yf, yb, bonus, g, ln_g, ln_b, x, gate, w, layer, seg):
    rows, d = x.shape
    tm = 512 if rows % 512 == 0 else 256
    row = lambda arr: pl.BlockSpec((tm, arr.shape[1]), lambda i: (i, 0))
    return pl.pallas_call(
        _outproj_body,
        grid=(rows // tm,),
        in_specs=[row(f), row(a), row(yf), row(yb), row(bonus), row(g),
                  _resident(ln_g.shape), _resident(ln_b.shape), row(x),
                  pl.BlockSpec((None, 1, d), _segment_index(seg, tm)),
                  _resident_layer(w.shape, layer)],
        out_specs=row(x),
        out_shape=jax.ShapeDtypeStruct(x.shape, F32),
        scratch_shapes=[pltpu.VMEM((tm, w.shape[1]), BF16)],
        compiler_params=_params(("parallel",), 48),
        name="out_proj",
    )(f, a, yf, yb, bonus, g, ln_g, ln_b, x, gate, w)


def _edge_masks(block_row, rows_lat, t_lat, t_ctx):
    in_lat = block_row < rows_lat
    length = jnp.where(in_lat, t_lat, t_ctx)
    off = jnp.where(in_lat, block_row, block_row - rows_lat)
    starts = (lax.rem(off, length) == 0).astype(F32)
    ends = (lax.rem(off + SEQ_ALIGN, length) == 0).astype(F32)
    sub = lax.broadcasted_iota(jnp.int32, (8, 1), 0)
    return 1.0 - starts * (sub == 0).astype(F32), 1.0 - ends * (sub == 7).astype(F32)


def _conv3_block(ext_ref, w_ref, b_ref, cols, r0, not_first8, not_last8):
    w0, w1, w2 = w_ref[0:1, cols], w_ref[1:2, cols], w_ref[2:3, cols]

    def rows(lo, hi, prev_mask=None, next_mask=None):
        at = lambda shift: ext_ref[HALO + r0 + shift + lo:HALO + r0 + shift + hi, :]
        prev, nxt = at(-1), at(1)
        if prev_mask is not None:
            prev = prev * prev_mask
        if next_mask is not None:
            nxt = nxt * next_mask
        return w0 * prev + w1 * at(0) + w2 * nxt

    n = SEQ_ALIGN
    out = jnp.concatenate([rows(0, 8, prev_mask=not_first8), rows(8, n - 8), rows(n - 8, n, next_mask=not_last8)],
                          axis=0)
    return out if b_ref is None else out + b_ref[:, cols]


def _ffn_body(x_ref, prev_ref, next_ref, g_ref, sc_ref, sh_ref, gate_ref, wa_ref, wg_ref,
              cwa_ref, cwg_ref, ba_ref, bg_ref, wd_ref, fg_ref, o_ref,
              h_scr, ua0_scr, ua1_scr, ug0_scr, ug1_scr, act_cur, act_new,
              *, tm, rows_lat, t_lat, t_ctx, final_norm):
    j = pl.program_id(1)
    nj = pl.num_programs(1) - 1

    @pl.when(j == 0)
    def _():
        norm = lambda x: _modulated_rmsnorm(x, g_ref[...], sc_ref[...], sh_ref[...]).astype(BF16)
        h_scr[0:HALO, :] = norm(prev_ref[...])
        h_scr[HALO:HALO + tm, :] = norm(x_ref[...])
        h_scr[HALO + tm:, :] = norm(next_ref[...])
        act_new[...] = jnp.zeros_like(act_new)
        o_ref[...] = jnp.zeros_like(o_ref)

    subs = [slice(s, s + FFN_SUB) for s in range(0, wa_ref.shape[1], FFN_SUB)]
    blocks = range(0, tm, SEQ_ALIGN)

    u_scr = ((ua0_scr, ug0_scr), (ua1_scr, ug1_scr))
    assert len(subs) == len(u_scr)

    def up_project(s):
        ua_scr, ug_scr = u_scr[s]
        ua_scr[...] = _dot(h_scr[...], wa_ref[:, subs[s]])
        ug_scr[...] = _dot(h_scr[...], wg_ref[:, subs[s]])

    def conv_gate(s):
        ua_scr, ug_scr = u_scr[s]
        for r0 in blocks:
            edge = _edge_masks(pl.program_id(0) * tm + r0, rows_lat, t_lat, t_ctx)
            ua = _conv3_block(ua_scr, cwa_ref, ba_ref, subs[s], r0, *edge)
            ug = _conv3_block(ug_scr, cwg_ref, bg_ref, subs[s], r0, *edge)
            act_new[r0:r0 + SEQ_ALIGN, subs[s]] = (ug * jax.nn.sigmoid(ug) * ua).astype(BF16)

    @pl.when(j < nj)
    def _():
        act_cur[...] = act_new[...]
        up_project(0)
        up_project(1)
        conv_gate(0)
        conv_gate(1)
        o_ref[...] += _dot(act_cur[...], wd_ref[...])

    @pl.when(j == nj)
    def _():
        out = x_ref[...] + gate_ref[...] * (o_ref[...] + _dot(act_new[...], wd_ref[...]))
        if final_norm:
            out = out * lax.rsqrt(jnp.mean(out * out, axis=-1, keepdims=True) + NORM_EPS) * fg_ref[...]
        o_ref[...] = out


def _conv_ffn(x, g, scale, shift, gate, wa, wg, cwa, cwg, ba, bg, wd, final_g, layer, seg, rows_lat, t_lat, t_ctx,
              final_norm):
    rows, d = x.shape
    tm = next(t for t in (1024, 512, 256) if rows % t == 0 and rows_lat % t == 0 and seg[1] % t == 0)
    tf = FFN_TILE
    nh = tm // HALO
    last = rows // HALO - 1
    nj = D_FF_PAD // tf
    seg = _segment_index(seg, tm)
    col = lambda r: pl.BlockSpec((r, tf), lambda i, j: (0, jnp.minimum(j, nj - 1)))
    up = pl.BlockSpec((None, None, d, tf), lambda i, j: (layer, jnp.minimum(j, nj - 1), 0, 0))
    return pl.pallas_call(
        functools.partial(_ffn_body, tm=tm, rows_lat=rows_lat, t_lat=t_lat, t_ctx=t_ctx, final_norm=final_norm),
        grid=(rows // tm, nj + 1),
        in_specs=[pl.BlockSpec((tm, d), lambda i, j: (i, 0), pipeline_mode=pl.Buffered(1)),
                  pl.BlockSpec((HALO, d), lambda i, j: (jnp.maximum(i * nh - 1, 0), 0)),
                  pl.BlockSpec((HALO, d), lambda i, j: (jnp.minimum((i + 1) * nh, last), 0)),
                  pl.BlockSpec((1, d), lambda i, j: (0, 0)),
                  pl.BlockSpec((None, 1, d), seg), pl.BlockSpec((None, 1, d), seg), pl.BlockSpec((None, 1, d), seg),
                  up, up, col(3), col(3), col(1), col(1),
                  pl.BlockSpec((None, tf, d), lambda i, j: (layer, jnp.maximum(j - 1, 0), 0)),
                  pl.BlockSpec((1, d), lambda i, j: (0, 0))],
        out_specs=pl.BlockSpec((tm, d), lambda i, j: (i, 0)),
        out_shape=jax.ShapeDtypeStruct(x.shape, F32),
        scratch_shapes=[pltpu.VMEM((tm + 2 * HALO, d), BF16),
                        pltpu.VMEM((tm + 2 * HALO, FFN_SUB), F32),
                        pltpu.VMEM((tm + 2 * HALO, FFN_SUB), F32),
                        pltpu.VMEM((tm + 2 * HALO, FFN_SUB), F32),
                        pltpu.VMEM((tm + 2 * HALO, FFN_SUB), F32),
                        pltpu.VMEM((tm, tf), BF16),
                        pltpu.VMEM((tm, tf), BF16)],
        compiler_params=_params(("parallel", "arbitrary"), 56),
        name="conv_ffn",
    )(x, x, x, g, scale, shift, gate, wa, wg, cwa, cwg, ba, bg, wd, final_g)


def _lora_weight(w2, a2, g2):
    c = RWKV_WIDTH
    wl = jnp.zeros((LORA_IN, 5 * c), F32)
    wl = wl.at[:DECAY_LORA, 0:c].set(w2[0]).at[:DECAY_LORA, c:2 * c].set(w2[1])
    wl = wl.at[DECAY_LORA:DECAY_LORA + ICLR_LORA, 2 * c:3 * c].set(a2[0])
    wl = wl.at[DECAY_LORA:DECAY_LORA + ICLR_LORA, 3 * c:4 * c].set(a2[1])
    return wl.at[DECAY_LORA + ICLR_LORA:, 4 * c:].set(g2)


LANES = 128


def _split_up_body(a_ref, g_ref, oa_ref, og_ref, *, nvalid):
    keep = pl.program_id(1) < nvalid
    oa_ref[...] = jnp.where(keep, a_ref[...], 0.0).astype(BF16)
    og_ref[...] = jnp.where(keep, g_ref[...], 0.0).astype(BF16)


def _split_ffn_up(ffn_up):
    depth, d, _ = ffn_up.shape
    nvalid = D_FF // LANES
    per_tile = FFN_TILE // LANES
    src = lambda half: pl.BlockSpec((None, d, LANES),
                                    lambda l, j: (l, 0, half * nvalid + jnp.minimum(j, nvalid - 1)))
    dst = pl.BlockSpec((None, None, d, LANES), lambda l, j: (l, j // per_tile, 0, j % per_tile))
    shape = jax.ShapeDtypeStruct((depth, D_FF_PAD // FFN_TILE, d, FFN_TILE), BF16)
    return pl.pallas_call(
        functools.partial(_split_up_body, nvalid=nvalid),
        grid=(depth, D_FF_PAD // LANES),
        in_specs=[src(0), src(1)], out_specs=[dst, dst], out_shape=[shape, shape],
        compiler_params=_params(("parallel", "parallel"), 32),
        name="split_ffn_up",
    )(ffn_up, ffn_up)


def _pad_down_body(w_ref, o_ref, *, tk):
    row = pl.program_id(1) * tk + lax.broadcasted_iota(jnp.int32, (tk, 1), 0)
    o_ref[...] = jnp.where(row < D_FF, w_ref[...], 0.0).astype(BF16)


def _pad_ffn_down(ffn_down):
    depth, _, d = ffn_down.shape
    tk = FFN_TILE
    return pl.pallas_call(
        functools.partial(_pad_down_body, tk=tk),
        grid=(depth, D_FF_PAD // tk),
        in_specs=[pl.BlockSpec((None, tk, d), lambda l, j: (l, j, 0))],
        out_specs=pl.BlockSpec((None, tk, d), lambda l, j: (l, j, 0)),
        out_shape=jax.ShapeDtypeStruct((depth, D_FF_PAD, d), BF16),
        compiler_params=_params(("parallel", "parallel"), 32),
        name="pad_ffn_down",
    )(ffn_down)


def _pad_cols(w):
    return jnp.pad(w, ((0, 0), (0, D_FF_PAD - D_FF)))


def _forward(x_lat, x_ctx, cache_k, cache_v, state, c, c_ctx, w_ada, b_ada, norm1_g, norm2_g, w_in, w_out,
             q_norm_g, k_norm_g, rw_conv, rw_w0, rw_w2, rw_a0, rw_a2, rw_g2, rw_kk, rw_ka, rw_rk,
             rw_lnx_g, rw_lnx_b, ffn_up, ffn_conv_w, ffn_conv_b, ffn_down, final_norm_g):
    n_lat, t_lat, d = x_lat.shape
    n_ctx, t_ctx, _ = x_ctx.shape
    depth = w_ada.shape[0]
    past = cache_k.shape[2]
    rows_lat, rows_ctx = n_lat * t_lat, n_ctx * t_ctx
    assert n_lat < MOD_ROWS
    assert t_lat % SEQ_ALIGN == 0 and t_ctx % SEQ_ALIGN == 0 and t_lat % GRID_W == 0

    cvec = jnp.concatenate([c, jnp.broadcast_to(c_ctx[None, :], (MOD_ROWS - n_lat, d))], axis=0)
    mod = _adaln_mod(cvec, w_ada, b_ada).reshape(depth, MOD_ROWS, 6, 1, d)

    ffn_wa, ffn_wg = _split_ffn_up(ffn_up)
    ffn_wd = _pad_ffn_down(ffn_down)
    rope_tabs = _rope_tables(t_lat)
    time_lat, chan_tab = _dft_tables(t_lat)
    time_ctx, _ = _dft_tables(t_ctx)
    zero_state = jnp.zeros((n_ctx, 1, 2, RWKV_HEADS, RWKV_N, RWKV_N), F32)
    row2 = lambda a: a.reshape(1, -1)

    cache = (cache_k.reshape(n_lat, depth, past, KV_WIDTH), cache_v.reshape(n_lat, depth, past, KV_WIDTH))

    def trunk_layer(x, l, latent):
        nseq, t = (n_lat, t_lat) if latent else (n_ctx, t_ctx)
        rows = nseq * t
        seg = (0, t_lat) if latent else (n_lat, rows)
        region_rows_lat = rows if latent else 0
        shift1, scale1, gate1, shift2, scale2, gate2 = (mod[l, :, i] for i in range(6))
        in_args = (x, row2(norm1_g[l]), scale1, shift1, w_in_b, l, seg, row2(q_norm_g[l]), row2(k_norm_g[l]))
        if latent:
            f, qb, kb, vb, rw = _in_proj(*in_args, rope_tabs)
            a_out = _attention(qb, kb, vb, nseq, t, cache[0], cache[1], l)
            k_norm = v = None
        else:
            f, qb, kb, vb, rw, k_norm, v = _in_proj(*in_args)
            a_out = _attention(qb, kb, vb, nseq, t)
        f_out = _fourier_mix(f, time_lat if latent else time_ctx, chan_tab, 0, nseq, t)

        r_, v_, kk, g_, bonus, lw, km, b_ = _rwkv_prep(
            rw, rw_conv[l], lora_w[l], rw_w0[l], rw_a0[l],
            row2(rw_kk[l]), row2(rw_ka[l]), row2(rw_rk[l]), region_rows_lat, t_lat, t_ctx)
        yf, yb, s_fin = _rwkv_scan(r_, v_, kk, lw, km, b_, state if latent else zero_state, l if latent else 0,
                                   0, nseq, t)
        x = _out_proj(f_out, a_out, yf, yb, bonus, g_, row2(rw_lnx_g[l]), row2(rw_lnx_b[l]), x, gate1,
                      w_out_b, l, seg)
        x = _conv_ffn(x, row2(norm2_g[l]), scale2, shift2, gate2, ffn_wa, ffn_wg,
                      _pad_cols(ffn_conv_w[l][:, :D_FF]), _pad_cols(ffn_conv_w[l][:, D_FF:]),
                      _pad_cols(row2(ffn_conv_b[l][:D_FF])), _pad_cols(row2(ffn_conv_b[l][D_FF:])),
                      ffn_wd, row2(final_norm_g), l, seg, region_rows_lat, t_lat, t_ctx, final_norm=l == depth - 1)
        return x, k_norm, v, s_fin

    w_in_b, w_out_b = w_in.astype(BF16), w_out.astype(BF16)
    lora_w = [_lora_weight(rw_w2[l], rw_a2[l], rw_g2[l]).astype(BF16) for l in range(depth)]
    xs, xc = x_lat.reshape(rows_lat, d), x_ctx.reshape(rows_ctx, d)
    new_k, new_v, new_s = [], [], []
    for l in range(depth):
        xc, k_norm, v_ctx, s_ctx = trunk_layer(xc, l, latent=False)
        new_k.append(k_norm.reshape(n_ctx, t_ctx, ATTN_KV_HEADS, HEAD_DIM))
        new_v.append(v_ctx.reshape(n_ctx, t_ctx, ATTN_KV_HEADS, HEAD_DIM))
        new_s.append(s_ctx)
        xs, _, _, _ = trunk_layer(xs, l, latent=True)

    y_lat, y_ctx = xs.reshape(n_lat, t_lat, d), xc.reshape(n_ctx, t_ctx, d)
    return (y_ctx, y_lat, jnp.stack(new_k, axis=1), jnp.stack(new_v, axis=1), jnp.stack(new_s, axis=1))


def kernel(x_prompt, x_sample, cache_attn_k, cache_attn_v, state_rwkv, c, c_ctx, w_ada, b_ada, norm1_g, norm2_g, w_in, w_out, q_norm_g, k_norm_g, rw_conv, rw_w0, rw_w2, rw_a0, rw_a2, rw_g2, rw_kk, rw_ka, rw_rk, rw_lnx_g, rw_lnx_b, ffn_up, ffn_conv_w, ffn_conv_b, ffn_down, final_norm_g):
    return _forward(x_sample, x_prompt, cache_attn_k, cache_attn_v, state_rwkv, c, c_ctx, w_ada, b_ada,
                    norm1_g, norm2_g, w_in, w_out, q_norm_g, k_norm_g, rw_conv, rw_w0, rw_w2, rw_a0, rw_a2,
                    rw_g2, rw_kk, rw_ka, rw_rk, rw_lnx_g, rw_lnx_b, ffn_up, ffn_conv_w, ffn_conv_b, ffn_down,
                    final_norm_g)
```

```python
import functools
import math

import jax
import jax.numpy as jnp
import numpy as np
from jax import lax
from jax.experimental import pallas as pl
from jax.experimental.pallas import tpu as pltpu

D_MODEL = 2048
GRID_W = 64
HEAD_DIM = 128
ATTN_HEADS = 8
ATTN_KV_HEADS = 2
KV_GROUP = ATTN_HEADS // ATTN_KV_HEADS
ATTN_WIDTH = ATTN_HEADS * HEAD_DIM
KV_WIDTH = ATTN_KV_HEADS * HEAD_DIM
FOURIER_WIDTH = 512
FOURIER_GROUPS = 4
FOURIER_GROUP_WIDTH = FOURIER_WIDTH // FOURIER_GROUPS
RWKV_WIDTH = 512
RWKV_N = 64
RWKV_HEADS = RWKV_WIDTH // RWKV_N
DECAY_LORA = 64
ICLR_LORA = 64
GATE_LORA = 128
LORA_IN = DECAY_LORA + ICLR_LORA + GATE_LORA
RWKV_IN = 3 * RWKV_WIDTH + LORA_IN
IN_WIDTH = FOURIER_WIDTH + ATTN_WIDTH + 2 * KV_WIDTH + RWKV_IN
D_FF = 5504
ROPE_THETA = 10000.0
NORM_EPS = 1e-6
GN_EPS = 64e-5

MOD_ROWS = 16
FFN_TILE = 512
FFN_SUB = 256
D_FF_PAD = -(-D_FF // FFN_TILE) * FFN_TILE
HALO = 16
CHUNK = 64
SCAN_SUB = 2
SEQ_ALIGN = 256
MIB = 2 ** 20

F32 = jnp.float32
BF16 = jnp.bfloat16


def _params(semantics, vmem_mib):
    return pltpu.CompilerParams(dimension_semantics=semantics, vmem_limit_bytes=vmem_mib * MIB)


def _resident(shape):
    return pl.BlockSpec(shape, lambda *_: (0,) * len(shape), pipeline_mode=pl.Buffered(1))


def _resident_layer(stacked_shape, layer):
    shape = stacked_shape[1:]
    return pl.BlockSpec((None,) + tuple(shape), lambda *_: (layer,) + (0,) * len(shape),
                        pipeline_mode=pl.Buffered(1))


def _dot(a, b):
    return jnp.dot(a, b, preferred_element_type=F32)


def _dot_nt(a, b):
    return lax.dot_general(a, b, (((1,), (1,)), ((), ())), preferred_element_type=F32)


def _dot_tn(a, b):
    return lax.dot_general(a, b, (((0,), (0,)), ((), ())), preferred_element_type=F32)


def _modulated_rmsnorm(x, g, scale, shift):
    ms = jnp.mean(x * x, axis=-1, keepdims=True)
    return (x * lax.rsqrt(ms + NORM_EPS) * g) * (1.0 + scale) + shift


def _mod_body(c_ref, w_ref, b_ref, o_ref):
    c = c_ref[...]
    s = (c * jax.nn.sigmoid(c)).astype(BF16)
    o_ref[...] = _dot(s, w_ref[...].astype(BF16)) + b_ref[...]


def _adaln_mod(cvec, w_ada, b_ada):
    depth, d, n = w_ada.shape
    tn = 1024
    return pl.pallas_call(
        _mod_body,
        grid=(depth, n // tn),
        in_specs=[pl.BlockSpec((MOD_ROWS, d), lambda l, j: (0, 0)),
                  pl.BlockSpec((None, d, tn), lambda l, j: (l, 0, j)),
                  pl.BlockSpec((None, 1, tn), lambda l, j: (l, 0, j))],
        out_specs=pl.BlockSpec((None, MOD_ROWS, tn), lambda l, j: (l, 0, j)),
        out_shape=jax.ShapeDtypeStruct((depth, MOD_ROWS, n), F32),
        compiler_params=_params(("parallel", "parallel"), 40),
        name="adaln_mod",
    )(cvec, w_ada, b_ada.reshape(depth, 1, n))


Q_COL = FOURIER_WIDTH
K_COL = Q_COL + ATTN_WIDTH
V_COL = K_COL + KV_WIDTH
RW_COL = V_COL + KV_WIDTH


def _inproj_body(*refs, rope):
    x_ref, g_ref, sc_ref, sh_ref, w_ref, gq_ref, gk_ref = refs[:7]
    if rope:
        cos_ref, sin_ref, f_ref, q_ref, k_ref, v_ref, rw_ref = refs[7:]
        cos, sin = cos_ref[...], sin_ref[...]
        lane = lax.broadcasted_iota(jnp.int32, (1, HEAD_DIM), 1)
        low = (lane % (HEAD_DIM // 2)) < (HEAD_DIM // 4)
    else:
        f_ref, q_ref, k_ref, v_ref, rw_ref, kn_ref, vf_ref = refs[7:]

    h = _modulated_rmsnorm(x_ref[...], g_ref[...], sc_ref[...], sh_ref[...]).astype(BF16)
    proj = lambda start, width: _dot(h, w_ref[:, start:start + width])
    q, k, v = proj(Q_COL, ATTN_WIDTH), proj(K_COL, KV_WIDTH), proj(V_COL, KV_WIDTH)
    f_ref[...] = proj(0, FOURIER_WIDTH).astype(BF16)
    rw_ref[...] = proj(RW_COL, RWKV_IN)

    def head_norm(xh, g):
        ms = jnp.mean(xh * xh, axis=-1, keepdims=True)
        return xh * lax.rsqrt(ms + NORM_EPS) * g

    def rotate(xh):
        partner = jnp.where(low, pltpu.roll(xh, HEAD_DIM - HEAD_DIM // 4, 1), pltpu.roll(xh, HEAD_DIM // 4, 1))
        return xh * cos + partner * sin

    scale = HEAD_DIM ** -0.5
    for hd in range(ATTN_HEADS):
        sl = slice(hd * HEAD_DIM, (hd + 1) * HEAD_DIM)
        qh = head_norm(q[:, sl], gq_ref[...])
        if rope:
            qh = rotate(qh)
        q_ref[:, sl] = (qh * scale).astype(BF16)
    for j in range(ATTN_KV_HEADS):
        sl = slice(j * HEAD_DIM, (j + 1) * HEAD_DIM)
        kh = head_norm(k[:, sl], gk_ref[...])
        if rope:
            kh = rotate(kh)
        else:
            kn_ref[:, sl] = kh
        k_ref[:, sl] = kh.astype(BF16)
    v_ref[...] = v.astype(BF16)
    if not rope:
        vf_ref[...] = v


def _segment_index(seg, tm):
    seg0, seg_rows = seg
    assert seg_rows % tm == 0
    return lambda i, *_: (seg0 + i * tm // seg_rows, 0, 0)


def _in_proj(x, g, scale, shift, w, layer, seg, gq, gk, rope_tabs=None):
    rows, d = x.shape
    tm = 512 if rows % 512 == 0 else 256
    seg = _segment_index(seg, tm)
    rope = rope_tabs is not None
    in_specs = [pl.BlockSpec((tm, d), lambda i: (i, 0)),
                _resident((1, d)),
                pl.BlockSpec((None, 1, d), seg),
                pl.BlockSpec((None, 1, d), seg),
                _resident_layer(w.shape, layer),
                _resident((1, HEAD_DIM)), _resident((1, HEAD_DIM))]
    args = [x, g, scale, shift, w, gq, gk]
    outs = [(FOURIER_WIDTH, BF16), (ATTN_WIDTH, BF16), (KV_WIDTH, BF16), (KV_WIDTH, BF16), (RWKV_IN, F32)]
    if rope:
        nt = rope_tabs[0].shape[0] // tm
        in_specs += [pl.BlockSpec((tm, HEAD_DIM), lambda i: (i % nt, 0))] * 2
        args += list(rope_tabs)
    else:
        outs += [(KV_WIDTH, F32), (KV_WIDTH, F32)]
    return pl.pallas_call(
        functools.partial(_inproj_body, rope=rope),
        grid=(rows // tm,),
        in_specs=in_specs,
        out_specs=[pl.BlockSpec((tm, width), lambda i: (i, 0)) for width, _ in outs],
        out_shape=[jax.ShapeDtypeStruct((rows, width), dt) for width, dt in outs],
        compiler_params=_params(("parallel",), 56),
        name="in_proj_rope" if rope else "in_proj",
    )(*args)


def _dft_tables(t):
    def angles(n):
        i = np.arange(n, dtype=np.int64)
        return (2.0 * math.pi / n) * ((i[:, None] * i[None, :]) % n)
    at = angles(t)
    time_tab = np.concatenate([np.cos(at), -np.sin(at)], axis=1).astype(np.float32)
    ac = angles(FOURIER_GROUP_WIDTH)
    eye = np.eye(FOURIER_GROUPS)
    chan_tab = np.concatenate([np.kron(eye, np.cos(ac)), np.kron(eye, np.sin(ac))], axis=1).astype(np.float32)
    return jnp.asarray(time_tab).astype(BF16), jnp.asarray(chan_tab).astype(BF16)


def _fourier_body(u_ref, ct_ref, cc_ref, o_ref, ab_scr, *, t, norm):
    @pl.when(pl.program_id(1) == 0)
    def _():
        ab = _dot(u_ref[...], cc_ref[...])
        ab_scr[0:t, :] = ab[:, :FOURIER_WIDTH].astype(BF16)
        ab_scr[t:2 * t, :] = ab[:, FOURIER_WIDTH:].astype(BF16)
    o_ref[...] = (_dot(ct_ref[...], ab_scr[...]) * norm).astype(o_ref.dtype)


def _fourier_mix(u, time_tab, chan_tab, row0, nseq, t):
    tm = min(t, 512)
    nt = t // tm
    seq0 = row0 // t
    return pl.pallas_call(
        functools.partial(_fourier_body, t=t, norm=1.0 / math.sqrt(t * FOURIER_GROUP_WIDTH)),
        grid=(nseq, nt),
        in_specs=[pl.BlockSpec((t, FOURIER_WIDTH), lambda b, i: (seq0 + b, 0)),
                  pl.BlockSpec((tm, 2 * t), lambda b, i: (i, 0)),
                  _resident(chan_tab.shape)],
        out_specs=pl.BlockSpec((tm, FOURIER_WIDTH), lambda b, i: (b * nt + i, 0)),
        out_shape=jax.ShapeDtypeStruct((nseq * t, FOURIER_WIDTH), BF16),
        scratch_shapes=[pltpu.VMEM((2 * t, FOURIER_WIDTH), BF16)],
        compiler_params=_params(("parallel", "arbitrary"), 40),
        name=f"fourier_mix_t{t}",
    )(u, time_tab, chan_tab)


def _rope_tables(t):
    pos = jnp.arange(t, dtype=jnp.int32)
    rows = (pos // GRID_W).astype(F32)
    cols = (pos % GRID_W).astype(F32)
    half = HEAD_DIM // 2
    inv = 1.0 / (ROPE_THETA ** (jnp.arange(0, half, 2, dtype=F32) / half))
    def tab(p):
        ang = p[:, None] * inv[None, :]
        return (jnp.concatenate([jnp.cos(ang), jnp.cos(ang)], -1),
                jnp.concatenate([-jnp.sin(ang), jnp.sin(ang)], -1))
    cr, sr = tab(rows)
    cc, sc = tab(cols)
    return jnp.concatenate([cr, cc], -1), jnp.concatenate([sr, sc], -1)


def _attn_body(*refs, cached):
    if cached:
        q_ref, k_ref, v_ref, kc_ref, vc_ref, o_ref = refs
    else:
        q_ref, k_ref, v_ref, o_ref = refs
    for j in range(ATTN_KV_HEADS):
        kv = slice(j * HEAD_DIM, (j + 1) * HEAD_DIM)
        kj, vj = k_ref[:, kv], v_ref[:, kv]
        if cached:
            kc, vc = kc_ref[:, kv].astype(BF16), vc_ref[:, kv].astype(BF16)
        for g in range(KV_GROUP):
            sl = slice((j * KV_GROUP + g) * HEAD_DIM, (j * KV_GROUP + g + 1) * HEAD_DIM)
            qh = q_ref[:, sl]
            s = _dot_nt(qh, kj)
            m = jnp.max(s, axis=-1, keepdims=True)
            if cached:
                sc = _dot_nt(qh, kc)
                m = jnp.maximum(m, jnp.max(sc, axis=-1, keepdims=True))
            p = jnp.exp(s - m)
            l = jnp.sum(p, axis=-1, keepdims=True)
            acc = _dot(p.astype(BF16), vj)
            if cached:
                pc = jnp.exp(sc - m)
                l = l + jnp.sum(pc, axis=-1, keepdims=True)
                acc = acc + _dot(pc.astype(BF16), vc)
            o_ref[:, sl] = (acc / l).astype(o_ref.dtype)


def _attention(q, k, v, nseq, t, cache_k=None, cache_v=None, layer=0):
    tq = 512 if t % 512 == 0 else 256
    nq = t // tq
    cached = cache_k is not None
    in_specs = [pl.BlockSpec((tq, ATTN_WIDTH), lambda b, i: (b * nq + i, 0)),
                pl.BlockSpec((t, KV_WIDTH), lambda b, i: (b, 0)),
                pl.BlockSpec((t, KV_WIDTH), lambda b, i: (b, 0))]
    args = [q, k, v]
    if cached:
        past = cache_k.shape[2]
        in_specs += [pl.BlockSpec((None, None, past, KV_WIDTH), lambda b, i: (b, layer, 0, 0))] * 2
        args += [cache_k, cache_v]
    return pl.pallas_call(
        functools.partial(_attn_body, cached=cached),
        grid=(nseq, nq),
        in_specs=in_specs,
        out_specs=pl.BlockSpec((tq, ATTN_WIDTH), lambda b, i: (b * nq + i, 0)),
        out_shape=jax.ShapeDtypeStruct((nseq * t, ATTN_WIDTH), BF16),
        compiler_params=_params(("parallel", "parallel"), 48),
        name="attention_cached" if cached else "attention",
    )(*args)


def _head_sum_matrix():
    i = lax.broadcasted_iota(jnp.int32, (RWKV_WIDTH, RWKV_WIDTH), 0) // RWKV_N
    j = lax.broadcasted_iota(jnp.int32, (RWKV_WIDTH, RWKV_WIDTH), 1) // RWKV_N
    return (i == j).astype(BF16)


def _bf16_pieces(x, n):
    pieces = []
    for _ in range(n - 1):
        p = x.astype(BF16)
        pieces.append(p)
        x = x - p.astype(F32)
    return pieces + [x.astype(BF16)]


def _head_sums(x, ones_bd):
    hi, lo = _bf16_pieces(x, 2)
    return _dot(hi, ones_bd) + _dot(lo, ones_bd)


def _rwkv_prep_body(rw_ref, prev_ref, next_ref, cw_ref, wl_ref, w0_ref, a0_ref, kks_ref, ka_ref, rk_ref,
                    r_ref, v_ref, kk_ref, g_ref, bonus_ref, lw_ref, km_ref, b_ref, ext_scr,
                    *, tm, rows_lat, t_lat, t_ctx):
    ext_scr[0:HALO, :] = prev_ref[...]
    ext_scr[HALO:HALO + tm, :] = rw_ref[...]
    ext_scr[HALO + tm:, :] = next_ref[...]
    edges = _edge_masks(pl.program_id(0) * tm, rows_lat, t_lat, t_ctx)
    z = _conv3_block(ext_scr, cw_ref, None, slice(None), 0, *edges)
    c = RWKV_WIDTH
    r, k, v = z[:, :c], z[:, c:2 * c], z[:, 2 * c:3 * c]
    zl = z[:, 3 * c:]
    lane = lax.broadcasted_iota(jnp.int32, (1, LORA_IN), 1)
    lora_in = jnp.where(lane < DECAY_LORA, jnp.tanh(zl),
                        jnp.where(lane < DECAY_LORA + ICLR_LORA, zl, jax.nn.sigmoid(zl)))
    lora = _dot(lora_in.astype(BF16), wl_ref[...])
    ones_bd = _head_sum_matrix()
    kk = k * kks_ref[...]
    kk = kk * lax.rsqrt(_head_sums(kk * kk, ones_bd) + 1e-12)
    r_ref[...] = r.astype(r_ref.dtype)
    v_ref[...] = v.astype(v_ref.dtype)
    kk_ref[...] = kk.astype(kk_ref.dtype)
    g_ref[...] = lora[:, 4 * c:5 * c]
    kmod_sum = jnp.zeros_like(k)
    for d in range(2):
        wpre = w0_ref[d:d + 1, :] + lora[:, d * c:(d + 1) * c]
        lw_ref[d] = -math.exp(-0.5) * jax.nn.sigmoid(wpre)
        a = jax.nn.sigmoid(a0_ref[d:d + 1, :] + lora[:, (2 + d) * c:(3 + d) * c])
        kmod = k * (1.0 + (a - 1.0) * ka_ref[...])
        km_ref[d] = kmod.astype(km_ref.dtype)
        b_ref[d] = (kk * a).astype(b_ref.dtype)
        kmod_sum = kmod_sum + kmod
    bonus_ref[...] = _head_sums(r * kmod_sum * rk_ref[...], ones_bd) * v


def _rwkv_prep(rw, conv_w, lora_w, w0, a0, kk_scale, ka, rk, rows_lat, t_lat, t_ctx):
    rows = rw.shape[0]
    tm = 256
    nh = tm // HALO
    last = rows // HALO - 1
    c = RWKV_WIDTH
    one = lambda: pl.BlockSpec((tm, c), lambda i: (i, 0))
    two = lambda: pl.BlockSpec((2, tm, c), lambda i: (0, i, 0))
    return pl.pallas_call(
        functools.partial(_rwkv_prep_body, tm=tm, rows_lat=rows_lat, t_lat=t_lat, t_ctx=t_ctx),
        grid=(rows // tm,),
        in_specs=[pl.BlockSpec((tm, RWKV_IN), lambda i: (i, 0)),
                  pl.BlockSpec((HALO, RWKV_IN), lambda i: (jnp.maximum(i * nh - 1, 0), 0)),
                  pl.BlockSpec((HALO, RWKV_IN), lambda i: (jnp.minimum((i + 1) * nh, last), 0)),
                  _resident(conv_w.shape), _resident(lora_w.shape), _resident(w0.shape), _resident(a0.shape),
                  _resident(kk_scale.shape), _resident(ka.shape), _resident(rk.shape)],
        out_specs=[one(), one(), one(), one(), one(), two(), two(), two()],
        out_shape=[jax.ShapeDtypeStruct((rows, c), dt) for dt in (BF16, BF16, BF16, F32, F32)]
        + [jax.ShapeDtypeStruct((2, rows, c), dt) for dt in (F32, BF16, BF16)],
        scratch_shapes=[pltpu.VMEM((tm + 2 * HALO, RWKV_IN), F32)],
        compiler_params=_params(("parallel",), 48),
        name="rwkv_prep",
    )(rw, rw, rw, conv_w, lora_w, w0, a0, kk_scale, ka, rk)


def _scan_direction_operands(r_ref, v_ref, kk_ref, lw_ref, km_ref, b_ref, rows, backward):
    n = CHUNK
    row = lax.broadcasted_iota(jnp.int32, (n, n), 0)
    col = lax.broadcasted_iota(jnp.int32, (n, n), 1)
    upto = (col >= row) if backward else (col <= row)
    lw = lw_ref[rows, :]
    tri = upto.astype(BF16)
    cs = sum(_dot(tri, piece) for piece in _bf16_pieces(lw, 3))
    tot = jnp.sum(lw, axis=0, keepdims=True)
    grow = jnp.exp(-cs)
    to_end = jnp.exp(tot - cs)
    b = b_ref[rows, :]
    km = km_ref[rows, :]
    return dict(
        kkt=(kk_ref[rows, :] * jnp.exp(cs - lw)).astype(BF16),
        rt=(r_ref[rows, :] * jnp.exp(cs)).astype(BF16),
        bt=(b * grow).astype(BF16), kt=(km * grow).astype(BF16),
        bh=(b * to_end).astype(BF16), kh=(km * to_end).astype(BF16),
        vb=v_ref[rows, :].astype(BF16), w_all=jnp.exp(tot))


def _rwkv_scan_body(rf_ref, vf_ref, kkf_ref, lwf_ref, kmf_ref, bf_ref,
                    rb_ref, vb_ref, kkb_ref, lwb_ref, kmb_ref, bb_ref, s0_ref,
                    yf_ref, yb_ref, sfin_ref, s_scr):
    c = pl.program_id(1)

    @pl.when(c == 0)
    def _():
        s_scr[...] = s0_ref[...]

    n = CHUNK
    row = lax.broadcasted_iota(jnp.int32, (2 * n, 2 * n), 0)
    col = lax.broadcasted_iota(jnp.int32, (2 * n, 2 * n), 1)
    t_idx, s_idx, read_rows = row % n, col % n, row >= n
    eye = (lax.broadcasted_iota(jnp.int32, (n, n), 0) == lax.broadcasted_iota(jnp.int32, (n, n), 1)).astype(F32)
    same_step = jnp.logical_and(read_rows, s_idx == t_idx)
    masks = (jnp.logical_or(s_idx < t_idx, same_step), jnp.logical_or(s_idx > t_idx, same_step))
    y_refs = (yf_ref, yb_ref)
    in_refs = ((rf_ref, vf_ref, kkf_ref, lwf_ref, kmf_ref, bf_ref), (rb_ref, vb_ref, kkb_ref, lwb_ref, kmb_ref, bb_ref))
    rows = [(slice(k * n, (k + 1) * n), slice((SCAN_SUB - 1 - k) * n, (SCAN_SUB - k) * n)) for k in range(SCAN_SUB)]
    ops = [[_scan_direction_operands(*in_refs[d], rows[k][d], backward=d == 1) for d in range(2)]
           for k in range(SCAN_SUB)]
    pairs = [(d, h) for d in range(2) for h in range(RWKV_HEADS)]
    chains = [(k, d, h) for k in range(SCAN_SUB) for d, h in pairs]
    head = lambda k, d, name, h: ops[k][d][name][:, h * RWKV_N:(h + 1) * RWKV_N]
    cat = lambda a, b: jnp.concatenate([a, b], axis=0)

    lhs = [cat(head(k, d, "kkt", h), head(k, d, "rt", h)) for k, d, h in chains]
    coef = [jnp.where(masks[d], _dot_nt(l, cat(head(k, d, "bt", h), head(k, d, "kt", h))), 0.0)
            for l, (k, d, h) in zip(lhs, chains)]
    n_mat = [a[:n, :n] for a in coef]
    akv = [_dot(a[:n, n:].astype(BF16), head(k, d, "vb", h)) for a, (k, d, h) in zip(coef, chains)]
    x = [eye - m for m in n_mat]
    p = [_dot(m.astype(BF16), m.astype(BF16)) for m in n_mat]
    steps = int(math.log2(n)) - 1
    for s in range(steps):
        pb = [q.astype(BF16) for q in p]
        x = [xi + _dot(xi.astype(BF16), q) for xi, q in zip(x, pb)]
        if s + 1 < steps:
            p = [_dot(q, q) for q in pb]

    for k in range(SCAN_SUB):
        mine = slice(k * len(pairs), (k + 1) * len(pairs))
        state = [s_scr[d, h] for d, h in pairs]
        read = [_dot_nt(l, s.astype(BF16)) for l, s in zip(lhs[mine], state)]
        u = [_dot(xi.astype(BF16), (-(rd[:n] + ak)).astype(BF16)) for xi, rd, ak in zip(x[mine], read, akv[mine])]
        uv = [cat(ui.astype(BF16), head(k, d, "vb", h)) for ui, (d, h) in zip(u, pairs)]
        for (d, h), a, rd, uvi, s in zip(pairs, coef[mine], read, uv, state):
            sl = slice(h * RWKV_N, (h + 1) * RWKV_N)
            y_refs[d][rows[k][d], sl] = rd[n:] + _dot(a[n:].astype(BF16), uvi)
            s_scr[d, h] = (s * ops[k][d]["w_all"][:, sl]
                           + _dot_tn(uvi, cat(head(k, d, "bh", h), head(k, d, "kh", h))))

    @pl.when(c == pl.num_programs(1) - 1)
    def _():
        sfin_ref[...] = s_scr[...]


def _rwkv_scan(r, v, kk, lw, km, b, states, layer, row0, nseq, t):
    n = CHUNK * SCAN_SUB
    assert t % n == 0 and row0 % n == 0
    nc = t // n
    c0 = row0 // n
    c = RWKV_WIDTH
    fwd = lambda bb, cc: c0 + bb * nc + cc
    bwd = lambda bb, cc: c0 + bb * nc + nc - 1 - cc
    one = lambda chunk: pl.BlockSpec((n, c), lambda bb, cc: (chunk(bb, cc), 0))
    two = lambda chunk, d: pl.BlockSpec((None, n, c), lambda bb, cc: (d, chunk(bb, cc), 0))
    state = pl.BlockSpec((None, 2, RWKV_HEADS, RWKV_N, RWKV_N), lambda bb, cc: (bb, 0, 0, 0, 0))
    state_in = pl.BlockSpec((None, None, 2, RWKV_HEADS, RWKV_N, RWKV_N), lambda bb, cc: (bb, layer, 0, 0, 0, 0))
    y_shape = jax.ShapeDtypeStruct((nseq * t, c), F32)
    return pl.pallas_call(
        _rwkv_scan_body,
        grid=(nseq, nc),
        in_specs=[one(fwd), one(fwd), one(fwd), two(fwd, 0), two(fwd, 0), two(fwd, 0),
                  one(bwd), one(bwd), one(bwd), two(bwd, 1), two(bwd, 1), two(bwd, 1), state_in],
        out_specs=[pl.BlockSpec((n, c), lambda bb, cc: (bb * nc + cc, 0)),
                   pl.BlockSpec((n, c), lambda bb, cc: (bb * nc + nc - 1 - cc, 0)), state],
        out_shape=[y_shape, y_shape, jax.ShapeDtypeStruct((nseq, 2, RWKV_HEADS, RWKV_N, RWKV_N), F32)],
        scratch_shapes=[pltpu.VMEM((2, RWKV_HEADS, RWKV_N, RWKV_N), F32)],
        compiler_params=_params(("parallel", "arbitrary"), 32),
        name=f"rwkv_scan_t{t}",
    )(r, v, kk, lw, km, b, r, v, kk, lw, km, b, states)


def _outproj_body(f_ref, a_ref, yf_ref, yb_ref, bonus_ref, g_ref, lg_ref, lb_ref, x_ref, gate_ref, w_ref, o_ref,
                  mix_scr):
    a0, r0 = FOURIER_WIDTH, FOURIER_WIDTH + ATTN_WIDTH
    mix_scr[:, :a0] = f_ref[...]
    mix_scr[:, a0:r0] = a_ref[...]
    ones_bd = _head_sum_matrix()
    y = yf_ref[...] + yb_ref[...]
    mu = _head_sums(y, ones_bd) * (1.0 / RWKV_N)
    yc = y - mu
    var = _head_sums(yc * yc, ones_bd) * (1.0 / RWKV_N)
    yn = yc * lax.rsqrt(var + GN_EPS) * lg_ref[...] + lb_ref[...]
    mix_scr[:, r0:] = ((yn + bonus_ref[...]) * g_ref[...]).astype(BF16)
    o_ref[...] = x_ref[...] + gate_ref[...] * _dot(mix_scr[...], w_ref[...])


def _out_proj(f, a, yf, yb, bonus, g, ln_g, ln_b, x, gate, w, layer, seg):
    rows, d = x.shape
    tm = 512 if rows % 512 == 0 else 256
    row = lambda arr: pl.BlockSpec((tm, arr.shape[1]), lambda i: (i, 0))
    return pl.pallas_call(
        _outproj_body,
        grid=(rows // tm,),
        in_specs=[row(f), row(a), row(yf), row(yb), row(bonus), row(g),
                  _resident(ln_g.shape), _resident(ln_b.shape), row(x),
                  pl.BlockSpec((None, 1, d), _segment_index(seg, tm)),
                  _resident_layer(w.shape, layer)],
        out_specs=row(x),
        out_shape=jax.ShapeDtypeStruct(x.shape, F32),
        scratch_shapes=[pltpu.VMEM((tm, w.shape[1]), BF16)],
        compiler_params=_params(("parallel",), 48),
        name="out_proj",
    )(f, a, yf, yb, bonus, g, ln_g, ln_b, x, gate, w)


def _edge_masks(block_row, rows_lat, t_lat, t_ctx):
    in_lat = block_row < rows_lat
    length = jnp.where(in_lat, t_lat, t_ctx)
    off = jnp.where(in_lat, block_row, block_row - rows_lat)
    starts = (lax.rem(off, length) == 0).astype(F32)
    ends = (lax.rem(off + SEQ_ALIGN, length) == 0).astype(F32)
    sub = lax.broadcasted_iota(jnp.int32, (8, 1), 0)
    return 1.0 - starts * (sub == 0).astype(F32), 1.0 - ends * (sub == 7).astype(F32)


def _conv3_block(ext_ref, w_ref, b_ref, cols, r0, not_first8, not_last8):
    w0, w1, w2 = w_ref[0:1, cols], w_ref[1:2, cols], w_ref[2:3, cols]

    def rows(lo, hi, prev_mask=None, next_mask=None):
        at = lambda shift: ext_ref[HALO + r0 + shift + lo:HALO + r0 + shift + hi, :]
        prev, nxt = at(-1), at(1)
        if prev_mask is not None:
            prev = prev * prev_mask
        if next_mask is not None:
            nxt = nxt * next_mask
        return w0 * prev + w1 * at(0) + w2 * nxt

    n = SEQ_ALIGN
    out = jnp.concatenate([rows(0, 8, prev_mask=not_first8), rows(8, n - 8), rows(n - 8, n, next_mask=not_last8)],
                          axis=0)
    return out if b_ref is None else out + b_ref[:, cols]


def _ffn_body(x_ref, prev_ref, next_ref, g_ref, sc_ref, sh_ref, gate_ref, wa_ref, wg_ref,
              cwa_ref, cwg_ref, ba_ref, bg_ref, wd_ref, fg_ref, o_ref,
              h_scr, ua0_scr, ua1_scr, ug0_scr, ug1_scr, act_cur, act_new,
              *, tm, rows_lat, t_lat, t_ctx, final_norm):
    j = pl.program_id(1)
    nj = pl.num_programs(1) - 1

    @pl.when(j == 0)
    def _():
        norm = lambda x: _modulated_rmsnorm(x, g_ref[...], sc_ref[...], sh_ref[...]).astype(BF16)
        h_scr[0:HALO, :] = norm(prev_ref[...])
        h_scr[HALO:HALO + tm, :] = norm(x_ref[...])
        h_scr[HALO + tm:, :] = norm(next_ref[...])
        act_new[...] = jnp.zeros_like(act_new)
        o_ref[...] = jnp.zeros_like(o_ref)

    subs = [slice(s, s + FFN_SUB) for s in range(0, wa_ref.shape[1], FFN_SUB)]
    blocks = range(0, tm, SEQ_ALIGN)

    u_scr = ((ua0_scr, ug0_scr), (ua1_scr, ug1_scr))
    assert len(subs) == len(u_scr)

    def up_project(s):
        ua_scr, ug_scr = u_scr[s]
        ua_scr[...] = _dot(h_scr[...], wa_ref[:, subs[s]])
        ug_scr[...] = _dot(h_scr[...], wg_ref[:, subs[s]])

    def conv_gate(s):
        ua_scr, ug_scr = u_scr[s]
        for r0 in blocks:
            edge = _edge_masks(pl.program_id(0) * tm + r0, rows_lat, t_lat, t_ctx)
            ua = _conv3_block(ua_scr, cwa_ref, ba_ref, subs[s], r0, *edge)
            ug = _conv3_block(ug_scr, cwg_ref, bg_ref, subs[s], r0, *edge)
            act_new[r0:r0 + SEQ_ALIGN, subs[s]] = (ug * jax.nn.sigmoid(ug) * ua).astype(BF16)

    @pl.when(j < nj)
    def _():
        act_cur[...] = act_new[...]
        up_project(0)
        up_project(1)
        conv_gate(0)
        conv_gate(1)
        o_ref[...] += _dot(act_cur[...], wd_ref[...])

    @pl.when(j == nj)
    def _():
        out = x_ref[...] + gate_ref[...] * (o_ref[...] + _dot(act_new[...], wd_ref[...]))
        if final_norm:
            out = out * lax.rsqrt(jnp.mean(out * out, axis=-1, keepdims=True) + NORM_EPS) * fg_ref[...]
        o_ref[...] = out


def _conv_ffn(x, g, scale, shift, gate, wa, wg, cwa, cwg, ba, bg, wd, final_g, layer, seg, rows_lat, t_lat, t_ctx,
              final_norm):
    rows, d = x.shape
    tm = next(t for t in (1024, 512, 256) if rows % t == 0 and rows_lat % t == 0 and seg[1] % t == 0)
    tf = FFN_TILE
    nh = tm // HALO
    last = rows // HALO - 1
    nj = D_FF_PAD // tf
    seg = _segment_index(seg, tm)
    col = lambda r: pl.BlockSpec((r, tf), lambda i, j: (0, jnp.minimum(j, nj - 1)))
    up = pl.BlockSpec((None, None, d, tf), lambda i, j: (layer, jnp.minimum(j, nj - 1), 0, 0))
    return pl.pallas_call(
        functools.partial(_ffn_body, tm=tm, rows_lat=rows_lat, t_lat=t_lat, t_ctx=t_ctx, final_norm=final_norm),
        grid=(rows // tm, nj + 1),
        in_specs=[pl.BlockSpec((tm, d), lambda i, j: (i, 0), pipeline_mode=pl.Buffered(1)),
                  pl.BlockSpec((HALO, d), lambda i, j: (jnp.maximum(i * nh - 1, 0), 0)),
                  pl.BlockSpec((HALO, d), lambda i, j: (jnp.minimum((i + 1) * nh, last), 0)),
                  pl.BlockSpec((1, d), lambda i, j: (0, 0)),
                  pl.BlockSpec((None, 1, d), seg), pl.BlockSpec((None, 1, d), seg), pl.BlockSpec((None, 1, d), seg),
                  up, up, col(3), col(3), col(1), col(1),
                  pl.BlockSpec((None, tf, d), lambda i, j: (layer, jnp.maximum(j - 1, 0), 0)),
                  pl.BlockSpec((1, d), lambda i, j: (0, 0))],
        out_specs=pl.BlockSpec((tm, d), lambda i, j: (i, 0)),
        out_shape=jax.ShapeDtypeStruct(x.shape, F32),
        scratch_shapes=[pltpu.VMEM((tm + 2 * HALO, d), BF16),
                        pltpu.VMEM((tm + 2 * HALO, FFN_SUB), F32),
                        pltpu.VMEM((tm + 2 * HALO, FFN_SUB), F32),
                        pltpu.VMEM((tm + 2 * HALO, FFN_SUB), F32),
                        pltpu.VMEM((tm + 2 * HALO, FFN_SUB), F32),
                        pltpu.VMEM((tm, tf), BF16),
                        pltpu.VMEM((tm, tf), BF16)],
        compiler_params=_params(("parallel", "arbitrary"), 56),
        name="conv_ffn",
    )(x, x, x, g, scale, shift, gate, wa, wg, cwa, cwg, ba, bg, wd, final_g)


def _lora_weight(w2, a2, g2):
    c = RWKV_WIDTH
    wl = jnp.zeros((LORA_IN, 5 * c), F32)
    wl = wl.at[:DECAY_LORA, 0:c].set(w2[0]).at[:DECAY_LORA, c:2 * c].set(w2[1])
    wl = wl.at[DECAY_LORA:DECAY_LORA + ICLR_LORA, 2 * c:3 * c].set(a2[0])
    wl = wl.at[DECAY_LORA:DECAY_LORA + ICLR_LORA, 3 * c:4 * c].set(a2[1])
    return wl.at[DECAY_LORA + ICLR_LORA:, 4 * c:].set(g2)


LANES = 128


def _split_up_body(a_ref, g_ref, oa_ref, og_ref, *, nvalid):
    keep = pl.program_id(1) < nvalid
    oa_ref[...] = jnp.where(keep, a_ref[...], 0.0).astype(BF16)
    og_ref[...] = jnp.where(keep, g_ref[...], 0.0).astype(BF16)


def _split_ffn_up(ffn_up):
    depth, d, _ = ffn_up.shape
    nvalid = D_FF // LANES
    per_tile = FFN_TILE // LANES
    src = lambda half: pl.BlockSpec((None, d, LANES),
                                    lambda l, j: (l, 0, half * nvalid + jnp.minimum(j, nvalid - 1)))
    dst = pl.BlockSpec((None, None, d, LANES), lambda l, j: (l, j // per_tile, 0, j % per_tile))
    shape = jax.ShapeDtypeStruct((depth, D_FF_PAD // FFN_TILE, d, FFN_TILE), BF16)
    return pl.pallas_call(
        functools.partial(_split_up_body, nvalid=nvalid),
        grid=(depth, D_FF_PAD // LANES),
        in_specs=[src(0), src(1)], out_specs=[dst, dst], out_shape=[shape, shape],
        compiler_params=_params(("parallel", "parallel"), 32),
        name="split_ffn_up",
    )(ffn_up, ffn_up)


def _pad_down_body(w_ref, o_ref, *, tk):
    row = pl.program_id(1) * tk + lax.broadcasted_iota(jnp.int32, (tk, 1), 0)
    o_ref[...] = jnp.where(row < D_FF, w_ref[...], 0.0).astype(BF16)


def _pad_ffn_down(ffn_down):
    depth, _, d = ffn_down.shape
    tk = FFN_TILE
    return pl.pallas_call(
        functools.partial(_pad_down_body, tk=tk),
        grid=(depth, D_FF_PAD // tk),
        in_specs=[pl.BlockSpec((None, tk, d), lambda l, j: (l, j, 0))],
        out_specs=pl.BlockSpec((None, tk, d), lambda l, j: (l, j, 0)),
        out_shape=jax.ShapeDtypeStruct((depth, D_FF_PAD, d), BF16),
        compiler_params=_params(("parallel", "parallel"), 32),
        name="pad_ffn_down",
    )(ffn_down)


def _pad_cols(w):
    return jnp.pad(w, ((0, 0), (0, D_FF_PAD - D_FF)))


def _forward(x_lat, x_ctx, cache_k, cache_v, state, c, c_ctx, w_ada, b_ada, norm1_g, norm2_g, w_in, w_out,
             q_norm_g, k_norm_g, rw_conv, rw_w0, rw_w2, rw_a0, rw_a2, rw_g2, rw_kk, rw_ka, rw_rk,
             rw_lnx_g, rw_lnx_b, ffn_up, ffn_conv_w, ffn_conv_b, ffn_down, final_norm_g):
    n_lat, t_lat, d = x_lat.shape
    n_ctx, t_ctx, _ = x_ctx.shape
    depth = w_ada.shape[0]
    past = cache_k.shape[2]
    rows_lat, rows_ctx = n_lat * t_lat, n_ctx * t_ctx
    assert n_lat < MOD_ROWS
    assert t_lat % SEQ_ALIGN == 0 and t_ctx % SEQ_ALIGN == 0 and t_lat % GRID_W == 0

    cvec = jnp.concatenate([c, jnp.broadcast_to(c_ctx[None, :], (MOD_ROWS - n_lat, d))], axis=0)
    mod = _adaln_mod(cvec, w_ada, b_ada).reshape(depth, MOD_ROWS, 6, 1, d)

    ffn_wa, ffn_wg = _split_ffn_up(ffn_up)
    ffn_wd = _pad_ffn_down(ffn_down)
    rope_tabs = _rope_tables(t_lat)
    time_lat, chan_tab = _dft_tables(t_lat)
    time_ctx, _ = _dft_tables(t_ctx)
    zero_state = jnp.zeros((n_ctx, 1, 2, RWKV_HEADS, RWKV_N, RWKV_N), F32)
    row2 = lambda a: a.reshape(1, -1)

    cache = (cache_k.reshape(n_lat, depth, past, KV_WIDTH), cache_v.reshape(n_lat, depth, past, KV_WIDTH))

    def trunk_layer(x, l, latent):
        nseq, t = (n_lat, t_lat) if latent else (n_ctx, t_ctx)
        rows = nseq * t
        seg = (0, t_lat) if latent else (n_lat, rows)
        region_rows_lat = rows if latent else 0
        shift1, scale1, gate1, shift2, scale2, gate2 = (mod[l, :, i] for i in range(6))
        in_args = (x, row2(norm1_g[l]), scale1, shift1, w_in_b, l, seg, row2(q_norm_g[l]), row2(k_norm_g[l]))
        if latent:
            f, qb, kb, vb, rw = _in_proj(*in_args, rope_tabs)
            a_out = _attention(qb, kb, vb, nseq, t, cache[0], cache[1], l)
            k_norm = v = None
        else:
            f, qb, kb, vb, rw, k_norm, v = _in_proj(*in_args)
            a_out = _attention(qb, kb, vb, nseq, t)
        f_out = _fourier_mix(f, time_lat if latent else time_ctx, chan_tab, 0, nseq, t)

        r_, v_, kk, g_, bonus, lw, km, b_ = _rwkv_prep(
            rw, rw_conv[l], lora_w[l], rw_w0[l], rw_a0[l],
            row2(rw_kk[l]), row2(rw_ka[l]), row2(rw_rk[l]), region_rows_lat, t_lat, t_ctx)
        yf, yb, s_fin = _rwkv_scan(r_, v_, kk, lw, km, b_, state if latent else zero_state, l if latent else 0,
                                   0, nseq, t)
        x = _out_proj(f_out, a_out, yf, yb, bonus, g_, row2(rw_lnx_g[l]), row2(rw_lnx_b[l]), x, gate1,
                      w_out_b, l, seg)
        x = _conv_ffn(x, row2(norm2_g[l]), scale2, shift2, gate2, ffn_wa, ffn_wg,
                      _pad_cols(ffn_conv_w[l][:, :D_FF]), _pad_cols(ffn_conv_w[l][:, D_FF:]),
                      _pad_cols(row2(ffn_conv_b[l][:D_FF])), _pad_cols(row2(ffn_conv_b[l][D_FF:])),
                      ffn_wd, row2(final_norm_g), l, seg, region_rows_lat, t_lat, t_ctx, final_norm=l == depth - 1)
        return x, k_norm, v, s_fin

    w_in_b, w_out_b = w_in.astype(BF16), w_out.astype(BF16)
    lora_w = [_lora_weight(rw_w2[l], rw_a2[l], rw_g2[l]).astype(BF16) for l in range(depth)]
    xs, xc = x_lat.reshape(rows_lat, d), x_ctx.reshape(rows_ctx, d)
    new_k, new_v, new_s = [], [], []
    for l in range(depth):
        xc, k_norm, v_ctx, s_ctx = trunk_layer(xc, l, latent=False)
        new_k.append(k_norm.reshape(n_ctx, t_ctx, ATTN_KV_HEADS, HEAD_DIM))
        new_v.append(v_ctx.reshape(n_ctx, t_ctx, ATTN_KV_HEADS, HEAD_DIM))
        new_s.append(s_ctx)
        xs, _, _, _ = trunk_layer(xs, l, latent=True)

    y_lat, y_ctx = xs.reshape(n_lat, t_lat, d), xc.reshape(n_ctx, t_ctx, d)
    return (y_ctx, y_lat, jnp.stack(new_k, axis=1), jnp.stack(new_v, axis=1), jnp.stack(new_s, axis=1))


def kernel(x_prompt, x_sample, cache_attn_k, cache_attn_v, state_rwkv, c, c_ctx, w_ada, b_ada, norm1_g, norm2_g, w_in, w_out, q_norm_g, k_norm_g, rw_conv, rw_w0, rw_w2, rw_a0, rw_a2, rw_g2, rw_kk, rw_ka, rw_rk, rw_lnx_g, rw_lnx_b, ffn_up, ffn_conv_w, ffn_conv_b, ffn_down, final_norm_g):
    return _forward(x_sample, x_prompt, cache_attn_k, cache_attn_v, state_rwkv, c, c_ctx, w_ada, b_ada,
                    norm1_g, norm2_g, w_in, w_out, q_norm_g, k_norm_g, rw_conv, rw_w0, rw_w2, rw_a0, rw_a2,
                    rw_g2, rw_kk, rw_ka, rw_rk, rw_lnx_g, rw_lnx_b, ffn_up, ffn_conv_w, ffn_conv_b, ffn_down,
                    final_norm_g)
```

```python
import functools
import math

import jax
import jax.numpy as jnp
import numpy as np
from jax import lax
from jax.experimental import pallas as pl
from jax.experimental.pallas import tpu as pltpu

D_MODEL = 2048
GRID_W = 64
HEAD_DIM = 128
ATTN_HEADS = 8
ATTN_KV_HEADS = 2
KV_GROUP = ATTN_HEADS // ATTN_KV_HEADS
ATTN_WIDTH = ATTN_HEADS * HEAD_DIM
KV_WIDTH = ATTN_KV_HEADS * HEAD_DIM
FOURIER_WIDTH = 512
FOURIER_GROUPS = 4
FOURIER_GROUP_WIDTH = FOURIER_WIDTH // FOURIER_GROUPS
RWKV_WIDTH = 512
RWKV_N = 64
RWKV_HEADS = RWKV_WIDTH // RWKV_N
DECAY_LORA = 64
ICLR_LORA = 64
GATE_LORA = 128
LORA_IN = DECAY_LORA + ICLR_LORA + GATE_LORA
RWKV_IN = 3 * RWKV_WIDTH + LORA_IN
IN_WIDTH = FOURIER_WIDTH + ATTN_WIDTH + 2 * KV_WIDTH + RWKV_IN
D_FF = 5504
ROPE_THETA = 10000.0
NORM_EPS = 1e-6
GN_EPS = 64e-5

MOD_ROWS = 16
FFN_TILE = 512
FFN_SUB = 256
D_FF_PAD = -(-D_FF // FFN_TILE) * FFN_TILE
HALO = 16
CHUNK = 64
SCAN_SUB = 2
SEQ_ALIGN = 256
MIB = 2 ** 20

F32 = jnp.float32
BF16 = jnp.bfloat16


def _params(semantics, vmem_mib):
    return pltpu.CompilerParams(dimension_semantics=semantics, vmem_limit_bytes=vmem_mib * MIB)


def _resident(shape):
    return pl.BlockSpec(shape, lambda *_: (0,) * len(shape), pipeline_mode=pl.Buffered(1))


def _resident_layer(stacked_shape, layer):
    shape = stacked_shape[1:]
    return pl.BlockSpec((None,) + tuple(shape), lambda *_: (layer,) + (0,) * len(shape),
                        pipeline_mode=pl.Buffered(1))


def _dot(a, b):
    return jnp.dot(a, b, preferred_element_type=F32)


def _dot_nt(a, b):
    return lax.dot_general(a, b, (((1,), (1,)), ((), ())), preferred_element_type=F32)


def _dot_tn(a, b):
    return lax.dot_general(a, b, (((0,), (0,)), ((), ())), preferred_element_type=F32)


def _modulated_rmsnorm(x, g, scale, shift):
    ms = jnp.mean(x * x, axis=-1, keepdims=True)
    return (x * lax.rsqrt(ms + NORM_EPS) * g) * (1.0 + scale) + shift


def _mod_body(c_ref, w_ref, b_ref, o_ref):
    c = c_ref[...]
    s = (c * jax.nn.sigmoid(c)).astype(BF16)
    o_ref[...] = _dot(s, w_ref[...].astype(BF16)) + b_ref[...]


def _adaln_mod(cvec, w_ada, b_ada):
    depth, d, n = w_ada.shape
    tn = 1024
    return pl.pallas_call(
        _mod_body,
        grid=(depth, n // tn),
        in_specs=[pl.BlockSpec((MOD_ROWS, d), lambda l, j: (0, 0)),
                  pl.BlockSpec((None, d, tn), lambda l, j: (l, 0, j)),
                  pl.BlockSpec((None, 1, tn), lambda l, j: (l, 0, j))],
        out_specs=pl.BlockSpec((None, MOD_ROWS, tn), lambda l, j: (l, 0, j)),
        out_shape=jax.ShapeDtypeStruct((depth, MOD_ROWS, n), F32),
        compiler_params=_params(("parallel", "parallel"), 40),
        name="adaln_mod",
    )(cvec, w_ada, b_ada.reshape(depth, 1, n))


Q_COL = FOURIER_WIDTH
K_COL = Q_COL + ATTN_WIDTH
V_COL = K_COL + KV_WIDTH
RW_COL = V_COL + KV_WIDTH


def _inproj_body(*refs, rope):
    x_ref, g_ref, sc_ref, sh_ref, w_ref, gq_ref, gk_ref = refs[:7]
    if rope:
        cos_ref, sin_ref, f_ref, q_ref, k_ref, v_ref, rw_ref = refs[7:]
        cos, sin = cos_ref[...], sin_ref[...]
        lane = lax.broadcasted_iota(jnp.int32, (1, HEAD_DIM), 1)
        low = (lane % (HEAD_DIM // 2)) < (HEAD_DIM // 4)
    else:
        f_ref, q_ref, k_ref, v_ref, rw_ref, kn_ref, vf_ref = refs[7:]

    h = _modulated_rmsnorm(x_ref[...], g_ref[...], sc_ref[...], sh_ref[...]).astype(BF16)
    proj = lambda start, width: _dot(h, w_ref[:, start:start + width])
    q, k, v = proj(Q_COL, ATTN_WIDTH), proj(K_COL, KV_WIDTH), proj(V_COL, KV_WIDTH)
    f_ref[...] = proj(0, FOURIER_WIDTH).astype(BF16)
    rw_ref[...] = proj(RW_COL, RWKV_IN)

    def head_norm(xh, g):
        ms = jnp.mean(xh * xh, axis=-1, keepdims=True)
        return xh * lax.rsqrt(ms + NORM_EPS) * g

    def rotate(xh):
        partner = jnp.where(low, pltpu.roll(xh, HEAD_DIM - HEAD_DIM // 4, 1), pltpu.roll(xh, HEAD_DIM // 4, 1))
        return xh * cos + partner * sin

    scale = HEAD_DIM ** -0.5
    for hd in range(ATTN_HEADS):
        sl = slice(hd * HEAD_DIM, (hd + 1) * HEAD_DIM)
        qh = head_norm(q[:, sl], gq_ref[...])
        if rope:
            qh = rotate(qh)
        q_ref[:, sl] = (qh * scale).astype(BF16)
    for j in range(ATTN_KV_HEADS):
        sl = slice(j * HEAD_DIM, (j + 1) * HEAD_DIM)
        kh = head_norm(k[:, sl], gk_ref[...])
        if rope:
            kh = rotate(kh)
        else:
            kn_ref[:, sl] = kh
        k_ref[:, sl] = kh.astype(BF16)
    v_ref[...] = v.astype(BF16)
    if not rope:
        vf_ref[...] = v


def _segment_index(seg, tm):
    seg0, seg_rows = seg
    assert seg_rows % tm == 0
    return lambda i, *_: (seg0 + i * tm // seg_rows, 0, 0)


def _in_proj(x, g, scale, shift, w, layer, seg, gq, gk, rope_tabs=None):
    rows, d = x.shape
    tm = 512 if rows % 512 == 0 else 256
    seg = _segment_index(seg, tm)
    rope = rope_tabs is not None
    in_specs = [pl.BlockSpec((tm, d), lambda i: (i, 0)),
                _resident((1, d)),
                pl.BlockSpec((None, 1, d), seg),
                pl.BlockSpec((None, 1, d), seg),
                _resident_layer(w.shape, layer),
                _resident((1, HEAD_DIM)), _resident((1, HEAD_DIM))]
    args = [x, g, scale, shift, w, gq, gk]
    outs = [(FOURIER_WIDTH, BF16), (ATTN_WIDTH, BF16), (KV_WIDTH, BF16), (KV_WIDTH, BF16), (RWKV_IN, F32)]
    if rope:
        nt = rope_tabs[0].shape[0] // tm
        in_specs += [pl.BlockSpec((tm, HEAD_DIM), lambda i: (i % nt, 0))] * 2
        args += list(rope_tabs)
    else:
        outs += [(KV_WIDTH, F32), (KV_WIDTH, F32)]
    return pl.pallas_call(
        functools.partial(_inproj_body, rope=rope),
        grid=(rows // tm,),
        in_specs=in_specs,
        out_specs=[pl.BlockSpec((tm, width), lambda i: (i, 0)) for width, _ in outs],
        out_shape=[jax.ShapeDtypeStruct((rows, width), dt) for width, dt in outs],
        compiler_params=_params(("parallel",), 56),
        name="in_proj_rope" if rope else "in_proj",
    )(*args)


def _dft_tables(t):
    def angles(n):
        i = np.arange(n, dtype=np.int64)
        return (2.0 * math.pi / n) * ((i[:, None] * i[None, :]) % n)
    at = angles(t)
    time_tab = np.concatenate([np.cos(at), -np.sin(at)], axis=1).astype(np.float32)
    ac = angles(FOURIER_GROUP_WIDTH)
    eye = np.eye(FOURIER_GROUPS)
    chan_tab = np.concatenate([np.kron(eye, np.cos(ac)), np.kron(eye, np.sin(ac))], axis=1).astype(np.float32)
    return jnp.asarray(time_tab).astype(BF16), jnp.asarray(chan_tab).astype(BF16)


def _fourier_body(u_ref, ct_ref, cc_ref, o_ref, ab_scr, *, t, norm):
    @pl.when(pl.program_id(1) == 0)
    def _():
        ab = _dot(u_ref[...], cc_ref[...])
        ab_scr[0:t, :] = ab[:, :FOURIER_WIDTH].astype(BF16)
        ab_scr[t:2 * t, :] = ab[:, FOURIER_WIDTH:].astype(BF16)
    o_ref[...] = (_dot(ct_ref[...], ab_scr[...]) * norm).astype(o_ref.dtype)


def _fourier_mix(u, time_tab, chan_tab, row0, nseq, t):
    tm = min(t, 512)
    nt = t // tm
    seq0 = row0 // t
    return pl.pallas_call(
        functools.partial(_fourier_body, t=t, norm=1.0 / math.sqrt(t * FOURIER_GROUP_WIDTH)),
        grid=(nseq, nt),
        in_specs=[pl.BlockSpec((t, FOURIER_WIDTH), lambda b, i: (seq0 + b, 0)),
                  pl.BlockSpec((tm, 2 * t), lambda b, i: (i, 0)),
                  _resident(chan_tab.shape)],
        out_specs=pl.BlockSpec((tm, FOURIER_WIDTH), lambda b, i: (b * nt + i, 0)),
        out_shape=jax.ShapeDtypeStruct((nseq * t, FOURIER_WIDTH), BF16),
        scratch_shapes=[pltpu.VMEM((2 * t, FOURIER_WIDTH), BF16)],
        compiler_params=_params(("parallel", "arbitrary"), 40),
        name=f"fourier_mix_t{t}",
    )(u, time_tab, chan_tab)


def _rope_tables(t):
    pos = jnp.arange(t, dtype=jnp.int32)
    rows = (pos // GRID_W).astype(F32)
    cols = (pos % GRID_W).astype(F32)
    half = HEAD_DIM // 2
    inv = 1.0 / (ROPE_THETA ** (jnp.arange(0, half, 2, dtype=F32) / half))
    def tab(p):
        ang = p[:, None] * inv[None, :]
        return (jnp.concatenate([jnp.cos(ang), jnp.cos(ang)], -1),
                jnp.concatenate([-jnp.sin(ang), jnp.sin(ang)], -1))
    cr, sr = tab(rows)
    cc, sc = tab(cols)
    return jnp.concatenate([cr, cc], -1), jnp.concatenate([sr, sc], -1)


def _attn_body(*refs, cached):
    if cached:
        q_ref, k_ref, v_ref, kc_ref, vc_ref, o_ref = refs
    else:
        q_ref, k_ref, v_ref, o_ref = refs
    head_cols = lambda h: slice(h * HEAD_DIM, (h + 1) * HEAD_DIM)
    if cached:
        cache = [(kc_ref[:, head_cols(j)].astype(BF16), vc_ref[:, head_cols(j)].astype(BF16))
                 for j in range(ATTN_KV_HEADS)]

    def scores(h):
        j = h // KV_GROUP
        qh = q_ref[:, head_cols(h)]
        return _dot_nt(qh, k_ref[:, head_cols(j)]), (_dot_nt(qh, cache[j][0]) if cached else None)

    pending = scores(0)
    for h in range(ATTN_HEADS):
        s, sc = pending
        if h + 1 < ATTN_HEADS:
            pending = scores(h + 1)
        j = h // KV_GROUP
        m = jnp.max(s, axis=-1, keepdims=True)
        if cached:
            m = jnp.maximum(m, jnp.max(sc, axis=-1, keepdims=True))
        p = jnp.exp(s - m)
        l = jnp.sum(p, axis=-1, keepdims=True)
        acc = _dot(p.astype(BF16), v_ref[:, head_cols(j)])
        if cached:
            pc = jnp.exp(sc - m)
            l = l + jnp.sum(pc, axis=-1, keepdims=True)
            acc = acc + _dot(pc.astype(BF16), cache[j][1])
        o_ref[:, head_cols(h)] = (acc / l).astype(o_ref.dtype)


def _attention(q, k, v, nseq, t, cache_k=None, cache_v=None, layer=0):
    tq = 512 if t % 512 == 0 else 256
    nq = t // tq
    cached = cache_k is not None
    in_specs = [pl.BlockSpec((tq, ATTN_WIDTH), lambda b, i: (b * nq + i, 0)),
                pl.BlockSpec((t, KV_WIDTH), lambda b, i: (b, 0)),
                pl.BlockSpec((t, KV_WIDTH), lambda b, i: (b, 0))]
    args = [q, k, v]
    if cached:
        past = cache_k.shape[2]
        in_specs += [pl.BlockSpec((None, None, past, KV_WIDTH), lambda b, i: (b, layer, 0, 0))] * 2
        args += [cache_k, cache_v]
    return pl.pallas_call(
        functools.partial(_attn_body, cached=cached),
        grid=(nseq, nq),
        in_specs=in_specs,
        out_specs=pl.BlockSpec((tq, ATTN_WIDTH), lambda b, i: (b * nq + i, 0)),
        out_shape=jax.ShapeDtypeStruct((nseq * t, ATTN_WIDTH), BF16),
        compiler_params=_params(("parallel", "parallel"), 48),
        name="attention_cached" if cached else "attention",
    )(*args)


def _head_sum_matrix():
    i = lax.broadcasted_iota(jnp.int32, (RWKV_WIDTH, RWKV_WIDTH), 0) // RWKV_N
    j = lax.broadcasted_iota(jnp.int32, (RWKV_WIDTH, RWKV_WIDTH), 1) // RWKV_N
    return (i == j).astype(BF16)


def _bf16_pieces(x, n):
    pieces = []
    for _ in range(n - 1):
        p = x.astype(BF16)
        pieces.append(p)
        x = x - p.astype(F32)
    return pieces + [x.astype(BF16)]


def _head_sums(x, ones_bd):
    hi, lo = _bf16_pieces(x, 2)
    return _dot(hi, ones_bd) + _dot(lo, ones_bd)


def _rwkv_prep_body(rw_ref, prev_ref, next_ref, cw_ref, wl_ref, w0_ref, a0_ref, kks_ref, ka_ref, rk_ref,
                    r_ref, v_ref, kk_ref, g_ref, bonus_ref, lw_ref, km_ref, b_ref, ext_scr,
                    *, tm, rows_lat, t_lat, t_ctx):
    ext_scr[0:HALO, :] = prev_ref[...]
    ext_scr[HALO:HALO + tm, :] = rw_ref[...]
    ext_scr[HALO + tm:, :] = next_ref[...]
    edges = _edge_masks(pl.program_id(0) * tm, rows_lat, t_lat, t_ctx)
    z = _conv3_block(ext_scr, cw_ref, None, slice(None), 0, *edges)
    c = RWKV_WIDTH
    r, k, v = z[:, :c], z[:, c:2 * c], z[:, 2 * c:3 * c]
    zl = z[:, 3 * c:]
    lane = lax.broadcasted_iota(jnp.int32, (1, LORA_IN), 1)
    lora_in = jnp.where(lane < DECAY_LORA, jnp.tanh(zl),
                        jnp.where(lane < DECAY_LORA + ICLR_LORA, zl, jax.nn.sigmoid(zl)))
    lora = _dot(lora_in.astype(BF16), wl_ref[...])
    ones_bd = _head_sum_matrix()
    kk = k * kks_ref[...]
    kk = kk * lax.rsqrt(_head_sums(kk * kk, ones_bd) + 1e-12)
    r_ref[...] = r.astype(r_ref.dtype)
    v_ref[...] = v.astype(v_ref.dtype)
    kk_ref[...] = kk.astype(kk_ref.dtype)
    g_ref[...] = lora[:, 4 * c:5 * c]
    kmod_sum = jnp.zeros_like(k)
    for d in range(2):
        wpre = w0_ref[d:d + 1, :] + lora[:, d * c:(d + 1) * c]
        lw_ref[d] = -math.exp(-0.5) * jax.nn.sigmoid(wpre)
        a = jax.nn.sigmoid(a0_ref[d:d + 1, :] + lora[:, (2 + d) * c:(3 + d) * c])
        kmod = k * (1.0 + (a - 1.0) * ka_ref[...])
        km_ref[d] = kmod.astype(km_ref.dtype)
        b_ref[d] = (kk * a).astype(b_ref.dtype)
        kmod_sum = kmod_sum + kmod
    bonus_ref[...] = _head_sums(r * kmod_sum * rk_ref[...], ones_bd) * v


def _rwkv_prep(rw, conv_w, lora_w, w0, a0, kk_scale, ka, rk, rows_lat, t_lat, t_ctx):
    rows = rw.shape[0]
    tm = 256
    nh = tm // HALO
    last = rows // HALO - 1
    c = RWKV_WIDTH
    one = lambda: pl.BlockSpec((tm, c), lambda i: (i, 0))
    two = lambda: pl.BlockSpec((2, tm, c), lambda i: (0, i, 0))
    return pl.pallas_call(
        functools.partial(_rwkv_prep_body, tm=tm, rows_lat=rows_lat, t_lat=t_lat, t_ctx=t_ctx),
        grid=(rows // tm,),
        in_specs=[pl.BlockSpec((tm, RWKV_IN), lambda i: (i, 0)),
                  pl.BlockSpec((HALO, RWKV_IN), lambda i: (jnp.maximum(i * nh - 1, 0), 0)),
                  pl.BlockSpec((HALO, RWKV_IN), lambda i: (jnp.minimum((i + 1) * nh, last), 0)),
                  _resident(conv_w.shape), _resident(lora_w.shape), _resident(w0.shape), _resident(a0.shape),
                  _resident(kk_scale.shape), _resident(ka.shape), _resident(rk.shape)],
        out_specs=[one(), one(), one(), one(), one(), two(), two(), two()],
        out_shape=[jax.ShapeDtypeStruct((rows, c), dt) for dt in (BF16, BF16, BF16, F32, F32)]
        + [jax.ShapeDtypeStruct((2, rows, c), dt) for dt in (F32, BF16, BF16)],
        scratch_shapes=[pltpu.VMEM((tm + 2 * HALO, RWKV_IN), F32)],
        compiler_params=_params(("parallel",), 48),
        name="rwkv_prep",
    )(rw, rw, rw, conv_w, lora_w, w0, a0, kk_scale, ka, rk)


def _scan_direction_operands(r_ref, v_ref, kk_ref, lw_ref, km_ref, b_ref, rows, backward):
    n = CHUNK
    row = lax.broadcasted_iota(jnp.int32, (n, n), 0)
    col = lax.broadcasted_iota(jnp.int32, (n, n), 1)
    upto = (col >= row) if backward else (col <= row)
    lw = lw_ref[rows, :]
    tri = upto.astype(BF16)
    cs = sum(_dot(tri, piece) for piece in _bf16_pieces(lw, 3))
    tot = jnp.sum(lw, axis=0, keepdims=True)
    grow = jnp.exp(-cs)
    to_end = jnp.exp(tot - cs)
    b = b_ref[rows, :]
    km = km_ref[rows, :]
    return dict(
        kkt=(kk_ref[rows, :] * jnp.exp(cs - lw)).astype(BF16),
        rt=(r_ref[rows, :] * jnp.exp(cs)).astype(BF16),
        bt=(b * grow).astype(BF16), kt=(km * grow).astype(BF16),
        bh=(b * to_end).astype(BF16), kh=(km * to_end).astype(BF16),
        vb=v_ref[rows, :].astype(BF16), w_all=jnp.exp(tot))


def _rwkv_scan_body(rf_ref, vf_ref, kkf_ref, lwf_ref, kmf_ref, bf_ref,
                    rb_ref, vb_ref, kkb_ref, lwb_ref, kmb_ref, bb_ref, s0_ref,
                    yf_ref, yb_ref, sfin_ref, s_scr):
    c = pl.program_id(1)

    @pl.when(c == 0)
    def _():
        s_scr[...] = s0_ref[...]

    n = CHUNK
    row = lax.broadcasted_iota(jnp.int32, (2 * n, 2 * n), 0)
    col = lax.broadcasted_iota(jnp.int32, (2 * n, 2 * n), 1)
    t_idx, s_idx, read_rows = row % n, col % n, row >= n
    eye = (lax.broadcasted_iota(jnp.int32, (n, n), 0) == lax.broadcasted_iota(jnp.int32, (n, n), 1)).astype(F32)
    same_step = jnp.logical_and(read_rows, s_idx == t_idx)
    masks = (jnp.logical_or(s_idx < t_idx, same_step), jnp.logical_or(s_idx > t_idx, same_step))
    y_refs = (yf_ref, yb_ref)
    in_refs = ((rf_ref, vf_ref, kkf_ref, lwf_ref, kmf_ref, bf_ref), (rb_ref, vb_ref, kkb_ref, lwb_ref, kmb_ref, bb_ref))
    rows = [(slice(k * n, (k + 1) * n), slice((SCAN_SUB - 1 - k) * n, (SCAN_SUB - k) * n)) for k in range(SCAN_SUB)]
    ops = [[_scan_direction_operands(*in_refs[d], rows[k][d], backward=d == 1) for d in range(2)]
           for k in range(SCAN_SUB)]
    pairs = [(d, h) for d in range(2) for h in range(RWKV_HEADS)]
    chains = [(k, d, h) for k in range(SCAN_SUB) for d, h in pairs]
    head = lambda k, d, name, h: ops[k][d][name][:, h * RWKV_N:(h + 1) * RWKV_N]
    cat = lambda a, b: jnp.concatenate([a, b], axis=0)

    lhs = [cat(head(k, d, "kkt", h), head(k, d, "rt", h)) for k, d, h in chains]
    coef = [jnp.where(masks[d], _dot_nt(l, cat(head(k, d, "bt", h), head(k, d, "kt", h))), 0.0)
            for l, (k, d, h) in zip(lhs, chains)]
    n_mat = [a[:n, :n] for a in coef]
    akv = [_dot(a[:n, n:].astype(BF16), head(k, d, "vb", h)) for a, (k, d, h) in zip(coef, chains)]
    x = [eye - m for m in n_mat]
    p = [_dot(m.astype(BF16), m.astype(BF16)) for m in n_mat]
    steps = int(math.log2(n)) - 1
    for s in range(steps):
        pb = [q.astype(BF16) for q in p]
        x = [xi + _dot(xi.astype(BF16), q) for xi, q in zip(x, pb)]
        if s + 1 < steps:
            p = [_dot(q, q) for q in pb]

    for k in range(SCAN_SUB):
        mine = slice(k * len(pairs), (k + 1) * len(pairs))
        state = [s_scr[d, h] for d, h in pairs]
        read = [_dot_nt(l, s.astype(BF16)) for l, s in zip(lhs[mine], state)]
        u = [_dot(xi.astype(BF16), (-(rd[:n] + ak)).astype(BF16)) for xi, rd, ak in zip(x[mine], read, akv[mine])]
        uv = [cat(ui.astype(BF16), head(k, d, "vb", h)) for ui, (d, h) in zip(u, pairs)]
        for (d, h), a, rd, uvi, s in zip(pairs, coef[mine], read, uv, state):
            sl = slice(h * RWKV_N, (h + 1) * RWKV_N)
            y_refs[d][rows[k][d], sl] = rd[n:] + _dot(a[n:].astype(BF16), uvi)
            s_scr[d, h] = (s * ops[k][d]["w_all"][:, sl]
                           + _dot_tn(uvi, cat(head(k, d, "bh", h), head(k, d, "kh", h))))

    @pl.when(c == pl.num_programs(1) - 1)
    def _():
        sfin_ref[...] = s_scr[...]


def _rwkv_scan(r, v, kk, lw, km, b, states, layer, row0, nseq, t):
    n = CHUNK * SCAN_SUB
    assert t % n == 0 and row0 % n == 0
    nc = t // n
    c0 = row0 // n
    c = RWKV_WIDTH
    fwd = lambda bb, cc: c0 + bb * nc + cc
    bwd = lambda bb, cc: c0 + bb * nc + nc - 1 - cc
    one = lambda chunk: pl.BlockSpec((n, c), lambda bb, cc: (chunk(bb, cc), 0))
    two = lambda chunk, d: pl.BlockSpec((None, n, c), lambda bb, cc: (d, chunk(bb, cc), 0))
    state = pl.BlockSpec((None, 2, RWKV_HEADS, RWKV_N, RWKV_N), lambda bb, cc: (bb, 0, 0, 0, 0))
    state_in = pl.BlockSpec((None, None, 2, RWKV_HEADS, RWKV_N, RWKV_N), lambda bb, cc: (bb, layer, 0, 0, 0, 0))
    y_shape = jax.ShapeDtypeStruct((nseq * t, c), F32)
    return pl.pallas_call(
        _rwkv_scan_body,
        grid=(nseq, nc),
        in_specs=[one(fwd), one(fwd), one(fwd), two(fwd, 0), two(fwd, 0), two(fwd, 0),
                  one(bwd), one(bwd), one(bwd), two(bwd, 1), two(bwd, 1), two(bwd, 1), state_in],
        out_specs=[pl.BlockSpec((n, c), lambda bb, cc: (bb * nc + cc, 0)),
                   pl.BlockSpec((n, c), lambda bb, cc: (bb * nc + nc - 1 - cc, 0)), state],
        out_shape=[y_shape, y_shape, jax.ShapeDtypeStruct((nseq, 2, RWKV_HEADS, RWKV_N, RWKV_N), F32)],
        scratch_shapes=[pltpu.VMEM((2, RWKV_HEADS, RWKV_N, RWKV_N), F32)],
        compiler_params=_params(("parallel", "arbitrary"), 32),
        name=f"rwkv_scan_t{t}",
    )(r, v, kk, lw, km, b, r, v, kk, lw, km, b, states)


def _outproj_body(f_ref, a_ref, yf_ref, yb_ref, bonus_ref, g_ref, lg_ref, lb_ref, x_ref, gate_ref, w_ref, o_ref,
                  mix_scr):
    a0, r0 = FOURIER_WIDTH, FOURIER_WIDTH + ATTN_WIDTH
    mix_scr[:, :a0] = f_ref[...]
    mix_scr[:, a0:r0] = a_ref[...]
    ones_bd = _head_sum_matrix()
    y = yf_ref[...] + yb_ref[...]
    mu = _head_sums(y, ones_bd) * (1.0 / RWKV_N)
    yc = y - mu
    var = _head_sums(yc * yc, ones_bd) * (1.0 / RWKV_N)
    yn = yc * lax.rsqrt(var + GN_EPS) * lg_ref[...] + lb_ref[...]
    mix_scr[:, r0:] = ((yn + bonus_ref[...]) * g_ref[...]).astype(BF16)
    o_ref[...] = x_ref[...] + gate_ref[...] * _dot(mix_scr[...], w_ref[...])


def _out_proj(f, a, yf, yb, bonus, g, ln_g, ln_b, x, gate, w, layer, seg):
    rows, d = x.shape
    tm = 512 if rows % 512 == 0 else 256
    row = lambda arr: pl.BlockSpec((tm, arr.shape[1]), lambda i: (i, 0))
    return pl.pallas_call(
        _outproj_body,
        grid=(rows // tm,),
        in_specs=[row(f), row(a), row(yf), row(yb), row(bonus), row(g),
                  _resident(ln_g.shape), _resident(ln_b.shape), row(x),
                  pl.BlockSpec((None, 1, d), _segment_index(seg, tm)),
                  _resident_layer(w.shape, layer)],
        out_specs=row(x),
        out_shape=jax.ShapeDtypeStruct(x.shape, F32),
        scratch_shapes=[pltpu.VMEM((tm, w.shape[1]), BF16)],
        compiler_params=_params(("parallel",), 48),
        name="out_proj",
    )(f, a, yf, yb, bonus, g, ln_g, ln_b, x, gate, w)


def _edge_masks(block_row, rows_lat, t_lat, t_ctx):
    in_lat = block_row < rows_lat
    length = jnp.where(in_lat, t_lat, t_ctx)
    off = jnp.where(in_lat, block_row, block_row - rows_lat)
    starts = (lax.rem(off, length) == 0).astype(F32)
    ends = (lax.rem(off + SEQ_ALIGN, length) == 0).astype(F32)
    sub = lax.broadcasted_iota(jnp.int32, (8, 1), 0)
    return 1.0 - starts * (sub == 0).astype(F32), 1.0 - ends * (sub == 7).astype(F32)


def _conv3_block(ext_ref, w_ref, b_ref, cols, r0, not_first8, not_last8):
    w0, w1, w2 = w_ref[0:1, cols], w_ref[1:2, cols], w_ref[2:3, cols]

    def rows(lo, hi, prev_mask=None, next_mask=None):
        at = lambda shift: ext_ref[HALO + r0 + shift + lo:HALO + r0 + shift + hi, :]
        prev, nxt = at(-1), at(1)
        if prev_mask is not None:
            prev = prev * prev_mask
        if next_mask is not None:
            nxt = nxt * next_mask
        return w0 * prev + w1 * at(0) + w2 * nxt

    n = SEQ_ALIGN
    out = jnp.concatenate([rows(0, 8, prev_mask=not_first8), rows(8, n - 8), rows(n - 8, n, next_mask=not_last8)],
                          axis=0)
    return out if b_ref is None else out + b_ref[:, cols]


def _ffn_body(x_ref, prev_ref, next_ref, g_ref, sc_ref, sh_ref, gate_ref, wa_ref, wg_ref,
              cwa_ref, cwg_ref, ba_ref, bg_ref, wd_ref, fg_ref, o_ref,
              h_scr, ua0_scr, ua1_scr, ug0_scr, ug1_scr, act_cur, act_new,
              *, tm, rows_lat, t_lat, t_ctx, final_norm):
    j = pl.program_id(1)
    nj = pl.num_programs(1) - 1

    @pl.when(j == 0)
    def _():
        norm = lambda x: _modulated_rmsnorm(x, g_ref[...], sc_ref[...], sh_ref[...]).astype(BF16)
        h_scr[0:HALO, :] = norm(prev_ref[...])
        h_scr[HALO:HALO + tm, :] = norm(x_ref[...])
        h_scr[HALO + tm:, :] = norm(next_ref[...])
        act_new[...] = jnp.zeros_like(act_new)
        o_ref[...] = jnp.zeros_like(o_ref)

    subs = [slice(s, s + FFN_SUB) for s in range(0, wa_ref.shape[1], FFN_SUB)]
    blocks = range(0, tm, SEQ_ALIGN)

    u_scr = ((ua0_scr, ug0_scr), (ua1_scr, ug1_scr))
    assert len(subs) == len(u_scr)

    def up_project(s):
        ua_scr, ug_scr = u_scr[s]
        ua_scr[...] = _dot(h_scr[...], wa_ref[:, subs[s]])
        ug_scr[...] = _dot(h_scr[...], wg_ref[:, subs[s]])

    def conv_gate(s):
        ua_scr, ug_scr = u_scr[s]
        for r0 in blocks:
            edge = _edge_masks(pl.program_id(0) * tm + r0, rows_lat, t_lat, t_ctx)
            ua = _conv3_block(ua_scr, cwa_ref, ba_ref, subs[s], r0, *edge)
            ug = _conv3_block(ug_scr, cwg_ref, bg_ref, subs[s], r0, *edge)
            act_new[r0:r0 + SEQ_ALIGN, subs[s]] = (ug * jax.nn.sigmoid(ug) * ua).astype(BF16)

    @pl.when(j < nj)
    def _():
        act_cur[...] = act_new[...]
        up_project(0)
        up_project(1)
        conv_gate(0)
        conv_gate(1)
        o_ref[...] += _dot(act_cur[...], wd_ref[...])

    @pl.when(j == nj)
    def _():
        out = x_ref[...] + gate_ref[...] * (o_ref[...] + _dot(act_new[...], wd_ref[...]))
        if final_norm:
            out = out * lax.rsqrt(jnp.mean(out * out, axis=-1, keepdims=True) + NORM_EPS) * fg_ref[...]
        o_ref[...] = out


def _conv_ffn(x, g, scale, shift, gate, wa, wg, cwa, cwg, ba, bg, wd, final_g, layer, seg, rows_lat, t_lat, t_ctx,
              final_norm):
    rows, d = x.shape
    tm = next(t for t in (1024, 512, 256) if rows % t == 0 and rows_lat % t == 0 and seg[1] % t == 0)
    tf = FFN_TILE
    nh = tm // HALO
    last = rows // HALO - 1
    nj = D_FF_PAD // tf
    seg = _segment_index(seg, tm)
    col = lambda r: pl.BlockSpec((r, tf), lambda i, j: (0, jnp.minimum(j, nj - 1)))
    up = pl.BlockSpec((None, None, d, tf), lambda i, j: (layer, jnp.minimum(j, nj - 1), 0, 0))
    return pl.pallas_call(
        functools.partial(_ffn_body, tm=tm, rows_lat=rows_lat, t_lat=t_lat, t_ctx=t_ctx, final_norm=final_norm),
        grid=(rows // tm, nj + 1),
        in_specs=[pl.BlockSpec((tm, d), lambda i, j: (i, 0), pipeline_mode=pl.Buffered(1)),
                  pl.BlockSpec((HALO, d), lambda i, j: (jnp.maximum(i * nh - 1, 0), 0)),
                  pl.BlockSpec((HALO, d), lambda i, j: (jnp.minimum((i + 1) * nh, last), 0)),
                  pl.BlockSpec((1, d), lambda i, j: (0, 0)),
                  pl.BlockSpec((None, 1, d), seg), pl.BlockSpec((None, 1, d), seg), pl.BlockSpec((None, 1, d), seg),
                  up, up, col(3), col(3), col(1), col(1),
                  pl.BlockSpec((None, tf, d), lambda i, j: (layer, jnp.maximum(j - 1, 0), 0)),
                  pl.BlockSpec((1, d), lambda i, j: (0, 0))],
        out_specs=pl.BlockSpec((tm, d), lambda i, j: (i, 0)),
        out_shape=jax.ShapeDtypeStruct(x.shape, F32),
        scratch_shapes=[pltpu.VMEM((tm + 2 * HALO, d), BF16),
                        pltpu.VMEM((tm + 2 * HALO, FFN_SUB), F32),
                        pltpu.VMEM((tm + 2 * HALO, FFN_SUB), F32),
                        pltpu.VMEM((tm + 2 * HALO, FFN_SUB), F32),
                        pltpu.VMEM((tm + 2 * HALO, FFN_SUB), F32),
                        pltpu.VMEM((tm, tf), BF16),
                        pltpu.VMEM((tm, tf), BF16)],
        compiler_params=_params(("parallel", "arbitrary"), 56),
        name="conv_ffn",
    )(x, x, x, g, scale, shift, gate, wa, wg, cwa, cwg, ba, bg, wd, final_g)


def _lora_weight(w2, a2, g2):
    c = RWKV_WIDTH
    wl = jnp.zeros((LORA_IN, 5 * c), F32)
    wl = wl.at[:DECAY_LORA, 0:c].set(w2[0]).at[:DECAY_LORA, c:2 * c].set(w2[1])
    wl = wl.at[DECAY_LORA:DECAY_LORA + ICLR_LORA, 2 * c:3 * c].set(a2[0])
    wl = wl.at[DECAY_LORA:DECAY_LORA + ICLR_LORA, 3 * c:4 * c].set(a2[1])
    return wl.at[DECAY_LORA + ICLR_LORA:, 4 * c:].set(g2)


LANES = 128


def _split_up_body(a_ref, g_ref, oa_ref, og_ref, *, nvalid):
    keep = pl.program_id(1) < nvalid
    oa_ref[...] = jnp.where(keep, a_ref[...], 0.0).astype(BF16)
    og_ref[...] = jnp.where(keep, g_ref[...], 0.0).astype(BF16)


def _split_ffn_up(ffn_up):
    depth, d, _ = ffn_up.shape
    nvalid = D_FF // LANES
    per_tile = FFN_TILE // LANES
    src = lambda half: pl.BlockSpec((None, d, LANES),
                                    lambda l, j: (l, 0, half * nvalid + jnp.minimum(j, nvalid - 1)))
    dst = pl.BlockSpec((None, None, d, LANES), lambda l, j: (l, j // per_tile, 0, j % per_tile))
    shape = jax.ShapeDtypeStruct((depth, D_FF_PAD // FFN_TILE, d, FFN_TILE), BF16)
    return pl.pallas_call(
        functools.partial(_split_up_body, nvalid=nvalid),
        grid=(depth, D_FF_PAD // LANES),
        in_specs=[src(0), src(1)], out_specs=[dst, dst], out_shape=[shape, shape],
        compiler_params=_params(("parallel", "parallel"), 32),
        name="split_ffn_up",
    )(ffn_up, ffn_up)


def _pad_down_body(w_ref, o_ref, *, tk):
    row = pl.program_id(1) * tk + lax.broadcasted_iota(jnp.int32, (tk, 1), 0)
    o_ref[...] = jnp.where(row < D_FF, w_ref[...], 0.0).astype(BF16)


def _pad_ffn_down(ffn_down):
    depth, _, d = ffn_down.shape
    tk = FFN_TILE
    return pl.pallas_call(
        functools.partial(_pad_down_body, tk=tk),
        grid=(depth, D_FF_PAD // tk),
        in_specs=[pl.BlockSpec((None, tk, d), lambda l, j: (l, j, 0))],
        out_specs=pl.BlockSpec((None, tk, d), lambda l, j: (l, j, 0)),
        out_shape=jax.ShapeDtypeStruct((depth, D_FF_PAD, d), BF16),
        compiler_params=_params(("parallel", "parallel"), 32),
        name="pad_ffn_down",
    )(ffn_down)


def _pad_cols(w):
    return jnp.pad(w, ((0, 0), (0, D_FF_PAD - D_FF)))


def _forward(x_lat, x_ctx, cache_k, cache_v, state, c, c_ctx, w_ada, b_ada, norm1_g, norm2_g, w_in, w_out,
             q_norm_g, k_norm_g, rw_conv, rw_w0, rw_w2, rw_a0, rw_a2, rw_g2, rw_kk, rw_ka, rw_rk,
             rw_lnx_g, rw_lnx_b, ffn_up, ffn_conv_w, ffn_conv_b, ffn_down, final_norm_g):
    n_lat, t_lat, d = x_lat.shape
    n_ctx, t_ctx, _ = x_ctx.shape
    depth = w_ada.shape[0]
    past = cache_k.shape[2]
    rows_lat, rows_ctx = n_lat * t_lat, n_ctx * t_ctx
    assert n_lat < MOD_ROWS
    assert t_lat % SEQ_ALIGN == 0 and t_ctx % SEQ_ALIGN == 0 and t_lat % GRID_W == 0

    cvec = jnp.concatenate([c, jnp.broadcast_to(c_ctx[None, :], (MOD_ROWS - n_lat, d))], axis=0)
    mod = _adaln_mod(cvec, w_ada, b_ada).reshape(depth, MOD_ROWS, 6, 1, d)

    ffn_wa, ffn_wg = _split_ffn_up(ffn_up)
    ffn_wd = _pad_ffn_down(ffn_down)
    rope_tabs = _rope_tables(t_lat)
    time_lat, chan_tab = _dft_tables(t_lat)
    time_ctx, _ = _dft_tables(t_ctx)
    zero_state = jnp.zeros((n_ctx, 1, 2, RWKV_HEADS, RWKV_N, RWKV_N), F32)
    row2 = lambda a: a.reshape(1, -1)

    cache = (cache_k.reshape(n_lat, depth, past, KV_WIDTH), cache_v.reshape(n_lat, depth, past, KV_WIDTH))

    def trunk_layer(x, l, latent):
        nseq, t = (n_lat, t_lat) if latent else (n_ctx, t_ctx)
        rows = nseq * t
        seg = (0, t_lat) if latent else (n_lat, rows)
        region_rows_lat = rows if latent else 0
        shift1, scale1, gate1, shift2, scale2, gate2 = (mod[l, :, i] for i in range(6))
        in_args = (x, row2(norm1_g[l]), scale1, shift1, w_in_b, l, seg, row2(q_norm_g[l]), row2(k_norm_g[l]))
        if latent:
            f, qb, kb, vb, rw = _in_proj(*in_args, rope_tabs)
            a_out = _attention(qb, kb, vb, nseq, t, cache[0], cache[1], l)
            k_norm = v = None
        else:
            f, qb, kb, vb, rw, k_norm, v = _in_proj(*in_args)
            a_out = _attention(qb, kb, vb, nseq, t)
        f_out = _fourier_mix(f, time_lat if latent else time_ctx, chan_tab, 0, nseq, t)

        r_, v_, kk, g_, bonus, lw, km, b_ = _rwkv_prep(
            rw, rw_conv[l], lora_w[l], rw_w0[l], rw_a0[l],
            row2(rw_kk[l]), row2(rw_ka[l]), row2(rw_rk[l]), region_rows_lat, t_lat, t_ctx)
        yf, yb, s_fin = _rwkv_scan(r_, v_, kk, lw, km, b_, state if latent else zero_state, l if latent else 0,
                                   0, nseq, t)
        x = _out_proj(f_out, a_out, yf, yb, bonus, g_, row2(rw_lnx_g[l]), row2(rw_lnx_b[l]), x, gate1,
                      w_out_b, l, seg)
        x = _conv_ffn(x, row2(norm2_g[l]), scale2, shift2, gate2, ffn_wa, ffn_wg,
                      _pad_cols(ffn_conv_w[l][:, :D_FF]), _pad_cols(ffn_conv_w[l][:, D_FF:]),
                      _pad_cols(row2(ffn_conv_b[l][:D_FF])), _pad_cols(row2(ffn_conv_b[l][D_FF:])),
                      ffn_wd, row2(final_norm_g), l, seg, region_rows_lat, t_lat, t_ctx, final_norm=l == depth - 1)
        return x, k_norm, v, s_fin

    w_in_b, w_out_b = w_in.astype(BF16), w_out.astype(BF16)
    lora_w = [_lora_weight(rw_w2[l], rw_a2[l], rw_g2[l]).astype(BF16) for l in range(depth)]
    xs, xc = x_lat.reshape(rows_lat, d), x_ctx.reshape(rows_ctx, d)
    new_k, new_v, new_s = [], [], []
    for l in range(depth):
        xc, k_norm, v_ctx, s_ctx = trunk_layer(xc, l, latent=False)
        new_k.append(k_norm.reshape(n_ctx, t_ctx, ATTN_KV_HEADS, HEAD_DIM))
        new_v.append(v_ctx.reshape(n_ctx, t_ctx, ATTN_KV_HEADS, HEAD_DIM))
        new_s.append(s_ctx)
        xs, _, _, _ = trunk_layer(xs, l, latent=True)

    y_lat, y_ctx = xs.reshape(n_lat, t_lat, d), xc.reshape(n_ctx, t_ctx, d)
    return (y_ctx, y_lat, jnp.stack(new_k, axis=1), jnp.stack(new_v, axis=1), jnp.stack(new_s, axis=1))


def kernel(x_prompt, x_sample, cache_attn_k, cache_attn_v, state_rwkv, c, c_ctx, w_ada, b_ada, norm1_g, norm2_g, w_in, w_out, q_norm_g, k_norm_g, rw_conv, rw_w0, rw_w2, rw_a0, rw_a2, rw_g2, rw_kk, rw_ka, rw_rk, rw_lnx_g, rw_lnx_b, ffn_up, ffn_conv_w, ffn_conv_b, ffn_down, final_norm_g):
    return _forward(x_sample, x_prompt, cache_attn_k, cache_attn_v, state_rwkv, c, c_ctx, w_ada, b_ada,
                    norm1_g, norm2_g, w_in, w_out, q_norm_g, k_norm_g, rw_conv, rw_w0, rw_w2, rw_a0, rw_a2,
                    rw_g2, rw_kk, rw_ka, rw_rk, rw_lnx_g, rw_lnx_b, ffn_up, ffn_conv_w, ffn_conv_b, ffn_down,
                    final_norm_g)
```

```python
import functools
import math

import jax
import jax.numpy as jnp
import numpy as np
from jax import lax
from jax.experimental import pallas as pl
from jax.experimental.pallas import tpu as pltpu

D_MODEL = 2048
GRID_W = 64
HEAD_DIM = 128
ATTN_HEADS = 8
ATTN_KV_HEADS = 2
KV_GROUP = ATTN_HEADS // ATTN_KV_HEADS
ATTN_WIDTH = ATTN_HEADS * HEAD_DIM
KV_WIDTH = ATTN_KV_HEADS * HEAD_DIM
FOURIER_WIDTH = 512
FOURIER_GROUPS = 4
FOURIER_GROUP_WIDTH = FOURIER_WIDTH // FOURIER_GROUPS
RWKV_WIDTH = 512
RWKV_N = 64
RWKV_HEADS = RWKV_WIDTH // RWKV_N
DECAY_LORA = 64
ICLR_LORA = 64
GATE_LORA = 128
LORA_IN = DECAY_LORA + ICLR_LORA + GATE_LORA
RWKV_IN = 3 * RWKV_WIDTH + LORA_IN
IN_WIDTH = FOURIER_WIDTH + ATTN_WIDTH + 2 * KV_WIDTH + RWKV_IN
D_FF = 5504
ROPE_THETA = 10000.0
NORM_EPS = 1e-6
GN_EPS = 64e-5

MOD_ROWS = 16
FFN_TILE = 512
FFN_SUB = 256
D_FF_PAD = -(-D_FF // FFN_TILE) * FFN_TILE
HALO = 16
CHUNK = 64
SCAN_SUB = 2
SEQ_ALIGN = 256
MIB = 2 ** 20

F32 = jnp.float32
BF16 = jnp.bfloat16


def _params(semantics, vmem_mib):
    return pltpu.CompilerParams(dimension_semantics=semantics, vmem_limit_bytes=vmem_mib * MIB)


def _resident(shape):
    return pl.BlockSpec(shape, lambda *_: (0,) * len(shape), pipeline_mode=pl.Buffered(1))


def _resident_layer(stacked_shape, layer):
    shape = stacked_shape[1:]
    return pl.BlockSpec((None,) + tuple(shape), lambda *_: (layer,) + (0,) * len(shape),
                        pipeline_mode=pl.Buffered(1))


def _dot(a, b):
    return jnp.dot(a, b, preferred_element_type=F32)


def _dot_nt(a, b):
    return lax.dot_general(a, b, (((1,), (1,)), ((), ())), preferred_element_type=F32)


def _dot_tn(a, b):
    return lax.dot_general(a, b, (((0,), (0,)), ((), ())), preferred_element_type=F32)


def _modulated_rmsnorm(x, g, scale, shift):
    ms = jnp.mean(x * x, axis=-1, keepdims=True)
    return (x * lax.rsqrt(ms + NORM_EPS) * g) * (1.0 + scale) + shift


def _mod_body(c_ref, w_ref, b_ref, o_ref):
    c = c_ref[...]
    s = (c * jax.nn.sigmoid(c)).astype(BF16)
    o_ref[...] = _dot(s, w_ref[...].astype(BF16)) + b_ref[...]


def _adaln_mod(cvec, w_ada, b_ada):
    depth, d, n = w_ada.shape
    tn = 1024
    return pl.pallas_call(
        _mod_body,
        grid=(depth, n // tn),
        in_specs=[pl.BlockSpec((MOD_ROWS, d), lambda l, j: (0, 0)),
                  pl.BlockSpec((None, d, tn), lambda l, j: (l, 0, j)),
                  pl.BlockSpec((None, 1, tn), lambda l, j: (l, 0, j))],
        out_specs=pl.BlockSpec((None, MOD_ROWS, tn), lambda l, j: (l, 0, j)),
        out_shape=jax.ShapeDtypeStruct((depth, MOD_ROWS, n), F32),
        compiler_params=_params(("parallel", "parallel"), 40),
        name="adaln_mod",
    )(cvec, w_ada, b_ada.reshape(depth, 1, n))


Q_COL = FOURIER_WIDTH
K_COL = Q_COL + ATTN_WIDTH
V_COL = K_COL + KV_WIDTH
RW_COL = V_COL + KV_WIDTH


def _inproj_body(*refs, rope):
    x_ref, g_ref, sc_ref, sh_ref, w_ref, gq_ref, gk_ref = refs[:7]
    if rope:
        cos_ref, sin_ref, f_ref, q_ref, k_ref, v_ref, rw_ref = refs[7:]
        cos, sin = cos_ref[...], sin_ref[...]
        lane = lax.broadcasted_iota(jnp.int32, (1, HEAD_DIM), 1)
        low = (lane % (HEAD_DIM // 2)) < (HEAD_DIM // 4)
    else:
        f_ref, q_ref, k_ref, v_ref, rw_ref, kn_ref, vf_ref = refs[7:]

    h = _modulated_rmsnorm(x_ref[...], g_ref[...], sc_ref[...], sh_ref[...]).astype(BF16)
    proj = lambda start, width: _dot(h, w_ref[:, start:start + width])
    q, k, v = proj(Q_COL, ATTN_WIDTH), proj(K_COL, KV_WIDTH), proj(V_COL, KV_WIDTH)
    f_ref[...] = proj(0, FOURIER_WIDTH).astype(BF16)
    rw_ref[...] = proj(RW_COL, RWKV_IN)

    def head_norm(xh, g):
        ms = jnp.mean(xh * xh, axis=-1, keepdims=True)
        return xh * lax.rsqrt(ms + NORM_EPS) * g

    def rotate(xh):
        partner = jnp.where(low, pltpu.roll(xh, HEAD_DIM - HEAD_DIM // 4, 1), pltpu.roll(xh, HEAD_DIM // 4, 1))
        return xh * cos + partner * sin

    scale = HEAD_DIM ** -0.5
    for hd in range(ATTN_HEADS):
        sl = slice(hd * HEAD_DIM, (hd + 1) * HEAD_DIM)
        qh = head_norm(q[:, sl], gq_ref[...])
        if rope:
            qh = rotate(qh)
        q_ref[:, sl] = (qh * scale).astype(BF16)
    for j in range(ATTN_KV_HEADS):
        sl = slice(j * HEAD_DIM, (j + 1) * HEAD_DIM)
        kh = head_norm(k[:, sl], gk_ref[...])
        if rope:
            kh = rotate(kh)
        else:
            kn_ref[:, sl] = kh
        k_ref[:, sl] = kh.astype(BF16)
    v_ref[...] = v.astype(BF16)
    if not rope:
        vf_ref[...] = v


def _segment_index(seg, tm):
    seg0, seg_rows = seg
    assert seg_rows % tm == 0
    return lambda i, *_: (seg0 + i * tm // seg_rows, 0, 0)


def _in_proj(x, g, scale, shift, w, layer, seg, gq, gk, rope_tabs=None):
    rows, d = x.shape
    tm = 512 if rows % 512 == 0 else 256
    seg = _segment_index(seg, tm)
    rope = rope_tabs is not None
    in_specs = [pl.BlockSpec((tm, d), lambda i: (i, 0)),
                _resident((1, d)),
                pl.BlockSpec((None, 1, d), seg),
                pl.BlockSpec((None, 1, d), seg),
                _resident_layer(w.shape, layer),
                _resident((1, HEAD_DIM)), _resident((1, HEAD_DIM))]
    args = [x, g, scale, shift, w, gq, gk]
    outs = [(FOURIER_WIDTH, BF16), (ATTN_WIDTH, BF16), (KV_WIDTH, BF16), (KV_WIDTH, BF16), (RWKV_IN, F32)]
    if rope:
        nt = rope_tabs[0].shape[0] // tm
        in_specs += [pl.BlockSpec((tm, HEAD_DIM), lambda i: (i % nt, 0))] * 2
        args += list(rope_tabs)
    else:
        outs += [(KV_WIDTH, F32), (KV_WIDTH, F32)]
    return pl.pallas_call(
        functools.partial(_inproj_body, rope=rope),
        grid=(rows // tm,),
        in_specs=in_specs,
        out_specs=[pl.BlockSpec((tm, width), lambda i: (i, 0)) for width, _ in outs],
        out_shape=[jax.ShapeDtypeStruct((rows, width), dt) for width, dt in outs],
        compiler_params=_params(("parallel",), 56),
        name="in_proj_rope" if rope else "in_proj",
    )(*args)


def _dft_tables(t):
    def angles(n):
        i = np.arange(n, dtype=np.int64)
        return (2.0 * math.pi / n) * ((i[:, None] * i[None, :]) % n)
    at = angles(t)
    time_tab = np.concatenate([np.cos(at), -np.sin(at)], axis=1).astype(np.float32)
    ac = angles(FOURIER_GROUP_WIDTH)
    eye = np.eye(FOURIER_GROUPS)
    chan_tab = np.concatenate([np.kron(eye, np.cos(ac)), np.kron(eye, np.sin(ac))], axis=1).astype(np.float32)
    return jnp.asarray(time_tab).astype(BF16), jnp.asarray(chan_tab).astype(BF16)


def _fourier_body(u_ref, ct_ref, cc_ref, o_ref, ab_scr, *, t, norm):
    @pl.when(pl.program_id(1) == 0)
    def _():
        ab = _dot(u_ref[...], cc_ref[...])
        ab_scr[0:t, :] = ab[:, :FOURIER_WIDTH].astype(BF16)
        ab_scr[t:2 * t, :] = ab[:, FOURIER_WIDTH:].astype(BF16)
    o_ref[...] = (_dot(ct_ref[...], ab_scr[...]) * norm).astype(o_ref.dtype)


def _fourier_mix(u, time_tab, chan_tab, row0, nseq, t):
    tm = min(t, 512)
    nt = t // tm
    seq0 = row0 // t
    return pl.pallas_call(
        functools.partial(_fourier_body, t=t, norm=1.0 / math.sqrt(t * FOURIER_GROUP_WIDTH)),
        grid=(nseq, nt),
        in_specs=[pl.BlockSpec((t, FOURIER_WIDTH), lambda b, i: (seq0 + b, 0)),
                  pl.BlockSpec((tm, 2 * t), lambda b, i: (i, 0)),
                  _resident(chan_tab.shape)],
        out_specs=pl.BlockSpec((tm, FOURIER_WIDTH), lambda b, i: (b * nt + i, 0)),
        out_shape=jax.ShapeDtypeStruct((nseq * t, FOURIER_WIDTH), BF16),
        scratch_shapes=[pltpu.VMEM((2 * t, FOURIER_WIDTH), BF16)],
        compiler_params=_params(("parallel", "arbitrary"), 40),
        name=f"fourier_mix_t{t}",
    )(u, time_tab, chan_tab)


def _rope_tables(t):
    pos = jnp.arange(t, dtype=jnp.int32)
    rows = (pos // GRID_W).astype(F32)
    cols = (pos % GRID_W).astype(F32)
    half = HEAD_DIM // 2
    inv = 1.0 / (ROPE_THETA ** (jnp.arange(0, half, 2, dtype=F32) / half))
    def tab(p):
        ang = p[:, None] * inv[None, :]
        return (jnp.concatenate([jnp.cos(ang), jnp.cos(ang)], -1),
                jnp.concatenate([-jnp.sin(ang), jnp.sin(ang)], -1))
    cr, sr = tab(rows)
    cc, sc = tab(cols)
    return jnp.concatenate([cr, cc], -1), jnp.concatenate([sr, sc], -1)


def _attn_body(*refs, cached):
    if cached:
        q_ref, k_ref, v_ref, kc_ref, vc_ref, o_ref = refs
    else:
        q_ref, k_ref, v_ref, o_ref = refs
    head_cols = lambda h: slice(h * HEAD_DIM, (h + 1) * HEAD_DIM)
    if cached:
        cache = [(kc_ref[:, head_cols(j)].astype(BF16), vc_ref[:, head_cols(j)].astype(BF16))
                 for j in range(ATTN_KV_HEADS)]

    def scores(h):
        j = h // KV_GROUP
        qh = q_ref[:, head_cols(h)]
        return _dot_nt(qh, k_ref[:, head_cols(j)]), (_dot_nt(qh, cache[j][0]) if cached else None)

    pending = scores(0)
    for h in range(ATTN_HEADS):
        s, sc = pending
        if h + 1 < ATTN_HEADS:
            pending = scores(h + 1)
        j = h // KV_GROUP
        m = jnp.max(s, axis=-1, keepdims=True)
        if cached:
            m = jnp.maximum(m, jnp.max(sc, axis=-1, keepdims=True))
        p = jnp.exp(s - m)
        l = jnp.sum(p, axis=-1, keepdims=True)
        acc = _dot(p.astype(BF16), v_ref[:, head_cols(j)])
        if cached:
            pc = jnp.exp(sc - m)
            l = l + jnp.sum(pc, axis=-1, keepdims=True)
            acc = acc + _dot(pc.astype(BF16), cache[j][1])
        o_ref[:, head_cols(h)] = (acc / l).astype(o_ref.dtype)


def _attention(q, k, v, nseq, t, cache_k=None, cache_v=None, layer=0):
    tq = 512 if t % 512 == 0 else 256
    nq = t // tq
    cached = cache_k is not None
    in_specs = [pl.BlockSpec((tq, ATTN_WIDTH), lambda b, i: (b * nq + i, 0)),
                pl.BlockSpec((t, KV_WIDTH), lambda b, i: (b, 0)),
                pl.BlockSpec((t, KV_WIDTH), lambda b, i: (b, 0))]
    args = [q, k, v]
    if cached:
        past = cache_k.shape[2]
        in_specs += [pl.BlockSpec((None, None, past, KV_WIDTH), lambda b, i: (b, layer, 0, 0))] * 2
        args += [cache_k, cache_v]
    return pl.pallas_call(
        functools.partial(_attn_body, cached=cached),
        grid=(nseq, nq),
        in_specs=in_specs,
        out_specs=pl.BlockSpec((tq, ATTN_WIDTH), lambda b, i: (b * nq + i, 0)),
        out_shape=jax.ShapeDtypeStruct((nseq * t, ATTN_WIDTH), BF16),
        compiler_params=_params(("parallel", "parallel"), 48),
        name="attention_cached" if cached else "attention",
    )(*args)


def _head_sum_matrix():
    i = lax.broadcasted_iota(jnp.int32, (RWKV_WIDTH, RWKV_WIDTH), 0) // RWKV_N
    j = lax.broadcasted_iota(jnp.int32, (RWKV_WIDTH, RWKV_WIDTH), 1) // RWKV_N
    return (i == j).astype(BF16)


def _bf16_pieces(x, n):
    pieces = []
    for _ in range(n - 1):
        p = x.astype(BF16)
        pieces.append(p)
        x = x - p.astype(F32)
    return pieces + [x.astype(BF16)]


def _head_sums(x, ones_bd):
    hi, lo = _bf16_pieces(x, 2)
    return _dot(hi, ones_bd) + _dot(lo, ones_bd)


def _rwkv_prep_body(rw_ref, prev_ref, next_ref, cw_ref, wl_ref, w0_ref, a0_ref, kks_ref, ka_ref, rk_ref,
                    r_ref, v_ref, kk_ref, g_ref, bonus_ref, lw_ref, km_ref, b_ref, ext_scr,
                    *, tm, rows_lat, t_lat, t_ctx):
    ext_scr[0:HALO, :] = prev_ref[...]
    ext_scr[HALO:HALO + tm, :] = rw_ref[...]
    ext_scr[HALO + tm:, :] = next_ref[...]
    edges = _edge_masks(pl.program_id(0) * tm, rows_lat, t_lat, t_ctx)
    z = _conv3_block(ext_scr, cw_ref, None, slice(None), 0, *edges)
    c = RWKV_WIDTH
    r, k, v = z[:, :c], z[:, c:2 * c], z[:, 2 * c:3 * c]
    zl = z[:, 3 * c:]
    lane = lax.broadcasted_iota(jnp.int32, (1, LORA_IN), 1)
    lora_in = jnp.where(lane < DECAY_LORA, jnp.tanh(zl),
                        jnp.where(lane < DECAY_LORA + ICLR_LORA, zl, jax.nn.sigmoid(zl)))
    lora = _dot(lora_in.astype(BF16), wl_ref[...])
    ones_bd = _head_sum_matrix()
    kk = k * kks_ref[...]
    kk = kk * lax.rsqrt(_head_sums(kk * kk, ones_bd) + 1e-12)
    r_ref[...] = r.astype(r_ref.dtype)
    v_ref[...] = v.astype(v_ref.dtype)
    kk_ref[...] = kk.astype(kk_ref.dtype)
    g_ref[...] = lora[:, 4 * c:5 * c]
    kmod_sum = jnp.zeros_like(k)
    for d in range(2):
        wpre = w0_ref[d:d + 1, :] + lora[:, d * c:(d + 1) * c]
        lw_ref[d] = -math.exp(-0.5) * jax.nn.sigmoid(wpre)
        a = jax.nn.sigmoid(a0_ref[d:d + 1, :] + lora[:, (2 + d) * c:(3 + d) * c])
        kmod = k * (1.0 + (a - 1.0) * ka_ref[...])
        km_ref[d] = kmod.astype(km_ref.dtype)
        b_ref[d] = (kk * a).astype(b_ref.dtype)
        kmod_sum = kmod_sum + kmod
    bonus_ref[...] = _head_sums(r * kmod_sum * rk_ref[...], ones_bd) * v


def _rwkv_prep(rw, conv_w, lora_w, w0, a0, kk_scale, ka, rk, rows_lat, t_lat, t_ctx):
    rows = rw.shape[0]
    tm = SEQ_ALIGN
    nh = tm // HALO
    last = rows // HALO - 1
    c = RWKV_WIDTH
    one = lambda: pl.BlockSpec((tm, c), lambda i: (i, 0))
    two = lambda: pl.BlockSpec((2, tm, c), lambda i: (0, i, 0))
    return pl.pallas_call(
        functools.partial(_rwkv_prep_body, tm=tm, rows_lat=rows_lat, t_lat=t_lat, t_ctx=t_ctx),
        grid=(rows // tm,),
        in_specs=[pl.BlockSpec((tm, RWKV_IN), lambda i: (i, 0)),
                  pl.BlockSpec((HALO, RWKV_IN), lambda i: (jnp.maximum(i * nh - 1, 0), 0)),
                  pl.BlockSpec((HALO, RWKV_IN), lambda i: (jnp.minimum((i + 1) * nh, last), 0)),
                  _resident(conv_w.shape), _resident(lora_w.shape), _resident(w0.shape), _resident(a0.shape),
                  _resident(kk_scale.shape), _resident(ka.shape), _resident(rk.shape)],
        out_specs=[one(), one(), one(), one(), one(), two(), two(), two()],
        out_shape=[jax.ShapeDtypeStruct((rows, c), dt) for dt in (BF16, BF16, BF16, F32, F32)]
        + [jax.ShapeDtypeStruct((2, rows, c), dt) for dt in (F32, BF16, BF16)],
        scratch_shapes=[pltpu.VMEM((tm + 2 * HALO, RWKV_IN), F32)],
        compiler_params=_params(("parallel",), 48),
        name="rwkv_prep",
    )(rw, rw, rw, conv_w, lora_w, w0, a0, kk_scale, ka, rk)


def _scan_direction_operands(r_ref, v_ref, kk_ref, lw_ref, km_ref, b_ref, rows, backward):
    n = CHUNK
    row = lax.broadcasted_iota(jnp.int32, (n, n), 0)
    col = lax.broadcasted_iota(jnp.int32, (n, n), 1)
    upto = (col >= row) if backward else (col <= row)
    lw = lw_ref[rows, :]
    tri = upto.astype(BF16)
    cs = sum(_dot(tri, piece) for piece in _bf16_pieces(lw, 3))
    tot = jnp.sum(lw, axis=0, keepdims=True)
    grow = jnp.exp(-cs)
    to_end = jnp.exp(tot - cs)
    b = b_ref[rows, :]
    km = km_ref[rows, :]
    return dict(
        kkt=(kk_ref[rows, :] * jnp.exp(cs - lw)).astype(BF16),
        rt=(r_ref[rows, :] * jnp.exp(cs)).astype(BF16),
        bt=(b * grow).astype(BF16), kt=(km * grow).astype(BF16),
        bh=(b * to_end).astype(BF16), kh=(km * to_end).astype(BF16),
        vb=v_ref[rows, :].astype(BF16), w_all=jnp.exp(tot))


def _rwkv_scan_body(rf_ref, vf_ref, kkf_ref, lwf_ref, kmf_ref, bf_ref,
                    rb_ref, vb_ref, kkb_ref, lwb_ref, kmb_ref, bb_ref, s0_ref,
                    yf_ref, yb_ref, sfin_ref, s_scr):
    c = pl.program_id(1)

    @pl.when(c == 0)
    def _():
        s_scr[...] = s0_ref[...]

    n = CHUNK
    row = lax.broadcasted_iota(jnp.int32, (2 * n, 2 * n), 0)
    col = lax.broadcasted_iota(jnp.int32, (2 * n, 2 * n), 1)
    t_idx, s_idx, read_rows = row % n, col % n, row >= n
    eye = (lax.broadcasted_iota(jnp.int32, (n, n), 0) == lax.broadcasted_iota(jnp.int32, (n, n), 1)).astype(F32)
    same_step = jnp.logical_and(read_rows, s_idx == t_idx)
    masks = (jnp.logical_or(s_idx < t_idx, same_step), jnp.logical_or(s_idx > t_idx, same_step))
    y_refs = (yf_ref, yb_ref)
    in_refs = ((rf_ref, vf_ref, kkf_ref, lwf_ref, kmf_ref, bf_ref), (rb_ref, vb_ref, kkb_ref, lwb_ref, kmb_ref, bb_ref))
    rows = [(slice(k * n, (k + 1) * n), slice((SCAN_SUB - 1 - k) * n, (SCAN_SUB - k) * n)) for k in range(SCAN_SUB)]
    ops = [[_scan_direction_operands(*in_refs[d], rows[k][d], backward=d == 1) for d in range(2)]
           for k in range(SCAN_SUB)]
    pairs = [(d, h) for d in range(2) for h in range(RWKV_HEADS)]
    chains = [(k, d, h) for k in range(SCAN_SUB) for d, h in pairs]
    head = lambda k, d, name, h: ops[k][d][name][:, h * RWKV_N:(h + 1) * RWKV_N]
    cat = lambda a, b: jnp.concatenate([a, b], axis=0)

    lhs = [cat(head(k, d, "kkt", h), head(k, d, "rt", h)) for k, d, h in chains]
    coef = [jnp.where(masks[d], _dot_nt(l, cat(head(k, d, "bt", h), head(k, d, "kt", h))), 0.0)
            for l, (k, d, h) in zip(lhs, chains)]
    n_mat = [a[:n, :n] for a in coef]
    akv = [_dot(a[:n, n:].astype(BF16), head(k, d, "vb", h)) for a, (k, d, h) in zip(coef, chains)]
    x = [eye - m for m in n_mat]
    p = [_dot(m.astype(BF16), m.astype(BF16)) for m in n_mat]
    steps = int(math.log2(n)) - 1
    for s in range(steps):
        pb = [q.astype(BF16) for q in p]
        x = [xi + _dot(xi.astype(BF16), q) for xi, q in zip(x, pb)]
        if s + 1 < steps:
            p = [_dot(q, q) for q in pb]

    for k in range(SCAN_SUB):
        mine = slice(k * len(pairs), (k + 1) * len(pairs))
        state = [s_scr[d, h] for d, h in pairs]
        read = [_dot_nt(l, s.astype(BF16)) for l, s in zip(lhs[mine], state)]
        u = [_dot(xi.astype(BF16), (-(rd[:n] + ak)).astype(BF16)) for xi, rd, ak in zip(x[mine], read, akv[mine])]
        uv = [cat(ui.astype(BF16), head(k, d, "vb", h)) for ui, (d, h) in zip(u, pairs)]
        for (d, h), a, rd, uvi, s in zip(pairs, coef[mine], read, uv, state):
            sl = slice(h * RWKV_N, (h + 1) * RWKV_N)
            y_refs[d][rows[k][d], sl] = rd[n:] + _dot(a[n:].astype(BF16), uvi)
            s_scr[d, h] = (s * ops[k][d]["w_all"][:, sl]
                           + _dot_tn(uvi, cat(head(k, d, "bh", h), head(k, d, "kh", h))))

    @pl.when(c == pl.num_programs(1) - 1)
    def _():
        sfin_ref[...] = s_scr[...]


def _rwkv_scan(r, v, kk, lw, km, b, states, layer, row0, nseq, t):
    n = CHUNK * SCAN_SUB
    assert t % n == 0 and row0 % n == 0
    nc = t // n
    c0 = row0 // n
    c = RWKV_WIDTH
    fwd = lambda bb, cc: c0 + bb * nc + cc
    bwd = lambda bb, cc: c0 + bb * nc + nc - 1 - cc
    one = lambda chunk: pl.BlockSpec((n, c), lambda bb, cc: (chunk(bb, cc), 0))
    two = lambda chunk, d: pl.BlockSpec((None, n, c), lambda bb, cc: (d, chunk(bb, cc), 0))
    state = pl.BlockSpec((None, 2, RWKV_HEADS, RWKV_N, RWKV_N), lambda bb, cc: (bb, 0, 0, 0, 0))
    state_in = pl.BlockSpec((None, None, 2, RWKV_HEADS, RWKV_N, RWKV_N), lambda bb, cc: (bb, layer, 0, 0, 0, 0))
    y_shape = jax.ShapeDtypeStruct((nseq * t, c), F32)
    return pl.pallas_call(
        _rwkv_scan_body,
        grid=(nseq, nc),
        in_specs=[one(fwd), one(fwd), one(fwd), two(fwd, 0), two(fwd, 0), two(fwd, 0),
                  one(bwd), one(bwd), one(bwd), two(bwd, 1), two(bwd, 1), two(bwd, 1), state_in],
        out_specs=[pl.BlockSpec((n, c), lambda bb, cc: (bb * nc + cc, 0)),
                   pl.BlockSpec((n, c), lambda bb, cc: (bb * nc + nc - 1 - cc, 0)), state],
        out_shape=[y_shape, y_shape, jax.ShapeDtypeStruct((nseq, 2, RWKV_HEADS, RWKV_N, RWKV_N), F32)],
        scratch_shapes=[pltpu.VMEM((2, RWKV_HEADS, RWKV_N, RWKV_N), F32)],
        compiler_params=_params(("parallel", "arbitrary"), 32),
        name=f"rwkv_scan_t{t}",
    )(r, v, kk, lw, km, b, r, v, kk, lw, km, b, states)


def _outproj_body(f_ref, a_ref, yf_ref, yb_ref, bonus_ref, g_ref, lg_ref, lb_ref, x_ref, gate_ref, w_ref, o_ref,
                  mix_scr):
    a0, r0 = FOURIER_WIDTH, FOURIER_WIDTH + ATTN_WIDTH
    mix_scr[:, :a0] = f_ref[...]
    mix_scr[:, a0:r0] = a_ref[...]
    ones_bd = _head_sum_matrix()
    y = yf_ref[...] + yb_ref[...]
    mu = _head_sums(y, ones_bd) * (1.0 / RWKV_N)
    yc = y - mu
    var = _head_sums(yc * yc, ones_bd) * (1.0 / RWKV_N)
    yn = yc * lax.rsqrt(var + GN_EPS) * lg_ref[...] + lb_ref[...]
    mix_scr[:, r0:] = ((yn + bonus_ref[...]) * g_ref[...]).astype(BF16)
    o_ref[...] = x_ref[...] + gate_ref[...] * _dot(mix_scr[...], w_ref[...])


def _out_proj(f, a, yf, yb, bonus, g, ln_g, ln_b, x, gate, w, layer, seg):
    rows, d = x.shape
    tm = 512 if rows % 512 == 0 else 256
    row = lambda arr: pl.BlockSpec((tm, arr.shape[1]), lambda i: (i, 0))
    return pl.pallas_call(
        _outproj_body,
        grid=(rows // tm,),
        in_specs=[row(f), row(a), row(yf), row(yb), row(bonus), row(g),
                  _resident(ln_g.shape), _resident(ln_b.shape), row(x),
                  pl.BlockSpec((None, 1, d), _segment_index(seg, tm)),
                  _resident_layer(w.shape, layer)],
        out_specs=row(x),
        out_shape=jax.ShapeDtypeStruct(x.shape, F32),
        scratch_shapes=[pltpu.VMEM((tm, w.shape[1]), BF16)],
        compiler_params=_params(("parallel",), 48),
        name="out_proj",
    )(f, a, yf, yb, bonus, g, ln_g, ln_b, x, gate, w)


def _edge_masks(block_row, rows_lat, t_lat, t_ctx):
    in_lat = block_row < rows_lat
    length = jnp.where(in_lat, t_lat, t_ctx)
    off = jnp.where(in_lat, block_row, block_row - rows_lat)
    starts = (lax.rem(off, length) == 0).astype(F32)
    ends = (lax.rem(off + SEQ_ALIGN, length) == 0).astype(F32)
    sub = lax.broadcasted_iota(jnp.int32, (8, 1), 0)
    return 1.0 - starts * (sub == 0).astype(F32), 1.0 - ends * (sub == 7).astype(F32)


def _conv3_block(ext_ref, w_ref, b_ref, cols, r0, not_first8, not_last8):
    w0, w1, w2 = w_ref[0:1, cols], w_ref[1:2, cols], w_ref[2:3, cols]

    def rows(lo, hi, prev_mask=None, next_mask=None):
        at = lambda shift: ext_ref[HALO + r0 + shift + lo:HALO + r0 + shift + hi, :]
        prev, nxt = at(-1), at(1)
        if prev_mask is not None:
            prev = prev * prev_mask
        if next_mask is not None:
            nxt = nxt * next_mask
        return w0 * prev + w1 * at(0) + w2 * nxt

    n = SEQ_ALIGN
    out = jnp.concatenate([rows(0, 8, prev_mask=not_first8), rows(8, n - 8), rows(n - 8, n, next_mask=not_last8)],
                          axis=0)
    return out if b_ref is None else out + b_ref[:, cols]


def _ffn_body(x_ref, prev_ref, next_ref, g_ref, sc_ref, sh_ref, gate_ref, wa_ref, wg_ref,
              cwa_ref, cwg_ref, ba_ref, bg_ref, wd_ref, fg_ref, o_ref,
              h_scr, ua0_scr, ua1_scr, ug0_scr, ug1_scr, act_cur, act_new,
              *, tm, rows_lat, t_lat, t_ctx, final_norm):
    j = pl.program_id(1)
    nj = pl.num_programs(1) - 1

    @pl.when(j == 0)
    def _():
        norm = lambda x: _modulated_rmsnorm(x, g_ref[...], sc_ref[...], sh_ref[...]).astype(BF16)
        h_scr[0:HALO, :] = norm(prev_ref[...])
        h_scr[HALO:HALO + tm, :] = norm(x_ref[...])
        h_scr[HALO + tm:, :] = norm(next_ref[...])
        act_new[...] = jnp.zeros_like(act_new)
        o_ref[...] = jnp.zeros_like(o_ref)

    subs = [slice(s, s + FFN_SUB) for s in range(0, wa_ref.shape[1], FFN_SUB)]
    blocks = range(0, tm, SEQ_ALIGN)

    u_scr = ((ua0_scr, ug0_scr), (ua1_scr, ug1_scr))
    assert len(subs) == len(u_scr)

    def up_project(s):
        ua_scr, ug_scr = u_scr[s]
        ua_scr[...] = _dot(h_scr[...], wa_ref[:, subs[s]])
        ug_scr[...] = _dot(h_scr[...], wg_ref[:, subs[s]])

    def conv_gate(s):
        ua_scr, ug_scr = u_scr[s]
        for r0 in blocks:
            edge = _edge_masks(pl.program_id(0) * tm + r0, rows_lat, t_lat, t_ctx)
            ua = _conv3_block(ua_scr, cwa_ref, ba_ref, subs[s], r0, *edge)
            ug = _conv3_block(ug_scr, cwg_ref, bg_ref, subs[s], r0, *edge)
            act_new[r0:r0 + SEQ_ALIGN, subs[s]] = (ug * jax.nn.sigmoid(ug) * ua).astype(BF16)

    @pl.when(j < nj)
    def _():
        act_cur[...] = act_new[...]
        up_project(0)
        up_project(1)
        conv_gate(0)
        conv_gate(1)
        o_ref[...] += _dot(act_cur[...], wd_ref[...])

    @pl.when(j == nj)
    def _():
        out = x_ref[...] + gate_ref[...] * (o_ref[...] + _dot(act_new[...], wd_ref[...]))
        if final_norm:
            out = out * lax.rsqrt(jnp.mean(out * out, axis=-1, keepdims=True) + NORM_EPS) * fg_ref[...]
        o_ref[...] = out


def _conv_ffn(x, g, scale, shift, gate, wa, wg, cwa, cwg, ba, bg, wd, final_g, layer, seg, rows_lat, t_lat, t_ctx,
              final_norm):
    rows, d = x.shape
    tm = next(t for t in (1024, 512, 256) if rows % t == 0 and rows_lat % t == 0 and seg[1] % t == 0)
    tf = FFN_TILE
    nh = tm // HALO
    last = rows // HALO - 1
    nj = D_FF_PAD // tf
    seg = _segment_index(seg, tm)
    col = lambda r: pl.BlockSpec((r, tf), lambda i, j: (0, jnp.minimum(j, nj - 1)))
    up = pl.BlockSpec((None, None, d, tf), lambda i, j: (layer, jnp.minimum(j, nj - 1), 0, 0))
    return pl.pallas_call(
        functools.partial(_ffn_body, tm=tm, rows_lat=rows_lat, t_lat=t_lat, t_ctx=t_ctx, final_norm=final_norm),
        grid=(rows // tm, nj + 1),
        in_specs=[pl.BlockSpec((tm, d), lambda i, j: (i, 0), pipeline_mode=pl.Buffered(1)),
                  pl.BlockSpec((HALO, d), lambda i, j: (jnp.maximum(i * nh - 1, 0), 0)),
                  pl.BlockSpec((HALO, d), lambda i, j: (jnp.minimum((i + 1) * nh, last), 0)),
                  pl.BlockSpec((1, d), lambda i, j: (0, 0)),
                  pl.BlockSpec((None, 1, d), seg), pl.BlockSpec((None, 1, d), seg), pl.BlockSpec((None, 1, d), seg),
                  up, up, col(3), col(3), col(1), col(1),
                  pl.BlockSpec((None, tf, d), lambda i, j: (layer, jnp.maximum(j - 1, 0), 0)),
                  pl.BlockSpec((1, d), lambda i, j: (0, 0))],
        out_specs=pl.BlockSpec((tm, d), lambda i, j: (i, 0)),
        out_shape=jax.ShapeDtypeStruct(x.shape, F32),
        scratch_shapes=[pltpu.VMEM((tm + 2 * HALO, d), BF16),
                        pltpu.VMEM((tm + 2 * HALO, FFN_SUB), F32),
                        pltpu.VMEM((tm + 2 * HALO, FFN_SUB), F32),
                        pltpu.VMEM((tm + 2 * HALO, FFN_SUB), F32),
                        pltpu.VMEM((tm + 2 * HALO, FFN_SUB), F32),
                        pltpu.VMEM((tm, tf), BF16),
                        pltpu.VMEM((tm, tf), BF16)],
        compiler_params=_params(("parallel", "arbitrary"), 56),
        name="conv_ffn",
    )(x, x, x, g, scale, shift, gate, wa, wg, cwa, cwg, ba, bg, wd, final_g)


def _lora_weight(w2, a2, g2):
    c = RWKV_WIDTH
    wl = jnp.zeros((LORA_IN, 5 * c), F32)
    wl = wl.at[:DECAY_LORA, 0:c].set(w2[0]).at[:DECAY_LORA, c:2 * c].set(w2[1])
    wl = wl.at[DECAY_LORA:DECAY_LORA + ICLR_LORA, 2 * c:3 * c].set(a2[0])
    wl = wl.at[DECAY_LORA:DECAY_LORA + ICLR_LORA, 3 * c:4 * c].set(a2[1])
    return wl.at[DECAY_LORA + ICLR_LORA:, 4 * c:].set(g2)


LANES = 128


def _split_up_body(a_ref, g_ref, oa_ref, og_ref, *, nvalid):
    keep = pl.program_id(1) < nvalid
    oa_ref[...] = jnp.where(keep, a_ref[...], 0.0).astype(BF16)
    og_ref[...] = jnp.where(keep, g_ref[...], 0.0).astype(BF16)


def _split_ffn_up(ffn_up):
    depth, d, _ = ffn_up.shape
    nvalid = D_FF // LANES
    per_tile = FFN_TILE // LANES
    src = lambda half: pl.BlockSpec((None, d, LANES),
                                    lambda l, j: (l, 0, half * nvalid + jnp.minimum(j, nvalid - 1)))
    dst = pl.BlockSpec((None, None, d, LANES), lambda l, j: (l, j // per_tile, 0, j % per_tile))
    shape = jax.ShapeDtypeStruct((depth, D_FF_PAD // FFN_TILE, d, FFN_TILE), BF16)
    return pl.pallas_call(
        functools.partial(_split_up_body, nvalid=nvalid),
        grid=(depth, D_FF_PAD // LANES),
        in_specs=[src(0), src(1)], out_specs=[dst, dst], out_shape=[shape, shape],
        compiler_params=_params(("parallel", "parallel"), 32),
        name="split_ffn_up",
    )(ffn_up, ffn_up)


def _pad_down_body(w_ref, o_ref, *, tk):
    row = pl.program_id(1) * tk + lax.broadcasted_iota(jnp.int32, (tk, 1), 0)
    o_ref[...] = jnp.where(row < D_FF, w_ref[...], 0.0).astype(BF16)


def _pad_ffn_down(ffn_down):
    depth, _, d = ffn_down.shape
    tk = FFN_TILE
    return pl.pallas_call(
        functools.partial(_pad_down_body, tk=tk),
        grid=(depth, D_FF_PAD // tk),
        in_specs=[pl.BlockSpec((None, tk, d), lambda l, j: (l, j, 0))],
        out_specs=pl.BlockSpec((None, tk, d), lambda l, j: (l, j, 0)),
        out_shape=jax.ShapeDtypeStruct((depth, D_FF_PAD, d), BF16),
        compiler_params=_params(("parallel", "parallel"), 32),
        name="pad_ffn_down",
    )(ffn_down)


def _pad_cols(w):
    return jnp.pad(w, ((0, 0), (0, D_FF_PAD - D_FF)))


def _forward(x_lat, x_ctx, cache_k, cache_v, state, c, c_ctx, w_ada, b_ada, norm1_g, norm2_g, w_in, w_out,
             q_norm_g, k_norm_g, rw_conv, rw_w0, rw_w2, rw_a0, rw_a2, rw_g2, rw_kk, rw_ka, rw_rk,
             rw_lnx_g, rw_lnx_b, ffn_up, ffn_conv_w, ffn_conv_b, ffn_down, final_norm_g):
    n_lat, t_lat, d = x_lat.shape
    n_ctx, t_ctx, _ = x_ctx.shape
    depth = w_ada.shape[0]
    past = cache_k.shape[2]
    rows_lat, rows_ctx = n_lat * t_lat, n_ctx * t_ctx
    assert n_lat < MOD_ROWS
    assert t_lat % SEQ_ALIGN == 0 and t_ctx % SEQ_ALIGN == 0 and t_lat % GRID_W == 0

    cvec = jnp.concatenate([c, jnp.broadcast_to(c_ctx[None, :], (MOD_ROWS - n_lat, d))], axis=0)
    mod = _adaln_mod(cvec, w_ada, b_ada).reshape(depth, MOD_ROWS, 6, 1, d)

    ffn_wa, ffn_wg = _split_ffn_up(ffn_up)
    ffn_wd = _pad_ffn_down(ffn_down)
    rope_tabs = _rope_tables(t_lat)
    time_lat, chan_tab = _dft_tables(t_lat)
    time_ctx, _ = _dft_tables(t_ctx)
    zero_state = jnp.zeros((n_ctx, 1, 2, RWKV_HEADS, RWKV_N, RWKV_N), F32)
    row2 = lambda a: a.reshape(1, -1)

    cache = (cache_k.reshape(n_lat, depth, past, KV_WIDTH), cache_v.reshape(n_lat, depth, past, KV_WIDTH))

    def trunk_layer(x, l, latent):
        nseq, t = (n_lat, t_lat) if latent else (n_ctx, t_ctx)
        rows = nseq * t
        seg = (0, t_lat) if latent else (n_lat, rows)
        region_rows_lat = rows if latent else 0
        shift1, scale1, gate1, shift2, scale2, gate2 = (mod[l, :, i] for i in range(6))
        in_args = (x, row2(norm1_g[l]), scale1, shift1, w_in_b, l, seg, row2(q_norm_g[l]), row2(k_norm_g[l]))
        if latent:
            f, qb, kb, vb, rw = _in_proj(*in_args, rope_tabs)
            a_out = _attention(qb, kb, vb, nseq, t, cache[0], cache[1], l)
            k_norm = v = None
        else:
            f, qb, kb, vb, rw, k_norm, v = _in_proj(*in_args)
            a_out = _attention(qb, kb, vb, nseq, t)
        f_out = _fourier_mix(f, time_lat if latent else time_ctx, chan_tab, 0, nseq, t)

        r_, v_, kk, g_, bonus, lw, km, b_ = _rwkv_prep(
            rw, rw_conv[l], lora_w[l], rw_w0[l], rw_a0[l],
            row2(rw_kk[l]), row2(rw_ka[l]), row2(rw_rk[l]), region_rows_lat, t_lat, t_ctx)
        yf, yb, s_fin = _rwkv_scan(r_, v_, kk, lw, km, b_, state if latent else zero_state, l if latent else 0,
                                   0, nseq, t)
        x = _out_proj(f_out, a_out, yf, yb, bonus, g_, row2(rw_lnx_g[l]), row2(rw_lnx_b[l]), x, gate1,
                      w_out_b, l, seg)
        x = _conv_ffn(x, row2(norm2_g[l]), scale2, shift2, gate2, ffn_wa, ffn_wg,
                      _pad_cols(ffn_conv_w[l][:, :D_FF]), _pad_cols(ffn_conv_w[l][:, D_FF:]),
                      _pad_cols(row2(ffn_conv_b[l][:D_FF])), _pad_cols(row2(ffn_conv_b[l][D_FF:])),
                      ffn_wd, row2(final_norm_g), l, seg, region_rows_lat, t_lat, t_ctx, final_norm=l == depth - 1)
        return x, k_norm, v, s_fin

    w_in_b, w_out_b = w_in.astype(BF16), w_out.astype(BF16)
    lora_w = [_lora_weight(rw_w2[l], rw_a2[l], rw_g2[l]).astype(BF16) for l in range(depth)]
    xs, xc = x_lat.reshape(rows_lat, d), x_ctx.reshape(rows_ctx, d)
    new_k, new_v, new_s = [], [], []
    for l in range(depth):
        xc, k_norm, v_ctx, s_ctx = trunk_layer(xc, l, latent=False)
        new_k.append(k_norm.reshape(n_ctx, t_ctx, ATTN_KV_HEADS, HEAD_DIM))
        new_v.append(v_ctx.reshape(n_ctx, t_ctx, ATTN_KV_HEADS, HEAD_DIM))
        new_s.append(s_ctx)
        xs, _, _, _ = trunk_layer(xs, l, latent=True)

    y_lat, y_ctx = xs.reshape(n_lat, t_lat, d), xc.reshape(n_ctx, t_ctx, d)
    return (y_ctx, y_lat, jnp.stack(new_k, axis=1), jnp.stack(new_v, axis=1), jnp.stack(new_s, axis=1))


def kernel(x_prompt, x_sample, cache_attn_k, cache_attn_v, state_rwkv, c, c_ctx, w_ada, b_ada, norm1_g, norm2_g, w_in, w_out, q_norm_g, k_norm_g, rw_conv, rw_w0, rw_w2, rw_a0, rw_a2, rw_g2, rw_kk, rw_ka, rw_rk, rw_lnx_g, rw_lnx_b, ffn_up, ffn_conv_w, ffn_conv_b, ffn_down, final_norm_g):
    return _forward(x_sample, x_prompt, cache_attn_k, cache_attn_v, state_rwkv, c, c_ctx, w_ada, b_ada,
                    norm1_g, norm2_g, w_in, w_out, q_norm_g, k_norm_g, rw_conv, rw_w0, rw_w2, rw_a0, rw_a2,
                    rw_g2, rw_kk, rw_ka, rw_rk, rw_lnx_g, rw_lnx_b, ffn_up, ffn_conv_w, ffn_conv_b, ffn_down,
                    final_norm_g)
```
